```python
import jax, jax.numpy as jnp
from jax import lax
import numpy as np

D_MODEL = 1024
BATCH = 1
SEQ = 16384
DEPTH = 2
DEC_BATCH = 8
DEC_SEQ = 32
PAST_LEN = 4096

CHUNK = 64
GLA_HEADS = 4
GLA_DK = D_MODEL // 8
GLA_DV = D_MODEL // 4
GLA_KEY = GLA_HEADS * GLA_DK
GLA_VAL = GLA_HEADS * GLA_DV
GATE_RANK = 16
GATE_NORMALIZER = 16.0
POOL_WIDTH = D_MODEL // 2
POOL_WINDOWS = (2, 4, 8, 16)
POOL_GROUPS = len(POOL_WINDOWS)
POOL_GROUP_DIM = POOL_WIDTH // POOL_GROUPS
POOL_HIST = max(POOL_WINDOWS) - 1
N_EXPERTS = 64
N_EXPERT_GROUPS = 8
TOPK_GROUPS = 4
TOP_K = 8
D_EXPERT = D_MODEL // 4
D_SHARED = D_MODEL // 4
ROUTED_SCALE = 2.5
ALPHA = (2 * DEPTH) ** 0.25
BETA = (8 * DEPTH) ** -0.25
LN_EPS = 1e-5
RMS_EPS = 1e-6
PROJ_SPLITS = tuple(int(s) for s in np.cumsum(
    [GLA_KEY, GLA_KEY, GLA_VAL, GLA_VAL, GATE_RANK, POOL_WIDTH, D_MODEL]))
N_PROJ = PROJ_SPLITS[-1] + D_MODEL

kernel_name = "gla_pool_moe_deepnorm_stream"


def layer_norm(x, g, b):
    xf = x.astype(jnp.float32)
    mu = jnp.mean(xf, axis=-1, keepdims=True)
    var = jnp.mean(jnp.square(xf - mu), axis=-1, keepdims=True)
    return ((xf - mu) * lax.rsqrt(var + LN_EPS) * g + b).astype(x.dtype)


def swiglu(x, wg, wu, wd):
    return (jax.nn.silu(x @ wg) * (x @ wu)) @ wd


def gla_scan(q, k, v, logf, s0):
    B, L, H, DK = q.shape
    DV = v.shape[-1]
    C = min(CHUNK, L)
    n = L // C
    causal = jnp.tril(jnp.ones((C, C), dtype=bool))

    def blocks(t):
        return t.reshape(B, n, C, H, t.shape[-1]).transpose(1, 0, 3, 2, 4)

    def step(S, inp):
        qc, kc, vc, gc = inp
        qf = qc.astype(jnp.float32)
        kf = kc.astype(jnp.float32)
        vf = vc.astype(jnp.float32)
        b = jnp.cumsum(gc, axis=2)
        o_inter = jnp.einsum('bhtk,bhkv->bhtv', qf * jnp.exp(b), S)
        diff = b[:, :, :, None, :] - b[:, :, None, :, :]
        decay = jnp.exp(jnp.where(causal[:, :, None], diff, -jnp.inf))
        att = jnp.einsum('bhtk,bhtsk,bhsk->bhts', qf, decay, kf)
        o = o_inter + jnp.einsum('bhts,bhsv->bhtv', att, vf)
        b_last = b[:, :, -1:, :]
        S = jnp.exp(b_last[:, :, 0, :])[..., None] * S + jnp.einsum(
            'bhsk,bhsv->bhkv', kf * jnp.exp(b_last - b), vf)
        return S, o

    S, o = lax.scan(step, s0.astype(jnp.float32),
                    (blocks(q), blocks(k), blocks(v), blocks(logf)))
    o = o.transpose(1, 0, 3, 2, 4).reshape(B, L, H, DV)
    return o, S


def multiscale_pool(u, hist, pos0, w_pool, pool_scale):
    L = u.shape[1]
    raw = jnp.concatenate([hist.astype(u.dtype), u], axis=1)
    ext = raw.astype(jnp.float32)
    cs = jnp.concatenate([jnp.zeros_like(ext[:, :1]), jnp.cumsum(ext, axis=1)], axis=1)
    pos = pos0 + jnp.arange(L)
    outs = []
    for g, w in enumerate(POOL_WINDOWS):
        sl = slice(g * POOL_GROUP_DIM, (g + 1) * POOL_GROUP_DIM)
        hi = cs[:, POOL_HIST + 1:POOL_HIST + 1 + L, sl]
        lo = cs[:, POOL_HIST + 1 - w:POOL_HIST + 1 - w + L, sl]
        cnt = jnp.minimum(w, pos + 1).astype(jnp.float32)
        d = (hi - lo) / cnt[None, :, None] - ext[:, POOL_HIST:, sl]
        outs.append(jnp.einsum('blc,cd->bld', d.astype(u.dtype), w_pool[g]))
    y = jnp.concatenate(outs, axis=-1) * pool_scale
    return y, raw[:, -POOL_HIST:]


def token_mixer(x, s0, hist, pos0, w_in, w_gk2, b_gk, gla_norm_g, w_pool, pool_scale,
                w_br_a, w_br_b, w_out):
    B, L, _ = x.shape
    q, k, v, r, gk_lr, u, ga, gb = jnp.split(x @ w_in, PROJ_SPLITS, axis=-1)
    q = q.reshape(B, L, GLA_HEADS, GLA_DK) * (GLA_DK ** -0.5)
    k = k.reshape(B, L, GLA_HEADS, GLA_DK)
    v = v.reshape(B, L, GLA_HEADS, GLA_DV)
    logf = jax.nn.log_sigmoid((gk_lr @ w_gk2 + b_gk).astype(jnp.float32)) / GATE_NORMALIZER
    logf = logf.reshape(B, L, GLA_HEADS, GLA_DK)
    o, s_new = gla_scan(q, k, v, logf, s0)
    o = o * lax.rsqrt(jnp.mean(jnp.square(o), axis=-1, keepdims=True) + RMS_EPS) * gla_norm_g
    o = (o.reshape(B, L, GLA_VAL) * jax.nn.silu(r.astype(jnp.float32))).astype(x.dtype)
    ya = o @ w_br_a
    yb, hist_new = multiscale_pool(u, hist, pos0, w_pool, pool_scale)
    yb = yb @ w_br_b
    y = (jax.nn.sigmoid(ga) * ya + jax.nn.sigmoid(gb) * yb) @ w_out
    return y, s_new.astype(s0.dtype), hist_new.astype(hist.dtype)


def routed_experts(xt, eidx, wsel, w_e_gate, w_e_up, w_e_down):
    T, D = xt.shape
    A = T * TOP_K
    m = 128 if A >= 128 * N_EXPERTS else 16
    NP = -(-(A + N_EXPERTS * (m - 1)) // m) * m
    nb = NP // m
    flat_e = eidx.reshape(-1).astype(jnp.int32)
    flat_tok = jnp.repeat(jnp.arange(T, dtype=jnp.int32), TOP_K)
    flat_w = wsel.reshape(-1)
    csum = jnp.cumsum(jax.nn.one_hot(flat_e, N_EXPERTS, dtype=jnp.int32), axis=0)
    counts = csum[-1]
    rank = jnp.take_along_axis(csum, flat_e[:, None], axis=1)[:, 0] - 1
    padded = (counts + m - 1) // m * m
    ends = jnp.cumsum(padded)
    starts = ends - padded
    dest = starts[flat_e] + rank
    row_tok = jnp.full((NP,), T, jnp.int32).at[dest].set(flat_tok)
    row_w = jnp.zeros((NP,), xt.dtype).at[dest].set(flat_w.astype(xt.dtype))
    block_e = jnp.minimum(jnp.searchsorted(ends, jnp.arange(nb, dtype=jnp.int32) * m,
                                           side='right'), N_EXPERTS - 1)
    x_pad = jnp.concatenate([xt, jnp.zeros((1, D), xt.dtype)], axis=0)

    def expert_block(args):
        rows, e = args
        return swiglu(x_pad[rows], w_e_gate[e], w_e_up[e], w_e_down[e])

    ys = lax.map(expert_block, (row_tok.reshape(nb, m), block_e)).reshape(NP, D)
    return jax.ops.segment_sum(ys * row_w[:, None], row_tok, num_segments=T + 1)[:T]


def moe(x, w_router, b_router, w_e_gate, w_e_up, w_e_down, w_sh_gate, w_sh_up, w_sh_down):
    B, L, D = x.shape
    T = B * L
    xt = x.reshape(T, D)
    scores = jax.nn.sigmoid((xt @ w_router).astype(jnp.float32))
    biased = scores + b_router.astype(jnp.float32)
    per_group = N_EXPERTS // N_EXPERT_GROUPS
    grp_score = lax.top_k(biased.reshape(T, N_EXPERT_GROUPS, per_group), 2)[0].sum(-1)
    _, gidx = lax.top_k(grp_score, TOPK_GROUPS)
    gmask = jax.nn.one_hot(gidx, N_EXPERT_GROUPS).sum(axis=1) > 0
    masked = jnp.where(jnp.repeat(gmask, per_group, axis=1), biased, -jnp.inf)
    _, eidx = lax.top_k(masked, TOP_K)
    wsel = jnp.take_along_axis(scores, eidx, axis=1)
    wsel = wsel / jnp.sum(wsel, axis=-1, keepdims=True) * ROUTED_SCALE
    routed = routed_experts(xt, eidx, wsel, w_e_gate, w_e_up, w_e_down)
    shared = swiglu(xt, w_sh_gate, w_sh_up, w_sh_down)
    return (shared + routed).reshape(B, L, D)


def trunk_layer(x, s0, hist, pos0, w_in, w_gk2, b_gk, gla_norm_g, w_pool, pool_scale,
                w_br_a, w_br_b, w_out, ln1_g, ln1_b, w_router, b_router, w_e_gate, w_e_up,
                w_e_down, w_sh_gate, w_sh_up, w_sh_down, ln2_g, ln2_b):
    mix, s_new, hist_new = token_mixer(x, s0, hist, pos0, w_in, w_gk2, b_gk, gla_norm_g,
                                       w_pool, pool_scale, w_br_a, w_br_b, w_out)
    x = layer_norm(ALPHA * x + mix, ln1_g, ln1_b)
    x = layer_norm(ALPHA * x + moe(x, w_router, b_router, w_e_gate, w_e_up, w_e_down,
                                   w_sh_gate, w_sh_up, w_sh_down), ln2_g, ln2_b)
    return x, s_new, hist_new


def setup_inputs(seed: int = 0) -> dict:
    key = jax.random.key(seed)
    ks = jax.random.split(key, 32)
    nrm = lambda i, shape, s: jax.random.normal(ks[i], shape, jnp.float32) * s
    D, E, F = D_MODEL, N_EXPERTS, D_EXPERT
    return {
        "x_prompt": nrm(0, (BATCH, SEQ, D), 1.0),
        "x_sample": nrm(1, (DEC_BATCH, DEC_SEQ, D), 1.0),
        "state_gla": nrm(2, (DEPTH, DEC_BATCH, GLA_HEADS, GLA_DK, GLA_DV), 1.0),
        "cache_pool": nrm(3, (DEPTH, DEC_BATCH, POOL_HIST, POOL_WIDTH), 1.0),
        "w_in": nrm(4, (DEPTH, D, N_PROJ), D ** -0.5),
        "w_gk2": nrm(5, (DEPTH, GATE_RANK, GLA_KEY), GATE_RANK ** -0.5),
        "b_gk": nrm(6, (DEPTH, GLA_KEY), 0.1),
        "gla_norm_g": 1.0 + nrm(7, (DEPTH, GLA_DV), 0.02),
        "w_pool": nrm(8, (DEPTH, POOL_GROUPS, POOL_GROUP_DIM, POOL_GROUP_DIM), POOL_GROUP_DIM ** -0.5),
        "pool_scale": 1.0 + nrm(9, (DEPTH, POOL_WIDTH), 0.02),
        "w_br_a": nrm(10, (DEPTH, GLA_VAL, D), BETA * GLA_VAL ** -0.5),
        "w_br_b": nrm(11, (DEPTH, POOL_WIDTH, D), BETA * POOL_WIDTH ** -0.5),
        "w_out": nrm(12, (DEPTH, D, D), BETA * D ** -0.5),
        "ln1_g": 1.0 + nrm(13, (DEPTH, D), 0.02),
        "ln1_b": nrm(14, (DEPTH, D), 0.02),
        "w_router": nrm(15, (DEPTH, D, E), D ** -0.5),
        "b_router": nrm(16, (DEPTH, E), 0.01),
        "w_e_gate": nrm(17, (DEPTH, E, D, F), D ** -0.5),
        "w_e_up": nrm(18, (DEPTH, E, D, F), D ** -0.5),
        "w_e_down": nrm(19, (DEPTH, E, F, D), BETA * F ** -0.5),
        "w_sh_gate": nrm(20, (DEPTH, D, D_SHARED), D ** -0.5),
        "w_sh_up": nrm(21, (DEPTH, D, D_SHARED), D ** -0.5),
        "w_sh_down": nrm(22, (DEPTH, D_SHARED, D), BETA * D_SHARED ** -0.5),
        "ln2_g": 1.0 + nrm(23, (DEPTH, D), 0.02),
        "ln2_b": nrm(24, (DEPTH, D), 0.02),
    }


def reference(x_prompt, x_sample, state_gla, cache_pool, w_in, w_gk2, b_gk, gla_norm_g,
              w_pool, pool_scale, w_br_a, w_br_b, w_out, ln1_g, ln1_b, w_router, b_router,
              w_e_gate, w_e_up, w_e_down, w_sh_gate, w_sh_up, w_sh_down, ln2_g, ln2_b):
    xp, xs = x_prompt, x_sample
    bp = x_prompt.shape[0]
    sp_list, hp_list, ss_list, hs_list = [], [], [], []
    for l in range(DEPTH):
        lw = (w_in[l], w_gk2[l], b_gk[l], gla_norm_g[l], w_pool[l], pool_scale[l],
              w_br_a[l], w_br_b[l], w_out[l], ln1_g[l], ln1_b[l], w_router[l], b_router[l],
              w_e_gate[l], w_e_up[l], w_e_down[l], w_sh_gate[l], w_sh_up[l], w_sh_down[l],
              ln2_g[l], ln2_b[l])
        s0_p = jnp.zeros((bp, GLA_HEADS, GLA_DK, GLA_DV), state_gla.dtype)
        h0_p = jnp.zeros((bp, POOL_HIST, POOL_WIDTH), cache_pool.dtype)
        xp, sp, hp = trunk_layer(xp, s0_p, h0_p, 0, *lw)
        xs, ss, hs = trunk_layer(xs, state_gla[l], cache_pool[l], PAST_LEN, *lw)
        sp_list.append(sp)
        hp_list.append(hp)
        ss_list.append(ss)
        hs_list.append(hs)
    return (xp, xs, jnp.stack(sp_list), jnp.stack(hp_list), jnp.stack(ss_list), jnp.stack(hs_list))
```

```python
import functools

import jax
import jax.numpy as jnp
from jax import lax
from jax.experimental import pallas as pl
from jax.experimental.pallas import tpu as pltpu

F32 = jnp.float32
BF16 = jnp.bfloat16

D_MODEL = 1024
DEPTH = 2
SEQ = 16384
DEC_BATCH = 8
DEC_SEQ = 32
PAST_LEN = 4096
CHUNK = 64
HEADS = 4
DK = 128
DV = 256
KEY = HEADS * DK
VAL = HEADS * DV
GATE_RANK = 16
GATE_NORMALIZER = 16.0
POOL_WIDTH = 512
POOL_WINDOWS = (2, 4, 8, 16)
POOL_GROUP_DIM = 128
POOL_HIST = 15
N_EXPERTS = 64
GROUP_SIZE = 8
TOPK_GROUPS = 4
TOP_K = 8
D_EXPERT = 256
ROUTED_SCALE = 2.5
ALPHA = (2 * DEPTH) ** 0.25
LN_EPS = 1e-5
RMS_EPS = 1e-6

LANES = 128
SUBLANES = 8
VMEM_LIMIT_BYTES = 56 * 1024 * 1024

T_PROMPT = SEQ
T_SAMPLE = DEC_BATCH * DEC_SEQ
T_ALL = T_PROMPT + T_SAMPLE
TM = 256
HALO = 16
TM_MOE = 1280
NEG_INF = float("-inf")


def _dot(a, b):
    return jnp.dot(a, b, preferred_element_type=F32)


def _dot_nt(a, b):
    return lax.dot_general(a, b, (((1,), (1,)), ((), ())), preferred_element_type=F32)


def _dot_tn(a, b):
    return lax.dot_general(a, b, (((0,), (0,)), ((), ())), preferred_element_type=F32)


def _sigmoid(x):
    return 1.0 / (1.0 + jnp.exp(-x))


def _silu(x):
    return x * _sigmoid(x)


def _layer_norm(x, g, b):
    mu = jnp.mean(x, axis=-1, keepdims=True)
    xc = x - mu
    var = jnp.mean(xc * xc, axis=-1, keepdims=True)
    return xc * lax.rsqrt(var + LN_EPS) * g + b


def _params(*sem):
    return pltpu.CompilerParams(dimension_semantics=sem, vmem_limit_bytes=VMEM_LIMIT_BYTES)


def _const_spec(shape):
    nd = len(shape)
    return pl.BlockSpec(shape, lambda *_: (0,) * nd)


def _inproj_kernel(x_ref, wq, wk, wv, wr, wgk, wgk2, bgk, wu, wga, wgb,
                   q_o, k_o, v_o, r_o, lf_o, u_o, ga_o, gb_o):
    xb = x_ref[...].astype(BF16)
    q_o[...] = _dot(xb, wq[...]) * (DK ** -0.5)
    k_o[...] = _dot(xb, wk[...])
    v_o[...] = _dot(xb, wv[...]).astype(BF16)
    r_o[...] = _dot(xb, wr[...])
    gk = _dot(xb, wgk[...]).astype(BF16)
    z = _dot(gk, wgk2[...]) + bgk[...]
    log_sig = jnp.minimum(z, 0.0) - jnp.log1p(jnp.exp(-jnp.abs(z)))
    lf_o[...] = log_sig * (1.0 / GATE_NORMALIZER)
    u_o[...] = _dot(xb, wu[...])
    ga_o[...] = _dot(xb, wga[...])
    gb_o[...] = _dot(xb, wgb[...])


def _inproj(x, w):
    n = x.shape[0] // TM
    row = lambda width: pl.BlockSpec((TM, width), lambda i: (i, 0))
    out_shapes = (
        jax.ShapeDtypeStruct((T_ALL, KEY), F32), jax.ShapeDtypeStruct((T_ALL, KEY), F32),
        jax.ShapeDtypeStruct((T_ALL, VAL), BF16), jax.ShapeDtypeStruct((T_ALL, VAL), F32),
        jax.ShapeDtypeStruct((T_ALL, KEY), F32), jax.ShapeDtypeStruct((T_ALL, POOL_WIDTH), F32),
        jax.ShapeDtypeStruct((T_ALL, D_MODEL), F32), jax.ShapeDtypeStruct((T_ALL, D_MODEL), F32))
    weights = (w["wq"], w["wk"], w["wv"], w["wr"], w["wgk"], w["wgk2"], w["bgk"], w["wu"], w["wga"], w["wgb"])
    return pl.pallas_call(
        _inproj_kernel, grid=(n,),
        in_specs=[row(D_MODEL)] + [_const_spec(a.shape) for a in weights],
        out_specs=[row(KEY), row(KEY), row(VAL), row(VAL), row(KEY), row(POOL_WIDTH), row(D_MODEL), row(D_MODEL)],
        out_shape=out_shapes, compiler_params=_params("arbitrary"), name="inproj",
    )(x, *weights)


def _gla_kernel(*refs, chunk, n_chunks, has_init):
    if has_init:
        q_ref, k_ref, v_ref, lf_ref, r_ref, gn_ref, s0_ref, o_ref, st_ref = refs
        st_ref[...] = s0_ref[...]
    else:
        q_ref, k_ref, v_ref, lf_ref, r_ref, gn_ref, o_ref, st_ref = refs

        @pl.when(pl.program_id(0) == 0)
        def _():
            st_ref[...] = jnp.zeros_like(st_ref)

    row = lax.broadcasted_iota(jnp.int32, (chunk, KEY), 0)
    ta = lax.broadcasted_iota(jnp.int32, (chunk, chunk), 0)
    sa = lax.broadcasted_iota(jnp.int32, (chunk, chunk), 1)
    gn = gn_ref[...]

    def chunk_body(c, carry):
        rows = pl.ds(pl.multiple_of(c * chunk, chunk), chunk)
        lf = lf_ref[rows, :]
        q = q_ref[rows, :]
        k = k_ref[rows, :]
        qb = q.astype(BF16)
        kb = k.astype(BF16)
        att = []
        for h in range(HEADS):
            hs = slice(h * DK, (h + 1) * DK)
            att.append(jnp.where(ta == sa, _dot_nt(qb[:, hs], kb[:, hs]), 0.0))
        seg, tot = lf, lf
        half = 1
        while half < chunk:
            qs = (q * jnp.exp(seg)).astype(BF16)
            ks = (k * jnp.exp(tot - seg)).astype(BF16)
            pair = ((ta ^ sa) < 2 * half) & ((ta & half) != 0) & ((sa & half) == 0)
            for h in range(HEADS):
                hs = slice(h * DK, (h + 1) * DK)
                att[h] = att[h] + jnp.where(pair, _dot_nt(qs[:, hs], ks[:, hs]), 0.0)
            upper = (row & half) != 0
            below = pltpu.roll(tot, half, 0)
            above = pltpu.roll(tot, chunk - half, 0)
            seg = seg + jnp.where(upper, below, 0.0)
            tot = tot + jnp.where(upper, below, above)
            half *= 2
        q_in = (q * jnp.exp(seg)).astype(BF16)
        k_out = (k * jnp.exp(tot - seg)).astype(BF16)
        decay = jnp.exp(tot[0:1, :])
        for h in range(HEADS):
            hs = slice(h * DK, (h + 1) * DK)
            vs = slice(h * DV, (h + 1) * DV)
            state = st_ref[0, h]
            vh = v_ref[rows, vs]
            o = _dot_nt(q_in[:, hs], state.astype(BF16)) + _dot(att[h].astype(BF16), vh)
            st_ref[0, h] = state * decay[:, hs] + _dot_tn(vh, k_out[:, hs])
            ms = jnp.mean(o * o, axis=-1, keepdims=True)
            o = o * lax.rsqrt(ms + RMS_EPS) * gn
            o_ref[rows, vs] = (o * _silu(r_ref[rows, vs])).astype(BF16)
        return carry

    lax.fori_loop(0, n_chunks, chunk_body, 0)


def _gla(q, k, v, lf, r, gn, s0t, *, tile, chunk, n_tiles, block0):
    has_init = s0t is not None
    row = lambda width: pl.BlockSpec((tile, width), lambda i: (block0 + i, 0))
    st_spec = pl.BlockSpec((1, HEADS, DV, DK), (lambda i: (i, 0, 0, 0)) if has_init else (lambda i: (0, 0, 0, 0)))
    n_states = n_tiles if has_init else 1
    in_specs = [row(KEY), row(KEY), row(VAL), row(KEY), row(VAL), _const_spec(gn.shape)]
    args = [q, k, v, lf, r, gn]
    if has_init:
        in_specs.append(st_spec)
        args.append(s0t)
    return pl.pallas_call(
        functools.partial(_gla_kernel, chunk=chunk, n_chunks=tile // chunk, has_init=has_init),
        grid=(n_tiles,), in_specs=in_specs,
        out_specs=[pl.BlockSpec((tile, VAL), lambda i: (i, 0)), st_spec],
        out_shape=(jax.ShapeDtypeStruct((n_tiles * tile, VAL), BF16),
                   jax.ShapeDtypeStruct((n_states, HEADS, DV, DK), F32)),
        compiler_params=_params("arbitrary"), name="gla_init" if has_init else "gla",
    )(*args)


def _mix_kernel(x_ref, og_ref, u_ref, halo_ref, ga_ref, gb_ref, wbra, wpool, pscale, wbrb, wout, g1, b1,
                x1_ref, *, tile, pos0, pos_stride):
    ya = _dot(og_ref[...], wbra[...])
    u = u_ref[...]
    ext = jnp.concatenate([halo_ref[0], u], axis=0)
    pos = pos0 + pl.program_id(0) * pos_stride + lax.broadcasted_iota(jnp.int32, (tile, 1), 0)
    parts = []
    for g, window in enumerate(POOL_WINDOWS):
        cols = slice(g * POOL_GROUP_DIM, (g + 1) * POOL_GROUP_DIM)
        win = ext[:, cols]
        shift = 1
        while shift < window:
            win = win + pltpu.roll(win, shift, 0)
            shift *= 2
        cnt = jnp.minimum(window, pos + 1).astype(F32)
        d = win[HALO:, :] / cnt - u[:, cols]
        parts.append(_dot(d.astype(BF16), wpool[g]))
    yb_in = jnp.concatenate(parts, axis=1) * pscale[...]
    yb = _dot(yb_in.astype(BF16), wbrb[...])
    mixed = _sigmoid(ga_ref[...]) * ya + _sigmoid(gb_ref[...]) * yb
    mix = _dot(mixed.astype(BF16), wout[...])
    x1_ref[...] = _layer_norm(ALPHA * x_ref[...] + mix, g1[...], b1[...])


def _mix(x, og, u, halo, ga, gb, w, *, tile, n_tiles, block0, pos0, pos_stride):
    row = lambda width: pl.BlockSpec((tile, width), lambda i: (block0 + i, 0))
    weights = (w["wbra"], w["wpool"], w["pscale"], w["wbrb"], w["wout"], w["g1"], w["b1"])
    return pl.pallas_call(
        functools.partial(_mix_kernel, tile=tile, pos0=pos0, pos_stride=pos_stride),
        grid=(n_tiles,),
        in_specs=[row(D_MODEL), pl.BlockSpec((tile, VAL), lambda i: (i, 0)), row(POOL_WIDTH),
                  pl.BlockSpec((1, HALO, POOL_WIDTH), lambda i: (i, 0, 0)), row(D_MODEL), row(D_MODEL)]
                 + [_const_spec(a.shape) for a in weights],
        out_specs=pl.BlockSpec((tile, D_MODEL), lambda i: (i, 0)),
        out_shape=jax.ShapeDtypeStruct((n_tiles * tile, D_MODEL), F32),
        compiler_params=_params("arbitrary"), name="mix",
    )(x, og, u, halo, ga, gb, *weights)


def _router_kernel(x1_ref, wrh_ref, wrl_ref, br_ref, wdt_ref, *, tile):
    x = x1_ref[...]
    xh = x.astype(BF16)
    xl = (x - xh.astype(F32)).astype(BF16)
    logits = _dot_nt(wrh_ref[...], xh) + _dot_nt(wrl_ref[...], xh) + _dot_nt(wrh_ref[...], xl)
    scores = _sigmoid(logits)
    rows_pad = 2 * N_EXPERTS
    n_grp = rows_pad // GROUP_SIZE
    erow = lax.broadcasted_iota(jnp.int32, (rows_pad, tile), 0)
    biased = jnp.where(erow < N_EXPERTS, scores + br_ref[...], NEG_INF)
    shape3 = (n_grp, GROUP_SIZE, tile)
    b3 = biased.reshape(shape3)
    s3 = scores.reshape(shape3)
    sub = lax.broadcasted_iota(jnp.int32, shape3, 1)
    gid = lax.broadcasted_iota(jnp.int32, shape3, 0)
    eid = gid * GROUP_SIZE + sub
    m1 = jnp.max(b3, axis=1, keepdims=True)
    i1 = jnp.min(jnp.where(b3 == m1, sub, GROUP_SIZE), axis=1, keepdims=True)
    m2 = jnp.max(jnp.where(sub == i1, NEG_INF, b3), axis=1, keepdims=True)
    gscore = m1 + m2
    gid1 = lax.broadcasted_iota(jnp.int32, (n_grp, 1, tile), 0)
    gsel = jnp.zeros((n_grp, 1, tile), jnp.bool_)
    for _ in range(TOPK_GROUPS):
        gm = jnp.max(gscore, axis=0, keepdims=True)
        gi = jnp.min(jnp.where(gscore == gm, gid1, n_grp), axis=0, keepdims=True)
        pick = gid1 == gi
        gsel = gsel | pick
        gscore = jnp.where(pick, NEG_INF, gscore)
    masked = jnp.where(gsel, b3, NEG_INF)
    wsel = jnp.zeros(shape3, F32)
    for _ in range(TOP_K):
        m = jnp.max(jnp.max(masked, axis=1, keepdims=True), axis=0, keepdims=True)
        idx = jnp.min(jnp.min(jnp.where(masked == m, eid, rows_pad), axis=1, keepdims=True), axis=0, keepdims=True)
        pick = eid == idx
        wsel = jnp.where(pick, s3, wsel)
        masked = jnp.where(pick, NEG_INF, masked)
    wsum = jnp.sum(jnp.sum(wsel, axis=1, keepdims=True), axis=0, keepdims=True)
    wdt_ref[...] = (wsel / wsum * ROUTED_SCALE).reshape(rows_pad, tile)


def _router(x1, w):
    n = x1.shape[0] // TM
    return pl.pallas_call(
        functools.partial(_router_kernel, tile=TM), grid=(n,),
        in_specs=[pl.BlockSpec((TM, D_MODEL), lambda i: (i, 0)), _const_spec(w["wrh"].shape),
                  _const_spec(w["wrl"].shape), _const_spec(w["br"].shape)],
        out_specs=pl.BlockSpec((2 * N_EXPERTS, TM), lambda i: (0, i)),
        out_shape=jax.ShapeDtypeStruct((2 * N_EXPERTS, x1.shape[0]), F32),
        compiler_params=_params("arbitrary"), name="router",
    )(x1, w["wrh"], w["wrl"], w["br"])


def _moe_kernel(x1_ref, wdt_ref, wg_ref, wu_ref, wd_ref, wsg, wsu, wsd, g2, b2, out_ref, xb_scr, wtok_scr, acc_scr):
    e = pl.program_id(1)

    @pl.when(e == 0)
    def _():
        xb0 = x1_ref[...].astype(BF16)
        xb_scr[...] = xb0
        wtok_scr[...] = wdt_ref[...].T
        hs = (_silu(_dot(xb0, wsg[...])) * _dot(xb0, wsu[...])).astype(BF16)
        acc_scr[...] = _dot(hs, wsd[...])

    xb = xb_scr[...]
    hid = (_silu(_dot(xb, wg_ref[0].astype(BF16))) * _dot(xb, wu_ref[0].astype(BF16))).astype(BF16)
    y = _dot(hid, wd_ref[0].astype(BF16))
    lane = lax.broadcasted_iota(jnp.int32, wtok_scr.shape, 1)
    wcol = jnp.sum(jnp.where(lane == e, wtok_scr[...], 0.0), axis=1, keepdims=True)
    acc_scr[...] += wcol * y

    @pl.when(e == N_EXPERTS - 1)
    def _():
        out_ref[...] = _layer_norm(ALPHA * x1_ref[...] + acc_scr[...], g2[...], b2[...])


def _moe(x1, wdt, w_e_gate, w_e_up, w_e_down, w, layer):
    n = x1.shape[0] // TM_MOE
    small = (w["wsg"], w["wsu"], w["wsd"], w["g2"], w["b2"])
    return pl.pallas_call(
        _moe_kernel, grid=(n, N_EXPERTS),
        in_specs=[pl.BlockSpec((TM_MOE, D_MODEL), lambda i, e: (i, 0)),
                  pl.BlockSpec((2 * N_EXPERTS, TM_MOE), lambda i, e: (0, i)),
                  pl.BlockSpec((None, 1, D_MODEL, D_EXPERT), lambda i, e: (layer, e, 0, 0)),
                  pl.BlockSpec((None, 1, D_MODEL, D_EXPERT), lambda i, e: (layer, e, 0, 0)),
                  pl.BlockSpec((None, 1, D_EXPERT, D_MODEL), lambda i, e: (layer, e, 0, 0))]
                 + [_const_spec(a.shape) for a in small],
        out_specs=pl.BlockSpec((TM_MOE, D_MODEL), lambda i, e: (i, 0)),
        out_shape=jax.ShapeDtypeStruct(x1.shape, F32),
        scratch_shapes=[pltpu.VMEM((TM_MOE, D_MODEL), BF16), pltpu.VMEM((TM_MOE, 2 * N_EXPERTS), F32),
                        pltpu.VMEM((TM_MOE, D_MODEL), F32)],
        compiler_params=_params("arbitrary", "arbitrary"), name="moe",
    )(x1, wdt, w_e_gate, w_e_up, w_e_down, *small)


def _layer_weights(l, w_in, w_gk2, b_gk, gla_norm_g, w_pool, pool_scale, w_br_a, w_br_b, w_out, ln1_g, ln1_b,
                   w_router, b_router, w_sh_gate, w_sh_up, w_sh_down, ln2_g, ln2_b):
    wi = w_in[l].astype(BF16)
    o = 0
    pieces = {}
    for name, width in (("wq", KEY), ("wk", KEY), ("wv", VAL), ("wr", VAL), ("wgk", GATE_RANK),
                        ("wu", POOL_WIDTH), ("wga", D_MODEL), ("wgb", D_MODEL)):
        pieces[name] = wi[:, o:o + width]
        o += width
    pad_rank = LANES - GATE_RANK
    pieces["wgk"] = jnp.pad(pieces["wgk"], ((0, 0), (0, pad_rank)))
    pieces["wgk2"] = jnp.pad(w_gk2[l].astype(BF16), ((0, pad_rank), (0, 0)))
    pieces["bgk"] = b_gk[l].reshape(1, KEY)
    pieces["gn"] = gla_norm_g[l].reshape(1, DV)
    pieces["wbra"] = w_br_a[l].astype(BF16)
    pieces["wpool"] = w_pool[l].astype(BF16)
    pieces["pscale"] = pool_scale[l].reshape(1, POOL_WIDTH)
    pieces["wbrb"] = w_br_b[l].astype(BF16)
    pieces["wout"] = w_out[l].astype(BF16)
    pieces["g1"] = ln1_g[l].reshape(1, D_MODEL)
    pieces["b1"] = ln1_b[l].reshape(1, D_MODEL)
    wrt = jnp.pad(w_router[l].T, ((0, N_EXPERTS), (0, 0)))
    wrh = wrt.astype(BF16)
    pieces["wrh"] = wrh
    pieces["wrl"] = (wrt - wrh.astype(F32)).astype(BF16)
    pieces["br"] = jnp.pad(b_router[l], (0, N_EXPERTS)).reshape(2 * N_EXPERTS, 1)
    pieces["wsg"] = w_sh_gate[l].astype(BF16)
    pieces["wsu"] = w_sh_up[l].astype(BF16)
    pieces["wsd"] = w_sh_down[l].astype(BF16)
    pieces["g2"] = ln2_g[l].reshape(1, D_MODEL)
    pieces["b2"] = ln2_b[l].reshape(1, D_MODEL)
    return pieces


def kernel(x_prompt, x_sample, state_gla, cache_pool, w_in, w_gk2, b_gk, gla_norm_g, w_pool, pool_scale, w_br_a, w_br_b, w_out, ln1_g, ln1_b, w_router, b_router, w_e_gate, w_e_up, w_e_down, w_sh_gate, w_sh_up, w_sh_down, ln2_g, ln2_b):
    x = jnp.concatenate([x_prompt.reshape(T_PROMPT, D_MODEL), x_sample.reshape(T_SAMPLE, D_MODEL)], axis=0)
    n_prompt_tiles = T_PROMPT // TM
    sample_block0 = T_PROMPT // DEC_SEQ
    sp, hp, ss, hs = [], [], [], []
    for l in range(DEPTH):
        w = _layer_weights(l, w_in, w_gk2, b_gk, gla_norm_g, w_pool, pool_scale, w_br_a, w_br_b, w_out, ln1_g,
                           ln1_b, w_router, b_router, w_sh_gate, w_sh_up, w_sh_down, ln2_g, ln2_b)
        q, k, v, r, lf, u, ga, gb = _inproj(x, w)

        og_p, st_p = _gla(q, k, v, lf, r, w["gn"], None, tile=TM, chunk=CHUNK, n_tiles=n_prompt_tiles, block0=0)
        s0t = jnp.swapaxes(state_gla[l], -1, -2)
        og_s, st_s = _gla(q, k, v, lf, r, w["gn"], s0t, tile=DEC_SEQ, chunk=DEC_SEQ, n_tiles=DEC_BATCH,
                          block0=sample_block0)

        u_p = u[:T_PROMPT].reshape(n_prompt_tiles, TM, POOL_WIDTH)
        halo_p = jnp.concatenate([jnp.zeros((1, HALO, POOL_WIDTH), F32), u_p[:-1, TM - HALO:, :]], axis=0)
        halo_s = jnp.concatenate([jnp.zeros((DEC_BATCH, HALO - POOL_HIST, POOL_WIDTH), F32), cache_pool[l]], axis=1)
        x1_p = _mix(x, og_p, u, halo_p, ga, gb, w, tile=TM, n_tiles=n_prompt_tiles, block0=0, pos0=0, pos_stride=TM)
        x1_s = _mix(x, og_s, u, halo_s, ga, gb, w, tile=DEC_SEQ, n_tiles=DEC_BATCH, block0=sample_block0,
                    pos0=PAST_LEN, pos_stride=0)
        x1 = jnp.concatenate([x1_p, x1_s], axis=0)

        wdt = _router(x1, w)
        x = _moe(x1, wdt, w_e_gate, w_e_up, w_e_down, w, l)

        sp.append(jnp.swapaxes(st_p, -1, -2))
        ss.append(jnp.swapaxes(st_s, -1, -2))
        hp.append(u[T_PROMPT - POOL_HIST:T_PROMPT].reshape(1, POOL_HIST, POOL_WIDTH))
        hs.append(u[T_PROMPT:].reshape(DEC_BATCH, DEC_SEQ, POOL_WIDTH)[:, DEC_SEQ - POOL_HIST:, :])
    y_prompt = x[:T_PROMPT].reshape(1, SEQ, D_MODEL)
    y_sample = x[T_PROMPT:].reshape(DEC_BATCH, DEC_SEQ, D_MODEL)
    return (y_prompt, y_sample, jnp.stack(sp), jnp.stack(hp), jnp.stack(ss), jnp.stack(hs))
```

```python
import functools

import jax
import jax.numpy as jnp
from jax import lax
from jax.experimental import pallas as pl
from jax.experimental.pallas import tpu as pltpu

F32 = jnp.float32
BF16 = jnp.bfloat16
U32 = jnp.uint32
I32 = jnp.int32

D_MODEL = 1024
DEPTH = 2
SEQ = 16384
DEC_BATCH = 8
DEC_SEQ = 32
PAST_LEN = 4096
CHUNK = 64
HEADS = 4
DK = 128
DV = 256
KEY = HEADS * DK
VAL = HEADS * DV
GATE_RANK = 16
GATE_NORMALIZER = 16.0
POOL_WIDTH = 512
POOL_WINDOWS = (2, 4, 8, 16)
POOL_GROUP_DIM = 128
POOL_HIST = 15
N_EXPERTS = 64
GROUP_SIZE = 8
TOPK_GROUPS = 4
TOP_K = 8
D_EXPERT = 256
ROUTED_SCALE = 2.5
ALPHA = (2 * DEPTH) ** 0.25
LN_EPS = 1e-5
RMS_EPS = 1e-6

LANES = 128
SUBLANES = 8
VMEM_LIMIT_BYTES = 56 * 1024 * 1024

T_PROMPT = SEQ
T_SAMPLE = DEC_BATCH * DEC_SEQ
T_ALL = T_PROMPT + T_SAMPLE
TM = 256
HALO = 16
NEG_INF = float("-inf")

HALF = D_MODEL // 2
GRANULE = SUBLANES
ROW_BLOCK = 256
RUN_CHUNK = 64
L_MAX = 2560
SEL_CHUNK = 512
HI_MASK = 0xFFFF0000


def _dot(a, b):
    return jnp.dot(a, b, preferred_element_type=F32)


def _dot_nt(a, b):
    return lax.dot_general(a, b, (((1,), (1,)), ((), ())), preferred_element_type=F32)


def _dot_tn(a, b):
    return lax.dot_general(a, b, (((0,), (0,)), ((), ())), preferred_element_type=F32)


def _sigmoid(x):
    return 1.0 / (1.0 + jnp.exp(-x))


def _silu(x):
    return x * _sigmoid(x)


def _layer_norm(x, g, b):
    mu = jnp.mean(x, axis=-1, keepdims=True)
    xc = x - mu
    var = jnp.mean(xc * xc, axis=-1, keepdims=True)
    return xc * lax.rsqrt(var + LN_EPS) * g + b


def _pack_halves(v):
    bits = lax.bitcast_convert_type(v, U32)
    return (bits[:, HALF:] & jnp.uint32(HI_MASK)) | (bits[:, :HALF] >> 16)


def _unpack_halves(w):
    lo = lax.bitcast_convert_type(w << 16, F32).astype(BF16)
    hi = lax.bitcast_convert_type(w & jnp.uint32(HI_MASK), F32).astype(BF16)
    return lo, hi


def _params(*sem):
    return pltpu.CompilerParams(dimension_semantics=sem, vmem_limit_bytes=VMEM_LIMIT_BYTES)


def _const_spec(shape):
    nd = len(shape)
    return pl.BlockSpec(shape, lambda *_: (0,) * nd)


def _inproj_kernel(x_ref, wq, wk, wv, wr, wgk, wgk2, bgk, wu, wga, wgb,
                   q_o, k_o, v_o, r_o, lf_o, u_o, ga_o, gb_o):
    xb = x_ref[...].astype(BF16)
    q_o[...] = _dot(xb, wq[...]) * (DK ** -0.5)
    k_o[...] = _dot(xb, wk[...])
    v_o[...] = _dot(xb, wv[...]).astype(BF16)
    r_o[...] = _dot(xb, wr[...])
    gk = _dot(xb, wgk[...]).astype(BF16)
    z = _dot(gk, wgk2[...]) + bgk[...]
    log_sig = jnp.minimum(z, 0.0) - jnp.log1p(jnp.exp(-jnp.abs(z)))
    lf_o[...] = log_sig * (1.0 / GATE_NORMALIZER)
    u_o[...] = _dot(xb, wu[...])
    ga_o[...] = _dot(xb, wga[...])
    gb_o[...] = _dot(xb, wgb[...])


def _inproj(x, w):
    n = x.shape[0] // TM
    row = lambda width: pl.BlockSpec((TM, width), lambda i: (i, 0))
    out_shapes = (
        jax.ShapeDtypeStruct((T_ALL, KEY), F32), jax.ShapeDtypeStruct((T_ALL, KEY), F32),
        jax.ShapeDtypeStruct((T_ALL, VAL), BF16), jax.ShapeDtypeStruct((T_ALL, VAL), F32),
        jax.ShapeDtypeStruct((T_ALL, KEY), F32), jax.ShapeDtypeStruct((T_ALL, POOL_WIDTH), F32),
        jax.ShapeDtypeStruct((T_ALL, D_MODEL), F32), jax.ShapeDtypeStruct((T_ALL, D_MODEL), F32))
    weights = (w["wq"], w["wk"], w["wv"], w["wr"], w["wgk"], w["wgk2"], w["bgk"], w["wu"], w["wga"], w["wgb"])
    return pl.pallas_call(
        _inproj_kernel, grid=(n,),
        in_specs=[row(D_MODEL)] + [_const_spec(a.shape) for a in weights],
        out_specs=[row(KEY), row(KEY), row(VAL), row(VAL), row(KEY), row(POOL_WIDTH), row(D_MODEL), row(D_MODEL)],
        out_shape=out_shapes, compiler_params=_params("arbitrary"), name="inproj",
    )(x, *weights)


def _gla_kernel(*refs, chunk, n_chunks, has_init):
    if has_init:
        q_ref, k_ref, v_ref, lf_ref, r_ref, gn_ref, s0_ref, o_ref, st_ref = refs
        st_ref[...] = s0_ref[...]
    else:
        q_ref, k_ref, v_ref, lf_ref, r_ref, gn_ref, o_ref, st_ref = refs

        @pl.when(pl.program_id(0) == 0)
        def _():
            st_ref[...] = jnp.zeros_like(st_ref)

    row = lax.broadcasted_iota(I32, (chunk, KEY), 0)
    ta = lax.broadcasted_iota(I32, (chunk, chunk), 0)
    sa = lax.broadcasted_iota(I32, (chunk, chunk), 1)
    gn = gn_ref[...]

    def chunk_body(c, carry):
        rows = pl.ds(pl.multiple_of(c * chunk, chunk), chunk)
        lf = lf_ref[rows, :]
        q = q_ref[rows, :]
        k = k_ref[rows, :]
        qb = q.astype(BF16)
        kb = k.astype(BF16)
        att = []
        for h in range(HEADS):
            hs = slice(h * DK, (h + 1) * DK)
            att.append(jnp.where(ta == sa, _dot_nt(qb[:, hs], kb[:, hs]), 0.0))
        seg, tot = lf, lf
        half = 1
        while half < chunk:
            qs = (q * jnp.exp(seg)).astype(BF16)
            ks = (k * jnp.exp(tot - seg)).astype(BF16)
            pair = ((ta ^ sa) < 2 * half) & ((ta & half) != 0) & ((sa & half) == 0)
            for h in range(HEADS):
                hs = slice(h * DK, (h + 1) * DK)
                att[h] = att[h] + jnp.where(pair, _dot_nt(qs[:, hs], ks[:, hs]), 0.0)
            upper = (row & half) != 0
            below = pltpu.roll(tot, half, 0)
            above = pltpu.roll(tot, chunk - half, 0)
            seg = seg + jnp.where(upper, below, 0.0)
            tot = tot + jnp.where(upper, below, above)
            half *= 2
        q_in = (q * jnp.exp(seg)).astype(BF16)
        k_out = (k * jnp.exp(tot - seg)).astype(BF16)
        decay = jnp.exp(tot[0:1, :])
        for h in range(HEADS):
            hs = slice(h * DK, (h + 1) * DK)
            vs = slice(h * DV, (h + 1) * DV)
            state = st_ref[0, h]
            vh = v_ref[rows, vs]
            o = _dot_nt(q_in[:, hs], state.astype(BF16)) + _dot(att[h].astype(BF16), vh)
            st_ref[0, h] = state * decay[:, hs] + _dot_tn(vh, k_out[:, hs])
            ms = jnp.mean(o * o, axis=-1, keepdims=True)
            o = o * lax.rsqrt(ms + RMS_EPS) * gn
            o_ref[rows, vs] = (o * _silu(r_ref[rows, vs])).astype(BF16)
        return carry

    lax.fori_loop(0, n_chunks, chunk_body, 0)


def _gla(q, k, v, lf, r, gn, s0t, *, tile, chunk, n_tiles, block0):
    has_init = s0t is not None
    row = lambda width: pl.BlockSpec((tile, width), lambda i: (block0 + i, 0))
    st_spec = pl.BlockSpec((1, HEADS, DV, DK), (lambda i: (i, 0, 0, 0)) if has_init else (lambda i: (0, 0, 0, 0)))
    n_states = n_tiles if has_init else 1
    in_specs = [row(KEY), row(KEY), row(VAL), row(KEY), row(VAL), _const_spec(gn.shape)]
    args = [q, k, v, lf, r, gn]
    if has_init:
        in_specs.append(st_spec)
        args.append(s0t)
    return pl.pallas_call(
        functools.partial(_gla_kernel, chunk=chunk, n_chunks=tile // chunk, has_init=has_init),
        grid=(n_tiles,), in_specs=in_specs,
        out_specs=[pl.BlockSpec((tile, VAL), lambda i: (i, 0)), st_spec],
        out_shape=(jax.ShapeDtypeStruct((n_tiles * tile, VAL), BF16),
                   jax.ShapeDtypeStruct((n_states, HEADS, DV, DK), F32)),
        compiler_params=_params("arbitrary"), name="gla_init" if has_init else "gla",
    )(*args)


def _mix_kernel(x_ref, og_ref, u_ref, halo_ref, ga_ref, gb_ref, wbra, wpool, pscale, wbrb, wout, g1, b1,
                x1_ref, *, tile, pos0, pos_stride):
    ya = _dot(og_ref[...], wbra[...])
    u = u_ref[...]
    ext = jnp.concatenate([halo_ref[0], u], axis=0)
    pos = pos0 + pl.program_id(0) * pos_stride + lax.broadcasted_iota(I32, (tile, 1), 0)
    parts = []
    for g, window in enumerate(POOL_WINDOWS):
        cols = slice(g * POOL_GROUP_DIM, (g + 1) * POOL_GROUP_DIM)
        win = ext[:, cols]
        shift = 1
        while shift < window:
            win = win + pltpu.roll(win, shift, 0)
            shift *= 2
        cnt = jnp.minimum(window, pos + 1).astype(F32)
        d = win[HALO:, :] / cnt - u[:, cols]
        parts.append(_dot(d.astype(BF16), wpool[g]))
    yb_in = jnp.concatenate(parts, axis=1) * pscale[...]
    yb = _dot(yb_in.astype(BF16), wbrb[...])
    mixed = _sigmoid(ga_ref[...]) * ya + _sigmoid(gb_ref[...]) * yb
    mix = _dot(mixed.astype(BF16), wout[...])
    x1_ref[...] = _layer_norm(ALPHA * x_ref[...] + mix, g1[...], b1[...])


def _mix(x, og, u, halo, ga, gb, w, *, tile, n_tiles, block0, pos0, pos_stride):
    row = lambda width: pl.BlockSpec((tile, width), lambda i: (block0 + i, 0))
    weights = (w["wbra"], w["wpool"], w["pscale"], w["wbrb"], w["wout"], w["g1"], w["b1"])
    return pl.pallas_call(
        functools.partial(_mix_kernel, tile=tile, pos0=pos0, pos_stride=pos_stride),
        grid=(n_tiles,),
        in_specs=[row(D_MODEL), pl.BlockSpec((tile, VAL), lambda i: (i, 0)), row(POOL_WIDTH),
                  pl.BlockSpec((1, HALO, POOL_WIDTH), lambda i: (i, 0, 0)), row(D_MODEL), row(D_MODEL)]
                 + [_const_spec(a.shape) for a in weights],
        out_specs=pl.BlockSpec((tile, D_MODEL), lambda i: (i, 0)),
        out_shape=jax.ShapeDtypeStruct((n_tiles * tile, D_MODEL), F32),
        compiler_params=_params("arbitrary"), name="mix",
    )(x, og, u, halo, ga, gb, *weights)


def _router_kernel(x1_ref, wrh_ref, wrl_ref, br_ref, wt_ref, rank_ref, cnt_ref, *, tile):
    x = x1_ref[...]
    xh = x.astype(BF16)
    xl = (x - xh.astype(F32)).astype(BF16)
    logits = _dot_nt(wrh_ref[...], xh) + _dot_nt(wrl_ref[...], xh) + _dot_nt(wrh_ref[...], xl)
    scores = _sigmoid(logits)
    rows_pad = 2 * N_EXPERTS
    n_grp = rows_pad // GROUP_SIZE
    erow = lax.broadcasted_iota(I32, (rows_pad, tile), 0)
    biased = jnp.where(erow < N_EXPERTS, scores + br_ref[...], NEG_INF)
    shape3 = (n_grp, GROUP_SIZE, tile)
    b3 = biased.reshape(shape3)
    s3 = scores.reshape(shape3)
    sub = lax.broadcasted_iota(I32, shape3, 1)
    gid = lax.broadcasted_iota(I32, shape3, 0)
    eid = gid * GROUP_SIZE + sub
    m1 = jnp.max(b3, axis=1, keepdims=True)
    i1 = jnp.min(jnp.where(b3 == m1, sub, GROUP_SIZE), axis=1, keepdims=True)
    m2 = jnp.max(jnp.where(sub == i1, NEG_INF, b3), axis=1, keepdims=True)
    gscore = m1 + m2
    gid1 = lax.broadcasted_iota(I32, (n_grp, 1, tile), 0)
    gsel = jnp.zeros((n_grp, 1, tile), jnp.bool_)
    for _ in range(TOPK_GROUPS):
        gm = jnp.max(gscore, axis=0, keepdims=True)
        gi = jnp.min(jnp.where(gscore == gm, gid1, n_grp), axis=0, keepdims=True)
        pick = gid1 == gi
        gsel = gsel | pick
        gscore = jnp.where(pick, NEG_INF, gscore)
    masked = jnp.where(gsel, b3, NEG_INF)
    wsel = jnp.zeros(shape3, F32)
    chosen = jnp.zeros(shape3, jnp.bool_)
    for _ in range(TOP_K):
        m = jnp.max(jnp.max(masked, axis=1, keepdims=True), axis=0, keepdims=True)
        idx = jnp.min(jnp.min(jnp.where(masked == m, eid, rows_pad), axis=1, keepdims=True), axis=0, keepdims=True)
        pick = eid == idx
        wsel = jnp.where(pick, s3, wsel)
        chosen = chosen | pick
        masked = jnp.where(pick, NEG_INF, masked)
    wsum = jnp.sum(jnp.sum(wsel, axis=1, keepdims=True), axis=0, keepdims=True)
    wt_ref[...] = (wsel / wsum * ROUTED_SCALE).reshape(rows_pad, tile)
    sel = jnp.where(chosen, 1.0, 0.0).reshape(rows_pad, tile)
    before = lax.broadcasted_iota(I32, (tile, tile), 0) < lax.broadcasted_iota(I32, (tile, tile), 1)
    rank = _dot(sel.astype(BF16), jnp.where(before, 1.0, 0.0).astype(BF16))
    rank_ref[...] = jnp.where(sel > 0.0, rank, -1.0).astype(I32)
    cnt = jnp.sum(sel, axis=1, keepdims=True).astype(I32)
    cnt_ref[0] = jnp.broadcast_to(cnt, (rows_pad, LANES))


def _router(x1, w):
    n = x1.shape[0] // TM
    rows_pad = 2 * N_EXPERTS
    tok = pl.BlockSpec((rows_pad, TM), lambda i: (0, i))
    return pl.pallas_call(
        functools.partial(_router_kernel, tile=TM), grid=(n,),
        in_specs=[pl.BlockSpec((TM, D_MODEL), lambda i: (i, 0)), _const_spec(w["wrh"].shape),
                  _const_spec(w["wrl"].shape), _const_spec(w["br"].shape)],
        out_specs=[tok, tok, pl.BlockSpec((1, rows_pad, LANES), lambda i: (i, 0, 0))],
        out_shape=(jax.ShapeDtypeStruct((rows_pad, x1.shape[0]), F32),
                   jax.ShapeDtypeStruct((rows_pad, x1.shape[0]), I32),
                   jax.ShapeDtypeStruct((n, rows_pad, LANES), I32)),
        compiler_params=_params("arbitrary"), name="router",
    )(x1, w["wrh"], w["wrl"], w["br"])


def _n_row_blocks(n_tokens):
    n_tiles = n_tokens // TM
    worst_rows = n_tokens * TOP_K + n_tiles * N_EXPERTS * (GRANULE - 1) + N_EXPERTS * (ROW_BLOCK - 1)
    return -(-worst_rows // ROW_BLOCK)


def _dispatch_tables(cnt, n_blocks):
    padded = (cnt + (GRANULE - 1)) // GRANULE * GRANULE
    run_end = jnp.cumsum(padded, axis=1)
    run_off = run_end - padded
    n_gran = run_end[:, -1] // GRANULE
    rows_e = jnp.sum(padded, axis=0)
    region = (rows_e + (ROW_BLOCK - 1)) // ROW_BLOCK * ROW_BLOCK
    region_end = jnp.cumsum(region)
    region_start = region_end - region
    run_base = region_start[None, :] + jnp.cumsum(padded, axis=0) - padded
    gran = jnp.arange(L_MAX // GRANULE, dtype=I32)
    owner = jax.vmap(lambda ends: jnp.searchsorted(ends, gran, side="right"))(run_end // GRANULE)
    owner = jnp.minimum(owner, N_EXPERTS - 1).astype(I32)
    gran_dst = (jnp.take_along_axis(run_base, owner, axis=1) // GRANULE + gran[None, :]
                - jnp.take_along_axis(run_off, owner, axis=1) // GRANULE)
    n_used = region_end[-1] // ROW_BLOCK
    block_expert = jnp.minimum(jnp.searchsorted(region_end // ROW_BLOCK, jnp.arange(n_blocks, dtype=I32),
                                                side="right"), N_EXPERTS - 1)
    tail = (region - rows_e) // GRANULE
    tail_end = jnp.cumsum(tail)
    slot = jnp.arange(N_EXPERTS * (ROW_BLOCK // GRANULE - 1), dtype=I32)
    t_owner = jnp.minimum(jnp.searchsorted(tail_end, slot, side="right"), N_EXPERTS - 1)
    tail_dst = ((region_start + rows_e) // GRANULE)[t_owner] + slot - (tail_end - tail)[t_owner]
    as_i32 = lambda a: a.astype(I32).reshape(-1)
    return dict(run_off=as_i32(run_off), run_len=as_i32(padded), n_gran=as_i32(n_gran), gran_dst=as_i32(gran_dst),
                n_used=as_i32(n_used), block_expert=as_i32(block_expert), n_tail=as_i32(tail_end[-1]),
                tail_dst=as_i32(tail_dst))


def _granule_copy(src, src_gran, dst, dst_gran, sem):
    s = pl.multiple_of(src_gran * GRANULE, GRANULE)
    d = pl.multiple_of(dst_gran * GRANULE, GRANULE)
    return pltpu.make_async_copy(src.at[pl.ds(s, GRANULE), :], dst.at[pl.ds(d, GRANULE), :], sem)


def _build_selection(sel_scr, rank, values, run_off_ref, run_len_ref, tile_idx):
    slot = lax.broadcasted_iota(I32, (RUN_CHUNK, TM), 0)
    for e in range(N_EXPERTS):
        off = run_off_ref[tile_idx * N_EXPERTS + e]
        n_chunks = (run_len_ref[tile_idx * N_EXPERTS + e] + (RUN_CHUNK - 1)) // RUN_CHUNK
        rank_e = rank[e:e + 1, :]
        val_e = 1.0 if values is None else values[e:e + 1, :]

        def chunk(c, carry, off=off, rank_e=rank_e, val_e=val_e):
            hit = rank_e == slot + c * RUN_CHUNK
            rows = pl.ds(pl.multiple_of(off + c * RUN_CHUNK, GRANULE), RUN_CHUNK)
            sel_scr[rows, :] = jnp.where(hit, val_e, 0.0)
            return carry

        lax.fori_loop(0, n_chunks, chunk, 0)


def _dispatch_kernel(run_off, run_len, n_gran, gran_dst, n_tail, tail_dst, n_used,
                     x1_ref, rank_ref, xs_hbm, sel_scr, buf_scr, zero_scr, sem):
    t = pl.program_id(0)

    @pl.when(t == 0)
    def _():
        sel_scr[...] = jnp.zeros_like(sel_scr)
        buf_scr[...] = jnp.zeros_like(buf_scr)
        zero_scr[...] = jnp.zeros_like(zero_scr)

    _build_selection(sel_scr, rank_ref[...], None, run_off, run_len, t)
    xb = x1_ref[...].astype(BF16)
    n_g = n_gran[t]
    for c in range(L_MAX // SEL_CHUNK):
        @pl.when(c * SEL_CHUNK < n_g * GRANULE)
        def _(c=c):
            rows = slice(c * SEL_CHUNK, (c + 1) * SEL_CHUNK)
            buf_scr[rows, :] = _pack_halves(_dot(sel_scr[rows, :].astype(BF16), xb))

    def start(g, carry):
        _granule_copy(buf_scr, g, xs_hbm, gran_dst[t * (L_MAX // GRANULE) + g], sem).start()
        return carry

    def wait(g, carry):
        _granule_copy(buf_scr, 0, xs_hbm, 0, sem).wait()
        return carry

    lax.fori_loop(0, n_g, start, 0)
    lax.fori_loop(0, n_g, wait, 0)

    @pl.when(t == pl.num_programs(0) - 1)
    def _():
        def start_tail(i, carry):
            _granule_copy(zero_scr, 0, xs_hbm, tail_dst[i], sem).start()
            return carry

        def wait_tail(i, carry):
            _granule_copy(zero_scr, 0, xs_hbm, 0, sem).wait()
            return carry

        lax.fori_loop(0, n_tail[0], start_tail, 0)
        lax.fori_loop(0, n_tail[0], wait_tail, 0)

        def spare_copy(b):
            rows = pl.ds(pl.multiple_of(b * ROW_BLOCK, ROW_BLOCK), ROW_BLOCK)
            return pltpu.make_async_copy(zero_scr, xs_hbm.at[rows, :], sem)

        def start_spare(b, carry):
            spare_copy(b).start()
            return carry

        def wait_spare(b, carry):
            spare_copy(b).wait()
            return carry

        n_blocks = xs_hbm.shape[0] // ROW_BLOCK
        lax.fori_loop(n_used[0], n_blocks, start_spare, 0)
        lax.fori_loop(n_used[0], n_blocks, wait_spare, 0)


def _dispatch(x1, rank, tables, n_blocks):
    n = x1.shape[0] // TM
    return pl.pallas_call(
        _dispatch_kernel,
        grid_spec=pltpu.PrefetchScalarGridSpec(
            num_scalar_prefetch=7, grid=(n,),
            in_specs=[pl.BlockSpec((TM, D_MODEL), lambda i, *_: (i, 0)),
                      pl.BlockSpec((2 * N_EXPERTS, TM), lambda i, *_: (0, i))],
            out_specs=pl.BlockSpec(memory_space=pl.ANY),
            scratch_shapes=[pltpu.VMEM((L_MAX, TM), F32), pltpu.VMEM((L_MAX, HALF), U32),
                            pltpu.VMEM((ROW_BLOCK, HALF), U32), pltpu.SemaphoreType.DMA(())]),
        out_shape=jax.ShapeDtypeStruct((n_blocks * ROW_BLOCK, HALF), U32),
        compiler_params=_params("arbitrary"), name="dispatch",
    )(tables["run_off"], tables["run_len"], tables["n_gran"], tables["gran_dst"], tables["n_tail"],
      tables["tail_dst"], tables["n_used"], x1, rank)


def _experts_kernel(block_expert, n_used, xs_ref, wg_ref, wu_ref, wd_ref, y_ref, wg_scr, wu_scr, wd_scr):
    b = pl.program_id(0)

    @pl.when(b >= n_used[0])
    def _():
        y_ref[...] = jnp.zeros_like(y_ref)

    @pl.when(b < n_used[0])
    def _():
        fresh =(b == 0) | (block_expert[b] != block_expert[jnp.maximum(b - 1, 0)])

        @pl.when(fresh)
        def _():
            wg_scr[...] = wg_ref[...].astype(BF16)
            wu_scr[...] = wu_ref[...].astype(BF16)
            wd_scr[...] = wd_ref[...].astype(BF16)

        lo, hi = _unpack_halves(xs_ref[...])
        gate = _dot(lo, wg_scr[:HALF, :]) + _dot(hi, wg_scr[HALF:, :])
        up = _dot(lo, wu_scr[:HALF, :]) + _dot(hi, wu_scr[HALF:, :])
        y = _dot((_silu(gate) * up).astype(BF16), wd_scr[...])
        y_ref[...] = _pack_halves(y.astype(BF16).astype(F32))


def _experts(xs, tables, w_e_gate, w_e_up, w_e_down, layer, n_blocks):
    blk = lambda b, be, nu: (jnp.minimum(b, nu[0] - 1), 0)
    wmap = lambda b, be, nu: (layer, be[b], 0, 0)
    return pl.pallas_call(
        _experts_kernel,
        grid_spec=pltpu.PrefetchScalarGridSpec(
            num_scalar_prefetch=2, grid=(n_blocks,),
            in_specs=[pl.BlockSpec((ROW_BLOCK, HALF), blk),
                      pl.BlockSpec((None, None, D_MODEL, D_EXPERT), wmap),
                      pl.BlockSpec((None, None, D_MODEL, D_EXPERT), wmap),
                      pl.BlockSpec((None, None, D_EXPERT, D_MODEL), wmap)],
            out_specs=pl.BlockSpec((ROW_BLOCK, HALF), lambda b, be, nu: (b, 0)),
            scratch_shapes=[pltpu.VMEM((D_MODEL, D_EXPERT), BF16), pltpu.VMEM((D_MODEL, D_EXPERT), BF16),
                            pltpu.VMEM((D_EXPERT, D_MODEL), BF16)]),
        out_shape=jax.ShapeDtypeStruct(xs.shape, U32),
        compiler_params=_params("arbitrary"), name="experts",
    )(tables["block_expert"], tables["n_used"], xs, w_e_gate, w_e_up, w_e_down)


def _combine_kernel(run_off, run_len, n_gran, gran_dst,
                    x1_ref, rank_ref, wt_ref, y_hbm, wsg, wsu, wsd, g2, b2, out_ref,
                    sel_scr, buf_scr, acc_scr, sem):
    t = pl.program_id(0)

    @pl.when(t == 0)
    def _():
        buf_scr[...] = jnp.zeros_like(buf_scr)

    n_g = n_gran[t]

    def start(g, carry):
        _granule_copy(y_hbm, gran_dst[t * (L_MAX // GRANULE) + g], buf_scr, g, sem).start()
        return carry

    def wait(g, carry):
        _granule_copy(y_hbm, 0, buf_scr, 0, sem).wait()
        return carry

    lax.fori_loop(0, n_g, start, 0)
    sel_scr[...] = jnp.zeros_like(sel_scr)
    _build_selection(sel_scr, rank_ref[...], wt_ref[...], run_off, run_len, t)
    x1 = x1_ref[...]
    xb = x1.astype(BF16)
    hs = (_silu(_dot(xb, wsg[...])) * _dot(xb, wsu[...])).astype(BF16)
    acc_scr[...] = _dot(hs, wsd[...])
    lax.fori_loop(0, n_g, wait, 0)
    for c in range(L_MAX // SEL_CHUNK):
        @pl.when(c * SEL_CHUNK < n_g * GRANULE)
        def _(c=c):
            rows = slice(c * SEL_CHUNK, (c + 1) * SEL_CHUNK)
            lo, hi = _unpack_halves(buf_scr[rows, :])
            sel = sel_scr[rows, :].astype(BF16)
            acc_scr[:, :HALF] += _dot_tn(sel, lo)
            acc_scr[:, HALF:] += _dot_tn(sel, hi)

    out_ref[...] = _layer_norm(ALPHA * x1 + acc_scr[...], g2[...], b2[...])


def _combine(x1, rank, wt, y, tables, w):
    n = x1.shape[0] // TM
    small = (w["wsg"], w["wsu"], w["wsd"], w["g2"], w["b2"])
    tok = pl.BlockSpec((2 * N_EXPERTS, TM), lambda i, *_: (0, i))
    return pl.pallas_call(
        _combine_kernel,
        grid_spec=pltpu.PrefetchScalarGridSpec(
            num_scalar_prefetch=4, grid=(n,),
            in_specs=[pl.BlockSpec((TM, D_MODEL), lambda i, *_: (i, 0)), tok, tok,
                      pl.BlockSpec(memory_space=pl.ANY)]
                     + [pl.BlockSpec(a.shape, lambda i, *_, nd=a.ndim: (0,) * nd) for a in small],
            out_specs=pl.BlockSpec((TM, D_MODEL), lambda i, *_: (i, 0)),
            scratch_shapes=[pltpu.VMEM((L_MAX, TM), F32), pltpu.VMEM((L_MAX, HALF), U32),
                            pltpu.VMEM((TM, D_MODEL), F32), pltpu.SemaphoreType.DMA(())]),
        out_shape=jax.ShapeDtypeStruct(x1.shape, F32),
        compiler_params=_params("arbitrary"), name="combine",
    )(tables["run_off"], tables["run_len"], tables["n_gran"], tables["gran_dst"], x1, rank, wt, y, *small)


def _moe(x1, w, w_e_gate, w_e_up, w_e_down, layer):
    n_blocks = _n_row_blocks(x1.shape[0])
    wt, rank, cnt = _router(x1, w)
    tables = _dispatch_tables(cnt[:, :N_EXPERTS, 0], n_blocks)
    xs = _dispatch(x1, rank, tables, n_blocks)
    y = _experts(xs, tables, w_e_gate, w_e_up, w_e_down, layer, n_blocks)
    return _combine(x1, rank, wt, y, tables, w)


def _layer_weights(l, w_in, w_gk2, b_gk, gla_norm_g, w_pool, pool_scale, w_br_a, w_br_b, w_out, ln1_g, ln1_b,
                   w_router, b_router, w_sh_gate, w_sh_up, w_sh_down, ln2_g, ln2_b):
    wi = w_in[l].astype(BF16)
    o = 0
    pieces = {}
    for name, width in (("wq", KEY), ("wk", KEY), ("wv", VAL), ("wr", VAL), ("wgk", GATE_RANK),
                        ("wu", POOL_WIDTH), ("wga", D_MODEL), ("wgb", D_MODEL)):
        pieces[name] = wi[:, o:o + width]
        o += width
    pad_rank = LANES - GATE_RANK
    pieces["wgk"] = jnp.pad(pieces["wgk"], ((0, 0), (0, pad_rank)))
    pieces["wgk2"] = jnp.pad(w_gk2[l].astype(BF16), ((0, pad_rank), (0, 0)))
    pieces["bgk"] = b_gk[l].reshape(1, KEY)
    pieces["gn"] = gla_norm_g[l].reshape(1, DV)
    pieces["wbra"] = w_br_a[l].astype(BF16)
    pieces["wpool"] = w_pool[l].astype(BF16)
    pieces["pscale"] = pool_scale[l].reshape(1, POOL_WIDTH)
    pieces["wbrb"] = w_br_b[l].astype(BF16)
    pieces["wout"] = w_out[l].astype(BF16)
    pieces["g1"] = ln1_g[l].reshape(1, D_MODEL)
    pieces["b1"] = ln1_b[l].reshape(1, D_MODEL)
    wrt = jnp.pad(w_router[l].T, ((0, N_EXPERTS), (0, 0)))
    wrh = wrt.astype(BF16)
    pieces["wrh"] = wrh
    pieces["wrl"] = (wrt - wrh.astype(F32)).astype(BF16)
    pieces["br"] = jnp.pad(b_router[l], (0, N_EXPERTS)).reshape(2 * N_EXPERTS, 1)
    pieces["wsg"] = w_sh_gate[l].astype(BF16)
    pieces["wsu"] = w_sh_up[l].astype(BF16)
    pieces["wsd"] = w_sh_down[l].astype(BF16)
    pieces["g2"] = ln2_g[l].reshape(1, D_MODEL)
    pieces["b2"] = ln2_b[l].reshape(1, D_MODEL)
    return pieces


def kernel(x_prompt, x_sample, state_gla, cache_pool, w_in, w_gk2, b_gk, gla_norm_g, w_pool, pool_scale, w_br_a, w_br_b, w_out, ln1_g, ln1_b, w_router, b_router, w_e_gate, w_e_up, w_e_down, w_sh_gate, w_sh_up, w_sh_down, ln2_g, ln2_b):
    x = jnp.concatenate([x_prompt.reshape(T_PROMPT, D_MODEL), x_sample.reshape(T_SAMPLE, D_MODEL)], axis=0)
    n_prompt_tiles = T_PROMPT // TM
    sample_block0 = T_PROMPT // DEC_SEQ
    sp, hp, ss, hs = [], [], [], []
    for l in range(DEPTH):
        w = _layer_weights(l, w_in, w_gk2, b_gk, gla_norm_g, w_pool, pool_scale, w_br_a, w_br_b, w_out, ln1_g,
                           ln1_b, w_router, b_router, w_sh_gate, w_sh_up, w_sh_down, ln2_g, ln2_b)
        q, k, v, r, lf, u, ga, gb = _inproj(x, w)

        og_p, st_p = _gla(q, k, v, lf, r, w["gn"], None, tile=TM, chunk=CHUNK, n_tiles=n_prompt_tiles, block0=0)
        s0t = jnp.swapaxes(state_gla[l], -1, -2)
        og_s, st_s = _gla(q, k, v, lf, r, w["gn"], s0t, tile=DEC_SEQ, chunk=DEC_SEQ, n_tiles=DEC_BATCH,
                          block0=sample_block0)

        u_p = u[:T_PROMPT].reshape(n_prompt_tiles, TM, POOL_WIDTH)
        halo_p = jnp.concatenate([jnp.zeros((1, HALO, POOL_WIDTH), F32), u_p[:-1, TM - HALO:, :]], axis=0)
        halo_s = jnp.concatenate([jnp.zeros((DEC_BATCH, HALO - POOL_HIST, POOL_WIDTH), F32), cache_pool[l]], axis=1)
        x1_p = _mix(x, og_p, u, halo_p, ga, gb, w, tile=TM, n_tiles=n_prompt_tiles, block0=0, pos0=0, pos_stride=TM)
        x1_s = _mix(x, og_s, u, halo_s, ga, gb, w, tile=DEC_SEQ, n_tiles=DEC_BATCH, block0=sample_block0,
                    pos0=PAST_LEN, pos_stride=0)
        x1 = jnp.concatenate([x1_p, x1_s], axis=0)

        x = _moe(x1, w, w_e_gate, w_e_up, w_e_down, l)

        sp.append(jnp.swapaxes(st_p, -1, -2))
        ss.append(jnp.swapaxes(st_s, -1, -2))
        hp.append(u[T_PROMPT - POOL_HIST:T_PROMPT].reshape(1, POOL_HIST, POOL_WIDTH))
        hs.append(u[T_PROMPT:].reshape(DEC_BATCH, DEC_SEQ, POOL_WIDTH)[:, DEC_SEQ - POOL_HIST:, :])
    y_prompt = x[:T_PROMPT].reshape(1, SEQ, D_MODEL)
    y_sample = x[T_PROMPT:].reshape(DEC_BATCH, DEC_SEQ, D_MODEL)
    return (y_prompt, y_sample, jnp.stack(sp), jnp.stack(hp), jnp.stack(ss), jnp.stack(hs))
```

```python
import functools

import jax
import jax.numpy as jnp
from jax import lax
from jax.experimental import pallas as pl
from jax.experimental.pallas import tpu as pltpu

F32 = jnp.float32
BF16 = jnp.bfloat16
U32 = jnp.uint32
I32 = jnp.int32

D_MODEL = 1024
DEPTH = 2
SEQ = 16384
DEC_BATCH = 8
DEC_SEQ = 32
PAST_LEN = 4096
CHUNK = 64
HEADS = 4
DK = 128
DV = 256
KEY = HEADS * DK
VAL = HEADS * DV
GATE_RANK = 16
GATE_NORMALIZER = 16.0
POOL_WIDTH = 512
POOL_WINDOWS = (2, 4, 8, 16)
POOL_GROUP_DIM = 128
POOL_HIST = 15
N_EXPERTS = 64
GROUP_SIZE = 8
TOPK_GROUPS = 4
TOP_K = 8
D_EXPERT = 256
ROUTED_SCALE = 2.5
ALPHA = (2 * DEPTH) ** 0.25
LN_EPS = 1e-5
RMS_EPS = 1e-6

LANES = 128
SUBLANES = 8
VMEM_LIMIT_BYTES = 56 * 1024 * 1024

T_PROMPT = SEQ
T_SAMPLE = DEC_BATCH * DEC_SEQ
T_ALL = T_PROMPT + T_SAMPLE
TM = 256
HALO = 16
NEG_INF = float("-inf")

HALF = D_MODEL // 2
GRANULE = SUBLANES
ROW_BLOCK = 512
COPY_GROUP = 8
RUN_CHUNK = 64
L_MAX = 2560
SEL_CHUNK = 512
HI_MASK = 0xFFFF0000


def _dot(a, b):
    return jnp.dot(a, b, preferred_element_type=F32)


def _dot_nt(a, b):
    return lax.dot_general(a, b, (((1,), (1,)), ((), ())), preferred_element_type=F32)


def _dot_tn(a, b):
    return lax.dot_general(a, b, (((0,), (0,)), ((), ())), preferred_element_type=F32)


def _sigmoid(x):
    return 1.0 / (1.0 + jnp.exp(-x))


def _silu(x):
    return x * _sigmoid(x)


def _layer_norm(x, g, b):
    mu = jnp.mean(x, axis=-1, keepdims=True)
    xc = x - mu
    var = jnp.mean(xc * xc, axis=-1, keepdims=True)
    return xc * lax.rsqrt(var + LN_EPS) * g + b


def _pack_halves(v):
    bits = lax.bitcast_convert_type(v, U32)
    return (bits[:, HALF:] & jnp.uint32(HI_MASK)) | (bits[:, :HALF] >> 16)


def _unpack_halves(w):
    lo = lax.bitcast_convert_type(w << 16, F32).astype(BF16)
    hi = lax.bitcast_convert_type(w & jnp.uint32(HI_MASK), F32).astype(BF16)
    return lo, hi


def _params(*sem):
    return pltpu.CompilerParams(dimension_semantics=sem, vmem_limit_bytes=VMEM_LIMIT_BYTES)


def _const_spec(shape):
    nd = len(shape)
    return pl.BlockSpec(shape, lambda *_: (0,) * nd)


def _inproj_kernel(x_ref, wq, wk, wv, wr, wgk, wgk2, bgk, wu, wga, wgb,
                   q_o, k_o, v_o, r_o, lf_o, u_o, ga_o, gb_o):
    xb = x_ref[...].astype(BF16)
    q_o[...] = _dot(xb, wq[...]) * (DK ** -0.5)
    k_o[...] = _dot(xb, wk[...])
    v_o[...] = _dot(xb, wv[...]).astype(BF16)
    r_o[...] = _dot(xb, wr[...])
    gk = _dot(xb, wgk[...]).astype(BF16)
    z = _dot(gk, wgk2[...]) + bgk[...]
    log_sig = jnp.minimum(z, 0.0) - jnp.log1p(jnp.exp(-jnp.abs(z)))
    lf_o[...] = log_sig * (1.0 / GATE_NORMALIZER)
    u_o[...] = _dot(xb, wu[...])
    ga_o[...] = _dot(xb, wga[...])
    gb_o[...] = _dot(xb, wgb[...])


def _inproj(x, w):
    n = x.shape[0] // TM
    row = lambda width: pl.BlockSpec((TM, width), lambda i: (i, 0))
    out_shapes = (
        jax.ShapeDtypeStruct((T_ALL, KEY), F32), jax.ShapeDtypeStruct((T_ALL, KEY), F32),
        jax.ShapeDtypeStruct((T_ALL, VAL), BF16), jax.ShapeDtypeStruct((T_ALL, VAL), F32),
        jax.ShapeDtypeStruct((T_ALL, KEY), F32), jax.ShapeDtypeStruct((T_ALL, POOL_WIDTH), F32),
        jax.ShapeDtypeStruct((T_ALL, D_MODEL), F32), jax.ShapeDtypeStruct((T_ALL, D_MODEL), F32))
    weights = (w["wq"], w["wk"], w["wv"], w["wr"], w["wgk"], w["wgk2"], w["bgk"], w["wu"], w["wga"], w["wgb"])
    return pl.pallas_call(
        _inproj_kernel, grid=(n,),
        in_specs=[row(D_MODEL)] + [_const_spec(a.shape) for a in weights],
        out_specs=[row(KEY), row(KEY), row(VAL), row(VAL), row(KEY), row(POOL_WIDTH), row(D_MODEL), row(D_MODEL)],
        out_shape=out_shapes, compiler_params=_params("arbitrary"), name="inproj",
    )(x, *weights)


def _gla_kernel(*refs, chunk, n_chunks, has_init):
    if has_init:
        q_ref, k_ref, v_ref, lf_ref, r_ref, gn_ref, s0_ref, o_ref, st_ref = refs
        st_ref[...] = s0_ref[...]
    else:
        q_ref, k_ref, v_ref, lf_ref, r_ref, gn_ref, o_ref, st_ref = refs

        @pl.when(pl.program_id(0) == 0)
        def _():
            st_ref[...] = jnp.zeros_like(st_ref)

    row = lax.broadcasted_iota(I32, (chunk, KEY), 0)
    ta = lax.broadcasted_iota(I32, (chunk, chunk), 0)
    sa = lax.broadcasted_iota(I32, (chunk, chunk), 1)
    gn = gn_ref[...]

    def chunk_body(c, carry):
        rows = pl.ds(pl.multiple_of(c * chunk, chunk), chunk)
        lf = lf_ref[rows, :]
        q = q_ref[rows, :]
        k = k_ref[rows, :]
        qb = q.astype(BF16)
        kb = k.astype(BF16)
        att = []
        for h in range(HEADS):
            hs = slice(h * DK, (h + 1) * DK)
            att.append(jnp.where(ta == sa, _dot_nt(qb[:, hs], kb[:, hs]), 0.0))
        seg, tot = lf, lf
        half = 1
        while half < chunk:
            qs = (q * jnp.exp(seg)).astype(BF16)
            ks = (k * jnp.exp(tot - seg)).astype(BF16)
            pair = ((ta ^ sa) < 2 * half) & ((ta & half) != 0) & ((sa & half) == 0)
            for h in range(HEADS):
                hs = slice(h * DK, (h + 1) * DK)
                att[h] = att[h] + jnp.where(pair, _dot_nt(qs[:, hs], ks[:, hs]), 0.0)
            upper = (row & half) != 0
            below = pltpu.roll(tot, half, 0)
            above = pltpu.roll(tot, chunk - half, 0)
            seg = seg + jnp.where(upper, below, 0.0)
            tot = tot + jnp.where(upper, below, above)
            half *= 2
        q_in = (q * jnp.exp(seg)).astype(BF16)
        k_out = (k * jnp.exp(tot - seg)).astype(BF16)
        decay = jnp.exp(tot[0:1, :])
        for h in range(HEADS):
            hs = slice(h * DK, (h + 1) * DK)
            vs = slice(h * DV, (h + 1) * DV)
            state = st_ref[0, h]
            vh = v_ref[rows, vs]
            o = _dot_nt(q_in[:, hs], state.astype(BF16)) + _dot(att[h].astype(BF16), vh)
            st_ref[0, h] = state * decay[:, hs] + _dot_tn(vh, k_out[:, hs])
            ms = jnp.mean(o * o, axis=-1, keepdims=True)
            o = o * lax.rsqrt(ms + RMS_EPS) * gn
            o_ref[rows, vs] = (o * _silu(r_ref[rows, vs])).astype(BF16)
        return carry

    lax.fori_loop(0, n_chunks, chunk_body, 0)


def _gla(q, k, v, lf, r, gn, s0t, *, tile, chunk, n_tiles, block0):
    has_init = s0t is not None
    row = lambda width: pl.BlockSpec((tile, width), lambda i: (block0 + i, 0))
    st_spec = pl.BlockSpec((1, HEADS, DV, DK), (lambda i: (i, 0, 0, 0)) if has_init else (lambda i: (0, 0, 0, 0)))
    n_states = n_tiles if has_init else 1
    in_specs = [row(KEY), row(KEY), row(VAL), row(KEY), row(VAL), _const_spec(gn.shape)]
    args = [q, k, v, lf, r, gn]
    if has_init:
        in_specs.append(st_spec)
        args.append(s0t)
    return pl.pallas_call(
        functools.partial(_gla_kernel, chunk=chunk, n_chunks=tile // chunk, has_init=has_init),
        grid=(n_tiles,), in_specs=in_specs,
        out_specs=[pl.BlockSpec((tile, VAL), lambda i: (i, 0)), st_spec],
        out_shape=(jax.ShapeDtypeStruct((n_tiles * tile, VAL), BF16),
                   jax.ShapeDtypeStruct((n_states, HEADS, DV, DK), F32)),
        compiler_params=_params("arbitrary"), name="gla_init" if has_init else "gla",
    )(*args)


def _mix_kernel(x_ref, og_ref, u_ref, halo_ref, ga_ref, gb_ref, wbra, wpool, pscale, wbrb, wout, g1, b1,
                x1_ref, *, tile, pos0, pos_stride):
    ya = _dot(og_ref[...], wbra[...])
    u = u_ref[...]
    ext = jnp.concatenate([halo_ref[0], u], axis=0)
    pos = pos0 + pl.program_id(0) * pos_stride + lax.broadcasted_iota(I32, (tile, 1), 0)
    parts = []
    for g, window in enumerate(POOL_WINDOWS):
        cols = slice(g * POOL_GROUP_DIM, (g + 1) * POOL_GROUP_DIM)
        win = ext[:, cols]
        shift = 1
        while shift < window:
            win = win + pltpu.roll(win, shift, 0)
            shift *= 2
        cnt = jnp.minimum(window, pos + 1).astype(F32)
        d = win[HALO:, :] / cnt - u[:, cols]
        parts.append(_dot(d.astype(BF16), wpool[g]))
    yb_in = jnp.concatenate(parts, axis=1) * pscale[...]
    yb = _dot(yb_in.astype(BF16), wbrb[...])
    mixed = _sigmoid(ga_ref[...]) * ya + _sigmoid(gb_ref[...]) * yb
    mix = _dot(mixed.astype(BF16), wout[...])
    x1_ref[...] = _layer_norm(ALPHA * x_ref[...] + mix, g1[...], b1[...])


def _mix(x, og, u, halo, ga, gb, w, *, tile, n_tiles, block0, pos0, pos_stride):
    row = lambda width: pl.BlockSpec((tile, width), lambda i: (block0 + i, 0))
    weights = (w["wbra"], w["wpool"], w["pscale"], w["wbrb"], w["wout"], w["g1"], w["b1"])
    return pl.pallas_call(
        functools.partial(_mix_kernel, tile=tile, pos0=pos0, pos_stride=pos_stride),
        grid=(n_tiles,),
        in_specs=[row(D_MODEL), pl.BlockSpec((tile, VAL), lambda i: (i, 0)), row(POOL_WIDTH),
                  pl.BlockSpec((1, HALO, POOL_WIDTH), lambda i: (i, 0, 0)), row(D_MODEL), row(D_MODEL)]
                 + [_const_spec(a.shape) for a in weights],
        out_specs=pl.BlockSpec((tile, D_MODEL), lambda i: (i, 0)),
        out_shape=jax.ShapeDtypeStruct((n_tiles * tile, D_MODEL), F32),
        compiler_params=_params("arbitrary"), name="mix",
    )(x, og, u, halo, ga, gb, *weights)


def _router_kernel(x1_ref, wrh_ref, wrl_ref, br_ref, wt_ref, rank_ref, cnt_ref, *, tile):
    x = x1_ref[...]
    xh = x.astype(BF16)
    xl = (x - xh.astype(F32)).astype(BF16)
    logits = _dot_nt(wrh_ref[...], xh) + _dot_nt(wrl_ref[...], xh) + _dot_nt(wrh_ref[...], xl)
    scores = _sigmoid(logits)
    rows_pad = 2 * N_EXPERTS
    n_grp = rows_pad // GROUP_SIZE
    erow = lax.broadcasted_iota(I32, (rows_pad, tile), 0)
    biased = jnp.where(erow < N_EXPERTS, scores + br_ref[...], NEG_INF)
    shape3 = (n_grp, GROUP_SIZE, tile)
    b3 = biased.reshape(shape3)
    s3 = scores.reshape(shape3)
    sub = lax.broadcasted_iota(I32, shape3, 1)
    gid = lax.broadcasted_iota(I32, shape3, 0)
    eid = gid * GROUP_SIZE + sub
    m1 = jnp.max(b3, axis=1, keepdims=True)
    i1 = jnp.min(jnp.where(b3 == m1, sub, GROUP_SIZE), axis=1, keepdims=True)
    m2 = jnp.max(jnp.where(sub == i1, NEG_INF, b3), axis=1, keepdims=True)
    gscore = m1 + m2
    gid1 = lax.broadcasted_iota(I32, (n_grp, 1, tile), 0)
    gsel = jnp.zeros((n_grp, 1, tile), jnp.bool_)
    for _ in range(TOPK_GROUPS):
        gm = jnp.max(gscore, axis=0, keepdims=True)
        gi = jnp.min(jnp.where(gscore == gm, gid1, n_grp), axis=0, keepdims=True)
        pick = gid1 == gi
        gsel = gsel | pick
        gscore = jnp.where(pick, NEG_INF, gscore)
    masked = jnp.where(gsel, b3, NEG_INF)
    wsel = jnp.zeros(shape3, F32)
    chosen = jnp.zeros(shape3, jnp.bool_)
    for _ in range(TOP_K):
        m = jnp.max(jnp.max(masked, axis=1, keepdims=True), axis=0, keepdims=True)
        idx = jnp.min(jnp.min(jnp.where(masked == m, eid, rows_pad), axis=1, keepdims=True), axis=0, keepdims=True)
        pick = eid == idx
        wsel = jnp.where(pick, s3, wsel)
        chosen = chosen | pick
        masked = jnp.where(pick, NEG_INF, masked)
    wsum = jnp.sum(jnp.sum(wsel, axis=1, keepdims=True), axis=0, keepdims=True)
    wt_ref[...] = (wsel / wsum * ROUTED_SCALE).reshape(rows_pad, tile)
    sel = jnp.where(chosen, 1.0, 0.0).reshape(rows_pad, tile)
    before = lax.broadcasted_iota(I32, (tile, tile), 0) < lax.broadcasted_iota(I32, (tile, tile), 1)
    rank = _dot(sel.astype(BF16), jnp.where(before, 1.0, 0.0).astype(BF16))
    rank_ref[...] = jnp.where(sel > 0.0, rank, -1.0).astype(I32)
    cnt = jnp.sum(sel, axis=1, keepdims=True).astype(I32)
    cnt_ref[0] = jnp.broadcast_to(cnt, (rows_pad, LANES))


def _router(x1, w):
    n = x1.shape[0] // TM
    rows_pad = 2 * N_EXPERTS
    tok = pl.BlockSpec((rows_pad, TM), lambda i: (0, i))
    return pl.pallas_call(
        functools.partial(_router_kernel, tile=TM), grid=(n,),
        in_specs=[pl.BlockSpec((TM, D_MODEL), lambda i: (i, 0)), _const_spec(w["wrh"].shape),
                  _const_spec(w["wrl"].shape), _const_spec(w["br"].shape)],
        out_specs=[tok, tok, pl.BlockSpec((1, rows_pad, LANES), lambda i: (i, 0, 0))],
        out_shape=(jax.ShapeDtypeStruct((rows_pad, x1.shape[0]), F32),
                   jax.ShapeDtypeStruct((rows_pad, x1.shape[0]), I32),
                   jax.ShapeDtypeStruct((n, rows_pad, LANES), I32)),
        compiler_params=_params("arbitrary"), name="router",
    )(x1, w["wrh"], w["wrl"], w["br"])


def _n_row_blocks(n_tokens):
    n_tiles = n_tokens // TM
    worst_rows = n_tokens * TOP_K + n_tiles * N_EXPERTS * (GRANULE - 1) + N_EXPERTS * (ROW_BLOCK - 1)
    return -(-worst_rows // ROW_BLOCK)


def _dispatch_tables(cnt, n_blocks):
    padded = (cnt + (GRANULE - 1)) // GRANULE * GRANULE
    run_end = jnp.cumsum(padded, axis=1)
    run_off = run_end - padded
    n_gran = run_end[:, -1] // GRANULE
    rows_e = jnp.sum(padded, axis=0)
    region = (rows_e + (ROW_BLOCK - 1)) // ROW_BLOCK * ROW_BLOCK
    region_end = jnp.cumsum(region)
    region_start = region_end - region
    run_base = region_start[None, :] + jnp.cumsum(padded, axis=0) - padded
    gran = jnp.arange(L_MAX // GRANULE, dtype=I32)
    shift = (run_base - run_off) // GRANULE
    step = shift - jnp.concatenate([jnp.zeros_like(shift[:, :1]), shift[:, :-1]], axis=1)
    run_first = run_off // GRANULE
    owned = gran[None, :, None] >= run_first[:, None, :]
    gran_dst = gran[None, :] + jnp.sum(jnp.where(owned, step[:, None, :], 0), axis=2)
    n_used = region_end[-1] // ROW_BLOCK
    blocks = jnp.arange(n_blocks, dtype=I32)
    block_expert = jnp.minimum(jnp.sum(blocks[:, None] >= (region_end // ROW_BLOCK)[None, :], axis=1), N_EXPERTS - 1)
    tail = (region - rows_e) // GRANULE
    tail_end = jnp.cumsum(tail)
    tail_first = tail_end - tail
    slot = jnp.arange(N_EXPERTS * (ROW_BLOCK // GRANULE - 1), dtype=I32)
    t_shift = (region_start + rows_e) // GRANULE - tail_first
    t_step = t_shift - jnp.concatenate([jnp.zeros_like(t_shift[:1]), t_shift[:-1]])
    tail_dst = slot + jnp.sum(jnp.where(slot[:, None] >= tail_first[None, :], t_step[None, :], 0), axis=1)
    as_i32 = lambda a: a.astype(I32).reshape(-1)
    return dict(run_off=as_i32(run_off), run_len=as_i32(padded), n_gran=as_i32(n_gran), gran_dst=as_i32(gran_dst),
                n_used=as_i32(n_used), block_expert=as_i32(block_expert), n_tail=as_i32(tail_end[-1]),
                tail_dst=as_i32(tail_dst))


def _granule_copy(src, src_gran, dst, dst_gran, sem, n=1):
    s = pl.multiple_of(src_gran * GRANULE, GRANULE)
    d = pl.multiple_of(dst_gran * GRANULE, GRANULE)
    return pltpu.make_async_copy(src.at[pl.ds(s, n * GRANULE), :], dst.at[pl.ds(d, n * GRANULE), :], sem)


def _start_granule_copies(n_gran, make_copy):
    n_groups = n_gran // COPY_GROUP

    def group(j, carry):
        for i in range(COPY_GROUP):
            make_copy(j * COPY_GROUP + i).start()
        return carry

    def single(g, carry):
        make_copy(g).start()
        return carry

    lax.fori_loop(0, n_groups, group, 0)
    lax.fori_loop(n_groups * COPY_GROUP, n_gran, single, 0)


def _wait_granule_copies(n_gran, src, dst, sem):
    n_groups = n_gran // COPY_GROUP

    def group(j, carry):
        _granule_copy(src, 0, dst, 0, sem, COPY_GROUP).wait()
        return carry

    def single(g, carry):
        _granule_copy(src, 0, dst, 0, sem).wait()
        return carry

    lax.fori_loop(0, n_groups, group, 0)
    lax.fori_loop(n_groups * COPY_GROUP, n_gran, single, 0)


def _build_selection(sel_scr, rank, values, run_off_ref, run_len_ref, tile_idx):
    slot = lax.broadcasted_iota(I32, (RUN_CHUNK, TM), 0)
    for e in range(N_EXPERTS):
        off = run_off_ref[tile_idx * N_EXPERTS + e]
        n_chunks = (run_len_ref[tile_idx * N_EXPERTS + e] + (RUN_CHUNK - 1)) // RUN_CHUNK
        rank_e = rank[e:e + 1, :]
        val_e = 1.0 if values is None else values[e:e + 1, :]

        def chunk(c, carry, off=off, rank_e=rank_e, val_e=val_e):
            hit = rank_e == slot + c * RUN_CHUNK
            rows = pl.ds(pl.multiple_of(off + c * RUN_CHUNK, GRANULE), RUN_CHUNK)
            sel_scr[rows, :] = jnp.where(hit, val_e, 0.0)
            return carry

        chunk(0, 0)
        lax.fori_loop(1, n_chunks, chunk, 0)


def _dispatch_kernel(run_off, run_len, n_gran, gran_dst, n_tail, tail_dst, n_used,
                     x1_ref, rank_ref, xs_hbm, sel_scr, buf_scr, zero_scr, sem):
    t = pl.program_id(0)

    @pl.when(t == 0)
    def _():
        sel_scr[...] = jnp.zeros_like(sel_scr)
        buf_scr[...] = jnp.zeros_like(buf_scr)
        zero_scr[...] = jnp.zeros_like(zero_scr)

    _build_selection(sel_scr, rank_ref[...], None, run_off, run_len, t)
    xb = x1_ref[...].astype(BF16)
    n_g = n_gran[t]
    buf = buf_scr.at[t % 2]
    for c in range(L_MAX // SEL_CHUNK):
        @pl.when(c * SEL_CHUNK < n_g * GRANULE)
        def _(c=c):
            rows = slice(c * SEL_CHUNK, (c + 1) * SEL_CHUNK)
            buf[rows, :] = _pack_halves(_dot(sel_scr[rows, :].astype(BF16), xb))

    @pl.when(t > 0)
    def _():
        _wait_granule_copies(n_gran[jnp.maximum(t - 1, 0)], buf, xs_hbm, sem)

    _start_granule_copies(
        n_g, lambda g: _granule_copy(buf, g, xs_hbm, gran_dst[t * (L_MAX // GRANULE) + g], sem))

    @pl.when(t == pl.num_programs(0) - 1)
    def _():
        _wait_granule_copies(n_g, buf, xs_hbm, sem)
        _start_granule_copies(n_tail[0], lambda i: _granule_copy(zero_scr, 0, xs_hbm, tail_dst[i], sem))
        _wait_granule_copies(n_tail[0], zero_scr, xs_hbm, sem)

        def spare_copy(b):
            rows = pl.ds(pl.multiple_of(b * ROW_BLOCK, ROW_BLOCK), ROW_BLOCK)
            return pltpu.make_async_copy(zero_scr, xs_hbm.at[rows, :], sem)

        def start_spare(b, carry):
            spare_copy(b).start()
            return carry

        def wait_spare(b, carry):
            spare_copy(b).wait()
            return carry

        n_blocks = xs_hbm.shape[0] // ROW_BLOCK
        lax.fori_loop(n_used[0], n_blocks, start_spare, 0)
        lax.fori_loop(n_used[0], n_blocks, wait_spare, 0)


def _dispatch(x1, rank, tables, n_blocks):
    n = x1.shape[0] // TM
    return pl.pallas_call(
        _dispatch_kernel,
        grid_spec=pltpu.PrefetchScalarGridSpec(
            num_scalar_prefetch=7, grid=(n,),
            in_specs=[pl.BlockSpec((TM, D_MODEL), lambda i, *_: (i, 0)),
                      pl.BlockSpec((2 * N_EXPERTS, TM), lambda i, *_: (0, i))],
            out_specs=pl.BlockSpec(memory_space=pl.ANY),
            scratch_shapes=[pltpu.VMEM((L_MAX, TM), F32), pltpu.VMEM((2, L_MAX, HALF), U32),
                            pltpu.VMEM((ROW_BLOCK, HALF), U32), pltpu.SemaphoreType.DMA(())]),
        out_shape=jax.ShapeDtypeStruct((n_blocks * ROW_BLOCK, HALF), U32),
        compiler_params=_params("arbitrary"), name="dispatch",
    )(tables["run_off"], tables["run_len"], tables["n_gran"], tables["gran_dst"], tables["n_tail"],
      tables["tail_dst"], tables["n_used"], x1, rank)


def _experts_kernel(block_expert, n_used, xs_ref, wg_ref, wu_ref, wd_ref, y_ref, wg_scr, wu_scr, wd_scr):
    b = pl.program_id(0)

    @pl.when(b >= n_used[0])
    def _():
        y_ref[...] = jnp.zeros_like(y_ref)

    @pl.when(b < n_used[0])
    def _():
        fresh =(b == 0) | (block_expert[b] != block_expert[jnp.maximum(b - 1, 0)])

        @pl.when(fresh)
        def _():
            wg_scr[...] = wg_ref[...].astype(BF16)
            wu_scr[...] = wu_ref[...].astype(BF16)
            wd_scr[...] = wd_ref[...].astype(BF16)

        lo, hi = _unpack_halves(xs_ref[...])
        gate = _dot(lo, wg_scr[:HALF, :]) + _dot(hi, wg_scr[HALF:, :])
        up = _dot(lo, wu_scr[:HALF, :]) + _dot(hi, wu_scr[HALF:, :])
        y = _dot((_silu(gate) * up).astype(BF16), wd_scr[...])
        y_ref[...] = _pack_halves(y.astype(BF16).astype(F32))


def _experts(xs, tables, w_e_gate, w_e_up, w_e_down, layer, n_blocks):
    blk = lambda b, be, nu: (jnp.minimum(b, nu[0] - 1), 0)
    wmap = lambda b, be, nu: (layer, be[b], 0, 0)
    return pl.pallas_call(
        _experts_kernel,
        grid_spec=pltpu.PrefetchScalarGridSpec(
            num_scalar_prefetch=2, grid=(n_blocks,),
            in_specs=[pl.BlockSpec((ROW_BLOCK, HALF), blk),
                      pl.BlockSpec((None, None, D_MODEL, D_EXPERT), wmap),
                      pl.BlockSpec((None, None, D_MODEL, D_EXPERT), wmap),
                      pl.BlockSpec((None, None, D_EXPERT, D_MODEL), wmap)],
            out_specs=pl.BlockSpec((ROW_BLOCK, HALF), lambda b, be, nu: (b, 0)),
            scratch_shapes=[pltpu.VMEM((D_MODEL, D_EXPERT), BF16), pltpu.VMEM((D_MODEL, D_EXPERT), BF16),
                            pltpu.VMEM((D_EXPERT, D_MODEL), BF16)]),
        out_shape=jax.ShapeDtypeStruct(xs.shape, U32),
        compiler_params=_params("arbitrary"), name="experts",
    )(tables["block_expert"], tables["n_used"], xs, w_e_gate, w_e_up, w_e_down)


def _combine_kernel(run_off, run_len, n_gran, gran_dst,
                    x1_ref, rank_ref, wt_ref, y_hbm, wsg, wsu, wsd, g2, b2, out_ref,
                    sel_scr, buf_scr, acc_scr, sem):
    t = pl.program_id(0)
    n_tiles = pl.num_programs(0)
    n_g = n_gran[t]

    def fetch(tile):
        dst = buf_scr.at[tile % 2]
        _start_granule_copies(
            n_gran[tile],
            lambda g: _granule_copy(y_hbm, gran_dst[tile * (L_MAX // GRANULE) + g], dst, g, sem))

    @pl.when(t == 0)
    def _():
        buf_scr[...] = jnp.zeros_like(buf_scr)
        fetch(t)

    sel_scr[...] = jnp.zeros_like(sel_scr)
    _build_selection(sel_scr, rank_ref[...], wt_ref[...], run_off, run_len, t)
    x1 = x1_ref[...]
    xb = x1.astype(BF16)
    hs = (_silu(_dot(xb, wsg[...])) * _dot(xb, wsu[...])).astype(BF16)
    acc_scr[...] = _dot(hs, wsd[...])
    buf = buf_scr.at[t % 2]
    _wait_granule_copies(n_g, y_hbm, buf, sem)

    @pl.when(t + 1 < n_tiles)
    def _():
        fetch(jnp.minimum(t + 1, n_tiles - 1))

    for c in range(L_MAX // SEL_CHUNK):
        @pl.when(c * SEL_CHUNK < n_g * GRANULE)
        def _(c=c):
            rows = slice(c * SEL_CHUNK, (c + 1) * SEL_CHUNK)
            lo, hi = _unpack_halves(buf[rows, :])
            sel = sel_scr[rows, :].astype(BF16)
            acc_scr[:, :HALF] += _dot_tn(sel, lo)
            acc_scr[:, HALF:] += _dot_tn(sel, hi)

    out_ref[...] = _layer_norm(ALPHA * x1 + acc_scr[...], g2[...], b2[...])


def _combine(x1, rank, wt, y, tables, w):
    n = x1.shape[0] // TM
    small = (w["wsg"], w["wsu"], w["wsd"], w["g2"], w["b2"])
    tok = pl.BlockSpec((2 * N_EXPERTS, TM), lambda i, *_: (0, i))
    return pl.pallas_call(
        _combine_kernel,
        grid_spec=pltpu.PrefetchScalarGridSpec(
            num_scalar_prefetch=4, grid=(n,),
            in_specs=[pl.BlockSpec((TM, D_MODEL), lambda i, *_: (i, 0)), tok, tok,
                      pl.BlockSpec(memory_space=pl.ANY)]
                     + [pl.BlockSpec(a.shape, lambda i, *_, nd=a.ndim: (0,) * nd) for a in small],
            out_specs=pl.BlockSpec((TM, D_MODEL), lambda i, *_: (i, 0)),
            scratch_shapes=[pltpu.VMEM((L_MAX, TM), F32), pltpu.VMEM((2, L_MAX, HALF), U32),
                            pltpu.VMEM((TM, D_MODEL), F32), pltpu.SemaphoreType.DMA(())]),
        out_shape=jax.ShapeDtypeStruct(x1.shape, F32),
        compiler_params=_params("arbitrary"), name="combine",
    )(tables["run_off"], tables["run_len"], tables["n_gran"], tables["gran_dst"], x1, rank, wt, y, *small)


def _moe(x1, w, w_e_gate, w_e_up, w_e_down, layer):
    n_blocks = _n_row_blocks(x1.shape[0])
    wt, rank, cnt = _router(x1, w)
    tables = _dispatch_tables(cnt[:, :N_EXPERTS, 0], n_blocks)
    xs = _dispatch(x1, rank, tables, n_blocks)
    y = _experts(xs, tables, w_e_gate, w_e_up, w_e_down, layer, n_blocks)
    return _combine(x1, rank, wt, y, tables, w)


def _layer_weights(l, w_in, w_gk2, b_gk, gla_norm_g, w_pool, pool_scale, w_br_a, w_br_b, w_out, ln1_g, ln1_b,
                   w_router, b_router, w_sh_gate, w_sh_up, w_sh_down, ln2_g, ln2_b):
    wi = w_in[l].astype(BF16)
    o = 0
    pieces = {}
    for name, width in (("wq", KEY), ("wk", KEY), ("wv", VAL), ("wr", VAL), ("wgk", GATE_RANK),
                        ("wu", POOL_WIDTH), ("wga", D_MODEL), ("wgb", D_MODEL)):
        pieces[name] = wi[:, o:o + width]
        o += width
    pad_rank = LANES - GATE_RANK
    pieces["wgk"] = jnp.pad(pieces["wgk"], ((0, 0), (0, pad_rank)))
    pieces["wgk2"] = jnp.pad(w_gk2[l].astype(BF16), ((0, pad_rank), (0, 0)))
    pieces["bgk"] = b_gk[l].reshape(1, KEY)
    pieces["gn"] = gla_norm_g[l].reshape(1, DV)
    pieces["wbra"] = w_br_a[l].astype(BF16)
    pieces["wpool"] = w_pool[l].astype(BF16)
    pieces["pscale"] = pool_scale[l].reshape(1, POOL_WIDTH)
    pieces["wbrb"] = w_br_b[l].astype(BF16)
    pieces["wout"] = w_out[l].astype(BF16)
    pieces["g1"] = ln1_g[l].reshape(1, D_MODEL)
    pieces["b1"] = ln1_b[l].reshape(1, D_MODEL)
    wrt = jnp.pad(w_router[l].T, ((0, N_EXPERTS), (0, 0)))
    wrh = wrt.astype(BF16)
    pieces["wrh"] = wrh
    pieces["wrl"] = (wrt - wrh.astype(F32)).astype(BF16)
    pieces["br"] = jnp.pad(b_router[l], (0, N_EXPERTS)).reshape(2 * N_EXPERTS, 1)
    pieces["wsg"] = w_sh_gate[l].astype(BF16)
    pieces["wsu"] = w_sh_up[l].astype(BF16)
    pieces["wsd"] = w_sh_down[l].astype(BF16)
    pieces["g2"] = ln2_g[l].reshape(1, D_MODEL)
    pieces["b2"] = ln2_b[l].reshape(1, D_MODEL)
    return pieces


def kernel(x_prompt, x_sample, state_gla, cache_pool, w_in, w_gk2, b_gk, gla_norm_g, w_pool, pool_scale, w_br_a, w_br_b, w_out, ln1_g, ln1_b, w_router, b_router, w_e_gate, w_e_up, w_e_down, w_sh_gate, w_sh_up, w_sh_down, ln2_g, ln2_b):
    x = jnp.concatenate([x_prompt.reshape(T_PROMPT, D_MODEL), x_sample.reshape(T_SAMPLE, D_MODEL)], axis=0)
    n_prompt_tiles = T_PROMPT // TM
    sample_block0 = T_PROMPT // DEC_SEQ
    sp, hp, ss, hs = [], [], [], []
    for l in range(DEPTH):
        w = _layer_weights(l, w_in, w_gk2, b_gk, gla_norm_g, w_pool, pool_scale, w_br_a, w_br_b, w_out, ln1_g,
                           ln1_b, w_router, b_router, w_sh_gate, w_sh_up, w_sh_down, ln2_g, ln2_b)
        q, k, v, r, lf, u, ga, gb = _inproj(x, w)

        og_p, st_p = _gla(q, k, v, lf, r, w["gn"], None, tile=TM, chunk=CHUNK, n_tiles=n_prompt_tiles, block0=0)
        s0t = jnp.swapaxes(state_gla[l], -1, -2)
        og_s, st_s = _gla(q, k, v, lf, r, w["gn"], s0t, tile=DEC_SEQ, chunk=DEC_SEQ, n_tiles=DEC_BATCH,
                          block0=sample_block0)

        u_p = u[:T_PROMPT].reshape(n_prompt_tiles, TM, POOL_WIDTH)
        halo_p = jnp.concatenate([jnp.zeros((1, HALO, POOL_WIDTH), F32), u_p[:-1, TM - HALO:, :]], axis=0)
        halo_s = jnp.concatenate([jnp.zeros((DEC_BATCH, HALO - POOL_HIST, POOL_WIDTH), F32), cache_pool[l]], axis=1)
        x1_p = _mix(x, og_p, u, halo_p, ga, gb, w, tile=TM, n_tiles=n_prompt_tiles, block0=0, pos0=0, pos_stride=TM)
        x1_s = _mix(x, og_s, u, halo_s, ga, gb, w, tile=DEC_SEQ, n_tiles=DEC_BATCH, block0=sample_block0,
                    pos0=PAST_LEN, pos_stride=0)
        x1 = jnp.concatenate([x1_p, x1_s], axis=0)

        x = _moe(x1, w, w_e_gate, w_e_up, w_e_down, l)

        sp.append(jnp.swapaxes(st_p, -1, -2))
        ss.append(jnp.swapaxes(st_s, -1, -2))
        hp.append(u[T_PROMPT - POOL_HIST:T_PROMPT].reshape(1, POOL_HIST, POOL_WIDTH))
        hs.append(u[T_PROMPT:].reshape(DEC_BATCH, DEC_SEQ, POOL_WIDTH)[:, DEC_SEQ - POOL_HIST:, :])
    y_prompt = x[:T_PROMPT].reshape(1, SEQ, D_MODEL)
    y_sample = x[T_PROMPT:].reshape(DEC_BATCH, DEC_SEQ, D_MODEL)
    return (y_prompt, y_sample, jnp.stack(sp), jnp.stack(hp), jnp.stack(ss), jnp.stack(hs))
```

```python
import functools

import jax
import jax.numpy as jnp
from jax import lax
from jax.experimental import pallas as pl
from jax.experimental.pallas import tpu as pltpu

F32 = jnp.float32
BF16 = jnp.bfloat16
U32 = jnp.uint32
I32 = jnp.int32

D_MODEL = 1024
DEPTH = 2
SEQ = 16384
DEC_BATCH = 8
DEC_SEQ = 32
PAST_LEN = 4096
CHUNK = 64
HEADS = 4
DK = 128
DV = 256
KEY = HEADS * DK
VAL = HEADS * DV
GATE_RANK = 16
GATE_NORMALIZER = 16.0
POOL_WIDTH = 512
POOL_WINDOWS = (2, 4, 8, 16)
POOL_GROUP_DIM = 128
POOL_HIST = 15
N_EXPERTS = 64
GROUP_SIZE = 8
TOPK_GROUPS = 4
TOP_K = 8
D_EXPERT = 256
ROUTED_SCALE = 2.5
ALPHA = (2 * DEPTH) ** 0.25
LN_EPS = 1e-5
RMS_EPS = 1e-6

LANES = 128
SUBLANES = 8
VMEM_LIMIT_BYTES = 56 * 1024 * 1024

T_PROMPT = SEQ
T_SAMPLE = DEC_BATCH * DEC_SEQ
T_ALL = T_PROMPT + T_SAMPLE
TM = 256
HALO = 16
NEG_INF = float("-inf")

HALF = D_MODEL // 2
GRANULE = SUBLANES
ROW_BLOCK = 512
COPY_GROUP = 8
RING = 3
RUN_CHUNK = 64
L_MAX = 2560
SEL_CHUNK = 512
HI_MASK = 0xFFFF0000


def _dot(a, b):
    return jnp.dot(a, b, preferred_element_type=F32)


def _dot_nt(a, b):
    return lax.dot_general(a, b, (((1,), (1,)), ((), ())), preferred_element_type=F32)


def _dot_tn(a, b):
    return lax.dot_general(a, b, (((0,), (0,)), ((), ())), preferred_element_type=F32)


def _sigmoid(x):
    return 1.0 / (1.0 + jnp.exp(-x))


def _silu(x):
    return x * _sigmoid(x)


def _layer_norm(x, g, b):
    mu = jnp.mean(x, axis=-1, keepdims=True)
    xc = x - mu
    var = jnp.mean(xc * xc, axis=-1, keepdims=True)
    return xc * lax.rsqrt(var + LN_EPS) * g + b


def _pack_halves(v):
    bits = lax.bitcast_convert_type(v, U32)
    return (bits[:, HALF:] & jnp.uint32(HI_MASK)) | (bits[:, :HALF] >> 16)


def _unpack_halves(w):
    lo = lax.bitcast_convert_type(w << 16, F32).astype(BF16)
    hi = lax.bitcast_convert_type(w & jnp.uint32(HI_MASK), F32).astype(BF16)
    return lo, hi


def _params(*sem):
    return pltpu.CompilerParams(dimension_semantics=sem, vmem_limit_bytes=VMEM_LIMIT_BYTES)


def _const_spec(shape):
    nd = len(shape)
    return pl.BlockSpec(shape, lambda *_: (0,) * nd)


def _is_tail_tile():
    return pl.program_id(0) == pl.num_programs(0) - 1


def _main_tail_specs(n_tiles, width):
    return [pl.BlockSpec((TM, width), lambda i, *_: (jnp.minimum(i, n_tiles - 2), 0)),
            pl.BlockSpec((TM, width), lambda i, *_: (0, 0))]


def _inproj_kernel(xm_ref, xt_ref, wq, wk, wv, wr, wgk, wgk2, bgk, wu, wga, wgb,
                   q_o, k_o, v_o, r_o, lf_o, u_o, ga_o, gb_o):
    xb = jnp.where(_is_tail_tile(), xt_ref[...], xm_ref[...]).astype(BF16)
    q_o[...] = _dot(xb, wq[...]) * (DK ** -0.5)
    k_o[...] = _dot(xb, wk[...])
    v_o[...] = _dot(xb, wv[...]).astype(BF16)
    r_o[...] = _dot(xb, wr[...])
    gk = _dot(xb, wgk[...]).astype(BF16)
    z = _dot(gk, wgk2[...]) + bgk[...]
    log_sig = jnp.minimum(z, 0.0) - jnp.log1p(jnp.exp(-jnp.abs(z)))
    lf_o[...] = log_sig * (1.0 / GATE_NORMALIZER)
    u_o[...] = _dot(xb, wu[...])
    ga_o[...] = _dot(xb, wga[...])
    gb_o[...] = _dot(xb, wgb[...])


def _inproj(xm, xt, w):
    n = xm.shape[0] // TM + 1
    rows = n * TM
    row = lambda width: pl.BlockSpec((TM, width), lambda i: (i, 0))
    out_shapes = (
        jax.ShapeDtypeStruct((rows, KEY), F32), jax.ShapeDtypeStruct((rows, KEY), F32),
        jax.ShapeDtypeStruct((rows, VAL), BF16), jax.ShapeDtypeStruct((rows, VAL), F32),
        jax.ShapeDtypeStruct((rows, KEY), F32), jax.ShapeDtypeStruct((rows, POOL_WIDTH), F32),
        jax.ShapeDtypeStruct((rows, D_MODEL), F32), jax.ShapeDtypeStruct((rows, D_MODEL), F32))
    weights = (w["wq"], w["wk"], w["wv"], w["wr"], w["wgk"], w["wgk2"], w["bgk"], w["wu"], w["wga"], w["wgb"])
    return pl.pallas_call(
        _inproj_kernel, grid=(n,),
        in_specs=_main_tail_specs(n, D_MODEL) + [_const_spec(a.shape) for a in weights],
        out_specs=[row(KEY), row(KEY), row(VAL), row(VAL), row(KEY), row(POOL_WIDTH), row(D_MODEL), row(D_MODEL)],
        out_shape=out_shapes, compiler_params=_params("arbitrary"), name="inproj",
    )(xm, xt, *weights)


def _gla_kernel(*refs, chunk, n_chunks, has_init):
    if has_init:
        q_ref, k_ref, v_ref, lf_ref, r_ref, gn_ref, s0_ref, o_ref, st_ref = refs
        st_ref[...] = s0_ref[...]
    else:
        q_ref, k_ref, v_ref, lf_ref, r_ref, gn_ref, o_ref, st_ref = refs

        @pl.when(pl.program_id(0) == 0)
        def _():
            st_ref[...] = jnp.zeros_like(st_ref)

    row = lax.broadcasted_iota(I32, (chunk, KEY), 0)
    ta = lax.broadcasted_iota(I32, (chunk, chunk), 0)
    sa = lax.broadcasted_iota(I32, (chunk, chunk), 1)
    gn = gn_ref[...]

    def chunk_body(c, carry):
        rows = pl.ds(pl.multiple_of(c * chunk, chunk), chunk)
        lf = lf_ref[rows, :]
        q = q_ref[rows, :]
        k = k_ref[rows, :]
        qb = q.astype(BF16)
        kb = k.astype(BF16)
        att = []
        for h in range(HEADS):
            hs = slice(h * DK, (h + 1) * DK)
            att.append(jnp.where(ta == sa, _dot_nt(qb[:, hs], kb[:, hs]), 0.0))
        seg, tot = lf, lf
        half = 1
        while half < chunk:
            qs = (q * jnp.exp(seg)).astype(BF16)
            ks = (k * jnp.exp(tot - seg)).astype(BF16)
            pair = ((ta ^ sa) < 2 * half) & ((ta & half) != 0) & ((sa & half) == 0)
            for h in range(HEADS):
                hs = slice(h * DK, (h + 1) * DK)
                att[h] = att[h] + jnp.where(pair, _dot_nt(qs[:, hs], ks[:, hs]), 0.0)
            upper = (row & half) != 0
            below = pltpu.roll(tot, half, 0)
            above = pltpu.roll(tot, chunk - half, 0)
            seg = seg + jnp.where(upper, below, 0.0)
            tot = tot + jnp.where(upper, below, above)
            half *= 2
        q_in = (q * jnp.exp(seg)).astype(BF16)
        k_out = (k * jnp.exp(tot - seg)).astype(BF16)
        decay = jnp.exp(tot[0:1, :])
        for h in range(HEADS):
            hs = slice(h * DK, (h + 1) * DK)
            vs = slice(h * DV, (h + 1) * DV)
            state = st_ref[0, h]
            vh = v_ref[rows, vs]
            o = _dot_nt(q_in[:, hs], state.astype(BF16)) + _dot(att[h].astype(BF16), vh)
            st_ref[0, h] = state * decay[:, hs] + _dot_tn(vh, k_out[:, hs])
            ms = jnp.mean(o * o, axis=-1, keepdims=True)
            o = o * lax.rsqrt(ms + RMS_EPS) * gn
            o_ref[rows, vs] = (o * _silu(r_ref[rows, vs])).astype(BF16)
        return carry

    lax.fori_loop(0, n_chunks, chunk_body, 0)


def _gla(q, k, v, lf, r, gn, s0t, *, tile, chunk, n_tiles, block0):
    has_init = s0t is not None
    row = lambda width: pl.BlockSpec((tile, width), lambda i: (block0 + i, 0))
    st_spec = pl.BlockSpec((1, HEADS, DV, DK), (lambda i: (i, 0, 0, 0)) if has_init else (lambda i: (0, 0, 0, 0)))
    n_states = n_tiles if has_init else 1
    in_specs = [row(KEY), row(KEY), row(VAL), row(KEY), row(VAL), _const_spec(gn.shape)]
    args = [q, k, v, lf, r, gn]
    if has_init:
        in_specs.append(st_spec)
        args.append(s0t)
    return pl.pallas_call(
        functools.partial(_gla_kernel, chunk=chunk, n_chunks=tile // chunk, has_init=has_init),
        grid=(n_tiles,), in_specs=in_specs,
        out_specs=[pl.BlockSpec((tile, VAL), lambda i: (i, 0)), st_spec],
        out_shape=(jax.ShapeDtypeStruct((n_tiles * tile, VAL), BF16),
                   jax.ShapeDtypeStruct((n_states, HEADS, DV, DK), F32)),
        compiler_params=_params("arbitrary"), name="gla_init" if has_init else "gla",
    )(*args)


def _window_sums(ext, window):
    shift = 1
    while shift < window:
        ext = ext + pltpu.roll(ext, shift, 0)
        shift *= 2
    return ext


def _mix_kernel(xm_ref, xt_ref, ogm_ref, ogt_ref, u_ref, halo_m_ref, halo_t_ref, ga_ref, gb_ref,
                wbra, wpool, pscale, wbrb, wout, g1, b1, x1_ref):
    tail = _is_tail_tile()
    ya = _dot(jnp.where(tail, ogt_ref[...], ogm_ref[...]), wbra[...])
    u = u_ref[...]
    ext_m = jnp.concatenate([halo_m_ref[0], u], axis=0)
    seg = HALO + DEC_SEQ
    pieces = []
    for b in range(DEC_BATCH):
        pieces += [halo_t_ref[b], u[b * DEC_SEQ:(b + 1) * DEC_SEQ, :]]
    ext_t = jnp.concatenate(pieces, axis=0)
    rowi = lax.broadcasted_iota(I32, (TM, 1), 0)
    pos = jnp.where(tail, PAST_LEN + (rowi & (DEC_SEQ - 1)), pl.program_id(0) * TM + rowi)
    parts = []
    for g, window in enumerate(POOL_WINDOWS):
        cols = slice(g * POOL_GROUP_DIM, (g + 1) * POOL_GROUP_DIM)
        win_m = _window_sums(ext_m[:, cols], window)[HALO:, :]
        win_all = _window_sums(ext_t[:, cols], window)
        win_t = jnp.concatenate([win_all[b * seg + HALO:(b + 1) * seg, :] for b in range(DEC_BATCH)], axis=0)
        cnt = jnp.minimum(window, pos + 1).astype(F32)
        d = jnp.where(tail, win_t, win_m) / cnt - u[:, cols]
        parts.append(_dot(d.astype(BF16), wpool[g]))
    yb_in = jnp.concatenate(parts, axis=1) * pscale[...]
    yb = _dot(yb_in.astype(BF16), wbrb[...])
    mixed = _sigmoid(ga_ref[...]) * ya + _sigmoid(gb_ref[...]) * yb
    mix = _dot(mixed.astype(BF16), wout[...])
    x = jnp.where(tail, xt_ref[...], xm_ref[...])
    x1_ref[...] = _layer_norm(ALPHA * x + mix, g1[...], b1[...])


def _mix(xm, xt, og_m, og_t, u, halo_m, halo_t, ga, gb, w):
    n = xm.shape[0] // TM + 1
    row = lambda width: pl.BlockSpec((TM, width), lambda i: (i, 0))
    weights = (w["wbra"], w["wpool"], w["pscale"], w["wbrb"], w["wout"], w["g1"], w["b1"])
    return pl.pallas_call(
        _mix_kernel, grid=(n,),
        in_specs=_main_tail_specs(n, D_MODEL) + _main_tail_specs(n, VAL) + [
            row(POOL_WIDTH), pl.BlockSpec((1, HALO, POOL_WIDTH), lambda i: (jnp.minimum(i, n - 2), 0, 0)),
            _const_spec(halo_t.shape), row(D_MODEL), row(D_MODEL)] + [_const_spec(a.shape) for a in weights],
        out_specs=row(D_MODEL),
        out_shape=jax.ShapeDtypeStruct((n * TM, D_MODEL), F32),
        compiler_params=_params("arbitrary"), name="mix",
    )(xm, xt, og_m, og_t, u, halo_m, halo_t, ga, gb, *weights)


def _router_kernel(x1_ref, wrh_ref, wrl_ref, br_ref, wt_ref, rank_ref, cnt_ref, *, tile):
    x = x1_ref[...]
    xh = x.astype(BF16)
    xl = (x - xh.astype(F32)).astype(BF16)
    logits = _dot_nt(wrh_ref[...], xh) + _dot_nt(wrl_ref[...], xh) + _dot_nt(wrh_ref[...], xl)
    scores = _sigmoid(logits)
    rows_pad = 2 * N_EXPERTS
    n_grp = rows_pad // GROUP_SIZE
    erow = lax.broadcasted_iota(I32, (rows_pad, tile), 0)
    biased = jnp.where(erow < N_EXPERTS, scores + br_ref[...], NEG_INF)
    shape3 = (n_grp, GROUP_SIZE, tile)
    b3 = biased.reshape(shape3)
    s3 = scores.reshape(shape3)
    sub = lax.broadcasted_iota(I32, shape3, 1)
    gid = lax.broadcasted_iota(I32, shape3, 0)
    eid = gid * GROUP_SIZE + sub
    m1 = jnp.max(b3, axis=1, keepdims=True)
    i1 = jnp.min(jnp.where(b3 == m1, sub, GROUP_SIZE), axis=1, keepdims=True)
    m2 = jnp.max(jnp.where(sub == i1, NEG_INF, b3), axis=1, keepdims=True)
    gscore = m1 + m2
    gid1 = lax.broadcasted_iota(I32, (n_grp, 1, tile), 0)
    gsel = jnp.zeros((n_grp, 1, tile), jnp.bool_)
    for _ in range(TOPK_GROUPS):
        gm = jnp.max(gscore, axis=0, keepdims=True)
        gi = jnp.min(jnp.where(gscore == gm, gid1, n_grp), axis=0, keepdims=True)
        pick = gid1 == gi
        gsel = gsel | pick
        gscore = jnp.where(pick, NEG_INF, gscore)
    masked = jnp.where(gsel, b3, NEG_INF)
    wsel = jnp.zeros(shape3, F32)
    chosen = jnp.zeros(shape3, jnp.bool_)
    for _ in range(TOP_K):
        m = jnp.max(jnp.max(masked, axis=1, keepdims=True), axis=0, keepdims=True)
        idx = jnp.min(jnp.min(jnp.where(masked == m, eid, rows_pad), axis=1, keepdims=True), axis=0, keepdims=True)
        pick = eid == idx
        wsel = jnp.where(pick, s3, wsel)
        chosen = chosen | pick
        masked = jnp.where(pick, NEG_INF, masked)
    wsum = jnp.sum(jnp.sum(wsel, axis=1, keepdims=True), axis=0, keepdims=True)
    wt_ref[...] = (wsel / wsum * ROUTED_SCALE).reshape(rows_pad, tile)
    sel = jnp.where(chosen, 1.0, 0.0).reshape(rows_pad, tile)
    before = lax.broadcasted_iota(I32, (tile, tile), 0) < lax.broadcasted_iota(I32, (tile, tile), 1)
    rank = _dot(sel.astype(BF16), jnp.where(before, 1.0, 0.0).astype(BF16))
    rank_ref[...] = jnp.where(sel > 0.0, rank, -1.0).astype(I32)
    cnt = jnp.sum(sel, axis=1, keepdims=True).astype(I32)
    cnt_ref[0] = jnp.broadcast_to(cnt, (rows_pad, LANES))


def _router(x1, w):
    n = x1.shape[0] // TM
    rows_pad = 2 * N_EXPERTS
    tok = pl.BlockSpec((rows_pad, TM), lambda i: (0, i))
    return pl.pallas_call(
        functools.partial(_router_kernel, tile=TM), grid=(n,),
        in_specs=[pl.BlockSpec((TM, D_MODEL), lambda i: (i, 0)), _const_spec(w["wrh"].shape),
                  _const_spec(w["wrl"].shape), _const_spec(w["br"].shape)],
        out_specs=[tok, tok, pl.BlockSpec((1, rows_pad, LANES), lambda i: (i, 0, 0))],
        out_shape=(jax.ShapeDtypeStruct((rows_pad, x1.shape[0]), F32),
                   jax.ShapeDtypeStruct((rows_pad, x1.shape[0]), I32),
                   jax.ShapeDtypeStruct((n, rows_pad, LANES), I32)),
        compiler_params=_params("arbitrary"), name="router",
    )(x1, w["wrh"], w["wrl"], w["br"])


def _n_row_blocks(n_tokens):
    n_tiles = n_tokens // TM
    worst_rows = n_tokens * TOP_K + n_tiles * N_EXPERTS * (GRANULE - 1) + N_EXPERTS * (ROW_BLOCK - 1)
    return -(-worst_rows // ROW_BLOCK)


def _dispatch_tables(cnt, n_blocks):
    padded = (cnt + (GRANULE - 1)) // GRANULE * GRANULE
    run_end = jnp.cumsum(padded, axis=1)
    run_off = run_end - padded
    n_gran = run_end[:, -1] // GRANULE
    rows_e = jnp.sum(padded, axis=0)
    region = (rows_e + (ROW_BLOCK - 1)) // ROW_BLOCK * ROW_BLOCK
    region_end = jnp.cumsum(region)
    region_start = region_end - region
    run_base = region_start[None, :] + jnp.cumsum(padded, axis=0) - padded
    gran = jnp.arange(L_MAX // GRANULE, dtype=I32)
    shift = (run_base - run_off) // GRANULE
    step = shift - jnp.concatenate([jnp.zeros_like(shift[:, :1]), shift[:, :-1]], axis=1)
    run_first = run_off // GRANULE
    owned = gran[None, :, None] >= run_first[:, None, :]
    gran_dst = gran[None, :] + jnp.sum(jnp.where(owned, step[:, None, :], 0), axis=2)
    n_used = region_end[-1] // ROW_BLOCK
    blocks = jnp.arange(n_blocks, dtype=I32)
    block_expert = jnp.minimum(jnp.sum(blocks[:, None] >= (region_end // ROW_BLOCK)[None, :], axis=1), N_EXPERTS - 1)
    tail = (region - rows_e) // GRANULE
    tail_end = jnp.cumsum(tail)
    tail_first = tail_end - tail
    slot = jnp.arange(N_EXPERTS * (ROW_BLOCK // GRANULE - 1), dtype=I32)
    t_shift = (region_start + rows_e) // GRANULE - tail_first
    t_step = t_shift - jnp.concatenate([jnp.zeros_like(t_shift[:1]), t_shift[:-1]])
    tail_dst = slot + jnp.sum(jnp.where(slot[:, None] >= tail_first[None, :], t_step[None, :], 0), axis=1)
    as_i32 = lambda a: a.astype(I32).reshape(-1)
    return dict(run_off=as_i32(run_off), run_len=as_i32(padded), n_gran=as_i32(n_gran), gran_dst=as_i32(gran_dst),
                n_used=as_i32(n_used), block_expert=as_i32(block_expert), n_tail=as_i32(tail_end[-1]),
                tail_dst=as_i32(tail_dst))


def _granule_copy(src, src_gran, dst, dst_gran, sem, n=1):
    s = pl.multiple_of(src_gran * GRANULE, GRANULE)
    d = pl.multiple_of(dst_gran * GRANULE, GRANULE)
    return pltpu.make_async_copy(src.at[pl.ds(s, n * GRANULE), :], dst.at[pl.ds(d, n * GRANULE), :], sem)


def _start_granule_copies(n_gran, make_copy):
    n_groups = n_gran // COPY_GROUP

    def group(j, carry):
        for i in range(COPY_GROUP):
            make_copy(j * COPY_GROUP + i).start()
        return carry

    def single(g, carry):
        make_copy(g).start()
        return carry

    lax.fori_loop(0, n_groups, group, 0)
    lax.fori_loop(n_groups * COPY_GROUP, n_gran, single, 0)


def _wait_granule_copies(n_gran, src, dst, sem):
    n_groups = n_gran // COPY_GROUP

    def group(j, carry):
        _granule_copy(src, 0, dst, 0, sem, COPY_GROUP).wait()
        return carry

    def single(g, carry):
        _granule_copy(src, 0, dst, 0, sem).wait()
        return carry

    lax.fori_loop(0, n_groups, group, 0)
    lax.fori_loop(n_groups * COPY_GROUP, n_gran, single, 0)


def _build_selection(sel_scr, rank, values, run_off_ref, run_len_ref, tile_idx):
    slot = lax.broadcasted_iota(I32, (RUN_CHUNK, TM), 0)
    for e in range(N_EXPERTS):
        off = run_off_ref[tile_idx * N_EXPERTS + e]
        n_chunks = (run_len_ref[tile_idx * N_EXPERTS + e] + (RUN_CHUNK - 1)) // RUN_CHUNK
        rank_e = rank[e:e + 1, :]
        val_e = 1.0 if values is None else values[e:e + 1, :]

        def chunk(c, carry, off=off, rank_e=rank_e, val_e=val_e):
            hit = rank_e == slot + c * RUN_CHUNK
            rows = pl.ds(pl.multiple_of(off + c * RUN_CHUNK, GRANULE), RUN_CHUNK)
            sel_scr[rows, :] = jnp.where(hit, val_e, 0.0)
            return carry

        chunk(0, 0)
        lax.fori_loop(1, n_chunks, chunk, 0)


def _dispatch_kernel(run_off, run_len, n_gran, gran_dst, n_tail, tail_dst, n_used,
                     x1_ref, rank_ref, xs_hbm, sel_scr, buf_scr, zero_scr, sem):
    t = pl.program_id(0)

    @pl.when(t == 0)
    def _():
        sel_scr[...] = jnp.zeros_like(sel_scr)
        buf_scr[...] = jnp.zeros_like(buf_scr)
        zero_scr[...] = jnp.zeros_like(zero_scr)

    _build_selection(sel_scr, rank_ref[...], None, run_off, run_len, t)
    xb = x1_ref[...].astype(BF16)
    n_g = n_gran[t]
    buf = buf_scr.at[t % 2]
    for c in range(L_MAX // SEL_CHUNK):
        @pl.when(c * SEL_CHUNK < n_g * GRANULE)
        def _(c=c):
            rows = slice(c * SEL_CHUNK, (c + 1) * SEL_CHUNK)
            buf[rows, :] = _pack_halves(_dot(sel_scr[rows, :].astype(BF16), xb))

    @pl.when(t > 0)
    def _():
        _wait_granule_copies(n_gran[jnp.maximum(t - 1, 0)], buf, xs_hbm, sem)

    _start_granule_copies(
        n_g, lambda g: _granule_copy(buf, g, xs_hbm, gran_dst[t * (L_MAX // GRANULE) + g], sem))

    @pl.when(t == pl.num_programs(0) - 1)
    def _():
        _wait_granule_copies(n_g, buf, xs_hbm, sem)
        _start_granule_copies(n_tail[0], lambda i: _granule_copy(zero_scr, 0, xs_hbm, tail_dst[i], sem))
        _wait_granule_copies(n_tail[0], zero_scr, xs_hbm, sem)

        def spare_copy(b):
            rows = pl.ds(pl.multiple_of(b * ROW_BLOCK, ROW_BLOCK), ROW_BLOCK)
            return pltpu.make_async_copy(zero_scr, xs_hbm.at[rows, :], sem)

        def start_spare(b, carry):
            spare_copy(b).start()
            return carry

        def wait_spare(b, carry):
            spare_copy(b).wait()
            return carry

        n_blocks = xs_hbm.shape[0] // ROW_BLOCK
        lax.fori_loop(n_used[0], n_blocks, start_spare, 0)
        lax.fori_loop(n_used[0], n_blocks, wait_spare, 0)


def _dispatch(x1, rank, tables, n_blocks):
    n = x1.shape[0] // TM
    return pl.pallas_call(
        _dispatch_kernel,
        grid_spec=pltpu.PrefetchScalarGridSpec(
            num_scalar_prefetch=7, grid=(n,),
            in_specs=[pl.BlockSpec((TM, D_MODEL), lambda i, *_: (i, 0)),
                      pl.BlockSpec((2 * N_EXPERTS, TM), lambda i, *_: (0, i))],
            out_specs=pl.BlockSpec(memory_space=pl.ANY),
            scratch_shapes=[pltpu.VMEM((L_MAX, TM), F32), pltpu.VMEM((2, L_MAX, HALF), U32),
                            pltpu.VMEM((ROW_BLOCK, HALF), U32), pltpu.SemaphoreType.DMA(())]),
        out_shape=jax.ShapeDtypeStruct((n_blocks * ROW_BLOCK, HALF), U32),
        compiler_params=_params("arbitrary"), name="dispatch",
    )(tables["run_off"], tables["run_len"], tables["n_gran"], tables["gran_dst"], tables["n_tail"],
      tables["tail_dst"], tables["n_used"], x1, rank)


def _experts_kernel(block_expert, n_used, xs_hbm, wg_ref, wu_ref, wd_ref, y_ref,
                    x_ring, wg_scr, wu_scr, wd_scr, sems):
    b = pl.program_id(0)
    n_u = n_used[0]

    def fetch(block):
        slot = block % RING
        rows = pl.ds(pl.multiple_of(block * ROW_BLOCK, ROW_BLOCK), ROW_BLOCK)
        return pltpu.make_async_copy(xs_hbm.at[rows, :], x_ring.at[slot], sems.at[slot])

    @pl.when(b == 0)
    def _():
        for first in range(RING - 1):
            @pl.when(first < n_u)
            def _(first=first):
                fetch(first).start()

    @pl.when(b + (RING - 1) < n_u)
    def _():
        fetch(b + (RING - 1)).start()

    @pl.when(b >= n_u)
    def _():
        y_ref[...] = jnp.zeros_like(y_ref)

    @pl.when(b < n_u)
    def _():
        fresh = (b == 0) | (block_expert[b] != block_expert[jnp.maximum(b - 1, 0)])

        @pl.when(fresh)
        def _():
            wg_scr[...] = wg_ref[...].astype(BF16)
            wu_scr[...] = wu_ref[...].astype(BF16)
            wd_scr[...] = wd_ref[...].astype(BF16)

        fetch(b).wait()
        lo, hi = _unpack_halves(x_ring[b % RING])
        gate = _dot(lo, wg_scr[:HALF, :]) + _dot(hi, wg_scr[HALF:, :])
        up = _dot(lo, wu_scr[:HALF, :]) + _dot(hi, wu_scr[HALF:, :])
        y = _dot((_silu(gate) * up).astype(BF16), wd_scr[...])
        y_ref[...] = _pack_halves(y.astype(BF16).astype(F32))


def _experts(xs, tables, w_e_gate, w_e_up, w_e_down, layer, n_blocks):
    wmap = lambda b, be, nu: (layer, be[b], 0, 0)
    return pl.pallas_call(
        _experts_kernel,
        grid_spec=pltpu.PrefetchScalarGridSpec(
            num_scalar_prefetch=2, grid=(n_blocks,),
            in_specs=[pl.BlockSpec(memory_space=pl.ANY),
                      pl.BlockSpec((None, None, D_MODEL, D_EXPERT), wmap),
                      pl.BlockSpec((None, None, D_MODEL, D_EXPERT), wmap),
                      pl.BlockSpec((None, None, D_EXPERT, D_MODEL), wmap)],
            out_specs=pl.BlockSpec((ROW_BLOCK, HALF), lambda b, be, nu: (b, 0)),
            scratch_shapes=[pltpu.VMEM((RING, ROW_BLOCK, HALF), U32),
                            pltpu.VMEM((D_MODEL, D_EXPERT), BF16), pltpu.VMEM((D_MODEL, D_EXPERT), BF16),
                            pltpu.VMEM((D_EXPERT, D_MODEL), BF16), pltpu.SemaphoreType.DMA((RING,))]),
        out_shape=jax.ShapeDtypeStruct(xs.shape, U32),
        compiler_params=_params("arbitrary"), name="experts",
    )(tables["block_expert"], tables["n_used"], xs, w_e_gate, w_e_up, w_e_down)


def _combine_kernel(run_off, run_len, n_gran, gran_dst,
                    x1_ref, rank_ref, wt_ref, y_hbm, wsg, wsu, wsd, g2, b2, out_m_ref, out_t_ref,
                    sel_scr, buf_scr, acc_scr, sem):
    t = pl.program_id(0)
    n_tiles = pl.num_programs(0)
    n_g = n_gran[t]

    def fetch(tile):
        dst = buf_scr.at[tile % 2]
        _start_granule_copies(
            n_gran[tile],
            lambda g: _granule_copy(y_hbm, gran_dst[tile * (L_MAX // GRANULE) + g], dst, g, sem))

    @pl.when(t == 0)
    def _():
        buf_scr[...] = jnp.zeros_like(buf_scr)
        fetch(t)

    sel_scr[...] = jnp.zeros_like(sel_scr)
    _build_selection(sel_scr, rank_ref[...], wt_ref[...], run_off, run_len, t)
    x1 = x1_ref[...]
    xb = x1.astype(BF16)
    hs = (_silu(_dot(xb, wsg[...])) * _dot(xb, wsu[...])).astype(BF16)
    acc_scr[...] = _dot(hs, wsd[...])
    buf = buf_scr.at[t % 2]
    _wait_granule_copies(n_g, y_hbm, buf, sem)

    @pl.when(t + 1 < n_tiles)
    def _():
        fetch(jnp.minimum(t + 1, n_tiles - 1))

    for c in range(L_MAX // SEL_CHUNK):
        @pl.when(c * SEL_CHUNK < n_g * GRANULE)
        def _(c=c):
            rows = slice(c * SEL_CHUNK, (c + 1) * SEL_CHUNK)
            lo, hi = _unpack_halves(buf[rows, :])
            sel = sel_scr[rows, :].astype(BF16)
            acc_scr[:, :HALF] += _dot_tn(sel, lo)
            acc_scr[:, HALF:] += _dot_tn(sel, hi)

    out = _layer_norm(ALPHA * x1 + acc_scr[...], g2[...], b2[...])

    @pl.when(t + 1 < n_tiles)
    def _():
        out_m_ref[...] = out

    @pl.when(t + 1 == n_tiles)
    def _():
        out_t_ref[...] = out


def _combine(x1, rank, wt, y, tables, w):
    n = x1.shape[0] // TM
    small = (w["wsg"], w["wsu"], w["wsd"], w["g2"], w["b2"])
    tok = pl.BlockSpec((2 * N_EXPERTS, TM), lambda i, *_: (0, i))
    return pl.pallas_call(
        _combine_kernel,
        grid_spec=pltpu.PrefetchScalarGridSpec(
            num_scalar_prefetch=4, grid=(n,),
            in_specs=[pl.BlockSpec((TM, D_MODEL), lambda i, *_: (i, 0)), tok, tok,
                      pl.BlockSpec(memory_space=pl.ANY)]
                     + [pl.BlockSpec(a.shape, lambda i, *_, nd=a.ndim: (0,) * nd) for a in small],
            out_specs=_main_tail_specs(n, D_MODEL),
            scratch_shapes=[pltpu.VMEM((L_MAX, TM), F32), pltpu.VMEM((2, L_MAX, HALF), U32),
                            pltpu.VMEM((TM, D_MODEL), F32), pltpu.SemaphoreType.DMA(())]),
        out_shape=(jax.ShapeDtypeStruct((x1.shape[0] - TM, D_MODEL), F32),
                   jax.ShapeDtypeStruct((TM, D_MODEL), F32)),
        compiler_params=_params("arbitrary"), name="combine",
    )(tables["run_off"], tables["run_len"], tables["n_gran"], tables["gran_dst"], x1, rank, wt, y, *small)


def _moe(x1, w, w_e_gate, w_e_up, w_e_down, layer):
    n_blocks = _n_row_blocks(x1.shape[0])
    wt, rank, cnt = _router(x1, w)
    tables = _dispatch_tables(cnt[:, :N_EXPERTS, 0], n_blocks)
    xs = _dispatch(x1, rank, tables, n_blocks)
    y = _experts(xs, tables, w_e_gate, w_e_up, w_e_down, layer, n_blocks)
    return _combine(x1, rank, wt, y, tables, w)


def _layer_weights(l, w_in, w_gk2, b_gk, gla_norm_g, w_pool, pool_scale, w_br_a, w_br_b, w_out, ln1_g, ln1_b,
                   w_router, b_router, w_sh_gate, w_sh_up, w_sh_down, ln2_g, ln2_b):
    wi = w_in[l].astype(BF16)
    o = 0
    pieces = {}
    for name, width in (("wq", KEY), ("wk", KEY), ("wv", VAL), ("wr", VAL), ("wgk", GATE_RANK),
                        ("wu", POOL_WIDTH), ("wga", D_MODEL), ("wgb", D_MODEL)):
        pieces[name] = wi[:, o:o + width]
        o += width
    pad_rank = LANES - GATE_RANK
    pieces["wgk"] = jnp.pad(pieces["wgk"], ((0, 0), (0, pad_rank)))
    pieces["wgk2"] = jnp.pad(w_gk2[l].astype(BF16), ((0, pad_rank), (0, 0)))
    pieces["bgk"] = b_gk[l].reshape(1, KEY)
    pieces["gn"] = gla_norm_g[l].reshape(1, DV)
    pieces["wbra"] = w_br_a[l].astype(BF16)
    pieces["wpool"] = w_pool[l].astype(BF16)
    pieces["pscale"] = pool_scale[l].reshape(1, POOL_WIDTH)
    pieces["wbrb"] = w_br_b[l].astype(BF16)
    pieces["wout"] = w_out[l].astype(BF16)
    pieces["g1"] = ln1_g[l].reshape(1, D_MODEL)
    pieces["b1"] = ln1_b[l].reshape(1, D_MODEL)
    wrt = jnp.pad(w_router[l].T, ((0, N_EXPERTS), (0, 0)))
    wrh = wrt.astype(BF16)
    pieces["wrh"] = wrh
    pieces["wrl"] = (wrt - wrh.astype(F32)).astype(BF16)
    pieces["br"] = jnp.pad(b_router[l], (0, N_EXPERTS)).reshape(2 * N_EXPERTS, 1)
    pieces["wsg"] = w_sh_gate[l].astype(BF16)
    pieces["wsu"] = w_sh_up[l].astype(BF16)
    pieces["wsd"] = w_sh_down[l].astype(BF16)
    pieces["g2"] = ln2_g[l].reshape(1, D_MODEL)
    pieces["b2"] = ln2_b[l].reshape(1, D_MODEL)
    return pieces


def kernel(x_prompt, x_sample, state_gla, cache_pool, w_in, w_gk2, b_gk, gla_norm_g, w_pool, pool_scale, w_br_a, w_br_b, w_out, ln1_g, ln1_b, w_router, b_router, w_e_gate, w_e_up, w_e_down, w_sh_gate, w_sh_up, w_sh_down, ln2_g, ln2_b):
    assert T_SAMPLE == TM, "the decode tokens must fill exactly one token tile"
    xm = x_prompt.reshape(T_PROMPT, D_MODEL)
    xt = x_sample.reshape(T_SAMPLE, D_MODEL)
    n_prompt_tiles = T_PROMPT // TM
    sample_block0 = T_PROMPT // DEC_SEQ
    sp, hp, ss, hs = [], [], [], []
    for l in range(DEPTH):
        w = _layer_weights(l, w_in, w_gk2, b_gk, gla_norm_g, w_pool, pool_scale, w_br_a, w_br_b, w_out, ln1_g,
                           ln1_b, w_router, b_router, w_sh_gate, w_sh_up, w_sh_down, ln2_g, ln2_b)
        q, k, v, r, lf, u, ga, gb = _inproj(xm, xt, w)

        og_p, st_p = _gla(q, k, v, lf, r, w["gn"], None, tile=TM, chunk=CHUNK, n_tiles=n_prompt_tiles, block0=0)
        s0t = jnp.swapaxes(state_gla[l], -1, -2)
        og_s, st_s = _gla(q, k, v, lf, r, w["gn"], s0t, tile=DEC_SEQ, chunk=DEC_SEQ, n_tiles=DEC_BATCH,
                          block0=sample_block0)

        u_p = u[:T_PROMPT].reshape(n_prompt_tiles, TM, POOL_WIDTH)
        halo_p = jnp.concatenate([jnp.zeros((1, HALO, POOL_WIDTH), F32), u_p[:-1, TM - HALO:, :]], axis=0)
        halo_s = jnp.concatenate([jnp.zeros((DEC_BATCH, HALO - POOL_HIST, POOL_WIDTH), F32), cache_pool[l]], axis=1)
        x1 = _mix(xm, xt, og_p, og_s, u, halo_p, halo_s, ga, gb, w)

        xm, xt = _moe(x1, w, w_e_gate, w_e_up, w_e_down, l)

        sp.append(jnp.swapaxes(st_p, -1, -2))
        ss.append(jnp.swapaxes(st_s, -1, -2))
        hp.append(u[T_PROMPT - POOL_HIST:T_PROMPT].reshape(1, POOL_HIST, POOL_WIDTH))
        hs.append(u[T_PROMPT:].reshape(DEC_BATCH, DEC_SEQ, POOL_WIDTH)[:, DEC_SEQ - POOL_HIST:, :])
    y_prompt = xm.reshape(1, SEQ, D_MODEL)
    y_sample = xt.reshape(DEC_BATCH, DEC_SEQ, D_MODEL)
    return (y_prompt, y_sample, jnp.stack(sp), jnp.stack(hp), jnp.stack(ss), jnp.stack(hs))
```

```python
import functools

import jax
import jax.numpy as jnp
from jax import lax
from jax.experimental import pallas as pl
from jax.experimental.pallas import tpu as pltpu

F32 = jnp.float32
BF16 = jnp.bfloat16
U32 = jnp.uint32
I32 = jnp.int32

D_MODEL = 1024
DEPTH = 2
SEQ = 16384
DEC_BATCH = 8
DEC_SEQ = 32
PAST_LEN = 4096
CHUNK = 64
HEADS = 4
DK = 128
DV = 256
KEY = HEADS * DK
VAL = HEADS * DV
GATE_RANK = 16
GATE_NORMALIZER = 16.0
POOL_WIDTH = 512
POOL_WINDOWS = (2, 4, 8, 16)
POOL_GROUP_DIM = 128
POOL_HIST = 15
N_EXPERTS = 64
GROUP_SIZE = 8
TOPK_GROUPS = 4
TOP_K = 8
D_EXPERT = 256
ROUTED_SCALE = 2.5
ALPHA = (2 * DEPTH) ** 0.25
LN_EPS = 1e-5
RMS_EPS = 1e-6

LANES = 128
SUBLANES = 8
VMEM_LIMIT_BYTES = 56 * 1024 * 1024

T_PROMPT = SEQ
T_SAMPLE = DEC_BATCH * DEC_SEQ
T_ALL = T_PROMPT + T_SAMPLE
TM = 256
HALO = 16
NEG_INF = float("-inf")

HALF = D_MODEL // 2
GRANULE = SUBLANES
ROW_BLOCK = 512
COPY_GROUP = 8
RING = 3
RUN_CHUNK = 64
L_MAX = 2560
SEL_CHUNK = 512
HI_MASK = 0xFFFF0000


def _dot(a, b):
    return jnp.dot(a, b, preferred_element_type=F32)


def _dot_nt(a, b):
    return lax.dot_general(a, b, (((1,), (1,)), ((), ())), preferred_element_type=F32)


def _dot_tn(a, b):
    return lax.dot_general(a, b, (((0,), (0,)), ((), ())), preferred_element_type=F32)


def _sigmoid(x):
    return 1.0 / (1.0 + jnp.exp(-x))


def _silu(x):
    return x * _sigmoid(x)


def _layer_norm(x, g, b):
    mu = jnp.mean(x, axis=-1, keepdims=True)
    xc = x - mu
    var = jnp.mean(xc * xc, axis=-1, keepdims=True)
    return xc * lax.rsqrt(var + LN_EPS) * g + b


def _pack_halves(v):
    bits = lax.bitcast_convert_type(v, U32)
    return (bits[:, HALF:] & jnp.uint32(HI_MASK)) | (bits[:, :HALF] >> 16)


def _unpack_halves(w):
    lo = lax.bitcast_convert_type(w << 16, F32).astype(BF16)
    hi = lax.bitcast_convert_type(w & jnp.uint32(HI_MASK), F32).astype(BF16)
    return lo, hi


def _params(*sem):
    return pltpu.CompilerParams(dimension_semantics=sem, vmem_limit_bytes=VMEM_LIMIT_BYTES)


def _const_spec(shape):
    nd = len(shape)
    return pl.BlockSpec(shape, lambda *_: (0,) * nd)


def _is_tail_tile():
    return pl.program_id(0) == pl.num_programs(0) - 1


def _main_tail_specs(n_tiles, width):
    return [pl.BlockSpec((TM, width), lambda i, *_: (jnp.minimum(i, n_tiles - 2), 0)),
            pl.BlockSpec((TM, width), lambda i, *_: (0, 0))]


def _inproj_kernel(xm_ref, xt_ref, wq, wk, wv, wr, wgk, wgk2, bgk, wu, wga, wgb,
                   q_o, k_o, v_o, r_o, lf_o, u_o, ga_o, gb_o):
    xb = jnp.where(_is_tail_tile(), xt_ref[...], xm_ref[...]).astype(BF16)
    q_o[...] = (_dot(xb, wq[...]) * (DK ** -0.5)).astype(BF16)
    k_o[...] = _dot(xb, wk[...]).astype(BF16)
    v_o[...] = _dot(xb, wv[...]).astype(BF16)
    r_o[...] = _dot(xb, wr[...]).astype(BF16)
    gk = _dot(xb, wgk[...]).astype(BF16)
    z = _dot(gk, wgk2[...]) + bgk[...]
    log_sig = jnp.minimum(z, 0.0) - jnp.log1p(jnp.exp(-jnp.abs(z)))
    lf_o[...] = log_sig * (1.0 / GATE_NORMALIZER)
    u_o[...] = _dot(xb, wu[...])
    ga_o[...] = _dot(xb, wga[...]).astype(BF16)
    gb_o[...] = _dot(xb, wgb[...]).astype(BF16)


def _inproj(xm, xt, w):
    n = xm.shape[0] // TM + 1
    rows = n * TM
    row = lambda width: pl.BlockSpec((TM, width), lambda i: (i, 0))
    out_shapes = (
        jax.ShapeDtypeStruct((rows, KEY), BF16), jax.ShapeDtypeStruct((rows, KEY), BF16),
        jax.ShapeDtypeStruct((rows, VAL), BF16), jax.ShapeDtypeStruct((rows, VAL), BF16),
        jax.ShapeDtypeStruct((rows, KEY), F32), jax.ShapeDtypeStruct((rows, POOL_WIDTH), F32),
        jax.ShapeDtypeStruct((rows, D_MODEL), BF16), jax.ShapeDtypeStruct((rows, D_MODEL), BF16))
    weights = (w["wq"], w["wk"], w["wv"], w["wr"], w["wgk"], w["wgk2"], w["bgk"], w["wu"], w["wga"], w["wgb"])
    return pl.pallas_call(
        _inproj_kernel, grid=(n,),
        in_specs=_main_tail_specs(n, D_MODEL) + [_const_spec(a.shape) for a in weights],
        out_specs=[row(KEY), row(KEY), row(VAL), row(VAL), row(KEY), row(POOL_WIDTH), row(D_MODEL), row(D_MODEL)],
        out_shape=out_shapes, compiler_params=_params("arbitrary"), name="inproj",
    )(xm, xt, *weights)


def _gla_kernel(*refs, chunk, n_chunks, has_init):
    if has_init:
        q_ref, k_ref, v_ref, lf_ref, r_ref, gn_ref, s0_ref, o_ref, st_ref = refs
        st_ref[...] = s0_ref[...]
    else:
        q_ref, k_ref, v_ref, lf_ref, r_ref, gn_ref, o_ref, st_ref = refs

        @pl.when(pl.program_id(0) == 0)
        def _():
            st_ref[...] = jnp.zeros_like(st_ref)

    row = lax.broadcasted_iota(I32, (chunk, KEY), 0)
    ta = lax.broadcasted_iota(I32, (chunk, chunk), 0)
    sa = lax.broadcasted_iota(I32, (chunk, chunk), 1)
    gn = gn_ref[...]

    def chunk_body(c, carry):
        rows = pl.ds(pl.multiple_of(c * chunk, chunk), chunk)
        lf = lf_ref[rows, :]
        qb = q_ref[rows, :]
        kb = k_ref[rows, :]
        q = qb.astype(F32)
        k = kb.astype(F32)
        att = []
        for h in range(HEADS):
            hs = slice(h * DK, (h + 1) * DK)
            att.append(jnp.where(ta == sa, _dot_nt(qb[:, hs], kb[:, hs]), 0.0))
        seg, tot = lf, lf
        half = 1
        while half < chunk:
            qs = (q * jnp.exp(seg)).astype(BF16)
            ks = (k * jnp.exp(tot - seg)).astype(BF16)
            pair = ((ta ^ sa) < 2 * half) & ((ta & half) != 0) & ((sa & half) == 0)
            for h in range(HEADS):
                hs = slice(h * DK, (h + 1) * DK)
                att[h] = att[h] + jnp.where(pair, _dot_nt(qs[:, hs], ks[:, hs]), 0.0)
            upper = (row & half) != 0
            below = pltpu.roll(tot, half, 0)
            above = pltpu.roll(tot, chunk - half, 0)
            seg = seg + jnp.where(upper, below, 0.0)
            tot = tot + jnp.where(upper, below, above)
            half *= 2
        q_in = (q * jnp.exp(seg)).astype(BF16)
        k_out = (k * jnp.exp(tot - seg)).astype(BF16)
        decay = jnp.exp(tot[0:1, :])
        for h in range(HEADS):
            hs = slice(h * DK, (h + 1) * DK)
            vs = slice(h * DV, (h + 1) * DV)
            state = st_ref[0, h]
            vh = v_ref[rows, vs]
            o = _dot_nt(q_in[:, hs], state.astype(BF16)) + _dot(att[h].astype(BF16), vh)
            st_ref[0, h] = state * decay[:, hs] + _dot_tn(vh, k_out[:, hs])
            ms = jnp.mean(o * o, axis=-1, keepdims=True)
            o = o * lax.rsqrt(ms + RMS_EPS) * gn
            o_ref[rows, vs] = (o * _silu(r_ref[rows, vs].astype(F32))).astype(BF16)
        return carry

    lax.fori_loop(0, n_chunks, chunk_body, 0)


def _gla(q, k, v, lf, r, gn, s0t, *, tile, chunk, n_tiles, block0):
    has_init = s0t is not None
    row = lambda width: pl.BlockSpec((tile, width), lambda i: (block0 + i, 0))
    st_spec = pl.BlockSpec((1, HEADS, DV, DK), (lambda i: (i, 0, 0, 0)) if has_init else (lambda i: (0, 0, 0, 0)))
    n_states = n_tiles if has_init else 1
    in_specs = [row(KEY), row(KEY), row(VAL), row(KEY), row(VAL), _const_spec(gn.shape)]
    args = [q, k, v, lf, r, gn]
    if has_init:
        in_specs.append(st_spec)
        args.append(s0t)
    return pl.pallas_call(
        functools.partial(_gla_kernel, chunk=chunk, n_chunks=tile // chunk, has_init=has_init),
        grid=(n_tiles,), in_specs=in_specs,
        out_specs=[pl.BlockSpec((tile, VAL), lambda i: (i, 0)), st_spec],
        out_shape=(jax.ShapeDtypeStruct((n_tiles * tile, VAL), BF16),
                   jax.ShapeDtypeStruct((n_states, HEADS, DV, DK), F32)),
        compiler_params=_params("arbitrary"), name="gla_init" if has_init else "gla",
    )(*args)


def _window_sums(ext, window):
    shift = 1
    while shift < window:
        ext = ext + pltpu.roll(ext, shift, 0)
        shift *= 2
    return ext


def _mix_kernel(xm_ref, xt_ref, ogm_ref, ogt_ref, u_ref, halo_m_ref, halo_t_ref, ga_ref, gb_ref,
                wbra, wpool, pscale, wbrb, wout, g1, b1, x1_ref):
    tail = _is_tail_tile()
    ya = _dot(jnp.where(tail, ogt_ref[...], ogm_ref[...]), wbra[...])
    u = u_ref[...]
    ext_m = jnp.concatenate([halo_m_ref[0], u], axis=0)
    seg = HALO + DEC_SEQ
    pieces = []
    for b in range(DEC_BATCH):
        pieces += [halo_t_ref[b], u[b * DEC_SEQ:(b + 1) * DEC_SEQ, :]]
    ext_t = jnp.concatenate(pieces, axis=0)
    rowi = lax.broadcasted_iota(I32, (TM, 1), 0)
    pos = jnp.where(tail, PAST_LEN + (rowi & (DEC_SEQ - 1)), pl.program_id(0) * TM + rowi)
    parts = []
    for g, window in enumerate(POOL_WINDOWS):
        cols = slice(g * POOL_GROUP_DIM, (g + 1) * POOL_GROUP_DIM)
        win_m = _window_sums(ext_m[:, cols], window)[HALO:, :]
        win_all = _window_sums(ext_t[:, cols], window)
        win_t = jnp.concatenate([win_all[b * seg + HALO:(b + 1) * seg, :] for b in range(DEC_BATCH)], axis=0)
        cnt = jnp.minimum(window, pos + 1).astype(F32)
        d = jnp.where(tail, win_t, win_m) / cnt - u[:, cols]
        parts.append(_dot(d.astype(BF16), wpool[g]))
    yb_in = jnp.concatenate(parts, axis=1) * pscale[...]
    yb = _dot(yb_in.astype(BF16), wbrb[...])
    mixed = _sigmoid(ga_ref[...].astype(F32)) * ya + _sigmoid(gb_ref[...].astype(F32)) * yb
    mix = _dot(mixed.astype(BF16), wout[...])
    x = jnp.where(tail, xt_ref[...], xm_ref[...])
    x1_ref[...] = _layer_norm(ALPHA * x + mix, g1[...], b1[...])


def _mix(xm, xt, og_m, og_t, u, halo_m, halo_t, ga, gb, w):
    n = xm.shape[0] // TM + 1
    row = lambda width: pl.BlockSpec((TM, width), lambda i: (i, 0))
    weights = (w["wbra"], w["wpool"], w["pscale"], w["wbrb"], w["wout"], w["g1"], w["b1"])
    return pl.pallas_call(
        _mix_kernel, grid=(n,),
        in_specs=_main_tail_specs(n, D_MODEL) + _main_tail_specs(n, VAL) + [
            row(POOL_WIDTH), pl.BlockSpec((1, HALO, POOL_WIDTH), lambda i: (jnp.minimum(i, n - 2), 0, 0)),
            _const_spec(halo_t.shape), row(D_MODEL), row(D_MODEL)] + [_const_spec(a.shape) for a in weights],
        out_specs=row(D_MODEL),
        out_shape=jax.ShapeDtypeStruct((n * TM, D_MODEL), F32),
        compiler_params=_params("arbitrary"), name="mix",
    )(xm, xt, og_m, og_t, u, halo_m, halo_t, ga, gb, *weights)


def _router_kernel(x1_ref, wrh_ref, wrl_ref, br_ref, wt_ref, rank_ref, cnt_ref, *, tile):
    x = x1_ref[...]
    xh = x.astype(BF16)
    xl = (x - xh.astype(F32)).astype(BF16)
    logits = _dot_nt(wrh_ref[...], xh) + _dot_nt(wrl_ref[...], xh) + _dot_nt(wrh_ref[...], xl)
    scores = _sigmoid(logits)
    rows_pad = 2 * N_EXPERTS
    n_grp = rows_pad // GROUP_SIZE
    erow = lax.broadcasted_iota(I32, (rows_pad, tile), 0)
    biased = jnp.where(erow < N_EXPERTS, scores + br_ref[...], NEG_INF)
    shape3 = (n_grp, GROUP_SIZE, tile)
    b3 = biased.reshape(shape3)
    s3 = scores.reshape(shape3)
    sub = lax.broadcasted_iota(I32, shape3, 1)
    gid = lax.broadcasted_iota(I32, shape3, 0)
    eid = gid * GROUP_SIZE + sub
    m1 = jnp.max(b3, axis=1, keepdims=True)
    i1 = jnp.min(jnp.where(b3 == m1, sub, GROUP_SIZE), axis=1, keepdims=True)
    m2 = jnp.max(jnp.where(sub == i1, NEG_INF, b3), axis=1, keepdims=True)
    gscore = m1 + m2
    gid1 = lax.broadcasted_iota(I32, (n_grp, 1, tile), 0)
    gsel = jnp.zeros((n_grp, 1, tile), jnp.bool_)
    for _ in range(TOPK_GROUPS):
        gm = jnp.max(gscore, axis=0, keepdims=True)
        gi = jnp.min(jnp.where(gscore == gm, gid1, n_grp), axis=0, keepdims=True)
        pick = gid1 == gi
        gsel = gsel | pick
        gscore = jnp.where(pick, NEG_INF, gscore)
    masked = jnp.where(gsel, b3, NEG_INF)
    wsel = jnp.zeros(shape3, F32)
    chosen = jnp.zeros(shape3, jnp.bool_)
    for _ in range(TOP_K):
        m = jnp.max(jnp.max(masked, axis=1, keepdims=True), axis=0, keepdims=True)
        idx = jnp.min(jnp.min(jnp.where(masked == m, eid, rows_pad), axis=1, keepdims=True), axis=0, keepdims=True)
        pick = eid == idx
        wsel = jnp.where(pick, s3, wsel)
        chosen = chosen | pick
        masked = jnp.where(pick, NEG_INF, masked)
    wsum = jnp.sum(jnp.sum(wsel, axis=1, keepdims=True), axis=0, keepdims=True)
    wt_ref[...] = (wsel / wsum * ROUTED_SCALE).reshape(rows_pad, tile)
    sel = jnp.where(chosen, 1.0, 0.0).reshape(rows_pad, tile)
    before = lax.broadcasted_iota(I32, (tile, tile), 0) < lax.broadcasted_iota(I32, (tile, tile), 1)
    rank = _dot(sel.astype(BF16), jnp.where(before, 1.0, 0.0).astype(BF16))
    rank_ref[...] = jnp.where(sel > 0.0, rank, -1.0).astype(I32)
    cnt = jnp.sum(sel, axis=1, keepdims=True).astype(I32)
    cnt_ref[0] = jnp.broadcast_to(cnt, (rows_pad, LANES))


def _router(x1, w):
    n = x1.shape[0] // TM
    rows_pad = 2 * N_EXPERTS
    tok = pl.BlockSpec((rows_pad, TM), lambda i: (0, i))
    return pl.pallas_call(
        functools.partial(_router_kernel, tile=TM), grid=(n,),
        in_specs=[pl.BlockSpec((TM, D_MODEL), lambda i: (i, 0)), _const_spec(w["wrh"].shape),
                  _const_spec(w["wrl"].shape), _const_spec(w["br"].shape)],
        out_specs=[tok, tok, pl.BlockSpec((1, rows_pad, LANES), lambda i: (i, 0, 0))],
        out_shape=(jax.ShapeDtypeStruct((rows_pad, x1.shape[0]), F32),
                   jax.ShapeDtypeStruct((rows_pad, x1.shape[0]), I32),
                   jax.ShapeDtypeStruct((n, rows_pad, LANES), I32)),
        compiler_params=_params("arbitrary"), name="router",
    )(x1, w["wrh"], w["wrl"], w["br"])


def _n_row_blocks(n_tokens):
    n_tiles = n_tokens // TM
    worst_rows = n_tokens * TOP_K + n_tiles * N_EXPERTS * (GRANULE - 1) + N_EXPERTS * (ROW_BLOCK - 1)
    return -(-worst_rows // ROW_BLOCK)


def _dispatch_tables(cnt, n_blocks):
    padded = (cnt + (GRANULE - 1)) // GRANULE * GRANULE
    run_end = jnp.cumsum(padded, axis=1)
    run_off = run_end - padded
    n_gran = run_end[:, -1] // GRANULE
    rows_e = jnp.sum(padded, axis=0)
    region = (rows_e + (ROW_BLOCK - 1)) // ROW_BLOCK * ROW_BLOCK
    region_end = jnp.cumsum(region)
    region_start = region_end - region
    run_base = region_start[None, :] + jnp.cumsum(padded, axis=0) - padded
    gran = jnp.arange(L_MAX // GRANULE, dtype=I32)
    shift = (run_base - run_off) // GRANULE
    step = shift - jnp.concatenate([jnp.zeros_like(shift[:, :1]), shift[:, :-1]], axis=1)
    run_first = run_off // GRANULE
    owned = gran[None, :, None] >= run_first[:, None, :]
    gran_dst = gran[None, :] + jnp.sum(jnp.where(owned, step[:, None, :], 0), axis=2)
    n_used = region_end[-1] // ROW_BLOCK
    blocks = jnp.arange(n_blocks, dtype=I32)
    block_expert = jnp.minimum(jnp.sum(blocks[:, None] >= (region_end // ROW_BLOCK)[None, :], axis=1), N_EXPERTS - 1)
    tail = (region - rows_e) // GRANULE
    tail_end = jnp.cumsum(tail)
    tail_first = tail_end - tail
    slot = jnp.arange(N_EXPERTS * (ROW_BLOCK // GRANULE - 1), dtype=I32)
    t_shift = (region_start + rows_e) // GRANULE - tail_first
    t_step = t_shift - jnp.concatenate([jnp.zeros_like(t_shift[:1]), t_shift[:-1]])
    tail_dst = slot + jnp.sum(jnp.where(slot[:, None] >= tail_first[None, :], t_step[None, :], 0), axis=1)
    as_i32 = lambda a: a.astype(I32).reshape(-1)
    return dict(run_off=as_i32(run_off), run_len=as_i32(padded), n_gran=as_i32(n_gran), gran_dst=as_i32(gran_dst),
                n_used=as_i32(n_used), block_expert=as_i32(block_expert), n_tail=as_i32(tail_end[-1]),
                tail_dst=as_i32(tail_dst))


def _granule_copy(src, src_gran, dst, dst_gran, sem, n=1):
    s = pl.multiple_of(src_gran * GRANULE, GRANULE)
    d = pl.multiple_of(dst_gran * GRANULE, GRANULE)
    return pltpu.make_async_copy(src.at[pl.ds(s, n * GRANULE), :], dst.at[pl.ds(d, n * GRANULE), :], sem)


def _start_granule_copies(n_gran, make_copy):
    n_groups = n_gran // COPY_GROUP

    def group(j, carry):
        for i in range(COPY_GROUP):
            make_copy(j * COPY_GROUP + i).start()
        return carry

    def single(g, carry):
        make_copy(g).start()
        return carry

    lax.fori_loop(0, n_groups, group, 0)
    lax.fori_loop(n_groups * COPY_GROUP, n_gran, single, 0)


def _wait_granule_copies(n_gran, src, dst, sem):
    n_groups = n_gran // COPY_GROUP

    def group(j, carry):
        _granule_copy(src, 0, dst, 0, sem, COPY_GROUP).wait()
        return carry

    def single(g, carry):
        _granule_copy(src, 0, dst, 0, sem).wait()
        return carry

    lax.fori_loop(0, n_groups, group, 0)
    lax.fori_loop(n_groups * COPY_GROUP, n_gran, single, 0)


def _build_selection(sel_scr, rank, values, run_off_ref, run_len_ref, tile_idx):
    slot = lax.broadcasted_iota(I32, (RUN_CHUNK, TM), 0)
    for e in range(N_EXPERTS):
        off = run_off_ref[tile_idx * N_EXPERTS + e]
        n_chunks = (run_len_ref[tile_idx * N_EXPERTS + e] + (RUN_CHUNK - 1)) // RUN_CHUNK
        rank_e = rank[e:e + 1, :]
        val_e = 1.0 if values is None else values[e:e + 1, :]

        def chunk(c, carry, off=off, rank_e=rank_e, val_e=val_e):
            hit = rank_e == slot + c * RUN_CHUNK
            rows = pl.ds(pl.multiple_of(off + c * RUN_CHUNK, GRANULE), RUN_CHUNK)
            sel_scr[rows, :] = jnp.where(hit, val_e, 0.0)
            return carry

        chunk(0, 0)
        lax.fori_loop(1, n_chunks, chunk, 0)


def _dispatch_kernel(run_off, run_len, n_gran, gran_dst, n_tail, tail_dst, n_used,
                     x1_ref, rank_ref, xs_hbm, sel_scr, buf_scr, zero_scr, sem):
    t = pl.program_id(0)

    @pl.when(t == 0)
    def _():
        sel_scr[...] = jnp.zeros_like(sel_scr)
        buf_scr[...] = jnp.zeros_like(buf_scr)
        zero_scr[...] = jnp.zeros_like(zero_scr)

    _build_selection(sel_scr, rank_ref[...], None, run_off, run_len, t)
    xb = x1_ref[...].astype(BF16)
    n_g = n_gran[t]
    buf = buf_scr.at[t % 2]
    for c in range(L_MAX // SEL_CHUNK):
        @pl.when(c * SEL_CHUNK < n_g * GRANULE)
        def _(c=c):
            rows = slice(c * SEL_CHUNK, (c + 1) * SEL_CHUNK)
            buf[rows, :] = _pack_halves(_dot(sel_scr[rows, :].astype(BF16), xb))

    @pl.when(t > 0)
    def _():
        _wait_granule_copies(n_gran[jnp.maximum(t - 1, 0)], buf, xs_hbm, sem)

    _start_granule_copies(
        n_g, lambda g: _granule_copy(buf, g, xs_hbm, gran_dst[t * (L_MAX // GRANULE) + g], sem))

    @pl.when(t == pl.num_programs(0) - 1)
    def _():
        _wait_granule_copies(n_g, buf, xs_hbm, sem)
        _start_granule_copies(n_tail[0], lambda i: _granule_copy(zero_scr, 0, xs_hbm, tail_dst[i], sem))
        _wait_granule_copies(n_tail[0], zero_scr, xs_hbm, sem)

        def spare_copy(b):
            rows = pl.ds(pl.multiple_of(b * ROW_BLOCK, ROW_BLOCK), ROW_BLOCK)
            return pltpu.make_async_copy(zero_scr, xs_hbm.at[rows, :], sem)

        def start_spare(b, carry):
            spare_copy(b).start()
            return carry

        def wait_spare(b, carry):
            spare_copy(b).wait()
            return carry

        n_blocks = xs_hbm.shape[0] // ROW_BLOCK
        lax.fori_loop(n_used[0], n_blocks, start_spare, 0)
        lax.fori_loop(n_used[0], n_blocks, wait_spare, 0)


def _dispatch(x1, rank, tables, n_blocks):
    n = x1.shape[0] // TM
    return pl.pallas_call(
        _dispatch_kernel,
        grid_spec=pltpu.PrefetchScalarGridSpec(
            num_scalar_prefetch=7, grid=(n,),
            in_specs=[pl.BlockSpec((TM, D_MODEL), lambda i, *_: (i, 0)),
                      pl.BlockSpec((2 * N_EXPERTS, TM), lambda i, *_: (0, i))],
            out_specs=pl.BlockSpec(memory_space=pl.ANY),
            scratch_shapes=[pltpu.VMEM((L_MAX, TM), F32), pltpu.VMEM((2, L_MAX, HALF), U32),
                            pltpu.VMEM((ROW_BLOCK, HALF), U32), pltpu.SemaphoreType.DMA(())]),
        out_shape=jax.ShapeDtypeStruct((n_blocks * ROW_BLOCK, HALF), U32),
        compiler_params=_params("arbitrary"), name="dispatch",
    )(tables["run_off"], tables["run_len"], tables["n_gran"], tables["gran_dst"], tables["n_tail"],
      tables["tail_dst"], tables["n_used"], x1, rank)


def _experts_kernel(block_expert, n_used, xs_hbm, wg_ref, wu_ref, wd_ref, y_ref,
                    x_ring, wg_scr, wu_scr, wd_scr, sems):
    b = pl.program_id(0)
    n_u = n_used[0]

    def fetch(block):
        slot = block % RING
        rows = pl.ds(pl.multiple_of(block * ROW_BLOCK, ROW_BLOCK), ROW_BLOCK)
        return pltpu.make_async_copy(xs_hbm.at[rows, :], x_ring.at[slot], sems.at[slot])

    @pl.when(b == 0)
    def _():
        for first in range(RING - 1):
            @pl.when(first < n_u)
            def _(first=first):
                fetch(first).start()

    @pl.when(b + (RING - 1) < n_u)
    def _():
        fetch(b + (RING - 1)).start()

    @pl.when(b >= n_u)
    def _():
        y_ref[...] = jnp.zeros_like(y_ref)

    @pl.when(b < n_u)
    def _():
        fresh = (b == 0) | (block_expert[b] != block_expert[jnp.maximum(b - 1, 0)])

        @pl.when(fresh)
        def _():
            wg_scr[...] = wg_ref[...].astype(BF16)
            wu_scr[...] = wu_ref[...].astype(BF16)
            wd_scr[...] = wd_ref[...].astype(BF16)

        fetch(b).wait()
        lo, hi = _unpack_halves(x_ring[b % RING])
        gate = _dot(lo, wg_scr[:HALF, :]) + _dot(hi, wg_scr[HALF:, :])
        up = _dot(lo, wu_scr[:HALF, :]) + _dot(hi, wu_scr[HALF:, :])
        y = _dot((_silu(gate) * up).astype(BF16), wd_scr[...])
        y_ref[...] = _pack_halves(y.astype(BF16).astype(F32))


def _experts(xs, tables, w_e_gate, w_e_up, w_e_down, layer, n_blocks):
    wmap = lambda b, be, nu: (layer, be[b], 0, 0)
    return pl.pallas_call(
        _experts_kernel,
        grid_spec=pltpu.PrefetchScalarGridSpec(
            num_scalar_prefetch=2, grid=(n_blocks,),
            in_specs=[pl.BlockSpec(memory_space=pl.ANY),
                      pl.BlockSpec((None, None, D_MODEL, D_EXPERT), wmap),
                      pl.BlockSpec((None, None, D_MODEL, D_EXPERT), wmap),
                      pl.BlockSpec((None, None, D_EXPERT, D_MODEL), wmap)],
            out_specs=pl.BlockSpec((ROW_BLOCK, HALF), lambda b, be, nu: (b, 0)),
            scratch_shapes=[pltpu.VMEM((RING, ROW_BLOCK, HALF), U32),
                            pltpu.VMEM((D_MODEL, D_EXPERT), BF16), pltpu.VMEM((D_MODEL, D_EXPERT), BF16),
                            pltpu.VMEM((D_EXPERT, D_MODEL), BF16), pltpu.SemaphoreType.DMA((RING,))]),
        out_shape=jax.ShapeDtypeStruct(xs.shape, U32),
        compiler_params=_params("arbitrary"), name="experts",
    )(tables["block_expert"], tables["n_used"], xs, w_e_gate, w_e_up, w_e_down)


def _combine_kernel(run_off, run_len, n_gran, gran_dst,
                    x1_ref, rank_ref, wt_ref, y_hbm, wsg, wsu, wsd, g2, b2, out_m_ref, out_t_ref,
                    sel_scr, buf_scr, acc_scr, sem):
    t = pl.program_id(0)
    n_tiles = pl.num_programs(0)
    n_g = n_gran[t]

    def fetch(tile):
        dst = buf_scr.at[tile % 2]
        _start_granule_copies(
            n_gran[tile],
            lambda g: _granule_copy(y_hbm, gran_dst[tile * (L_MAX // GRANULE) + g], dst, g, sem))

    @pl.when(t == 0)
    def _():
        buf_scr[...] = jnp.zeros_like(buf_scr)
        fetch(t)

    sel_scr[...] = jnp.zeros_like(sel_scr)
    _build_selection(sel_scr, rank_ref[...], wt_ref[...], run_off, run_len, t)
    x1 = x1_ref[...]
    xb = x1.astype(BF16)
    hs = (_silu(_dot(xb, wsg[...])) * _dot(xb, wsu[...])).astype(BF16)
    acc_scr[...] = _dot(hs, wsd[...])
    buf = buf_scr.at[t % 2]
    _wait_granule_copies(n_g, y_hbm, buf, sem)

    @pl.when(t + 1 < n_tiles)
    def _():
        fetch(jnp.minimum(t + 1, n_tiles - 1))

    for c in range(L_MAX // SEL_CHUNK):
        @pl.when(c * SEL_CHUNK < n_g * GRANULE)
        def _(c=c):
            rows = slice(c * SEL_CHUNK, (c + 1) * SEL_CHUNK)
            lo, hi = _unpack_halves(buf[rows, :])
            sel = sel_scr[rows, :].astype(BF16)
            acc_scr[:, :HALF] += _dot_tn(sel, lo)
            acc_scr[:, HALF:] += _dot_tn(sel, hi)

    out = _layer_norm(ALPHA * x1 + acc_scr[...], g2[...], b2[...])

    @pl.when(t + 1 < n_tiles)
    def _():
        out_m_ref[...] = out

    @pl.when(t + 1 == n_tiles)
    def _():
        out_t_ref[...] = out


def _combine(x1, rank, wt, y, tables, w):
    n = x1.shape[0] // TM
    small = (w["wsg"], w["wsu"], w["wsd"], w["g2"], w["b2"])
    tok = pl.BlockSpec((2 * N_EXPERTS, TM), lambda i, *_: (0, i))
    return pl.pallas_call(
        _combine_kernel,
        grid_spec=pltpu.PrefetchScalarGridSpec(
            num_scalar_prefetch=4, grid=(n,),
            in_specs=[pl.BlockSpec((TM, D_MODEL), lambda i, *_: (i, 0)), tok, tok,
                      pl.BlockSpec(memory_space=pl.ANY)]
                     + [pl.BlockSpec(a.shape, lambda i, *_, nd=a.ndim: (0,) * nd) for a in small],
            out_specs=_main_tail_specs(n, D_MODEL),
            scratch_shapes=[pltpu.VMEM((L_MAX, TM), F32), pltpu.VMEM((2, L_MAX, HALF), U32),
                            pltpu.VMEM((TM, D_MODEL), F32), pltpu.SemaphoreType.DMA(())]),
        out_shape=(jax.ShapeDtypeStruct((x1.shape[0] - TM, D_MODEL), F32),
                   jax.ShapeDtypeStruct((TM, D_MODEL), F32)),
        compiler_params=_params("arbitrary"), name="combine",
    )(tables["run_off"], tables["run_len"], tables["n_gran"], tables["gran_dst"], x1, rank, wt, y, *small)


def _moe(x1, w, w_e_gate, w_e_up, w_e_down, layer):
    n_blocks = _n_row_blocks(x1.shape[0])
    wt, rank, cnt = _router(x1, w)
    tables = _dispatch_tables(cnt[:, :N_EXPERTS, 0], n_blocks)
    xs = _dispatch(x1, rank, tables, n_blocks)
    y = _experts(xs, tables, w_e_gate, w_e_up, w_e_down, layer, n_blocks)
    return _combine(x1, rank, wt, y, tables, w)


def _layer_weights(l, w_in, w_gk2, b_gk, gla_norm_g, w_pool, pool_scale, w_br_a, w_br_b, w_out, ln1_g, ln1_b,
                   w_router, b_router, w_sh_gate, w_sh_up, w_sh_down, ln2_g, ln2_b):
    wi = w_in[l].astype(BF16)
    o = 0
    pieces = {}
    for name, width in (("wq", KEY), ("wk", KEY), ("wv", VAL), ("wr", VAL), ("wgk", GATE_RANK),
                        ("wu", POOL_WIDTH), ("wga", D_MODEL), ("wgb", D_MODEL)):
        pieces[name] = wi[:, o:o + width]
        o += width
    pad_rank = LANES - GATE_RANK
    pieces["wgk"] = jnp.pad(pieces["wgk"], ((0, 0), (0, pad_rank)))
    pieces["wgk2"] = jnp.pad(w_gk2[l].astype(BF16), ((0, pad_rank), (0, 0)))
    pieces["bgk"] = b_gk[l].reshape(1, KEY)
    pieces["gn"] = gla_norm_g[l].reshape(1, DV)
    pieces["wbra"] = w_br_a[l].astype(BF16)
    pieces["wpool"] = w_pool[l].astype(BF16)
    pieces["pscale"] = pool_scale[l].reshape(1, POOL_WIDTH)
    pieces["wbrb"] = w_br_b[l].astype(BF16)
    pieces["wout"] = w_out[l].astype(BF16)
    pieces["g1"] = ln1_g[l].reshape(1, D_MODEL)
    pieces["b1"] = ln1_b[l].reshape(1, D_MODEL)
    wrt = jnp.pad(w_router[l].T, ((0, N_EXPERTS), (0, 0)))
    wrh = wrt.astype(BF16)
    pieces["wrh"] = wrh
    pieces["wrl"] = (wrt - wrh.astype(F32)).astype(BF16)
    pieces["br"] = jnp.pad(b_router[l], (0, N_EXPERTS)).reshape(2 * N_EXPERTS, 1)
    pieces["wsg"] = w_sh_gate[l].astype(BF16)
    pieces["wsu"] = w_sh_up[l].astype(BF16)
    pieces["wsd"] = w_sh_down[l].astype(BF16)
    pieces["g2"] = ln2_g[l].reshape(1, D_MODEL)
    pieces["b2"] = ln2_b[l].reshape(1, D_MODEL)
    return pieces


def kernel(x_prompt, x_sample, state_gla, cache_pool, w_in, w_gk2, b_gk, gla_norm_g, w_pool, pool_scale, w_br_a, w_br_b, w_out, ln1_g, ln1_b, w_router, b_router, w_e_gate, w_e_up, w_e_down, w_sh_gate, w_sh_up, w_sh_down, ln2_g, ln2_b):
    assert T_SAMPLE == TM, "the decode tokens must fill exactly one token tile"
    xm = x_prompt.reshape(T_PROMPT, D_MODEL)
    xt = x_sample.reshape(T_SAMPLE, D_MODEL)
    n_prompt_tiles = T_PROMPT // TM
    sample_block0 = T_PROMPT // DEC_SEQ
    sp, hp, ss, hs = [], [], [], []
    for l in range(DEPTH):
        w = _layer_weights(l, w_in, w_gk2, b_gk, gla_norm_g, w_pool, pool_scale, w_br_a, w_br_b, w_out, ln1_g,
                           ln1_b, w_router, b_router, w_sh_gate, w_sh_up, w_sh_down, ln2_g, ln2_b)
        q, k, v, r, lf, u, ga, gb = _inproj(xm, xt, w)

        og_p, st_p = _gla(q, k, v, lf, r, w["gn"], None, tile=TM, chunk=CHUNK, n_tiles=n_prompt_tiles, block0=0)
        s0t = jnp.swapaxes(state_gla[l], -1, -2)
        og_s, st_s = _gla(q, k, v, lf, r, w["gn"], s0t, tile=DEC_SEQ, chunk=DEC_SEQ, n_tiles=DEC_BATCH,
                          block0=sample_block0)

        u_p = u[:T_PROMPT].reshape(n_prompt_tiles, TM, POOL_WIDTH)
        halo_p = jnp.concatenate([jnp.zeros((1, HALO, POOL_WIDTH), F32), u_p[:-1, TM - HALO:, :]], axis=0)
        halo_s = jnp.concatenate([jnp.zeros((DEC_BATCH, HALO - POOL_HIST, POOL_WIDTH), F32), cache_pool[l]], axis=1)
        x1 = _mix(xm, xt, og_p, og_s, u, halo_p, halo_s, ga, gb, w)

        xm, xt = _moe(x1, w, w_e_gate, w_e_up, w_e_down, l)

        sp.append(jnp.swapaxes(st_p, -1, -2))
        ss.append(jnp.swapaxes(st_s, -1, -2))
        hp.append(u[T_PROMPT - POOL_HIST:T_PROMPT].reshape(1, POOL_HIST, POOL_WIDTH))
        hs.append(u[T_PROMPT:].reshape(DEC_BATCH, DEC_SEQ, POOL_WIDTH)[:, DEC_SEQ - POOL_HIST:, :])
    y_prompt = xm.reshape(1, SEQ, D_MODEL)
    y_sample = xt.reshape(DEC_BATCH, DEC_SEQ, D_MODEL)
    return (y_prompt, y_sample, jnp.stack(sp), jnp.stack(hp), jnp.stack(ss), jnp.stack(hs))
```

```python
import functools

import jax
import jax.numpy as jnp
from jax import lax
from jax.experimental import pallas as pl
from jax.experimental.pallas import tpu as pltpu

F32 = jnp.float32
BF16 = jnp.bfloat16
U32 = jnp.uint32
I32 = jnp.int32

D_MODEL = 1024
DEPTH = 2
SEQ = 16384
DEC_BATCH = 8
DEC_SEQ = 32
PAST_LEN = 4096
CHUNK = 64
HEADS = 4
DK = 128
DV = 256
KEY = HEADS * DK
VAL = HEADS * DV
GATE_RANK = 16
GATE_NORMALIZER = 16.0
POOL_WIDTH = 512
POOL_WINDOWS = (2, 4, 8, 16)
POOL_GROUP_DIM = 128
POOL_HIST = 15
N_EXPERTS = 64
GROUP_SIZE = 8
TOPK_GROUPS = 4
TOP_K = 8
D_EXPERT = 256
ROUTED_SCALE = 2.5
ALPHA = (2 * DEPTH) ** 0.25
LN_EPS = 1e-5
RMS_EPS = 1e-6

LANES = 128
SUBLANES = 8
VMEM_LIMIT_BYTES = 56 * 1024 * 1024

T_PROMPT = SEQ
T_SAMPLE = DEC_BATCH * DEC_SEQ
T_ALL = T_PROMPT + T_SAMPLE
TM = 256
HALO = 16
NEG_INF = float("-inf")

HALF = D_MODEL // 2
GRANULE = SUBLANES
ROW_BLOCK = 512
COPY_GROUP = 8
RING = 3
RUN_CHUNK = 64
L_MAX = 2560
SEL_CHUNK = 512
HI_MASK = 0xFFFF0000


def _dot(a, b):
    return jnp.dot(a, b, preferred_element_type=F32)


def _dot_nt(a, b):
    return lax.dot_general(a, b, (((1,), (1,)), ((), ())), preferred_element_type=F32)


def _dot_tn(a, b):
    return lax.dot_general(a, b, (((0,), (0,)), ((), ())), preferred_element_type=F32)


def _sigmoid(x):
    return 1.0 / (1.0 + jnp.exp(-x))


def _silu(x):
    return x * _sigmoid(x)


def _layer_norm(x, g, b):
    mu = jnp.mean(x, axis=-1, keepdims=True)
    xc = x - mu
    var = jnp.mean(xc * xc, axis=-1, keepdims=True)
    return xc * lax.rsqrt(var + LN_EPS) * g + b


def _pack_halves(v):
    bits = lax.bitcast_convert_type(v, U32)
    return (bits[:, HALF:] & jnp.uint32(HI_MASK)) | (bits[:, :HALF] >> 16)


def _unpack_halves(w):
    lo = lax.bitcast_convert_type(w << 16, F32).astype(BF16)
    hi = lax.bitcast_convert_type(w & jnp.uint32(HI_MASK), F32).astype(BF16)
    return lo, hi


def _params(*sem):
    return pltpu.CompilerParams(dimension_semantics=sem, vmem_limit_bytes=VMEM_LIMIT_BYTES)


def _const_spec(shape):
    nd = len(shape)
    return pl.BlockSpec(shape, lambda *_: (0,) * nd)


def _is_tail_tile():
    return pl.program_id(0) == pl.num_programs(0) - 1


def _main_tail_specs(n_tiles, width):
    return [pl.BlockSpec((TM, width), lambda i, *_: (jnp.minimum(i, n_tiles - 2), 0)),
            pl.BlockSpec((TM, width), lambda i, *_: (0, 0))]


def _inproj_kernel(xm_ref, xt_ref, wq, wk, wv, wr, wgk, wgk2, bgk, wu, wga, wgb,
                   q_o, k_o, v_o, r_o, lf_o, u_o, ga_o, gb_o):
    xb = jnp.where(_is_tail_tile(), xt_ref[...], xm_ref[...]).astype(BF16)
    q_o[...] = _dot(xb, wq[...]) * (DK ** -0.5)
    k_o[...] = _dot(xb, wk[...])
    v_o[...] = _dot(xb, wv[...]).astype(BF16)
    r_o[...] = _dot(xb, wr[...])
    gk = _dot(xb, wgk[...]).astype(BF16)
    z = _dot(gk, wgk2[...]) + bgk[...]
    log_sig = jnp.minimum(z, 0.0) - jnp.log1p(jnp.exp(-jnp.abs(z)))
    lf_o[...] = log_sig * (1.0 / GATE_NORMALIZER)
    u_o[...] = _dot(xb, wu[...])
    ga_o[...] = _dot(xb, wga[...])
    gb_o[...] = _dot(xb, wgb[...])


def _inproj(xm, xt, w):
    n = xm.shape[0] // TM + 1
    rows = n * TM
    row = lambda width: pl.BlockSpec((TM, width), lambda i: (i, 0))
    out_shapes = (
        jax.ShapeDtypeStruct((rows, KEY), F32), jax.ShapeDtypeStruct((rows, KEY), F32),
        jax.ShapeDtypeStruct((rows, VAL), BF16), jax.ShapeDtypeStruct((rows, VAL), F32),
        jax.ShapeDtypeStruct((rows, KEY), F32), jax.ShapeDtypeStruct((rows, POOL_WIDTH), F32),
        jax.ShapeDtypeStruct((rows, D_MODEL), F32), jax.ShapeDtypeStruct((rows, D_MODEL), F32))
    weights = (w["wq"], w["wk"], w["wv"], w["wr"], w["wgk"], w["wgk2"], w["bgk"], w["wu"], w["wga"], w["wgb"])
    return pl.pallas_call(
        _inproj_kernel, grid=(n,),
        in_specs=_main_tail_specs(n, D_MODEL) + [_const_spec(a.shape) for a in weights],
        out_specs=[row(KEY), row(KEY), row(VAL), row(VAL), row(KEY), row(POOL_WIDTH), row(D_MODEL), row(D_MODEL)],
        out_shape=out_shapes, compiler_params=_params("arbitrary"), name="inproj",
    )(xm, xt, *weights)


def _gla_kernel(*refs, chunk, n_chunks, has_init):
    if has_init:
        q_ref, k_ref, v_ref, lf_ref, r_ref, gn_ref, s0_ref, o_ref, st_ref = refs
        st_ref[...] = s0_ref[...]
    else:
        q_ref, k_ref, v_ref, lf_ref, r_ref, gn_ref, o_ref, st_ref = refs

        @pl.when(pl.program_id(0) == 0)
        def _():
            st_ref[...] = jnp.zeros_like(st_ref)

    row = lax.broadcasted_iota(I32, (chunk, KEY), 0)
    ta = lax.broadcasted_iota(I32, (chunk, chunk), 0)
    sa = lax.broadcasted_iota(I32, (chunk, chunk), 1)
    gn = gn_ref[...]

    def chunk_body(c, carry):
        rows = pl.ds(pl.multiple_of(c * chunk, chunk), chunk)
        lf = lf_ref[rows, :]
        q = q_ref[rows, :]
        k = k_ref[rows, :]
        qb = q.astype(BF16)
        kb = k.astype(BF16)
        att = []
        for h in range(HEADS):
            hs = slice(h * DK, (h + 1) * DK)
            att.append(jnp.where(ta == sa, _dot_nt(qb[:, hs], kb[:, hs]), 0.0))
        seg, tot = lf, lf
        half = 1
        while half < chunk:
            qs = (q * jnp.exp(seg)).astype(BF16)
            ks = (k * jnp.exp(tot - seg)).astype(BF16)
            pair = ((ta ^ sa) < 2 * half) & ((ta & half) != 0) & ((sa & half) == 0)
            for h in range(HEADS):
                hs = slice(h * DK, (h + 1) * DK)
                att[h] = jnp.where(pair, _dot_nt(qs[:, hs], ks[:, hs]), att[h])
            upper = (row & half) != 0
            below = pltpu.roll(tot, half, 0)
            above = pltpu.roll(tot, chunk - half, 0)
            seg = seg + jnp.where(upper, below, 0.0)
            tot = tot + jnp.where(upper, below, above)
            half *= 2
        q_in = (q * jnp.exp(seg)).astype(BF16)
        k_out = (k * jnp.exp(tot - seg)).astype(BF16)
        decay = jnp.exp(tot[0:1, :])
        for h in range(HEADS):
            hs = slice(h * DK, (h + 1) * DK)
            vs = slice(h * DV, (h + 1) * DV)
            state = st_ref[0, h]
            vh = v_ref[rows, vs]
            o = _dot_nt(q_in[:, hs], state.astype(BF16)) + _dot(att[h].astype(BF16), vh)
            st_ref[0, h] = state * decay[:, hs] + _dot_tn(vh, k_out[:, hs])
            ms = jnp.mean(o * o, axis=-1, keepdims=True)
            o = o * lax.rsqrt(ms + RMS_EPS) * gn
            o_ref[rows, vs] = (o * _silu(r_ref[rows, vs])).astype(BF16)
        return carry

    lax.fori_loop(0, n_chunks, chunk_body, 0)


def _gla(q, k, v, lf, r, gn, s0t, *, tile, chunk, n_tiles, block0):
    has_init = s0t is not None
    row = lambda width: pl.BlockSpec((tile, width), lambda i: (block0 + i, 0))
    st_spec = pl.BlockSpec((1, HEADS, DV, DK), (lambda i: (i, 0, 0, 0)) if has_init else (lambda i: (0, 0, 0, 0)))
    n_states = n_tiles if has_init else 1
    in_specs = [row(KEY), row(KEY), row(VAL), row(KEY), row(VAL), _const_spec(gn.shape)]
    args = [q, k, v, lf, r, gn]
    if has_init:
        in_specs.append(st_spec)
        args.append(s0t)
    return pl.pallas_call(
        functools.partial(_gla_kernel, chunk=chunk, n_chunks=tile // chunk, has_init=has_init),
        grid=(n_tiles,), in_specs=in_specs,
        out_specs=[pl.BlockSpec((tile, VAL), lambda i: (i, 0)), st_spec],
        out_shape=(jax.ShapeDtypeStruct((n_tiles * tile, VAL), BF16),
                   jax.ShapeDtypeStruct((n_states, HEADS, DV, DK), F32)),
        compiler_params=_params("arbitrary"), name="gla_init" if has_init else "gla",
    )(*args)


def _window_sums(ext, window):
    shift = 1
    while shift < window:
        ext = ext + pltpu.roll(ext, shift, 0)
        shift *= 2
    return ext


def _mix_kernel(xm_ref, xt_ref, ogm_ref, ogt_ref, u_ref, halo_m_ref, halo_t_ref, ga_ref, gb_ref,
                wbra, wpool, pscale, wbrb, wout, g1, b1, x1_ref):
    tail = _is_tail_tile()
    ya = _dot(jnp.where(tail, ogt_ref[...], ogm_ref[...]), wbra[...])
    u = u_ref[...]
    ext_m = jnp.concatenate([halo_m_ref[0], u], axis=0)
    seg = HALO + DEC_SEQ
    pieces = []
    for b in range(DEC_BATCH):
        pieces += [halo_t_ref[b], u[b * DEC_SEQ:(b + 1) * DEC_SEQ, :]]
    ext_t = jnp.concatenate(pieces, axis=0)
    rowi = lax.broadcasted_iota(I32, (TM, 1), 0)
    pos = jnp.where(tail, PAST_LEN + (rowi & (DEC_SEQ - 1)), pl.program_id(0) * TM + rowi)
    parts = []
    for g, window in enumerate(POOL_WINDOWS):
        cols = slice(g * POOL_GROUP_DIM, (g + 1) * POOL_GROUP_DIM)
        win_m = _window_sums(ext_m[:, cols], window)[HALO:, :]
        win_all = _window_sums(ext_t[:, cols], window)
        win_t = jnp.concatenate([win_all[b * seg + HALO:(b + 1) * seg, :] for b in range(DEC_BATCH)], axis=0)
        cnt = jnp.minimum(window, pos + 1).astype(F32)
        d = jnp.where(tail, win_t, win_m) / cnt - u[:, cols]
        parts.append(_dot(d.astype(BF16), wpool[g]))
    yb_in = jnp.concatenate(parts, axis=1) * pscale[...]
    yb = _dot(yb_in.astype(BF16), wbrb[...])
    mixed = _sigmoid(ga_ref[...]) * ya + _sigmoid(gb_ref[...]) * yb
    mix = _dot(mixed.astype(BF16), wout[...])
    x = jnp.where(tail, xt_ref[...], xm_ref[...])
    x1_ref[...] = _layer_norm(ALPHA * x + mix, g1[...], b1[...])


def _mix(xm, xt, og_m, og_t, u, halo_m, halo_t, ga, gb, w):
    n = xm.shape[0] // TM + 1
    row = lambda width: pl.BlockSpec((TM, width), lambda i: (i, 0))
    weights = (w["wbra"], w["wpool"], w["pscale"], w["wbrb"], w["wout"], w["g1"], w["b1"])
    return pl.pallas_call(
        _mix_kernel, grid=(n,),
        in_specs=_main_tail_specs(n, D_MODEL) + _main_tail_specs(n, VAL) + [
            row(POOL_WIDTH), pl.BlockSpec((1, HALO, POOL_WIDTH), lambda i: (jnp.minimum(i, n - 2), 0, 0)),
            _const_spec(halo_t.shape), row(D_MODEL), row(D_MODEL)] + [_const_spec(a.shape) for a in weights],
        out_specs=row(D_MODEL),
        out_shape=jax.ShapeDtypeStruct((n * TM, D_MODEL), F32),
        compiler_params=_params("arbitrary"), name="mix",
    )(xm, xt, og_m, og_t, u, halo_m, halo_t, ga, gb, *weights)


def _router_kernel(x1_ref, wrh_ref, wrl_ref, br_ref, wt_ref, rank_ref, cnt_ref, *, tile):
    x = x1_ref[...]
    xh = x.astype(BF16)
    xl = (x - xh.astype(F32)).astype(BF16)
    logits = _dot_nt(wrh_ref[...], xh) + _dot_nt(wrl_ref[...], xh) + _dot_nt(wrh_ref[...], xl)
    scores = _sigmoid(logits)
    rows_pad = 2 * N_EXPERTS
    n_grp = rows_pad // GROUP_SIZE
    erow = lax.broadcasted_iota(I32, (rows_pad, tile), 0)
    biased = jnp.where(erow < N_EXPERTS, scores + br_ref[...], NEG_INF)
    shape3 = (n_grp, GROUP_SIZE, tile)
    b3 = biased.reshape(shape3)
    s3 = scores.reshape(shape3)
    sub = lax.broadcasted_iota(I32, shape3, 1)
    gid = lax.broadcasted_iota(I32, shape3, 0)
    eid = gid * GROUP_SIZE + sub
    m1 = jnp.max(b3, axis=1, keepdims=True)
    i1 = jnp.min(jnp.where(b3 == m1, sub, GROUP_SIZE), axis=1, keepdims=True)
    m2 = jnp.max(jnp.where(sub == i1, NEG_INF, b3), axis=1, keepdims=True)
    gscore = m1 + m2
    gid1 = lax.broadcasted_iota(I32, (n_grp, 1, tile), 0)
    gsel = jnp.zeros((n_grp, 1, tile), jnp.bool_)
    for _ in range(TOPK_GROUPS):
        gm = jnp.max(gscore, axis=0, keepdims=True)
        gi = jnp.min(jnp.where(gscore == gm, gid1, n_grp), axis=0, keepdims=True)
        pick = gid1 == gi
        gsel = gsel | pick
        gscore = jnp.where(pick, NEG_INF, gscore)
    masked = jnp.where(gsel, b3, NEG_INF)
    wsel = jnp.zeros(shape3, F32)
    chosen = jnp.zeros(shape3, jnp.bool_)
    for _ in range(TOP_K):
        m = jnp.max(jnp.max(masked, axis=1, keepdims=True), axis=0, keepdims=True)
        idx = jnp.min(jnp.min(jnp.where(masked == m, eid, rows_pad), axis=1, keepdims=True), axis=0, keepdims=True)
        pick = eid == idx
        wsel = jnp.where(pick, s3, wsel)
        chosen = chosen | pick
        masked = jnp.where(pick, NEG_INF, masked)
    wsum = jnp.sum(jnp.sum(wsel, axis=1, keepdims=True), axis=0, keepdims=True)
    wt_ref[...] = (wsel / wsum * ROUTED_SCALE).reshape(rows_pad, tile)
    sel = jnp.where(chosen, 1.0, 0.0).reshape(rows_pad, tile)
    before = lax.broadcasted_iota(I32, (tile, tile), 0) < lax.broadcasted_iota(I32, (tile, tile), 1)
    rank = _dot(sel.astype(BF16), jnp.where(before, 1.0, 0.0).astype(BF16))
    rank_ref[...] = jnp.where(sel > 0.0, rank, -1.0).astype(I32)
    cnt = jnp.sum(sel, axis=1, keepdims=True).astype(I32)
    cnt_ref[0] = jnp.broadcast_to(cnt, (rows_pad, LANES))


def _router(x1, w):
    n = x1.shape[0] // TM
    rows_pad = 2 * N_EXPERTS
    tok = pl.BlockSpec((rows_pad, TM), lambda i: (0, i))
    return pl.pallas_call(
        functools.partial(_router_kernel, tile=TM), grid=(n,),
        in_specs=[pl.BlockSpec((TM, D_MODEL), lambda i: (i, 0)), _const_spec(w["wrh"].shape),
                  _const_spec(w["wrl"].shape), _const_spec(w["br"].shape)],
        out_specs=[tok, tok, pl.BlockSpec((1, rows_pad, LANES), lambda i: (i, 0, 0))],
        out_shape=(jax.ShapeDtypeStruct((rows_pad, x1.shape[0]), F32),
                   jax.ShapeDtypeStruct((rows_pad, x1.shape[0]), I32),
                   jax.ShapeDtypeStruct((n, rows_pad, LANES), I32)),
        compiler_params=_params("arbitrary"), name="router",
    )(x1, w["wrh"], w["wrl"], w["br"])


def _n_row_blocks(n_tokens):
    n_tiles = n_tokens // TM
    worst_rows = n_tokens * TOP_K + n_tiles * N_EXPERTS * (GRANULE - 1) + N_EXPERTS * (ROW_BLOCK - 1)
    return -(-worst_rows // ROW_BLOCK)


def _dispatch_tables(cnt, n_blocks):
    padded = (cnt + (GRANULE - 1)) // GRANULE * GRANULE
    run_end = jnp.cumsum(padded, axis=1)
    run_off = run_end - padded
    n_gran = run_end[:, -1] // GRANULE
    rows_e = jnp.sum(padded, axis=0)
    region = (rows_e + (ROW_BLOCK - 1)) // ROW_BLOCK * ROW_BLOCK
    region_end = jnp.cumsum(region)
    region_start = region_end - region
    run_base = region_start[None, :] + jnp.cumsum(padded, axis=0) - padded
    gran = jnp.arange(L_MAX // GRANULE, dtype=I32)
    shift = (run_base - run_off) // GRANULE
    step = shift - jnp.concatenate([jnp.zeros_like(shift[:, :1]), shift[:, :-1]], axis=1)
    run_first = run_off // GRANULE
    owned = gran[None, :, None] >= run_first[:, None, :]
    gran_dst = gran[None, :] + jnp.sum(jnp.where(owned, step[:, None, :], 0), axis=2)
    n_used = region_end[-1] // ROW_BLOCK
    blocks = jnp.arange(n_blocks, dtype=I32)
    block_expert = jnp.minimum(jnp.sum(blocks[:, None] >= (region_end // ROW_BLOCK)[None, :], axis=1), N_EXPERTS - 1)
    tail = (region - rows_e) // GRANULE
    tail_end = jnp.cumsum(tail)
    tail_first = tail_end - tail
    slot = jnp.arange(N_EXPERTS * (ROW_BLOCK // GRANULE - 1), dtype=I32)
    t_shift = (region_start + rows_e) // GRANULE - tail_first
    t_step = t_shift - jnp.concatenate([jnp.zeros_like(t_shift[:1]), t_shift[:-1]])
    tail_dst = slot + jnp.sum(jnp.where(slot[:, None] >= tail_first[None, :], t_step[None, :], 0), axis=1)
    as_i32 = lambda a: a.astype(I32).reshape(-1)
    long_run = jnp.any(padded > RUN_CHUNK, axis=1)
    return dict(run_off=as_i32(run_off), run_len=as_i32(padded), long_run=as_i32(long_run),
                n_gran=as_i32(n_gran), gran_dst=as_i32(gran_dst),
                n_used=as_i32(n_used), block_expert=as_i32(block_expert), n_tail=as_i32(tail_end[-1]),
                tail_dst=as_i32(tail_dst))


def _granule_copy(src, src_gran, dst, dst_gran, sem, n=1):
    s = pl.multiple_of(src_gran * GRANULE, GRANULE)
    d = pl.multiple_of(dst_gran * GRANULE, GRANULE)
    return pltpu.make_async_copy(src.at[pl.ds(s, n * GRANULE), :], dst.at[pl.ds(d, n * GRANULE), :], sem)


def _start_granule_copies(n_gran, make_copy):
    n_groups = n_gran // COPY_GROUP

    def group(j, carry):
        for i in range(COPY_GROUP):
            make_copy(j * COPY_GROUP + i).start()
        return carry

    def single(g, carry):
        make_copy(g).start()
        return carry

    lax.fori_loop(0, n_groups, group, 0)
    lax.fori_loop(n_groups * COPY_GROUP, n_gran, single, 0)


def _wait_granule_copies(n_gran, src, dst, sem):
    n_groups = n_gran // COPY_GROUP

    def group(j, carry):
        _granule_copy(src, 0, dst, 0, sem, COPY_GROUP).wait()
        return carry

    def single(g, carry):
        _granule_copy(src, 0, dst, 0, sem).wait()
        return carry

    lax.fori_loop(0, n_groups, group, 0)
    lax.fori_loop(n_groups * COPY_GROUP, n_gran, single, 0)


def _build_selection(sel_scr, rank, values, run_off_ref, run_len_ref, long_run_ref, tile_idx):
    slot = lax.broadcasted_iota(I32, (RUN_CHUNK, TM), 0)

    def write_runs(all_chunks):
        for e in range(N_EXPERTS):
            off = run_off_ref[tile_idx * N_EXPERTS + e]
            rank_e = rank[e:e + 1, :]
            val_e = 1.0 if values is None else values[e:e + 1, :]

            def chunk(c, carry, off=off, rank_e=rank_e, val_e=val_e):
                hit = rank_e == slot + c * RUN_CHUNK
                rows = pl.ds(pl.multiple_of(off + c * RUN_CHUNK, GRANULE), RUN_CHUNK)
                sel_scr[rows, :] = jnp.where(hit, val_e, 0.0)
                return carry

            chunk(0, 0)
            if all_chunks:
                n_chunks = (run_len_ref[tile_idx * N_EXPERTS + e] + (RUN_CHUNK - 1)) // RUN_CHUNK
                lax.fori_loop(1, n_chunks, chunk, 0)

    write_runs(False)

    @pl.when(long_run_ref[tile_idx] != 0)
    def _():
        write_runs(True)


def _dispatch_kernel(run_off, run_len, long_run, n_gran, gran_dst, n_tail, tail_dst, n_used,
                     x1_ref, rank_ref, xs_hbm, sel_scr, buf_scr, zero_scr, sem):
    t = pl.program_id(0)

    @pl.when(t == 0)
    def _():
        sel_scr[...] = jnp.zeros_like(sel_scr)
        buf_scr[...] = jnp.zeros_like(buf_scr)
        zero_scr[...] = jnp.zeros_like(zero_scr)

    _build_selection(sel_scr, rank_ref[...], None, run_off, run_len, long_run, t)
    xb = x1_ref[...].astype(BF16)
    n_g = n_gran[t]
    buf = buf_scr.at[t % 2]
    for c in range(L_MAX // SEL_CHUNK):
        @pl.when(c * SEL_CHUNK < n_g * GRANULE)
        def _(c=c):
            rows = slice(c * SEL_CHUNK, (c + 1) * SEL_CHUNK)
            buf[rows, :] = _pack_halves(_dot(sel_scr[rows, :].astype(BF16), xb))

    @pl.when(t > 0)
    def _():
        _wait_granule_copies(n_gran[jnp.maximum(t - 1, 0)], buf, xs_hbm, sem)

    _start_granule_copies(
        n_g, lambda g: _granule_copy(buf, g, xs_hbm, gran_dst[t * (L_MAX // GRANULE) + g], sem))

    @pl.when(t == pl.num_programs(0) - 1)
    def _():
        _wait_granule_copies(n_g, buf, xs_hbm, sem)
        _start_granule_copies(n_tail[0], lambda i: _granule_copy(zero_scr, 0, xs_hbm, tail_dst[i], sem))
        _wait_granule_copies(n_tail[0], zero_scr, xs_hbm, sem)

        def spare_copy(b):
            rows = pl.ds(pl.multiple_of(b * ROW_BLOCK, ROW_BLOCK), ROW_BLOCK)
            return pltpu.make_async_copy(zero_scr, xs_hbm.at[rows, :], sem)

        def start_spare(b, carry):
            spare_copy(b).start()
            return carry

        def wait_spare(b, carry):
            spare_copy(b).wait()
            return carry

        n_blocks = xs_hbm.shape[0] // ROW_BLOCK
        lax.fori_loop(n_used[0], n_blocks, start_spare, 0)
        lax.fori_loop(n_used[0], n_blocks, wait_spare, 0)


def _dispatch(x1, rank, tables, n_blocks):
    n = x1.shape[0] // TM
    return pl.pallas_call(
        _dispatch_kernel,
        grid_spec=pltpu.PrefetchScalarGridSpec(
            num_scalar_prefetch=8, grid=(n,),
            in_specs=[pl.BlockSpec((TM, D_MODEL), lambda i, *_: (i, 0)),
                      pl.BlockSpec((2 * N_EXPERTS, TM), lambda i, *_: (0, i))],
            out_specs=pl.BlockSpec(memory_space=pl.ANY),
            scratch_shapes=[pltpu.VMEM((L_MAX, TM), F32), pltpu.VMEM((2, L_MAX, HALF), U32),
                            pltpu.VMEM((ROW_BLOCK, HALF), U32), pltpu.SemaphoreType.DMA(())]),
        out_shape=jax.ShapeDtypeStruct((n_blocks * ROW_BLOCK, HALF), U32),
        compiler_params=_params("arbitrary"), name="dispatch",
    )(tables["run_off"], tables["run_len"], tables["long_run"], tables["n_gran"], tables["gran_dst"], tables["n_tail"],
      tables["tail_dst"], tables["n_used"], x1, rank)


def _experts_kernel(block_expert, n_used, xs_hbm, wg_ref, wu_ref, wd_ref, y_ref,
                    x_ring, wgu_scr, wd_scr, sems):
    b = pl.program_id(0)
    n_u = n_used[0]

    def fetch(block):
        slot = block % RING
        rows = pl.ds(pl.multiple_of(block * ROW_BLOCK, ROW_BLOCK), ROW_BLOCK)
        return pltpu.make_async_copy(xs_hbm.at[rows, :], x_ring.at[slot], sems.at[slot])

    @pl.when(b == 0)
    def _():
        for first in range(RING - 1):
            @pl.when(first < n_u)
            def _(first=first):
                fetch(first).start()

    @pl.when(b + (RING - 1) < n_u)
    def _():
        fetch(b + (RING - 1)).start()

    @pl.when(b >= n_u)
    def _():
        y_ref[...] = jnp.zeros_like(y_ref)

    @pl.when(b < n_u)
    def _():
        fresh = (b == 0) | (block_expert[b] != block_expert[jnp.maximum(b - 1, 0)])

        @pl.when(fresh)
        def _():
            wgu_scr[:, :D_EXPERT] = wg_ref[...].astype(BF16)
            wgu_scr[:, D_EXPERT:] = wu_ref[...].astype(BF16)
            wd_scr[...] = wd_ref[...].astype(BF16)

        fetch(b).wait()
        lo, hi = _unpack_halves(x_ring[b % RING])
        gu = _dot(lo, wgu_scr[:HALF, :]) + _dot(hi, wgu_scr[HALF:, :])
        hid = (_silu(gu[:, :D_EXPERT]) * gu[:, D_EXPERT:]).astype(BF16)
        y = _dot(hid, wd_scr[...])
        y_ref[...] = _pack_halves(y.astype(BF16).astype(F32))


def _experts(xs, tables, w_e_gate, w_e_up, w_e_down, layer, n_blocks):
    wmap = lambda b, be, nu: (layer, be[b], 0, 0)
    return pl.pallas_call(
        _experts_kernel,
        grid_spec=pltpu.PrefetchScalarGridSpec(
            num_scalar_prefetch=2, grid=(n_blocks,),
            in_specs=[pl.BlockSpec(memory_space=pl.ANY),
                      pl.BlockSpec((None, None, D_MODEL, D_EXPERT), wmap),
                      pl.BlockSpec((None, None, D_MODEL, D_EXPERT), wmap),
                      pl.BlockSpec((None, None, D_EXPERT, D_MODEL), wmap)],
            out_specs=pl.BlockSpec((ROW_BLOCK, HALF), lambda b, be, nu: (b, 0)),
            scratch_shapes=[pltpu.VMEM((RING, ROW_BLOCK, HALF), U32),
                            pltpu.VMEM((D_MODEL, 2 * D_EXPERT), BF16),
                            pltpu.VMEM((D_EXPERT, D_MODEL), BF16), pltpu.SemaphoreType.DMA((RING,))]),
        out_shape=jax.ShapeDtypeStruct(xs.shape, U32),
        compiler_params=_params("arbitrary"), name="experts",
    )(tables["block_expert"], tables["n_used"], xs, w_e_gate, w_e_up, w_e_down)


def _combine_kernel(run_off, run_len, long_run, n_gran, gran_dst,
                    x1_ref, rank_ref, wt_ref, y_hbm, wsg, wsu, wsd, g2, b2, out_m_ref, out_t_ref,
                    sel_scr, buf_scr, acc_scr, sem):
    t = pl.program_id(0)
    n_tiles = pl.num_programs(0)
    n_g = n_gran[t]

    def fetch(tile):
        dst = buf_scr.at[tile % 2]
        _start_granule_copies(
            n_gran[tile],
            lambda g: _granule_copy(y_hbm, gran_dst[tile * (L_MAX // GRANULE) + g], dst, g, sem))

    @pl.when(t == 0)
    def _():
        buf_scr[...] = jnp.zeros_like(buf_scr)
        fetch(t)

    stale = pl.multiple_of(jnp.minimum(n_g * GRANULE, L_MAX - SEL_CHUNK), GRANULE)
    sel_scr[pl.ds(stale, SEL_CHUNK), :] = jnp.zeros((SEL_CHUNK, TM), F32)
    _build_selection(sel_scr, rank_ref[...], wt_ref[...], run_off, run_len, long_run, t)
    x1 = x1_ref[...]
    xb = x1.astype(BF16)
    hs = (_silu(_dot(xb, wsg[...])) * _dot(xb, wsu[...])).astype(BF16)
    acc_scr[...] = _dot(hs, wsd[...])
    buf = buf_scr.at[t % 2]
    _wait_granule_copies(n_g, y_hbm, buf, sem)

    @pl.when(t + 1 < n_tiles)
    def _():
        fetch(jnp.minimum(t + 1, n_tiles - 1))

    for c in range(L_MAX // SEL_CHUNK):
        @pl.when(c * SEL_CHUNK < n_g * GRANULE)
        def _(c=c):
            rows = slice(c * SEL_CHUNK, (c + 1) * SEL_CHUNK)
            lo, hi = _unpack_halves(buf[rows, :])
            sel = sel_scr[rows, :].astype(BF16)
            acc_scr[:, :HALF] += _dot_tn(sel, lo)
            acc_scr[:, HALF:] += _dot_tn(sel, hi)

    out = _layer_norm(ALPHA * x1 + acc_scr[...], g2[...], b2[...])

    @pl.when(t + 1 < n_tiles)
    def _():
        out_m_ref[...] = out

    @pl.when(t + 1 == n_tiles)
    def _():
        out_t_ref[...] = out


def _combine(x1, rank, wt, y, tables, w):
    n = x1.shape[0] // TM
    small = (w["wsg"], w["wsu"], w["wsd"], w["g2"], w["b2"])
    tok = pl.BlockSpec((2 * N_EXPERTS, TM), lambda i, *_: (0, i))
    return pl.pallas_call(
        _combine_kernel,
        grid_spec=pltpu.PrefetchScalarGridSpec(
            num_scalar_prefetch=5, grid=(n,),
            in_specs=[pl.BlockSpec((TM, D_MODEL), lambda i, *_: (i, 0)), tok, tok,
                      pl.BlockSpec(memory_space=pl.ANY)]
                     + [pl.BlockSpec(a.shape, lambda i, *_, nd=a.ndim: (0,) * nd) for a in small],
            out_specs=_main_tail_specs(n, D_MODEL),
            scratch_shapes=[pltpu.VMEM((L_MAX, TM), F32), pltpu.VMEM((2, L_MAX, HALF), U32),
                            pltpu.VMEM((TM, D_MODEL), F32), pltpu.SemaphoreType.DMA(())]),
        out_shape=(jax.ShapeDtypeStruct((x1.shape[0] - TM, D_MODEL), F32),
                   jax.ShapeDtypeStruct((TM, D_MODEL), F32)),
        compiler_params=_params("arbitrary"), name="combine",
    )(tables["run_off"], tables["run_len"], tables["long_run"], tables["n_gran"], tables["gran_dst"],
      x1, rank, wt, y, *small)


def _moe(x1, w, w_e_gate, w_e_up, w_e_down, layer):
    n_blocks = _n_row_blocks(x1.shape[0])
    wt, rank, cnt = _router(x1, w)
    tables = _dispatch_tables(cnt[:, :N_EXPERTS, 0], n_blocks)
    xs = _dispatch(x1, rank, tables, n_blocks)
    y = _experts(xs, tables, w_e_gate, w_e_up, w_e_down, layer, n_blocks)
    return _combine(x1, rank, wt, y, tables, w)


def _layer_weights(l, w_in, w_gk2, b_gk, gla_norm_g, w_pool, pool_scale, w_br_a, w_br_b, w_out, ln1_g, ln1_b,
                   w_router, b_router, w_sh_gate, w_sh_up, w_sh_down, ln2_g, ln2_b):
    wi = w_in[l].astype(BF16)
    o = 0
    pieces = {}
    for name, width in (("wq", KEY), ("wk", KEY), ("wv", VAL), ("wr", VAL), ("wgk", GATE_RANK),
                        ("wu", POOL_WIDTH), ("wga", D_MODEL), ("wgb", D_MODEL)):
        pieces[name] = wi[:, o:o + width]
        o += width
    pad_rank = LANES - GATE_RANK
    pieces["wgk"] = jnp.pad(pieces["wgk"], ((0, 0), (0, pad_rank)))
    pieces["wgk2"] = jnp.pad(w_gk2[l].astype(BF16), ((0, pad_rank), (0, 0)))
    pieces["bgk"] = b_gk[l].reshape(1, KEY)
    pieces["gn"] = gla_norm_g[l].reshape(1, DV)
    pieces["wbra"] = w_br_a[l].astype(BF16)
    pieces["wpool"] = w_pool[l].astype(BF16)
    pieces["pscale"] = pool_scale[l].reshape(1, POOL_WIDTH)
    pieces["wbrb"] = w_br_b[l].astype(BF16)
    pieces["wout"] = w_out[l].astype(BF16)
    pieces["g1"] = ln1_g[l].reshape(1, D_MODEL)
    pieces["b1"] = ln1_b[l].reshape(1, D_MODEL)
    wrt = jnp.pad(w_router[l].T, ((0, N_EXPERTS), (0, 0)))
    wrh = wrt.astype(BF16)
    pieces["wrh"] = wrh
    pieces["wrl"] = (wrt - wrh.astype(F32)).astype(BF16)
    pieces["br"] = jnp.pad(b_router[l], (0, N_EXPERTS)).reshape(2 * N_EXPERTS, 1)
    pieces["wsg"] = w_sh_gate[l].astype(BF16)
    pieces["wsu"] = w_sh_up[l].astype(BF16)
    pieces["wsd"] = w_sh_down[l].astype(BF16)
    pieces["g2"] = ln2_g[l].reshape(1, D_MODEL)
    pieces["b2"] = ln2_b[l].reshape(1, D_MODEL)
    return pieces


def kernel(x_prompt, x_sample, state_gla, cache_pool, w_in, w_gk2, b_gk, gla_norm_g, w_pool, pool_scale, w_br_a, w_br_b, w_out, ln1_g, ln1_b, w_router, b_router, w_e_gate, w_e_up, w_e_down, w_sh_gate, w_sh_up, w_sh_down, ln2_g, ln2_b):
    assert T_SAMPLE == TM, "the decode tokens must fill exactly one token tile"
    xm = x_prompt.reshape(T_PROMPT, D_MODEL)
    xt = x_sample.reshape(T_SAMPLE, D_MODEL)
    n_prompt_tiles = T_PROMPT // TM
    sample_block0 = T_PROMPT // DEC_SEQ
    sp, hp, ss, hs = [], [], [], []
    for l in range(DEPTH):
        w = _layer_weights(l, w_in, w_gk2, b_gk, gla_norm_g, w_pool, pool_scale, w_br_a, w_br_b, w_out, ln1_g,
                           ln1_b, w_router, b_router, w_sh_gate, w_sh_up, w_sh_down, ln2_g, ln2_b)
        q, k, v, r, lf, u, ga, gb = _inproj(xm, xt, w)

        og_p, st_p = _gla(q, k, v, lf, r, w["gn"], None, tile=TM, chunk=CHUNK, n_tiles=n_prompt_tiles, block0=0)
        s0t = jnp.swapaxes(state_gla[l], -1, -2)
        og_s, st_s = _gla(q, k, v, lf, r, w["gn"], s0t, tile=DEC_SEQ, chunk=DEC_SEQ, n_tiles=DEC_BATCH,
                          block0=sample_block0)

        u_p = u[:T_PROMPT].reshape(n_prompt_tiles, TM, POOL_WIDTH)
        halo_p = jnp.concatenate([jnp.zeros((1, HALO, POOL_WIDTH), F32), u_p[:-1, TM - HALO:, :]], axis=0)
        halo_s = jnp.concatenate([jnp.zeros((DEC_BATCH, HALO - POOL_HIST, POOL_WIDTH), F32), cache_pool[l]], axis=1)
        x1 = _mix(xm, xt, og_p, og_s, u, halo_p, halo_s, ga, gb, w)

        xm, xt = _moe(x1, w, w_e_gate, w_e_up, w_e_down, l)

        sp.append(jnp.swapaxes(st_p, -1, -2))
        ss.append(jnp.swapaxes(st_s, -1, -2))
        hp.append(u[T_PROMPT - POOL_HIST:T_PROMPT].reshape(1, POOL_HIST, POOL_WIDTH))
        hs.append(u[T_PROMPT:].reshape(DEC_BATCH, DEC_SEQ, POOL_WIDTH)[:, DEC_SEQ - POOL_HIST:, :])
    y_prompt = xm.reshape(1, SEQ, D_MODEL)
    y_sample = xt.reshape(DEC_BATCH, DEC_SEQ, D_MODEL)
    return (y_prompt, y_sample, jnp.stack(sp), jnp.stack(hp), jnp.stack(ss), jnp.stack(hs))
```

```python
import functools

import jax
import jax.numpy as jnp
from jax import lax
from jax.experimental import pallas as pl
from jax.experimental.pallas import tpu as pltpu

F32 = jnp.float32
BF16 = jnp.bfloat16
U32 = jnp.uint32
I32 = jnp.int32

D_MODEL = 1024
DEPTH = 2
SEQ = 16384
DEC_BATCH = 8
DEC_SEQ = 32
PAST_LEN = 4096
CHUNK = 64
HEADS = 4
DK = 128
DV = 256
KEY = HEADS * DK
VAL = HEADS * DV
GATE_RANK = 16
GATE_NORMALIZER = 16.0
POOL_WIDTH = 512
POOL_WINDOWS = (2, 4, 8, 16)
POOL_GROUP_DIM = 128
POOL_HIST = 15
N_EXPERTS = 64
GROUP_SIZE = 8
TOPK_GROUPS = 4
TOP_K = 8
D_EXPERT = 256
ROUTED_SCALE = 2.5
ALPHA = (2 * DEPTH) ** 0.25
LN_EPS = 1e-5
RMS_EPS = 1e-6

LANES = 128
SUBLANES = 8
VMEM_LIMIT_BYTES = 56 * 1024 * 1024

T_PROMPT = SEQ
T_SAMPLE = DEC_BATCH * DEC_SEQ
T_ALL = T_PROMPT + T_SAMPLE
TM = 256
HALO = 16
NEG_INF = float("-inf")

HALF = D_MODEL // 2
GRANULE = SUBLANES
ROW_BLOCK = 512
COPY_GROUP = 8
COPY_ROWS = (32, 16, 8)
RING = 3
RUN_CHUNK = 64
L_MAX = 2560
SEL_CHUNK = 512
COPY_CAP = (L_MAX // COPY_ROWS[0], N_EXPERTS, N_EXPERTS)
HI_MASK = 0xFFFF0000


def _dot(a, b):
    return jnp.dot(a, b, preferred_element_type=F32)


def _dot_nt(a, b):
    return lax.dot_general(a, b, (((1,), (1,)), ((), ())), preferred_element_type=F32)


def _dot_tn(a, b):
    return lax.dot_general(a, b, (((0,), (0,)), ((), ())), preferred_element_type=F32)


def _sigmoid(x):
    return 1.0 / (1.0 + jnp.exp(-x))


def _silu(x):
    return x * _sigmoid(x)


def _layer_norm(x, g, b):
    mu = jnp.mean(x, axis=-1, keepdims=True)
    xc = x - mu
    var = jnp.mean(xc * xc, axis=-1, keepdims=True)
    return xc * lax.rsqrt(var + LN_EPS) * g + b


def _pack_halves(v):
    bits = lax.bitcast_convert_type(v, U32)
    return (bits[:, HALF:] & jnp.uint32(HI_MASK)) | (bits[:, :HALF] >> 16)


def _unpack_halves(w):
    lo = lax.bitcast_convert_type(w << 16, F32).astype(BF16)
    hi = lax.bitcast_convert_type(w & jnp.uint32(HI_MASK), F32).astype(BF16)
    return lo, hi


def _params(*sem):
    return pltpu.CompilerParams(dimension_semantics=sem, vmem_limit_bytes=VMEM_LIMIT_BYTES)


def _const_spec(shape):
    nd = len(shape)
    return pl.BlockSpec(shape, lambda *_: (0,) * nd)


def _is_tail_tile():
    return pl.program_id(0) == pl.num_programs(0) - 1


def _main_tail_specs(n_tiles, width):
    return [pl.BlockSpec((TM, width), lambda i, *_: (jnp.minimum(i, n_tiles - 2), 0)),
            pl.BlockSpec((TM, width), lambda i, *_: (0, 0))]


def _inproj_kernel(xm_ref, xt_ref, wq, wk, wv, wr, wgk, wgk2, bgk, wu, wga, wgb,
                   q_o, k_o, v_o, r_o, lf_o, u_o, ga_o, gb_o):
    xb = jnp.where(_is_tail_tile(), xt_ref[...], xm_ref[...]).astype(BF16)
    q_o[...] = _dot(xb, wq[...]) * (DK ** -0.5)
    k_o[...] = _dot(xb, wk[...])
    v_o[...] = _dot(xb, wv[...]).astype(BF16)
    r_o[...] = _dot(xb, wr[...])
    gk = _dot(xb, wgk[...]).astype(BF16)
    z = _dot(gk, wgk2[...]) + bgk[...]
    log_sig = jnp.minimum(z, 0.0) - jnp.log1p(jnp.exp(-jnp.abs(z)))
    lf_o[...] = log_sig * (1.0 / GATE_NORMALIZER)
    u_o[...] = _dot(xb, wu[...])
    ga_o[...] = _dot(xb, wga[...])
    gb_o[...] = _dot(xb, wgb[...])


def _inproj(xm, xt, w):
    n = xm.shape[0] // TM + 1
    rows = n * TM
    row = lambda width: pl.BlockSpec((TM, width), lambda i: (i, 0))
    out_shapes = (
        jax.ShapeDtypeStruct((rows, KEY), F32), jax.ShapeDtypeStruct((rows, KEY), F32),
        jax.ShapeDtypeStruct((rows, VAL), BF16), jax.ShapeDtypeStruct((rows, VAL), F32),
        jax.ShapeDtypeStruct((rows, KEY), F32), jax.ShapeDtypeStruct((rows, POOL_WIDTH), F32),
        jax.ShapeDtypeStruct((rows, D_MODEL), F32), jax.ShapeDtypeStruct((rows, D_MODEL), F32))
    weights = (w["wq"], w["wk"], w["wv"], w["wr"], w["wgk"], w["wgk2"], w["bgk"], w["wu"], w["wga"], w["wgb"])
    return pl.pallas_call(
        _inproj_kernel, grid=(n,),
        in_specs=_main_tail_specs(n, D_MODEL) + [_const_spec(a.shape) for a in weights],
        out_specs=[row(KEY), row(KEY), row(VAL), row(VAL), row(KEY), row(POOL_WIDTH), row(D_MODEL), row(D_MODEL)],
        out_shape=out_shapes, compiler_params=_params("arbitrary"), name="inproj",
    )(xm, xt, *weights)


def _gla_kernel(*refs, chunk, n_chunks, has_init):
    if has_init:
        q_ref, k_ref, v_ref, lf_ref, r_ref, gn_ref, s0_ref, o_ref, st_ref = refs
        st_ref[...] = s0_ref[...]
    else:
        q_ref, k_ref, v_ref, lf_ref, r_ref, gn_ref, o_ref, st_ref = refs

        @pl.when(pl.program_id(0) == 0)
        def _():
            st_ref[...] = jnp.zeros_like(st_ref)

    row = lax.broadcasted_iota(I32, (chunk, KEY), 0)
    ta = lax.broadcasted_iota(I32, (chunk, chunk), 0)
    sa = lax.broadcasted_iota(I32, (chunk, chunk), 1)
    gn = gn_ref[...]

    def chunk_body(c, carry):
        rows = pl.ds(pl.multiple_of(c * chunk, chunk), chunk)
        lf = lf_ref[rows, :]
        q = q_ref[rows, :]
        k = k_ref[rows, :]
        qb = q.astype(BF16)
        kb = k.astype(BF16)
        att = []
        for h in range(HEADS):
            hs = slice(h * DK, (h + 1) * DK)
            att.append(jnp.where(ta == sa, _dot_nt(qb[:, hs], kb[:, hs]), 0.0))
        seg, tot = lf, lf
        half = 1
        while half < chunk:
            qs = (q * jnp.exp(seg)).astype(BF16)
            ks = (k * jnp.exp(tot - seg)).astype(BF16)
            pair = ((ta ^ sa) < 2 * half) & ((ta & half) != 0) & ((sa & half) == 0)
            for h in range(HEADS):
                hs = slice(h * DK, (h + 1) * DK)
                att[h] = jnp.where(pair, _dot_nt(qs[:, hs], ks[:, hs]), att[h])
            upper = (row & half) != 0
            below = pltpu.roll(tot, half, 0)
            above = pltpu.roll(tot, chunk - half, 0)
            seg = seg + jnp.where(upper, below, 0.0)
            tot = tot + jnp.where(upper, below, above)
            half *= 2
        q_in = (q * jnp.exp(seg)).astype(BF16)
        k_out = (k * jnp.exp(tot - seg)).astype(BF16)
        decay = jnp.exp(tot[0:1, :])
        for h in range(HEADS):
            hs = slice(h * DK, (h + 1) * DK)
            vs = slice(h * DV, (h + 1) * DV)
            state = st_ref[0, h]
            vh = v_ref[rows, vs]
            o = _dot_nt(q_in[:, hs], state.astype(BF16)) + _dot(att[h].astype(BF16), vh)
            st_ref[0, h] = state * decay[:, hs] + _dot_tn(vh, k_out[:, hs])
            ms = jnp.mean(o * o, axis=-1, keepdims=True)
            o = o * lax.rsqrt(ms + RMS_EPS) * gn
            o_ref[rows, vs] = (o * _silu(r_ref[rows, vs])).astype(BF16)
        return carry

    lax.fori_loop(0, n_chunks, chunk_body, 0)


def _gla(q, k, v, lf, r, gn, s0t, *, tile, chunk, n_tiles, block0):
    has_init = s0t is not None
    row = lambda width: pl.BlockSpec((tile, width), lambda i: (block0 + i, 0))
    st_spec = pl.BlockSpec((1, HEADS, DV, DK), (lambda i: (i, 0, 0, 0)) if has_init else (lambda i: (0, 0, 0, 0)))
    n_states = n_tiles if has_init else 1
    in_specs = [row(KEY), row(KEY), row(VAL), row(KEY), row(VAL), _const_spec(gn.shape)]
    args = [q, k, v, lf, r, gn]
    if has_init:
        in_specs.append(st_spec)
        args.append(s0t)
    return pl.pallas_call(
        functools.partial(_gla_kernel, chunk=chunk, n_chunks=tile // chunk, has_init=has_init),
        grid=(n_tiles,), in_specs=in_specs,
        out_specs=[pl.BlockSpec((tile, VAL), lambda i: (i, 0)), st_spec],
        out_shape=(jax.ShapeDtypeStruct((n_tiles * tile, VAL), BF16),
                   jax.ShapeDtypeStruct((n_states, HEADS, DV, DK), F32)),
        compiler_params=_params("arbitrary"), name="gla_init" if has_init else "gla",
    )(*args)


def _window_sums(ext, window):
    shift = 1
    while shift < window:
        ext = ext + pltpu.roll(ext, shift, 0)
        shift *= 2
    return ext


def _mix_kernel(xm_ref, xt_ref, ogm_ref, ogt_ref, u_ref, halo_m_ref, halo_t_ref, ga_ref, gb_ref,
                wbra, wpool, pscale, wbrb, wout, g1, b1, x1_ref):
    tail = _is_tail_tile()
    ya = _dot(jnp.where(tail, ogt_ref[...], ogm_ref[...]), wbra[...])
    u = u_ref[...]
    ext_m = jnp.concatenate([halo_m_ref[0], u], axis=0)
    seg = HALO + DEC_SEQ
    pieces = []
    for b in range(DEC_BATCH):
        pieces += [halo_t_ref[b], u[b * DEC_SEQ:(b + 1) * DEC_SEQ, :]]
    ext_t = jnp.concatenate(pieces, axis=0)
    rowi = lax.broadcasted_iota(I32, (TM, 1), 0)
    pos = jnp.where(tail, PAST_LEN + (rowi & (DEC_SEQ - 1)), pl.program_id(0) * TM + rowi)
    parts = []
    for g, window in enumerate(POOL_WINDOWS):
        cols = slice(g * POOL_GROUP_DIM, (g + 1) * POOL_GROUP_DIM)
        win_m = _window_sums(ext_m[:, cols], window)[HALO:, :]
        win_all = _window_sums(ext_t[:, cols], window)
        win_t = jnp.concatenate([win_all[b * seg + HALO:(b + 1) * seg, :] for b in range(DEC_BATCH)], axis=0)
        cnt = jnp.minimum(window, pos + 1).astype(F32)
        d = jnp.where(tail, win_t, win_m) / cnt - u[:, cols]
        parts.append(_dot(d.astype(BF16), wpool[g]))
    yb_in = jnp.concatenate(parts, axis=1) * pscale[...]
    yb = _dot(yb_in.astype(BF16), wbrb[...])
    mixed = _sigmoid(ga_ref[...]) * ya + _sigmoid(gb_ref[...]) * yb
    mix = _dot(mixed.astype(BF16), wout[...])
    x = jnp.where(tail, xt_ref[...], xm_ref[...])
    x1_ref[...] = _layer_norm(ALPHA * x + mix, g1[...], b1[...])


def _mix(xm, xt, og_m, og_t, u, halo_m, halo_t, ga, gb, w):
    n = xm.shape[0] // TM + 1
    row = lambda width: pl.BlockSpec((TM, width), lambda i: (i, 0))
    weights = (w["wbra"], w["wpool"], w["pscale"], w["wbrb"], w["wout"], w["g1"], w["b1"])
    return pl.pallas_call(
        _mix_kernel, grid=(n,),
        in_specs=_main_tail_specs(n, D_MODEL) + _main_tail_specs(n, VAL) + [
            row(POOL_WIDTH), pl.BlockSpec((1, HALO, POOL_WIDTH), lambda i: (jnp.minimum(i, n - 2), 0, 0)),
            _const_spec(halo_t.shape), row(D_MODEL), row(D_MODEL)] + [_const_spec(a.shape) for a in weights],
        out_specs=row(D_MODEL),
        out_shape=jax.ShapeDtypeStruct((n * TM, D_MODEL), F32),
        compiler_params=_params("arbitrary"), name="mix",
    )(xm, xt, og_m, og_t, u, halo_m, halo_t, ga, gb, *weights)


def _router_kernel(x1_ref, wrh_ref, wrl_ref, br_ref, wt_ref, rank_ref, cnt_ref, *, tile):
    x = x1_ref[...]
    xh = x.astype(BF16)
    xl = (x - xh.astype(F32)).astype(BF16)
    logits = _dot_nt(wrh_ref[...], xh) + _dot_nt(wrl_ref[...], xh) + _dot_nt(wrh_ref[...], xl)
    scores = _sigmoid(logits)
    rows_pad = 2 * N_EXPERTS
    n_grp = rows_pad // GROUP_SIZE
    erow = lax.broadcasted_iota(I32, (rows_pad, tile), 0)
    biased = jnp.where(erow < N_EXPERTS, scores + br_ref[...], NEG_INF)
    shape3 = (n_grp, GROUP_SIZE, tile)
    b3 = biased.reshape(shape3)
    s3 = scores.reshape(shape3)
    sub = lax.broadcasted_iota(I32, shape3, 1)
    gid = lax.broadcasted_iota(I32, shape3, 0)
    eid = gid * GROUP_SIZE + sub
    m1 = jnp.max(b3, axis=1, keepdims=True)
    i1 = jnp.min(jnp.where(b3 == m1, sub, GROUP_SIZE), axis=1, keepdims=True)
    m2 = jnp.max(jnp.where(sub == i1, NEG_INF, b3), axis=1, keepdims=True)
    gscore = m1 + m2
    gid1 = lax.broadcasted_iota(I32, (n_grp, 1, tile), 0)
    gsel = jnp.zeros((n_grp, 1, tile), jnp.bool_)
    for _ in range(TOPK_GROUPS):
        gm = jnp.max(gscore, axis=0, keepdims=True)
        gi = jnp.min(jnp.where(gscore == gm, gid1, n_grp), axis=0, keepdims=True)
        pick = gid1 == gi
        gsel = gsel | pick
        gscore = jnp.where(pick, NEG_INF, gscore)
    masked = jnp.where(gsel, b3, NEG_INF)
    wsel = jnp.zeros(shape3, F32)
    chosen = jnp.zeros(shape3, jnp.bool_)
    for _ in range(TOP_K):
        m = jnp.max(jnp.max(masked, axis=1, keepdims=True), axis=0, keepdims=True)
        idx = jnp.min(jnp.min(jnp.where(masked == m, eid, rows_pad), axis=1, keepdims=True), axis=0, keepdims=True)
        pick = eid == idx
        wsel = jnp.where(pick, s3, wsel)
        chosen = chosen | pick
        masked = jnp.where(pick, NEG_INF, masked)
    wsum = jnp.sum(jnp.sum(wsel, axis=1, keepdims=True), axis=0, keepdims=True)
    wt_ref[...] = (wsel / wsum * ROUTED_SCALE).reshape(rows_pad, tile)
    sel = jnp.where(chosen, 1.0, 0.0).reshape(rows_pad, tile)
    before = lax.broadcasted_iota(I32, (tile, tile), 0) < lax.broadcasted_iota(I32, (tile, tile), 1)
    rank = _dot(sel.astype(BF16), jnp.where(before, 1.0, 0.0).astype(BF16))
    rank_ref[...] = jnp.where(sel > 0.0, rank, -1.0).astype(I32)
    cnt = jnp.sum(sel, axis=1, keepdims=True).astype(I32)
    cnt_ref[0] = jnp.broadcast_to(cnt, (rows_pad, LANES))


def _router(x1, w):
    n = x1.shape[0] // TM
    rows_pad = 2 * N_EXPERTS
    tok = pl.BlockSpec((rows_pad, TM), lambda i: (0, i))
    return pl.pallas_call(
        functools.partial(_router_kernel, tile=TM), grid=(n,),
        in_specs=[pl.BlockSpec((TM, D_MODEL), lambda i: (i, 0)), _const_spec(w["wrh"].shape),
                  _const_spec(w["wrl"].shape), _const_spec(w["br"].shape)],
        out_specs=[tok, tok, pl.BlockSpec((1, rows_pad, LANES), lambda i: (i, 0, 0))],
        out_shape=(jax.ShapeDtypeStruct((rows_pad, x1.shape[0]), F32),
                   jax.ShapeDtypeStruct((rows_pad, x1.shape[0]), I32),
                   jax.ShapeDtypeStruct((n, rows_pad, LANES), I32)),
        compiler_params=_params("arbitrary"), name="router",
    )(x1, w["wrh"], w["wrl"], w["br"])


def _n_row_blocks(n_tokens):
    n_tiles = n_tokens // TM
    worst_rows = n_tokens * TOP_K + n_tiles * N_EXPERTS * (GRANULE - 1) + N_EXPERTS * (ROW_BLOCK - 1)
    return -(-worst_rows // ROW_BLOCK)


def _steps(shift):
    return shift - jnp.concatenate([jnp.zeros_like(shift[..., :1]), shift[..., :-1]], axis=-1)


def _dispatch_tables(cnt, n_blocks):
    padded = (cnt + (GRANULE - 1)) // GRANULE * GRANULE
    run_end = jnp.cumsum(padded, axis=1)
    run_off = run_end - padded
    n_gran = run_end[:, -1] // GRANULE
    rows_e = jnp.sum(padded, axis=0)
    region = (rows_e + (ROW_BLOCK - 1)) // ROW_BLOCK * ROW_BLOCK
    region_end = jnp.cumsum(region)
    region_start = region_end - region
    run_base = region_start[None, :] + jnp.cumsum(padded, axis=0) - padded

    big = padded // COPY_ROWS[0]
    big_first = jnp.cumsum(big, axis=1) - big
    k_big = jnp.arange(COPY_CAP[0], dtype=I32)
    owned = k_big[None, :, None] >= big_first[:, None, :]
    big_list = lambda row0: COPY_ROWS[0] * k_big[None, :] + jnp.sum(
        jnp.where(owned, _steps(row0 - COPY_ROWS[0] * big_first)[:, None, :], 0), axis=2)
    lists_src, lists_dst, counts = [big_list(run_off)], [big_list(run_base)], [jnp.sum(big, axis=1)]
    done = big * COPY_ROWS[0]
    for rows, cap in zip(COPY_ROWS[1:], COPY_CAP[1:]):
        has = (padded // rows) % 2
        pos = jnp.cumsum(has, axis=1) - has
        hit = (has[:, None, :] != 0) & (pos[:, None, :] == jnp.arange(cap, dtype=I32)[None, :, None])
        lists_src.append(jnp.sum(jnp.where(hit, (run_off + done)[:, None, :], 0), axis=2))
        lists_dst.append(jnp.sum(jnp.where(hit, (run_base + done)[:, None, :], 0), axis=2))
        counts.append(jnp.sum(has, axis=1))
        done = done + has * rows
    copy_src = jnp.concatenate(lists_src, axis=1)
    copy_dst = jnp.concatenate(lists_dst, axis=1)
    copy_n = jnp.stack(counts, axis=1)

    n_used = region_end[-1] // ROW_BLOCK
    blocks = jnp.arange(n_blocks, dtype=I32)
    block_expert = jnp.minimum(jnp.sum(blocks[:, None] >= (region_end // ROW_BLOCK)[None, :], axis=1), N_EXPERTS - 1)
    tail = (region - rows_e) // GRANULE
    tail_end = jnp.cumsum(tail)
    tail_first = tail_end - tail
    slot = jnp.arange(N_EXPERTS * (ROW_BLOCK // GRANULE - 1), dtype=I32)
    t_step = _steps((region_start + rows_e) // GRANULE - tail_first)
    tail_dst = slot + jnp.sum(jnp.where(slot[:, None] >= tail_first[None, :], t_step[None, :], 0), axis=1)
    as_i32 = lambda a: a.astype(I32).reshape(-1)
    long_run = jnp.any(padded > RUN_CHUNK, axis=1)
    return dict(run_off=as_i32(run_off), run_len=as_i32(padded), long_run=as_i32(long_run),
                n_gran=as_i32(n_gran), copy_src=as_i32(copy_src), copy_dst=as_i32(copy_dst), copy_n=as_i32(copy_n),
                n_used=as_i32(n_used), block_expert=as_i32(block_expert), n_tail=as_i32(tail_end[-1]),
                tail_dst=as_i32(tail_dst * GRANULE))


def _row_copy(src, src_row, dst, dst_row, sem, rows):
    s = pl.multiple_of(src_row, GRANULE)
    d = pl.multiple_of(dst_row, GRANULE)
    return pltpu.make_async_copy(src.at[pl.ds(s, rows), :], dst.at[pl.ds(d, rows), :], sem)


def _grouped_loop(n, body):
    n_groups = n // COPY_GROUP

    def group(j, carry):
        for i in range(COPY_GROUP):
            body(j * COPY_GROUP + i)
        return carry

    def single(i, carry):
        body(i)
        return carry

    lax.fori_loop(0, n_groups, group, 0)
    lax.fori_loop(n_groups * COPY_GROUP, n, single, 0)


def _start_copies(n, make_copy):
    _grouped_loop(n, lambda i: make_copy(i).start())


def _wait_copies(n, src, dst, sem, rows):
    n_groups = n // COPY_GROUP

    def group(j, carry):
        _row_copy(src, 0, dst, 0, sem, COPY_GROUP * rows).wait()
        return carry

    def single(i, carry):
        _row_copy(src, 0, dst, 0, sem, rows).wait()
        return carry

    lax.fori_loop(0, n_groups, group, 0)
    lax.fori_loop(n_groups * COPY_GROUP, n, single, 0)


def _start_tile_copies(tile, copy_n, make_copy):
    for c, rows in enumerate(COPY_ROWS):
        base = tile * sum(COPY_CAP) + sum(COPY_CAP[:c])
        _start_copies(copy_n[tile * len(COPY_ROWS) + c], lambda i, base=base, rows=rows: make_copy(base + i, rows))


def _wait_tile_copies(tile, copy_n, src, dst, sem):
    for c, rows in enumerate(COPY_ROWS):
        _wait_copies(copy_n[tile * len(COPY_ROWS) + c], src, dst, sem, rows)


def _build_selection(sel_scr, rank, values, run_off_ref, run_len_ref, long_run_ref, tile_idx):
    slot = lax.broadcasted_iota(I32, (RUN_CHUNK, TM), 0)

    def write_runs(all_chunks):
        for e in range(N_EXPERTS):
            off = run_off_ref[tile_idx * N_EXPERTS + e]
            rank_e = rank[e:e + 1, :]
            val_e = 1.0 if values is None else values[e:e + 1, :]

            def chunk(c, carry, off=off, rank_e=rank_e, val_e=val_e):
                hit = rank_e == slot + c * RUN_CHUNK
                rows = pl.ds(pl.multiple_of(off + c * RUN_CHUNK, GRANULE), RUN_CHUNK)
                sel_scr[rows, :] = jnp.where(hit, val_e, 0.0)
                return carry

            chunk(0, 0)
            if all_chunks:
                n_chunks = (run_len_ref[tile_idx * N_EXPERTS + e] + (RUN_CHUNK - 1)) // RUN_CHUNK
                lax.fori_loop(1, n_chunks, chunk, 0)

    write_runs(False)

    @pl.when(long_run_ref[tile_idx] != 0)
    def _():
        write_runs(True)


def _dispatch_kernel(run_off, run_len, long_run, n_gran, copy_src, copy_dst, copy_n, n_tail, tail_dst, n_used,
                     x1_ref, rank_ref, xs_hbm, sel_scr, buf_scr, zero_scr, sem):
    t = pl.program_id(0)

    @pl.when(t == 0)
    def _():
        sel_scr[...] = jnp.zeros_like(sel_scr)
        buf_scr[...] = jnp.zeros_like(buf_scr)
        zero_scr[...] = jnp.zeros_like(zero_scr)

    _build_selection(sel_scr, rank_ref[...], None, run_off, run_len, long_run, t)
    xb = x1_ref[...].astype(BF16)
    n_g = n_gran[t]
    buf = buf_scr.at[t % 2]
    for c in range(L_MAX // SEL_CHUNK):
        @pl.when(c * SEL_CHUNK < n_g * GRANULE)
        def _(c=c):
            rows = slice(c * SEL_CHUNK, (c + 1) * SEL_CHUNK)
            buf[rows, :] = _pack_halves(_dot(sel_scr[rows, :].astype(BF16), xb))

    @pl.when(t > 0)
    def _():
        _wait_tile_copies(jnp.maximum(t - 1, 0), copy_n, buf, xs_hbm, sem)

    _start_tile_copies(
        t, copy_n, lambda i, rows: _row_copy(buf, copy_src[i], xs_hbm, copy_dst[i], sem, rows))

    @pl.when(t == pl.num_programs(0) - 1)
    def _():
        _wait_tile_copies(t, copy_n, buf, xs_hbm, sem)
        _start_copies(n_tail[0], lambda i: _row_copy(zero_scr, 0, xs_hbm, tail_dst[i], sem, GRANULE))
        _wait_copies(n_tail[0], zero_scr, xs_hbm, sem, GRANULE)

        def spare_copy(b):
            rows = pl.ds(pl.multiple_of(b * ROW_BLOCK, ROW_BLOCK), ROW_BLOCK)
            return pltpu.make_async_copy(zero_scr, xs_hbm.at[rows, :], sem)

        def start_spare(b, carry):
            spare_copy(b).start()
            return carry

        def wait_spare(b, carry):
            spare_copy(b).wait()
            return carry

        n_blocks = xs_hbm.shape[0] // ROW_BLOCK
        lax.fori_loop(n_used[0], n_blocks, start_spare, 0)
        lax.fori_loop(n_used[0], n_blocks, wait_spare, 0)


def _dispatch(x1, rank, tables, n_blocks):
    n = x1.shape[0] // TM
    return pl.pallas_call(
        _dispatch_kernel,
        grid_spec=pltpu.PrefetchScalarGridSpec(
            num_scalar_prefetch=10, grid=(n,),
            in_specs=[pl.BlockSpec((TM, D_MODEL), lambda i, *_: (i, 0)),
                      pl.BlockSpec((2 * N_EXPERTS, TM), lambda i, *_: (0, i))],
            out_specs=pl.BlockSpec(memory_space=pl.ANY),
            scratch_shapes=[pltpu.VMEM((L_MAX, TM), F32), pltpu.VMEM((2, L_MAX, HALF), U32),
                            pltpu.VMEM((ROW_BLOCK, HALF), U32), pltpu.SemaphoreType.DMA(())]),
        out_shape=jax.ShapeDtypeStruct((n_blocks * ROW_BLOCK, HALF), U32),
        compiler_params=_params("arbitrary"), name="dispatch",
    )(tables["run_off"], tables["run_len"], tables["long_run"], tables["n_gran"], tables["copy_src"],
      tables["copy_dst"], tables["copy_n"], tables["n_tail"],
      tables["tail_dst"], tables["n_used"], x1, rank)


def _experts_kernel(block_expert, n_used, xs_hbm, wg_ref, wu_ref, wd_ref, y_ref,
                    x_ring, wgu_scr, wd_scr, sems):
    b = pl.program_id(0)
    n_u = n_used[0]

    def fetch(block):
        slot = block % RING
        rows = pl.ds(pl.multiple_of(block * ROW_BLOCK, ROW_BLOCK), ROW_BLOCK)
        return pltpu.make_async_copy(xs_hbm.at[rows, :], x_ring.at[slot], sems.at[slot])

    @pl.when(b == 0)
    def _():
        for first in range(RING - 1):
            @pl.when(first < n_u)
            def _(first=first):
                fetch(first).start()

    @pl.when(b + (RING - 1) < n_u)
    def _():
        fetch(b + (RING - 1)).start()

    @pl.when(b >= n_u)
    def _():
        y_ref[...] = jnp.zeros_like(y_ref)

    @pl.when(b < n_u)
    def _():
        fresh = (b == 0) | (block_expert[b] != block_expert[jnp.maximum(b - 1, 0)])

        @pl.when(fresh)
        def _():
            wgu_scr[:, :D_EXPERT] = wg_ref[...].astype(BF16)
            wgu_scr[:, D_EXPERT:] = wu_ref[...].astype(BF16)
            wd_scr[...] = wd_ref[...].astype(BF16)

        fetch(b).wait()
        lo, hi = _unpack_halves(x_ring[b % RING])
        gu = _dot(lo, wgu_scr[:HALF, :]) + _dot(hi, wgu_scr[HALF:, :])
        hid = (_silu(gu[:, :D_EXPERT]) * gu[:, D_EXPERT:]).astype(BF16)
        y = _dot(hid, wd_scr[...])
        y_ref[...] = _pack_halves(y.astype(BF16).astype(F32))


def _experts(xs, tables, w_e_gate, w_e_up, w_e_down, layer, n_blocks):
    wmap = lambda b, be, nu: (layer, be[b], 0, 0)
    return pl.pallas_call(
        _experts_kernel,
        grid_spec=pltpu.PrefetchScalarGridSpec(
            num_scalar_prefetch=2, grid=(n_blocks,),
            in_specs=[pl.BlockSpec(memory_space=pl.ANY),
                      pl.BlockSpec((None, None, D_MODEL, D_EXPERT), wmap),
                      pl.BlockSpec((None, None, D_MODEL, D_EXPERT), wmap),
                      pl.BlockSpec((None, None, D_EXPERT, D_MODEL), wmap)],
            out_specs=pl.BlockSpec((ROW_BLOCK, HALF), lambda b, be, nu: (b, 0)),
            scratch_shapes=[pltpu.VMEM((RING, ROW_BLOCK, HALF), U32),
                            pltpu.VMEM((D_MODEL, 2 * D_EXPERT), BF16),
                            pltpu.VMEM((D_EXPERT, D_MODEL), BF16), pltpu.SemaphoreType.DMA((RING,))]),
        out_shape=jax.ShapeDtypeStruct(xs.shape, U32),
        compiler_params=_params("arbitrary"), name="experts",
    )(tables["block_expert"], tables["n_used"], xs, w_e_gate, w_e_up, w_e_down)


def _combine_kernel(run_off, run_len, long_run, n_gran, copy_src, copy_dst, copy_n,
                    x1_ref, rank_ref, wt_ref, y_hbm, wsg, wsu, wsd, g2, b2, out_m_ref, out_t_ref,
                    sel_scr, buf_scr, acc_scr, sem):
    t = pl.program_id(0)
    n_tiles = pl.num_programs(0)
    n_g = n_gran[t]

    def fetch(tile):
        dst = buf_scr.at[tile % 2]
        _start_tile_copies(
            tile, copy_n, lambda i, rows: _row_copy(y_hbm, copy_dst[i], dst, copy_src[i], sem, rows))

    @pl.when(t == 0)
    def _():
        buf_scr[...] = jnp.zeros_like(buf_scr)
        fetch(t)

    stale = pl.multiple_of(jnp.minimum(n_g * GRANULE, L_MAX - SEL_CHUNK), GRANULE)
    sel_scr[pl.ds(stale, SEL_CHUNK), :] = jnp.zeros((SEL_CHUNK, TM), F32)
    _build_selection(sel_scr, rank_ref[...], wt_ref[...], run_off, run_len, long_run, t)
    x1 = x1_ref[...]
    xb = x1.astype(BF16)
    hs = (_silu(_dot(xb, wsg[...])) * _dot(xb, wsu[...])).astype(BF16)
    acc_scr[...] = _dot(hs, wsd[...])
    buf = buf_scr.at[t % 2]
    _wait_tile_copies(t, copy_n, y_hbm, buf, sem)

    @pl.when(t + 1 < n_tiles)
    def _():
        fetch(jnp.minimum(t + 1, n_tiles - 1))

    for c in range(L_MAX // SEL_CHUNK):
        @pl.when(c * SEL_CHUNK < n_g * GRANULE)
        def _(c=c):
            rows = slice(c * SEL_CHUNK, (c + 1) * SEL_CHUNK)
            lo, hi = _unpack_halves(buf[rows, :])
            sel = sel_scr[rows, :].astype(BF16)
            acc_scr[:, :HALF] += _dot_tn(sel, lo)
            acc_scr[:, HALF:] += _dot_tn(sel, hi)

    out = _layer_norm(ALPHA * x1 + acc_scr[...], g2[...], b2[...])

    @pl.when(t + 1 < n_tiles)
    def _():
        out_m_ref[...] = out

    @pl.when(t + 1 == n_tiles)
    def _():
        out_t_ref[...] = out


def _combine(x1, rank, wt, y, tables, w):
    n = x1.shape[0] // TM
    small = (w["wsg"], w["wsu"], w["wsd"], w["g2"], w["b2"])
    tok = pl.BlockSpec((2 * N_EXPERTS, TM), lambda i, *_: (0, i))
    return pl.pallas_call(
        _combine_kernel,
        grid_spec=pltpu.PrefetchScalarGridSpec(
            num_scalar_prefetch=7, grid=(n,),
            in_specs=[pl.BlockSpec((TM, D_MODEL), lambda i, *_: (i, 0)), tok, tok,
                      pl.BlockSpec(memory_space=pl.ANY)]
                     + [pl.BlockSpec(a.shape, lambda i, *_, nd=a.ndim: (0,) * nd) for a in small],
            out_specs=_main_tail_specs(n, D_MODEL),
            scratch_shapes=[pltpu.VMEM((L_MAX, TM), F32), pltpu.VMEM((2, L_MAX, HALF), U32),
                            pltpu.VMEM((TM, D_MODEL), F32), pltpu.SemaphoreType.DMA(())]),
        out_shape=(jax.ShapeDtypeStruct((x1.shape[0] - TM, D_MODEL), F32),
                   jax.ShapeDtypeStruct((TM, D_MODEL), F32)),
        compiler_params=_params("arbitrary"), name="combine",
    )(tables["run_off"], tables["run_len"], tables["long_run"], tables["n_gran"], tables["copy_src"],
      tables["copy_dst"], tables["copy_n"], x1, rank, wt, y, *small)


def _moe(x1, w, w_e_gate, w_e_up, w_e_down, layer):
    n_blocks = _n_row_blocks(x1.shape[0])
    wt, rank, cnt = _router(x1, w)
    tables = _dispatch_tables(cnt[:, :N_EXPERTS, 0], n_blocks)
    xs = _dispatch(x1, rank, tables, n_blocks)
    y = _experts(xs, tables, w_e_gate, w_e_up, w_e_down, layer, n_blocks)
    return _combine(x1, rank, wt, y, tables, w)


def _layer_weights(l, w_in, w_gk2, b_gk, gla_norm_g, w_pool, pool_scale, w_br_a, w_br_b, w_out, ln1_g, ln1_b,
                   w_router, b_router, w_sh_gate, w_sh_up, w_sh_down, ln2_g, ln2_b):
    wi = w_in[l].astype(BF16)
    o = 0
    pieces = {}
    for name, width in (("wq", KEY), ("wk", KEY), ("wv", VAL), ("wr", VAL), ("wgk", GATE_RANK),
                        ("wu", POOL_WIDTH), ("wga", D_MODEL), ("wgb", D_MODEL)):
        pieces[name] = wi[:, o:o + width]
        o += width
    pad_rank = LANES - GATE_RANK
    pieces["wgk"] = jnp.pad(pieces["wgk"], ((0, 0), (0, pad_rank)))
    pieces["wgk2"] = jnp.pad(w_gk2[l].astype(BF16), ((0, pad_rank), (0, 0)))
    pieces["bgk"] = b_gk[l].reshape(1, KEY)
    pieces["gn"] = gla_norm_g[l].reshape(1, DV)
    pieces["wbra"] = w_br_a[l].astype(BF16)
    pieces["wpool"] = w_pool[l].astype(BF16)
    pieces["pscale"] = pool_scale[l].reshape(1, POOL_WIDTH)
    pieces["wbrb"] = w_br_b[l].astype(BF16)
    pieces["wout"] = w_out[l].astype(BF16)
    pieces["g1"] = ln1_g[l].reshape(1, D_MODEL)
    pieces["b1"] = ln1_b[l].reshape(1, D_MODEL)
    wrt = jnp.pad(w_router[l].T, ((0, N_EXPERTS), (0, 0)))
    wrh = wrt.astype(BF16)
    pieces["wrh"] = wrh
    pieces["wrl"] = (wrt - wrh.astype(F32)).astype(BF16)
    pieces["br"] = jnp.pad(b_router[l], (0, N_EXPERTS)).reshape(2 * N_EXPERTS, 1)
    pieces["wsg"] = w_sh_gate[l].astype(BF16)
    pieces["wsu"] = w_sh_up[l].astype(BF16)
    pieces["wsd"] = w_sh_down[l].astype(BF16)
    pieces["g2"] = ln2_g[l].reshape(1, D_MODEL)
    pieces["b2"] = ln2_b[l].reshape(1, D_MODEL)
    return pieces


def kernel(x_prompt, x_sample, state_gla, cache_pool, w_in, w_gk2, b_gk, gla_norm_g, w_pool, pool_scale, w_br_a, w_br_b, w_out, ln1_g, ln1_b, w_router, b_router, w_e_gate, w_e_up, w_e_down, w_sh_gate, w_sh_up, w_sh_down, ln2_g, ln2_b):
    assert T_SAMPLE == TM, "the decode tokens must fill exactly one token tile"
    xm = x_prompt.reshape(T_PROMPT, D_MODEL)
    xt = x_sample.reshape(T_SAMPLE, D_MODEL)
    n_prompt_tiles = T_PROMPT // TM
    sample_block0 = T_PROMPT // DEC_SEQ
    sp, hp, ss, hs = [], [], [], []
    for l in range(DEPTH):
        w = _layer_weights(l, w_in, w_gk2, b_gk, gla_norm_g, w_pool, pool_scale, w_br_a, w_br_b, w_out, ln1_g,
                           ln1_b, w_router, b_router, w_sh_gate, w_sh_up, w_sh_down, ln2_g, ln2_b)
        q, k, v, r, lf, u, ga, gb = _inproj(xm, xt, w)

        og_p, st_p = _gla(q, k, v, lf, r, w["gn"], None, tile=TM, chunk=CHUNK, n_tiles=n_prompt_tiles, block0=0)
        s0t = jnp.swapaxes(state_gla[l], -1, -2)
        og_s, st_s = _gla(q, k, v, lf, r, w["gn"], s0t, tile=DEC_SEQ, chunk=DEC_SEQ, n_tiles=DEC_BATCH,
                          block0=sample_block0)

        u_p = u[:T_PROMPT].reshape(n_prompt_tiles, TM, POOL_WIDTH)
        halo_p = jnp.concatenate([jnp.zeros((1, HALO, POOL_WIDTH), F32), u_p[:-1, TM - HALO:, :]], axis=0)
        halo_s = jnp.concatenate([jnp.zeros((DEC_BATCH, HALO - POOL_HIST, POOL_WIDTH), F32), cache_pool[l]], axis=1)
        x1 = _mix(xm, xt, og_p, og_s, u, halo_p, halo_s, ga, gb, w)

        xm, xt = _moe(x1, w, w_e_gate, w_e_up, w_e_down, l)

        sp.append(jnp.swapaxes(st_p, -1, -2))
        ss.append(jnp.swapaxes(st_s, -1, -2))
        hp.append(u[T_PROMPT - POOL_HIST:T_PROMPT].reshape(1, POOL_HIST, POOL_WIDTH))
        hs.append(u[T_PROMPT:].reshape(DEC_BATCH, DEC_SEQ, POOL_WIDTH)[:, DEC_SEQ - POOL_HIST:, :])
    y_prompt = xm.reshape(1, SEQ, D_MODEL)
    y_sample = xt.reshape(DEC_BATCH, DEC_SEQ, D_MODEL)
    return (y_prompt, y_sample, jnp.stack(sp), jnp.stack(hp), jnp.stack(ss), jnp.stack(hs))
```

```python
import functools

import jax
import jax.numpy as jnp
from jax import lax
from jax.experimental import pallas as pl
from jax.experimental.pallas import tpu as pltpu

F32 = jnp.float32
BF16 = jnp.bfloat16
U32 = jnp.uint32
I32 = jnp.int32

D_MODEL = 1024
DEPTH = 2
SEQ = 16384
DEC_BATCH = 8
DEC_SEQ = 32
PAST_LEN = 4096
CHUNK = 64
HEADS = 4
DK = 128
DV = 256
KEY = HEADS * DK
VAL = HEADS * DV
GATE_RANK = 16
GATE_NORMALIZER = 16.0
POOL_WIDTH = 512
POOL_WINDOWS = (2, 4, 8, 16)
POOL_GROUP_DIM = 128
POOL_HIST = 15
N_EXPERTS = 64
GROUP_SIZE = 8
TOPK_GROUPS = 4
TOP_K = 8
D_EXPERT = 256
ROUTED_SCALE = 2.5
ALPHA = (2 * DEPTH) ** 0.25
LN_EPS = 1e-5
RMS_EPS = 1e-6
LOG2_E = 1.4426950408889634

LANES = 128
SUBLANES = 8
VMEM_LIMIT_BYTES = 56 * 1024 * 1024

T_PROMPT = SEQ
T_SAMPLE = DEC_BATCH * DEC_SEQ
T_ALL = T_PROMPT + T_SAMPLE
TM = 256
HALO = 16
NEG_INF = float("-inf")

HALF = D_MODEL // 2
GRANULE = SUBLANES
ROW_BLOCK = 512
COPY_GROUP = 8
COPY_ROWS = (32, 16, 8)
RING = 3
RUN_CHUNK = 64
L_MAX = 2560
SEL_CHUNK = 512
COPY_CAP = (L_MAX // COPY_ROWS[0], N_EXPERTS, N_EXPERTS)
HI_MASK = 0xFFFF0000


def _dot(a, b):
    return jnp.dot(a, b, preferred_element_type=F32)


def _dot_nt(a, b):
    return lax.dot_general(a, b, (((1,), (1,)), ((), ())), preferred_element_type=F32)


def _dot_tn(a, b):
    return lax.dot_general(a, b, (((0,), (0,)), ((), ())), preferred_element_type=F32)


def _sigmoid(x):
    return 1.0 / (1.0 + jnp.exp(-x))


def _silu(x):
    return x * _sigmoid(x)


def _layer_norm(x, g, b):
    mu = jnp.mean(x, axis=-1, keepdims=True)
    xc = x - mu
    var = jnp.mean(xc * xc, axis=-1, keepdims=True)
    return xc * lax.rsqrt(var + LN_EPS) * g + b


def _pack_halves(v):
    bits = lax.bitcast_convert_type(v, U32)
    return (bits[:, HALF:] & jnp.uint32(HI_MASK)) | (bits[:, :HALF] >> 16)


def _unpack_halves(w):
    lo = lax.bitcast_convert_type(w << 16, F32).astype(BF16)
    hi = lax.bitcast_convert_type(w & jnp.uint32(HI_MASK), F32).astype(BF16)
    return lo, hi


def _params(*sem):
    return pltpu.CompilerParams(dimension_semantics=sem, vmem_limit_bytes=VMEM_LIMIT_BYTES)


def _const_spec(shape):
    nd = len(shape)
    return pl.BlockSpec(shape, lambda *_: (0,) * nd)


def _is_tail_tile():
    return pl.program_id(0) == pl.num_programs(0) - 1


def _main_tail_specs(n_tiles, width):
    return [pl.BlockSpec((TM, width), lambda i, *_: (jnp.minimum(i, n_tiles - 2), 0)),
            pl.BlockSpec((TM, width), lambda i, *_: (0, 0))]


def _inproj_kernel(xm_ref, xt_ref, wq, wk, wv, wr, wgk, wgk2, bgk, wu, wga, wgb,
                   q_o, k_o, v_o, r_o, lf_o, u_o, ga_o, gb_o):
    xb = jnp.where(_is_tail_tile(), xt_ref[...], xm_ref[...]).astype(BF16)
    q_o[...] = _dot(xb, wq[...]) * (DK ** -0.5)
    k_o[...] = _dot(xb, wk[...])
    v_o[...] = _dot(xb, wv[...]).astype(BF16)
    r_o[...] = _dot(xb, wr[...])
    gk = _dot(xb, wgk[...]).astype(BF16)
    z = _dot(gk, wgk2[...]) + bgk[...]
    log_sig = jnp.minimum(z, 0.0) - jnp.log1p(jnp.exp(-jnp.abs(z)))
    lf_o[...] = log_sig * (1.0 / GATE_NORMALIZER)
    u_o[...] = _dot(xb, wu[...])
    ga_o[...] = _dot(xb, wga[...])
    gb_o[...] = _dot(xb, wgb[...])


def _inproj(xm, xt, w):
    n = xm.shape[0] // TM + 1
    rows = n * TM
    row = lambda width: pl.BlockSpec((TM, width), lambda i: (i, 0))
    out_shapes = (
        jax.ShapeDtypeStruct((rows, KEY), F32), jax.ShapeDtypeStruct((rows, KEY), F32),
        jax.ShapeDtypeStruct((rows, VAL), BF16), jax.ShapeDtypeStruct((rows, VAL), F32),
        jax.ShapeDtypeStruct((rows, KEY), F32), jax.ShapeDtypeStruct((rows, POOL_WIDTH), F32),
        jax.ShapeDtypeStruct((rows, D_MODEL), F32), jax.ShapeDtypeStruct((rows, D_MODEL), F32))
    weights = (w["wq"], w["wk"], w["wv"], w["wr"], w["wgk"], w["wgk2"], w["bgk"], w["wu"], w["wga"], w["wgb"])
    return pl.pallas_call(
        _inproj_kernel, grid=(n,),
        in_specs=_main_tail_specs(n, D_MODEL) + [_const_spec(a.shape) for a in weights],
        out_specs=[row(KEY), row(KEY), row(VAL), row(VAL), row(KEY), row(POOL_WIDTH), row(D_MODEL), row(D_MODEL)],
        out_shape=out_shapes, compiler_params=_params("arbitrary"), name="inproj",
    )(xm, xt, *weights)


def _gla_kernel(*refs, chunk, n_chunks, has_init):
    if has_init:
        q_ref, k_ref, v_ref, lf_ref, r_ref, gn_ref, s0_ref, o_ref, st_ref = refs
        st_ref[...] = s0_ref[...]
    else:
        q_ref, k_ref, v_ref, lf_ref, r_ref, gn_ref, o_ref, st_ref = refs

        @pl.when(pl.program_id(0) == 0)
        def _():
            st_ref[...] = jnp.zeros_like(st_ref)

    row = lax.broadcasted_iota(I32, (chunk, KEY), 0)
    ta = lax.broadcasted_iota(I32, (chunk, chunk), 0)
    sa = lax.broadcasted_iota(I32, (chunk, chunk), 1)
    gn = gn_ref[...]
    halves = [1 << i for i in range(chunk.bit_length() - 1)]
    diag = ta == sa
    pairs = [((ta ^ sa) < 2 * half) & ((ta & half) != 0) & ((sa & half) == 0) for half in halves]

    def chunk_body(c, carry):
        rows = pl.ds(pl.multiple_of(c * chunk, chunk), chunk)
        lf = lf_ref[rows, :] * LOG2_E
        q = q_ref[rows, :]
        k = k_ref[rows, :]
        qb = q.astype(BF16)
        kb = k.astype(BF16)
        att = []
        for h in range(HEADS):
            hs = slice(h * DK, (h + 1) * DK)
            att.append(jnp.where(diag, _dot_nt(qb[:, hs], kb[:, hs]), 0.0))
        seg, tot = lf, lf
        for half, pair in zip(halves, pairs):
            qs = (q * jnp.exp2(seg)).astype(BF16)
            ks = (k * jnp.exp2(tot - seg)).astype(BF16)
            for h in range(HEADS):
                hs = slice(h * DK, (h + 1) * DK)
                att[h] = jnp.where(pair, _dot_nt(qs[:, hs], ks[:, hs]), att[h])
            if half < SUBLANES:
                upper = (row & half) != 0
                below = pltpu.roll(tot, half, 0)
                above = pltpu.roll(tot, chunk - half, 0)
                seg = seg + jnp.where(upper, below, 0.0)
                tot = tot + jnp.where(upper, below, above)
            else:
                step, n_tiles = half // SUBLANES, chunk // SUBLANES
                seg_t = [seg[i * SUBLANES:(i + 1) * SUBLANES, :] for i in range(n_tiles)]
                tot_t = [tot[i * SUBLANES:(i + 1) * SUBLANES, :] for i in range(n_tiles)]
                seg = jnp.concatenate([seg_t[i] + tot_t[i - step] if i & step else seg_t[i] for i in range(n_tiles)], axis=0)
                tot = jnp.concatenate([tot_t[i] + tot_t[i ^ step] for i in range(n_tiles)], axis=0)
        q_in = (q * jnp.exp2(seg)).astype(BF16)
        k_out = (k * jnp.exp2(tot - seg)).astype(BF16)
        decay = jnp.exp2(tot[0:1, :])
        for h in range(HEADS):
            hs = slice(h * DK, (h + 1) * DK)
            vs = slice(h * DV, (h + 1) * DV)
            state = st_ref[0, h]
            vh = v_ref[rows, vs]
            o = _dot_nt(q_in[:, hs], state.astype(BF16)) + _dot(att[h].astype(BF16), vh)
            st_ref[0, h] = state * decay[:, hs] + _dot_tn(vh, k_out[:, hs])
            ms = jnp.mean(o * o, axis=-1, keepdims=True)
            o = o * lax.rsqrt(ms + RMS_EPS) * gn
            o_ref[rows, vs] = (o * _silu(r_ref[rows, vs])).astype(BF16)
        return carry

    lax.fori_loop(0, n_chunks, chunk_body, 0)


def _gla(q, k, v, lf, r, gn, s0t, *, tile, chunk, n_tiles, block0):
    has_init = s0t is not None
    row = lambda width: pl.BlockSpec((tile, width), lambda i: (block0 + i, 0))
    st_spec = pl.BlockSpec((1, HEADS, DV, DK), (lambda i: (i, 0, 0, 0)) if has_init else (lambda i: (0, 0, 0, 0)))
    n_states = n_tiles if has_init else 1
    in_specs = [row(KEY), row(KEY), row(VAL), row(KEY), row(VAL), _const_spec(gn.shape)]
    args = [q, k, v, lf, r, gn]
    if has_init:
        in_specs.append(st_spec)
        args.append(s0t)
    return pl.pallas_call(
        functools.partial(_gla_kernel, chunk=chunk, n_chunks=tile // chunk, has_init=has_init),
        grid=(n_tiles,), in_specs=in_specs,
        out_specs=[pl.BlockSpec((tile, VAL), lambda i: (i, 0)), st_spec],
        out_shape=(jax.ShapeDtypeStruct((n_tiles * tile, VAL), BF16),
                   jax.ShapeDtypeStruct((n_states, HEADS, DV, DK), F32)),
        compiler_params=_params("arbitrary"), name="gla_init" if has_init else "gla",
    )(*args)


def _window_sums(ext, window):
    shift = 1
    while shift < window:
        ext = ext + pltpu.roll(ext, shift, 0)
        shift *= 2
    return ext


def _mix_kernel(xm_ref, xt_ref, ogm_ref, ogt_ref, u_ref, halo_m_ref, halo_t_ref, ga_ref, gb_ref,
                wbra, wpool, pscale, wbrb, wout, g1, b1, x1_ref):
    tail = _is_tail_tile()
    ya = _dot(jnp.where(tail, ogt_ref[...], ogm_ref[...]), wbra[...])
    u = u_ref[...]
    ext_m = jnp.concatenate([halo_m_ref[0], u], axis=0)
    seg = HALO + DEC_SEQ
    pieces = []
    for b in range(DEC_BATCH):
        pieces += [halo_t_ref[b], u[b * DEC_SEQ:(b + 1) * DEC_SEQ, :]]
    ext_t = jnp.concatenate(pieces, axis=0)
    rowi = lax.broadcasted_iota(I32, (TM, 1), 0)
    pos = jnp.where(tail, PAST_LEN + (rowi & (DEC_SEQ - 1)), pl.program_id(0) * TM + rowi)
    parts = []
    for g, window in enumerate(POOL_WINDOWS):
        cols = slice(g * POOL_GROUP_DIM, (g + 1) * POOL_GROUP_DIM)
        win_m = _window_sums(ext_m[:, cols], window)[HALO:, :]
        win_all = _window_sums(ext_t[:, cols], window)
        win_t = jnp.concatenate([win_all[b * seg + HALO:(b + 1) * seg, :] for b in range(DEC_BATCH)], axis=0)
        cnt = jnp.minimum(window, pos + 1).astype(F32)
        d = jnp.where(tail, win_t, win_m) / cnt - u[:, cols]
        parts.append(_dot(d.astype(BF16), wpool[g]))
    yb_in = jnp.concatenate(parts, axis=1) * pscale[...]
    yb = _dot(yb_in.astype(BF16), wbrb[...])
    mixed = _sigmoid(ga_ref[...]) * ya + _sigmoid(gb_ref[...]) * yb
    mix = _dot(mixed.astype(BF16), wout[...])
    x = jnp.where(tail, xt_ref[...], xm_ref[...])
    x1_ref[...] = _layer_norm(ALPHA * x + mix, g1[...], b1[...])


def _mix(xm, xt, og_m, og_t, u, halo_m, halo_t, ga, gb, w):
    n = xm.shape[0] // TM + 1
    row = lambda width: pl.BlockSpec((TM, width), lambda i: (i, 0))
    weights = (w["wbra"], w["wpool"], w["pscale"], w["wbrb"], w["wout"], w["g1"], w["b1"])
    return pl.pallas_call(
        _mix_kernel, grid=(n,),
        in_specs=_main_tail_specs(n, D_MODEL) + _main_tail_specs(n, VAL) + [
            row(POOL_WIDTH), pl.BlockSpec((1, HALO, POOL_WIDTH), lambda i: (jnp.minimum(i, n - 2), 0, 0)),
            _const_spec(halo_t.shape), row(D_MODEL), row(D_MODEL)] + [_const_spec(a.shape) for a in weights],
        out_specs=row(D_MODEL),
        out_shape=jax.ShapeDtypeStruct((n * TM, D_MODEL), F32),
        compiler_params=_params("arbitrary"), name="mix",
    )(xm, xt, og_m, og_t, u, halo_m, halo_t, ga, gb, *weights)


def _router_kernel(x1_ref, wrh_ref, wrl_ref, br_ref, wt_ref, rank_ref, cnt_ref, *, tile):
    x = x1_ref[...]
    xh = x.astype(BF16)
    xl = (x - xh.astype(F32)).astype(BF16)
    logits = _dot_nt(wrh_ref[...], xh) + _dot_nt(wrl_ref[...], xh) + _dot_nt(wrh_ref[...], xl)
    scores = _sigmoid(logits)
    rows_pad = N_EXPERTS
    n_grp = rows_pad // GROUP_SIZE
    biased = scores + br_ref[...]
    shape3 = (n_grp, GROUP_SIZE, tile)
    b3 = biased.reshape(shape3)
    s3 = scores.reshape(shape3)
    sub = lax.broadcasted_iota(I32, shape3, 1)
    gid = lax.broadcasted_iota(I32, shape3, 0)
    eid = gid * GROUP_SIZE + sub
    m1 = jnp.max(b3, axis=1, keepdims=True)
    i1 = jnp.min(jnp.where(b3 == m1, sub, GROUP_SIZE), axis=1, keepdims=True)
    m2 = jnp.max(jnp.where(sub == i1, NEG_INF, b3), axis=1, keepdims=True)
    gscore = m1 + m2
    gid1 = lax.broadcasted_iota(I32, (n_grp, 1, tile), 0)
    gsel = jnp.zeros((n_grp, 1, tile), jnp.bool_)
    for _ in range(TOPK_GROUPS):
        gm = jnp.max(gscore, axis=0, keepdims=True)
        gi = jnp.min(jnp.where(gscore == gm, gid1, n_grp), axis=0, keepdims=True)
        pick = gid1 == gi
        gsel = gsel | pick
        gscore = jnp.where(pick, NEG_INF, gscore)
    masked = jnp.where(gsel, b3, NEG_INF)
    wsel = jnp.zeros(shape3, F32)
    chosen = jnp.zeros(shape3, jnp.bool_)
    for _ in range(TOP_K):
        m = jnp.max(jnp.max(masked, axis=1, keepdims=True), axis=0, keepdims=True)
        idx = jnp.min(jnp.min(jnp.where(masked == m, eid, rows_pad), axis=1, keepdims=True), axis=0, keepdims=True)
        pick = eid == idx
        wsel = jnp.where(pick, s3, wsel)
        chosen = chosen | pick
        masked = jnp.where(pick, NEG_INF, masked)
    wsum = jnp.sum(jnp.sum(wsel, axis=1, keepdims=True), axis=0, keepdims=True)
    wt_ref[...] = (wsel / wsum * ROUTED_SCALE).reshape(rows_pad, tile)
    sel = jnp.where(chosen, 1.0, 0.0).reshape(rows_pad, tile)
    before = lax.broadcasted_iota(I32, (tile, tile), 0) < lax.broadcasted_iota(I32, (tile, tile), 1)
    rank = _dot(sel.astype(BF16), jnp.where(before, 1.0, 0.0).astype(BF16))
    rank_ref[...] = jnp.where(sel > 0.0, rank, -1.0).astype(I32)
    cnt = jnp.sum(sel, axis=1, keepdims=True).astype(I32)
    cnt_ref[0] = jnp.broadcast_to(cnt, (rows_pad, LANES))


def _router(x1, w):
    n = x1.shape[0] // TM
    rows_pad = N_EXPERTS
    tok = pl.BlockSpec((rows_pad, TM), lambda i: (0, i))
    return pl.pallas_call(
        functools.partial(_router_kernel, tile=TM), grid=(n,),
        in_specs=[pl.BlockSpec((TM, D_MODEL), lambda i: (i, 0)), _const_spec(w["wrh"].shape),
                  _const_spec(w["wrl"].shape), _const_spec(w["br"].shape)],
        out_specs=[tok, tok, pl.BlockSpec((1, rows_pad, LANES), lambda i: (i, 0, 0))],
        out_shape=(jax.ShapeDtypeStruct((rows_pad, x1.shape[0]), F32),
                   jax.ShapeDtypeStruct((rows_pad, x1.shape[0]), I32),
                   jax.ShapeDtypeStruct((n, rows_pad, LANES), I32)),
        compiler_params=_params("arbitrary"), name="router",
    )(x1, w["wrh"], w["wrl"], w["br"])


def _n_row_blocks(n_tokens):
    n_tiles = n_tokens // TM
    worst_rows = n_tokens * TOP_K + n_tiles * N_EXPERTS * (GRANULE - 1) + N_EXPERTS * (ROW_BLOCK - 1)
    return -(-worst_rows // ROW_BLOCK)


def _steps(shift):
    return shift - jnp.concatenate([jnp.zeros_like(shift[..., :1]), shift[..., :-1]], axis=-1)


def _dispatch_tables(cnt, n_blocks):
    padded = (cnt + (GRANULE - 1)) // GRANULE * GRANULE
    run_end = jnp.cumsum(padded, axis=1)
    run_off = run_end - padded
    n_gran = run_end[:, -1] // GRANULE
    rows_e = jnp.sum(padded, axis=0)
    region = (rows_e + (ROW_BLOCK - 1)) // ROW_BLOCK * ROW_BLOCK
    region_end = jnp.cumsum(region)
    region_start = region_end - region
    run_base = region_start[None, :] + jnp.cumsum(padded, axis=0) - padded

    big = padded // COPY_ROWS[0]
    big_first = jnp.cumsum(big, axis=1) - big
    k_big = jnp.arange(COPY_CAP[0], dtype=I32)
    owned = k_big[None, :, None] >= big_first[:, None, :]
    big_list = lambda row0: COPY_ROWS[0] * k_big[None, :] + jnp.sum(
        jnp.where(owned, _steps(row0 - COPY_ROWS[0] * big_first)[:, None, :], 0), axis=2)
    lists_src, lists_dst, counts = [big_list(run_off)], [big_list(run_base)], [jnp.sum(big, axis=1)]
    done = big * COPY_ROWS[0]
    for rows, cap in zip(COPY_ROWS[1:], COPY_CAP[1:]):
        has = (padded // rows) % 2
        pos = jnp.cumsum(has, axis=1) - has
        hit = (has[:, None, :] != 0) & (pos[:, None, :] == jnp.arange(cap, dtype=I32)[None, :, None])
        lists_src.append(jnp.sum(jnp.where(hit, (run_off + done)[:, None, :], 0), axis=2))
        lists_dst.append(jnp.sum(jnp.where(hit, (run_base + done)[:, None, :], 0), axis=2))
        counts.append(jnp.sum(has, axis=1))
        done = done + has * rows
    copy_src = jnp.concatenate(lists_src, axis=1)
    copy_dst = jnp.concatenate(lists_dst, axis=1)
    copy_n = jnp.stack(counts, axis=1)

    n_used = region_end[-1] // ROW_BLOCK
    blocks = jnp.arange(n_blocks, dtype=I32)
    block_expert = jnp.minimum(jnp.sum(blocks[:, None] >= (region_end // ROW_BLOCK)[None, :], axis=1), N_EXPERTS - 1)
    tail = (region - rows_e) // GRANULE
    tail_end = jnp.cumsum(tail)
    tail_first = tail_end - tail
    slot = jnp.arange(N_EXPERTS * (ROW_BLOCK // GRANULE - 1), dtype=I32)
    t_step = _steps((region_start + rows_e) // GRANULE - tail_first)
    tail_dst = slot + jnp.sum(jnp.where(slot[:, None] >= tail_first[None, :], t_step[None, :], 0), axis=1)
    as_i32 = lambda a: a.astype(I32).reshape(-1)
    long_run = jnp.any(padded > RUN_CHUNK, axis=1)
    return dict(run_off=as_i32(run_off), run_len=as_i32(padded), long_run=as_i32(long_run),
                n_gran=as_i32(n_gran), copy_src=as_i32(copy_src), copy_dst=as_i32(copy_dst), copy_n=as_i32(copy_n),
                n_used=as_i32(n_used), block_expert=as_i32(block_expert), n_tail=as_i32(tail_end[-1]),
                tail_dst=as_i32(tail_dst * GRANULE))


def _row_copy(src, src_row, dst, dst_row, sem, rows):
    s = pl.multiple_of(src_row, GRANULE)
    d = pl.multiple_of(dst_row, GRANULE)
    return pltpu.make_async_copy(src.at[pl.ds(s, rows), :], dst.at[pl.ds(d, rows), :], sem)


def _grouped_loop(n, body):
    n_groups = n // COPY_GROUP

    def group(j, carry):
        for i in range(COPY_GROUP):
            body(j * COPY_GROUP + i)
        return carry

    def single(i, carry):
        body(i)
        return carry

    lax.fori_loop(0, n_groups, group, 0)
    lax.fori_loop(n_groups * COPY_GROUP, n, single, 0)


def _start_copies(n, make_copy):
    _grouped_loop(n, lambda i: make_copy(i).start())


def _wait_copies(n, src, dst, sem, rows):
    n_groups = n // COPY_GROUP

    def group(j, carry):
        _row_copy(src, 0, dst, 0, sem, COPY_GROUP * rows).wait()
        return carry

    def single(i, carry):
        _row_copy(src, 0, dst, 0, sem, rows).wait()
        return carry

    lax.fori_loop(0, n_groups, group, 0)
    lax.fori_loop(n_groups * COPY_GROUP, n, single, 0)


def _start_tile_copies(tile, copy_n, make_copy):
    for c, rows in enumerate(COPY_ROWS):
        base = tile * sum(COPY_CAP) + sum(COPY_CAP[:c])
        _start_copies(copy_n[tile * len(COPY_ROWS) + c], lambda i, base=base, rows=rows: make_copy(base + i, rows))


def _wait_tile_copies(tile, copy_n, src, dst, sem):
    for c, rows in enumerate(COPY_ROWS):
        _wait_copies(copy_n[tile * len(COPY_ROWS) + c], src, dst, sem, rows)


def _build_selection(sel_scr, rank, values, run_off_ref, run_len_ref, long_run_ref, tile_idx):
    slot = lax.broadcasted_iota(I32, (RUN_CHUNK, TM), 0)

    def write_runs(all_chunks):
        for e in range(N_EXPERTS):
            off = run_off_ref[tile_idx * N_EXPERTS + e]
            rank_e = rank[e:e + 1, :]
            val_e = 1.0 if values is None else values[e:e + 1, :]

            def chunk(c, carry, off=off, rank_e=rank_e, val_e=val_e):
                hit = rank_e == slot + c * RUN_CHUNK
                rows = pl.ds(pl.multiple_of(off + c * RUN_CHUNK, GRANULE), RUN_CHUNK)
                sel_scr[rows, :] = jnp.where(hit, val_e, 0.0)
                return carry

            chunk(0, 0)
            if all_chunks:
                n_chunks = (run_len_ref[tile_idx * N_EXPERTS + e] + (RUN_CHUNK - 1)) // RUN_CHUNK
                lax.fori_loop(1, n_chunks, chunk, 0)

    write_runs(False)

    @pl.when(long_run_ref[tile_idx] != 0)
    def _():
        write_runs(True)


def _dispatch_kernel(run_off, run_len, long_run, n_gran, copy_src, copy_dst, copy_n, n_tail, tail_dst, n_used,
                     x1_ref, rank_ref, xs_hbm, sel_scr, buf_scr, zero_scr, sem):
    t = pl.program_id(0)

    @pl.when(t == 0)
    def _():
        sel_scr[...] = jnp.zeros_like(sel_scr)
        buf_scr[...] = jnp.zeros_like(buf_scr)
        zero_scr[...] = jnp.zeros_like(zero_scr)

    _build_selection(sel_scr, rank_ref[...], None, run_off, run_len, long_run, t)
    xb = x1_ref[...].astype(BF16)
    n_g = n_gran[t]
    buf = buf_scr.at[t % 2]
    for c in range(L_MAX // SEL_CHUNK):
        @pl.when(c * SEL_CHUNK < n_g * GRANULE)
        def _(c=c):
            rows = slice(c * SEL_CHUNK, (c + 1) * SEL_CHUNK)
            buf[rows, :] = _pack_halves(_dot(sel_scr[rows, :].astype(BF16), xb))

    @pl.when(t > 0)
    def _():
        _wait_tile_copies(jnp.maximum(t - 1, 0), copy_n, buf, xs_hbm, sem)

    _start_tile_copies(
        t, copy_n, lambda i, rows: _row_copy(buf, copy_src[i], xs_hbm, copy_dst[i], sem, rows))

    @pl.when(t == pl.num_programs(0) - 1)
    def _():
        _wait_tile_copies(t, copy_n, buf, xs_hbm, sem)
        _start_copies(n_tail[0], lambda i: _row_copy(zero_scr, 0, xs_hbm, tail_dst[i], sem, GRANULE))
        _wait_copies(n_tail[0], zero_scr, xs_hbm, sem, GRANULE)

        def spare_copy(b):
            rows = pl.ds(pl.multiple_of(b * ROW_BLOCK, ROW_BLOCK), ROW_BLOCK)
            return pltpu.make_async_copy(zero_scr, xs_hbm.at[rows, :], sem)

        def start_spare(b, carry):
            spare_copy(b).start()
            return carry

        def wait_spare(b, carry):
            spare_copy(b).wait()
            return carry

        n_blocks = xs_hbm.shape[0] // ROW_BLOCK
        lax.fori_loop(n_used[0], n_blocks, start_spare, 0)
        lax.fori_loop(n_used[0], n_blocks, wait_spare, 0)


def _dispatch(x1, rank, tables, n_blocks):
    n = x1.shape[0] // TM
    return pl.pallas_call(
        _dispatch_kernel,
        grid_spec=pltpu.PrefetchScalarGridSpec(
            num_scalar_prefetch=10, grid=(n,),
            in_specs=[pl.BlockSpec((TM, D_MODEL), lambda i, *_: (i, 0)),
                      pl.BlockSpec((N_EXPERTS, TM), lambda i, *_: (0, i))],
            out_specs=pl.BlockSpec(memory_space=pl.ANY),
            scratch_shapes=[pltpu.VMEM((L_MAX, TM), F32), pltpu.VMEM((2, L_MAX, HALF), U32),
                            pltpu.VMEM((ROW_BLOCK, HALF), U32), pltpu.SemaphoreType.DMA(())]),
        out_shape=jax.ShapeDtypeStruct((n_blocks * ROW_BLOCK, HALF), U32),
        compiler_params=_params("arbitrary"), name="dispatch",
    )(tables["run_off"], tables["run_len"], tables["long_run"], tables["n_gran"], tables["copy_src"],
      tables["copy_dst"], tables["copy_n"], tables["n_tail"],
      tables["tail_dst"], tables["n_used"], x1, rank)


def _experts_kernel(block_expert, n_used, xs_hbm, wg_ref, wu_ref, wd_ref, y_ref,
                    x_ring, wgu_scr, wd_scr, sems):
    b = pl.program_id(0)
    n_u = n_used[0]

    def fetch(block):
        slot = block % RING
        rows = pl.ds(pl.multiple_of(block * ROW_BLOCK, ROW_BLOCK), ROW_BLOCK)
        return pltpu.make_async_copy(xs_hbm.at[rows, :], x_ring.at[slot], sems.at[slot])

    @pl.when(b == 0)
    def _():
        for first in range(RING - 1):
            @pl.when(first < n_u)
            def _(first=first):
                fetch(first).start()

    @pl.when(b + (RING - 1) < n_u)
    def _():
        fetch(b + (RING - 1)).start()

    @pl.when(b >= n_u)
    def _():
        y_ref[...] = jnp.zeros_like(y_ref)

    @pl.when(b < n_u)
    def _():
        fresh = (b == 0) | (block_expert[b] != block_expert[jnp.maximum(b - 1, 0)])

        @pl.when(fresh)
        def _():
            wgu_scr[:, :D_EXPERT] = wg_ref[...].astype(BF16)
            wgu_scr[:, D_EXPERT:] = wu_ref[...].astype(BF16)
            wd_scr[...] = wd_ref[...].astype(BF16)

        fetch(b).wait()
        lo, hi = _unpack_halves(x_ring[b % RING])
        gu = _dot(lo, wgu_scr[:HALF, :]) + _dot(hi, wgu_scr[HALF:, :])
        hid = (_silu(gu[:, :D_EXPERT]) * gu[:, D_EXPERT:]).astype(BF16)
        y = _dot(hid, wd_scr[...])
        y_ref[...] = _pack_halves(y.astype(BF16).astype(F32))


def _experts(xs, tables, w_e_gate, w_e_up, w_e_down, layer, n_blocks):
    wmap = lambda b, be, nu: (layer, be[b], 0, 0)
    return pl.pallas_call(
        _experts_kernel,
        grid_spec=pltpu.PrefetchScalarGridSpec(
            num_scalar_prefetch=2, grid=(n_blocks,),
            in_specs=[pl.BlockSpec(memory_space=pl.ANY),
                      pl.BlockSpec((None, None, D_MODEL, D_EXPERT), wmap),
                      pl.BlockSpec((None, None, D_MODEL, D_EXPERT), wmap),
                      pl.BlockSpec((None, None, D_EXPERT, D_MODEL), wmap)],
            out_specs=pl.BlockSpec((ROW_BLOCK, HALF), lambda b, be, nu: (b, 0)),
            scratch_shapes=[pltpu.VMEM((RING, ROW_BLOCK, HALF), U32),
                            pltpu.VMEM((D_MODEL, 2 * D_EXPERT), BF16),
                            pltpu.VMEM((D_EXPERT, D_MODEL), BF16), pltpu.SemaphoreType.DMA((RING,))]),
        out_shape=jax.ShapeDtypeStruct(xs.shape, U32),
        compiler_params=_params("arbitrary"), name="experts",
    )(tables["block_expert"], tables["n_used"], xs, w_e_gate, w_e_up, w_e_down)


def _combine_kernel(run_off, run_len, long_run, n_gran, copy_src, copy_dst, copy_n,
                    x1_ref, rank_ref, wt_ref, y_hbm, wsgu, wsd, g2, b2, out_m_ref, out_t_ref,
                    sel_scr, buf_scr, acc_scr, sem):
    t = pl.program_id(0)
    n_tiles = pl.num_programs(0)
    n_g = n_gran[t]

    def fetch(tile):
        dst = buf_scr.at[tile % 2]
        _start_tile_copies(
            tile, copy_n, lambda i, rows: _row_copy(y_hbm, copy_dst[i], dst, copy_src[i], sem, rows))

    @pl.when(t == 0)
    def _():
        buf_scr[...] = jnp.zeros_like(buf_scr)
        fetch(t)

    stale = pl.multiple_of(jnp.minimum(n_g * GRANULE, L_MAX - SEL_CHUNK), GRANULE)
    sel_scr[pl.ds(stale, SEL_CHUNK), :] = jnp.zeros((SEL_CHUNK, TM), F32)
    _build_selection(sel_scr, rank_ref[...], wt_ref[...], run_off, run_len, long_run, t)
    x1 = x1_ref[...]
    xb = x1.astype(BF16)
    gu = _dot(xb, wsgu[...])
    hs = (_silu(gu[:, :D_EXPERT]) * gu[:, D_EXPERT:]).astype(BF16)
    acc_scr[...] = _dot(hs, wsd[...])
    buf = buf_scr.at[t % 2]
    _wait_tile_copies(t, copy_n, y_hbm, buf, sem)

    @pl.when(t + 1 < n_tiles)
    def _():
        fetch(jnp.minimum(t + 1, n_tiles - 1))

    for c in range(L_MAX // SEL_CHUNK):
        @pl.when(c * SEL_CHUNK < n_g * GRANULE)
        def _(c=c):
            rows = slice(c * SEL_CHUNK, (c + 1) * SEL_CHUNK)
            lo, hi = _unpack_halves(buf[rows, :])
            sel = sel_scr[rows, :].astype(BF16)
            acc_scr[:, :HALF] += _dot_tn(sel, lo)
            acc_scr[:, HALF:] += _dot_tn(sel, hi)

    out = _layer_norm(ALPHA * x1 + acc_scr[...], g2[...], b2[...])

    @pl.when(t + 1 < n_tiles)
    def _():
        out_m_ref[...] = out

    @pl.when(t + 1 == n_tiles)
    def _():
        out_t_ref[...] = out


def _combine(x1, rank, wt, y, tables, w):
    n = x1.shape[0] // TM
    small = (w["wsgu"], w["wsd"], w["g2"], w["b2"])
    tok = pl.BlockSpec((N_EXPERTS, TM), lambda i, *_: (0, i))
    return pl.pallas_call(
        _combine_kernel,
        grid_spec=pltpu.PrefetchScalarGridSpec(
            num_scalar_prefetch=7, grid=(n,),
            in_specs=[pl.BlockSpec((TM, D_MODEL), lambda i, *_: (i, 0)), tok, tok,
                      pl.BlockSpec(memory_space=pl.ANY)]
                     + [pl.BlockSpec(a.shape, lambda i, *_, nd=a.ndim: (0,) * nd) for a in small],
            out_specs=_main_tail_specs(n, D_MODEL),
            scratch_shapes=[pltpu.VMEM((L_MAX, TM), F32), pltpu.VMEM((2, L_MAX, HALF), U32),
                            pltpu.VMEM((TM, D_MODEL), F32), pltpu.SemaphoreType.DMA(())]),
        out_shape=(jax.ShapeDtypeStruct((x1.shape[0] - TM, D_MODEL), F32),
                   jax.ShapeDtypeStruct((TM, D_MODEL), F32)),
        compiler_params=_params("arbitrary"), name="combine",
    )(tables["run_off"], tables["run_len"], tables["long_run"], tables["n_gran"], tables["copy_src"],
      tables["copy_dst"], tables["copy_n"], x1, rank, wt, y, *small)


def _moe(x1, w, w_e_gate, w_e_up, w_e_down, layer):
    n_blocks = _n_row_blocks(x1.shape[0])
    wt, rank, cnt = _router(x1, w)
    tables = _dispatch_tables(cnt[:, :N_EXPERTS, 0], n_blocks)
    xs = _dispatch(x1, rank, tables, n_blocks)
    y = _experts(xs, tables, w_e_gate, w_e_up, w_e_down, layer, n_blocks)
    return _combine(x1, rank, wt, y, tables, w)


def _layer_weights(l, w_in, w_gk2, b_gk, gla_norm_g, w_pool, pool_scale, w_br_a, w_br_b, w_out, ln1_g, ln1_b,
                   w_router, b_router, w_sh_gate, w_sh_up, w_sh_down, ln2_g, ln2_b):
    wi = w_in[l].astype(BF16)
    o = 0
    pieces = {}
    for name, width in (("wq", KEY), ("wk", KEY), ("wv", VAL), ("wr", VAL), ("wgk", GATE_RANK),
                        ("wu", POOL_WIDTH), ("wga", D_MODEL), ("wgb", D_MODEL)):
        pieces[name] = wi[:, o:o + width]
        o += width
    pad_rank = LANES - GATE_RANK
    pieces["wgk"] = jnp.pad(pieces["wgk"], ((0, 0), (0, pad_rank)))
    pieces["wgk2"] = jnp.pad(w_gk2[l].astype(BF16), ((0, pad_rank), (0, 0)))
    pieces["bgk"] = b_gk[l].reshape(1, KEY)
    pieces["gn"] = gla_norm_g[l].reshape(1, DV)
    pieces["wbra"] = w_br_a[l].astype(BF16)
    pieces["wpool"] = w_pool[l].astype(BF16)
    pieces["pscale"] = pool_scale[l].reshape(1, POOL_WIDTH)
    pieces["wbrb"] = w_br_b[l].astype(BF16)
    pieces["wout"] = w_out[l].astype(BF16)
    pieces["g1"] = ln1_g[l].reshape(1, D_MODEL)
    pieces["b1"] = ln1_b[l].reshape(1, D_MODEL)
    wrt = w_router[l].T
    wrh = wrt.astype(BF16)
    pieces["wrh"] = wrh
    pieces["wrl"] = (wrt - wrh.astype(F32)).astype(BF16)
    pieces["br"] = b_router[l].reshape(N_EXPERTS, 1)
    pieces["wsgu"] = jnp.concatenate([w_sh_gate[l], w_sh_up[l]], axis=1).astype(BF16)
    pieces["wsd"] = w_sh_down[l].astype(BF16)
    pieces["g2"] = ln2_g[l].reshape(1, D_MODEL)
    pieces["b2"] = ln2_b[l].reshape(1, D_MODEL)
    return pieces


def kernel(x_prompt, x_sample, state_gla, cache_pool, w_in, w_gk2, b_gk, gla_norm_g, w_pool, pool_scale, w_br_a, w_br_b, w_out, ln1_g, ln1_b, w_router, b_router, w_e_gate, w_e_up, w_e_down, w_sh_gate, w_sh_up, w_sh_down, ln2_g, ln2_b):
    assert T_SAMPLE == TM, "the decode tokens must fill exactly one token tile"
    xm = x_prompt.reshape(T_PROMPT, D_MODEL)
    xt = x_sample.reshape(T_SAMPLE, D_MODEL)
    n_prompt_tiles = T_PROMPT // TM
    sample_block0 = T_PROMPT // DEC_SEQ
    sp, hp, ss, hs = [], [], [], []
    for l in range(DEPTH):
        w = _layer_weights(l, w_in, w_gk2, b_gk, gla_norm_g, w_pool, pool_scale, w_br_a, w_br_b, w_out, ln1_g,
                           ln1_b, w_router, b_router, w_sh_gate, w_sh_up, w_sh_down, ln2_g, ln2_b)
        q, k, v, r, lf, u, ga, gb = _inproj(xm, xt, w)

        og_p, st_p = _gla(q, k, v, lf, r, w["gn"], None, tile=TM, chunk=CHUNK, n_tiles=n_prompt_tiles, block0=0)
        s0t = jnp.swapaxes(state_gla[l], -1, -2)
        og_s, st_s = _gla(q, k, v, lf, r, w["gn"], s0t, tile=DEC_SEQ, chunk=DEC_SEQ, n_tiles=DEC_BATCH,
                          block0=sample_block0)

        u_p = u[:T_PROMPT].reshape(n_prompt_tiles, TM, POOL_WIDTH)
        halo_p = jnp.concatenate([jnp.zeros((1, HALO, POOL_WIDTH), F32), u_p[:-1, TM - HALO:, :]], axis=0)
        halo_s = jnp.concatenate([jnp.zeros((DEC_BATCH, HALO - POOL_HIST, POOL_WIDTH), F32), cache_pool[l]], axis=1)
        x1 = _mix(xm, xt, og_p, og_s, u, halo_p, halo_s, ga, gb, w)

        xm, xt = _moe(x1, w, w_e_gate, w_e_up, w_e_down, l)

        sp.append(jnp.swapaxes(st_p, -1, -2))
        ss.append(jnp.swapaxes(st_s, -1, -2))
        hp.append(u[T_PROMPT - POOL_HIST:T_PROMPT].reshape(1, POOL_HIST, POOL_WIDTH))
        hs.append(u[T_PROMPT:].reshape(DEC_BATCH, DEC_SEQ, POOL_WIDTH)[:, DEC_SEQ - POOL_HIST:, :])
    y_prompt = xm.reshape(1, SEQ, D_MODEL)
    y_sample = xt.reshape(DEC_BATCH, DEC_SEQ, D_MODEL)
    return (y_prompt, y_sample, jnp.stack(sp), jnp.stack(hp), jnp.stack(ss), jnp.stack(hs))
```

```python
import functools

import jax
import jax.numpy as jnp
from jax import lax
from jax.experimental import pallas as pl
from jax.experimental.pallas import tpu as pltpu

F32 = jnp.float32
BF16 = jnp.bfloat16
U32 = jnp.uint32
I32 = jnp.int32

D_MODEL = 1024
DEPTH = 2
SEQ = 16384
DEC_BATCH = 8
DEC_SEQ = 32
PAST_LEN = 4096
CHUNK = 64
HEADS = 4
DK = 128
DV = 256
KEY = HEADS * DK
VAL = HEADS * DV
GATE_RANK = 16
GATE_NORMALIZER = 16.0
POOL_WIDTH = 512
POOL_WINDOWS = (2, 4, 8, 16)
POOL_GROUP_DIM = 128
POOL_HIST = 15
N_EXPERTS = 64
GROUP_SIZE = 8
TOPK_GROUPS = 4
TOP_K = 8
D_EXPERT = 256
ROUTED_SCALE = 2.5
ALPHA = (2 * DEPTH) ** 0.25
LN_EPS = 1e-5
RMS_EPS = 1e-6
LOG2_E = 1.4426950408889634

LANES = 128
SUBLANES = 8
VMEM_LIMIT_BYTES = 56 * 1024 * 1024

T_PROMPT = SEQ
T_SAMPLE = DEC_BATCH * DEC_SEQ
T_ALL = T_PROMPT + T_SAMPLE
TM = 256
HALO = 16
NEG_INF = float("-inf")

HALF = D_MODEL // 2
GRANULE = SUBLANES
ROW_BLOCK = 512
COPY_GROUP = 8
COPY_ROWS = (32, 16, 8)
RING = 3
RUN_CHUNK = 64
L_MAX = 2560
SEL_CHUNK = 512
COPY_CAP = (L_MAX // COPY_ROWS[0], N_EXPERTS, N_EXPERTS)
HI_MASK = 0xFFFF0000


def _dot(a, b):
    return jnp.dot(a, b, preferred_element_type=F32)


def _dot_nt(a, b):
    return lax.dot_general(a, b, (((1,), (1,)), ((), ())), preferred_element_type=F32)


def _dot_tn(a, b):
    return lax.dot_general(a, b, (((0,), (0,)), ((), ())), preferred_element_type=F32)


def _sigmoid(x):
    return 1.0 / (1.0 + jnp.exp(-x))


def _silu(x):
    return x * _sigmoid(x)


def _layer_norm(x, g, b):
    mu = jnp.mean(x, axis=-1, keepdims=True)
    xc = x - mu
    var = jnp.mean(xc * xc, axis=-1, keepdims=True)
    return xc * lax.rsqrt(var + LN_EPS) * g + b


def _pack_halves(v):
    bits = lax.bitcast_convert_type(v, U32)
    return (bits[:, HALF:] & jnp.uint32(HI_MASK)) | (bits[:, :HALF] >> 16)


def _unpack_halves(w):
    lo = lax.bitcast_convert_type(w << 16, F32).astype(BF16)
    hi = lax.bitcast_convert_type(w & jnp.uint32(HI_MASK), F32).astype(BF16)
    return lo, hi


def _params(*sem):
    return pltpu.CompilerParams(dimension_semantics=sem, vmem_limit_bytes=VMEM_LIMIT_BYTES)


def _const_spec(shape):
    nd = len(shape)
    return pl.BlockSpec(shape, lambda *_: (0,) * nd)


def _is_tail_tile():
    return pl.program_id(0) == pl.num_programs(0) - 1


def _main_tail_specs(n_tiles, width):
    return [pl.BlockSpec((TM, width), lambda i, *_: (jnp.minimum(i, n_tiles - 2), 0)),
            pl.BlockSpec((TM, width), lambda i, *_: (0, 0))]


IN_PIECES = (("q", KEY), ("k", KEY), ("v", VAL), ("r", VAL), ("gk", LANES), ("u", POOL_WIDTH),
             ("ga", D_MODEL), ("gb", D_MODEL))
IN_COLS = {}
_col = 0
for _name, _width in IN_PIECES:
    IN_COLS[_name] = slice(_col, _col + _width)
    _col += _width
IN_WIDTH = _col


def _inproj_kernel(xm_ref, xt_ref, w_ref, wgk2, bgk, q_o, k_o, v_o, r_o, lf_o, u_o, ga_o, gb_o):
    xb = jnp.where(_is_tail_tile(), xt_ref[...], xm_ref[...]).astype(BF16)
    proj = lambda name: _dot(xb, w_ref[:, IN_COLS[name]])
    q_o[...] = proj("q") * (DK ** -0.5)
    k_o[...] = proj("k")
    v_o[...] = proj("v").astype(BF16)
    r_o[...] = proj("r")
    gk = proj("gk").astype(BF16)
    z = _dot(gk, wgk2[...]) + bgk[...]
    log_sig = jnp.minimum(z, 0.0) - jnp.log1p(jnp.exp(-jnp.abs(z)))
    lf_o[...] = log_sig * (1.0 / GATE_NORMALIZER)
    u_o[...] = proj("u")
    ga_o[...] = proj("ga")
    gb_o[...] = proj("gb")


def _inproj(xm, xt, w):
    n = xm.shape[0] // TM + 1
    rows = n * TM
    row = lambda width: pl.BlockSpec((TM, width), lambda i: (i, 0))
    out_shapes = (
        jax.ShapeDtypeStruct((rows, KEY), F32), jax.ShapeDtypeStruct((rows, KEY), F32),
        jax.ShapeDtypeStruct((rows, VAL), BF16), jax.ShapeDtypeStruct((rows, VAL), F32),
        jax.ShapeDtypeStruct((rows, KEY), F32), jax.ShapeDtypeStruct((rows, POOL_WIDTH), F32),
        jax.ShapeDtypeStruct((rows, D_MODEL), F32), jax.ShapeDtypeStruct((rows, D_MODEL), F32))
    weights = (w["win"], w["wgk2"], w["bgk"])
    return pl.pallas_call(
        _inproj_kernel, grid=(n,),
        in_specs=_main_tail_specs(n, D_MODEL) + [_const_spec(a.shape) for a in weights],
        out_specs=[row(KEY), row(KEY), row(VAL), row(VAL), row(KEY), row(POOL_WIDTH), row(D_MODEL), row(D_MODEL)],
        out_shape=out_shapes, compiler_params=_params("arbitrary"), name="inproj",
    )(xm, xt, *weights)


def _gla_kernel(*refs, chunk, n_chunks, has_init):
    if has_init:
        q_ref, k_ref, v_ref, lf_ref, r_ref, gn_ref, s0_ref, o_ref, st_ref = refs
        st_ref[...] = s0_ref[...]
    else:
        q_ref, k_ref, v_ref, lf_ref, r_ref, gn_ref, o_ref, st_ref = refs

        @pl.when(pl.program_id(0) == 0)
        def _():
            st_ref[...] = jnp.zeros_like(st_ref)

    row = lax.broadcasted_iota(I32, (chunk, KEY), 0)
    ta = lax.broadcasted_iota(I32, (chunk, chunk), 0)
    sa = lax.broadcasted_iota(I32, (chunk, chunk), 1)
    gn = gn_ref[...]
    halves = [1 << i for i in range(chunk.bit_length() - 1)]
    diag = ta == sa
    pairs = [((ta ^ sa) < 2 * half) & ((ta & half) != 0) & ((sa & half) == 0) for half in halves]

    def chunk_body(c, carry):
        rows = pl.ds(pl.multiple_of(c * chunk, chunk), chunk)
        lf = lf_ref[rows, :] * LOG2_E
        q = q_ref[rows, :]
        k = k_ref[rows, :]
        qb = q.astype(BF16)
        kb = k.astype(BF16)
        att = []
        for h in range(HEADS):
            hs = slice(h * DK, (h + 1) * DK)
            att.append(jnp.where(diag, _dot_nt(qb[:, hs], kb[:, hs]), 0.0))
        seg, tot = lf, lf
        for half, pair in zip(halves, pairs):
            qs = (q * jnp.exp2(seg)).astype(BF16)
            ks = (k * jnp.exp2(tot - seg)).astype(BF16)
            for h in range(HEADS):
                hs = slice(h * DK, (h + 1) * DK)
                att[h] = jnp.where(pair, _dot_nt(qs[:, hs], ks[:, hs]), att[h])
            if half < SUBLANES:
                upper = (row & half) != 0
                below = pltpu.roll(tot, half, 0)
                above = pltpu.roll(tot, chunk - half, 0)
                seg = seg + jnp.where(upper, below, 0.0)
                tot = tot + jnp.where(upper, below, above)
            else:
                step, n_tiles = half // SUBLANES, chunk // SUBLANES
                seg_t = [seg[i * SUBLANES:(i + 1) * SUBLANES, :] for i in range(n_tiles)]
                tot_t = [tot[i * SUBLANES:(i + 1) * SUBLANES, :] for i in range(n_tiles)]
                seg = jnp.concatenate([seg_t[i] + tot_t[i - step] if i & step else seg_t[i] for i in range(n_tiles)], axis=0)
                tot = jnp.concatenate([tot_t[i] + tot_t[i ^ step] for i in range(n_tiles)], axis=0)
        q_in = (q * jnp.exp2(seg)).astype(BF16)
        k_out = (k * jnp.exp2(tot - seg)).astype(BF16)
        decay = jnp.exp2(tot[0:1, :])
        for h in range(HEADS):
            hs = slice(h * DK, (h + 1) * DK)
            vs = slice(h * DV, (h + 1) * DV)
            state = st_ref[0, h]
            vh = v_ref[rows, vs]
            o = _dot_nt(q_in[:, hs], state.astype(BF16)) + _dot(att[h].astype(BF16), vh)
            st_ref[0, h] = state * decay[:, hs] + _dot_tn(vh, k_out[:, hs])
            ms = jnp.mean(o * o, axis=-1, keepdims=True)
            o = o * lax.rsqrt(ms + RMS_EPS) * gn
            o_ref[rows, vs] = (o * _silu(r_ref[rows, vs])).astype(BF16)
        return carry

    lax.fori_loop(0, n_chunks, chunk_body, 0)


def _gla(q, k, v, lf, r, gn, s0t, *, tile, chunk, n_tiles, block0):
    has_init = s0t is not None
    row = lambda width: pl.BlockSpec((tile, width), lambda i: (block0 + i, 0))
    st_spec = pl.BlockSpec((1, HEADS, DV, DK), (lambda i: (i, 0, 0, 0)) if has_init else (lambda i: (0, 0, 0, 0)))
    n_states = n_tiles if has_init else 1
    in_specs = [row(KEY), row(KEY), row(VAL), row(KEY), row(VAL), _const_spec(gn.shape)]
    args = [q, k, v, lf, r, gn]
    if has_init:
        in_specs.append(st_spec)
        args.append(s0t)
    return pl.pallas_call(
        functools.partial(_gla_kernel, chunk=chunk, n_chunks=tile // chunk, has_init=has_init),
        grid=(n_tiles,), in_specs=in_specs,
        out_specs=[pl.BlockSpec((tile, VAL), lambda i: (i, 0)), st_spec],
        out_shape=(jax.ShapeDtypeStruct((n_tiles * tile, VAL), BF16),
                   jax.ShapeDtypeStruct((n_states, HEADS, DV, DK), F32)),
        compiler_params=_params("arbitrary"), name="gla_init" if has_init else "gla",
    )(*args)


def _window_sums(ext, window):
    shift = 1
    while shift < window:
        ext = ext + pltpu.roll(ext, shift, 0)
        shift *= 2
    return ext


def _mix_kernel(xm_ref, xt_ref, ogm_ref, ogt_ref, u_ref, halo_m_ref, halo_t_ref, ga_ref, gb_ref,
                wbra, wpool, pscale, wbrb, wout, g1, b1, x1_ref):
    tail = _is_tail_tile()
    ya = _dot(jnp.where(tail, ogt_ref[...], ogm_ref[...]), wbra[...])
    u = u_ref[...]
    ext_m = jnp.concatenate([halo_m_ref[0], u], axis=0)
    seg = HALO + DEC_SEQ
    pieces = []
    for b in range(DEC_BATCH):
        pieces += [halo_t_ref[b], u[b * DEC_SEQ:(b + 1) * DEC_SEQ, :]]
    ext_t = jnp.concatenate(pieces, axis=0)
    rowi = lax.broadcasted_iota(I32, (TM, 1), 0)
    pos = jnp.where(tail, PAST_LEN + (rowi & (DEC_SEQ - 1)), pl.program_id(0) * TM + rowi)
    parts = []
    for g, window in enumerate(POOL_WINDOWS):
        cols = slice(g * POOL_GROUP_DIM, (g + 1) * POOL_GROUP_DIM)
        win_m = _window_sums(ext_m[:, cols], window)[HALO:, :]
        win_all = _window_sums(ext_t[:, cols], window)
        win_t = jnp.concatenate([win_all[b * seg + HALO:(b + 1) * seg, :] for b in range(DEC_BATCH)], axis=0)
        cnt = jnp.minimum(window, pos + 1).astype(F32)
        d = jnp.where(tail, win_t, win_m) / cnt - u[:, cols]
        parts.append(_dot(d.astype(BF16), wpool[g]))
    yb_in = jnp.concatenate(parts, axis=1) * pscale[...]
    yb = _dot(yb_in.astype(BF16), wbrb[...])
    mixed = _sigmoid(ga_ref[...]) * ya + _sigmoid(gb_ref[...]) * yb
    mix = _dot(mixed.astype(BF16), wout[...])
    x = jnp.where(tail, xt_ref[...], xm_ref[...])
    x1_ref[...] = _layer_norm(ALPHA * x + mix, g1[...], b1[...])


def _mix(xm, xt, og_m, og_t, u, halo_m, halo_t, ga, gb, w):
    n = xm.shape[0] // TM + 1
    row = lambda width: pl.BlockSpec((TM, width), lambda i: (i, 0))
    weights = (w["wbra"], w["wpool"], w["pscale"], w["wbrb"], w["wout"], w["g1"], w["b1"])
    return pl.pallas_call(
        _mix_kernel, grid=(n,),
        in_specs=_main_tail_specs(n, D_MODEL) + _main_tail_specs(n, VAL) + [
            row(POOL_WIDTH), pl.BlockSpec((1, HALO, POOL_WIDTH), lambda i: (jnp.minimum(i, n - 2), 0, 0)),
            _const_spec(halo_t.shape), row(D_MODEL), row(D_MODEL)] + [_const_spec(a.shape) for a in weights],
        out_specs=row(D_MODEL),
        out_shape=jax.ShapeDtypeStruct((n * TM, D_MODEL), F32),
        compiler_params=_params("arbitrary"), name="mix",
    )(xm, xt, og_m, og_t, u, halo_m, halo_t, ga, gb, *weights)


def _router_kernel(x1_ref, wrh_ref, wrl_ref, br_ref, wt_ref, rank_ref, cnt_ref, *, tile):
    x = x1_ref[...]
    xh = x.astype(BF16)
    xl = (x - xh.astype(F32)).astype(BF16)
    logits = _dot_nt(wrh_ref[...], xh) + _dot_nt(wrl_ref[...], xh) + _dot_nt(wrh_ref[...], xl)
    scores = _sigmoid(logits)
    rows_pad = N_EXPERTS
    n_grp = rows_pad // GROUP_SIZE
    biased = scores + br_ref[...]
    shape3 = (n_grp, GROUP_SIZE, tile)
    b3 = biased.reshape(shape3)
    s3 = scores.reshape(shape3)
    sub = lax.broadcasted_iota(I32, shape3, 1)
    gid = lax.broadcasted_iota(I32, shape3, 0)
    eid = gid * GROUP_SIZE + sub
    m1 = jnp.max(b3, axis=1, keepdims=True)
    i1 = jnp.min(jnp.where(b3 == m1, sub, GROUP_SIZE), axis=1, keepdims=True)
    m2 = jnp.max(jnp.where(sub == i1, NEG_INF, b3), axis=1, keepdims=True)
    gscore = m1 + m2
    gid1 = lax.broadcasted_iota(I32, (n_grp, 1, tile), 0)
    gsel = jnp.zeros((n_grp, 1, tile), jnp.bool_)
    for _ in range(TOPK_GROUPS):
        gm = jnp.max(gscore, axis=0, keepdims=True)
        gi = jnp.min(jnp.where(gscore == gm, gid1, n_grp), axis=0, keepdims=True)
        pick = gid1 == gi
        gsel = gsel | pick
        gscore = jnp.where(pick, NEG_INF, gscore)
    masked = jnp.where(gsel, b3, NEG_INF)
    wsel = jnp.zeros(shape3, F32)
    chosen = jnp.zeros(shape3, jnp.bool_)
    for _ in range(TOP_K):
        m = jnp.max(jnp.max(masked, axis=1, keepdims=True), axis=0, keepdims=True)
        idx = jnp.min(jnp.min(jnp.where(masked == m, eid, rows_pad), axis=1, keepdims=True), axis=0, keepdims=True)
        pick = eid == idx
        wsel = jnp.where(pick, s3, wsel)
        chosen = chosen | pick
        masked = jnp.where(pick, NEG_INF, masked)
    wsum = jnp.sum(jnp.sum(wsel, axis=1, keepdims=True), axis=0, keepdims=True)
    wt_ref[...] = (wsel / wsum * ROUTED_SCALE).reshape(rows_pad, tile)
    sel = jnp.where(chosen, 1.0, 0.0).reshape(rows_pad, tile)
    before = lax.broadcasted_iota(I32, (tile, tile), 0) < lax.broadcasted_iota(I32, (tile, tile), 1)
    rank = _dot(sel.astype(BF16), jnp.where(before, 1.0, 0.0).astype(BF16))
    rank_ref[...] = jnp.where(sel > 0.0, rank, -1.0).astype(I32)
    cnt = jnp.sum(sel, axis=1, keepdims=True).astype(I32)
    cnt_ref[0] = jnp.broadcast_to(cnt, (rows_pad, LANES))


def _router(x1, w):
    n = x1.shape[0] // TM
    rows_pad = N_EXPERTS
    tok = pl.BlockSpec((rows_pad, TM), lambda i: (0, i))
    return pl.pallas_call(
        functools.partial(_router_kernel, tile=TM), grid=(n,),
        in_specs=[pl.BlockSpec((TM, D_MODEL), lambda i: (i, 0)), _const_spec(w["wrh"].shape),
                  _const_spec(w["wrl"].shape), _const_spec(w["br"].shape)],
        out_specs=[tok, tok, pl.BlockSpec((1, rows_pad, LANES), lambda i: (i, 0, 0))],
        out_shape=(jax.ShapeDtypeStruct((rows_pad, x1.shape[0]), F32),
                   jax.ShapeDtypeStruct((rows_pad, x1.shape[0]), I32),
                   jax.ShapeDtypeStruct((n, rows_pad, LANES), I32)),
        compiler_params=_params("arbitrary"), name="router",
    )(x1, w["wrh"], w["wrl"], w["br"])


def _n_row_blocks(n_tokens):
    n_tiles = n_tokens // TM
    worst_rows = n_tokens * TOP_K + n_tiles * N_EXPERTS * (GRANULE - 1) + N_EXPERTS * (ROW_BLOCK - 1)
    return -(-worst_rows // ROW_BLOCK)


def _steps(shift):
    return shift - jnp.concatenate([jnp.zeros_like(shift[..., :1]), shift[..., :-1]], axis=-1)


def _dispatch_tables(cnt, n_blocks):
    padded = (cnt + (GRANULE - 1)) // GRANULE * GRANULE
    run_end = jnp.cumsum(padded, axis=1)
    run_off = run_end - padded
    n_gran = run_end[:, -1] // GRANULE
    rows_e = jnp.sum(padded, axis=0)
    region = (rows_e + (ROW_BLOCK - 1)) // ROW_BLOCK * ROW_BLOCK
    region_end = jnp.cumsum(region)
    region_start = region_end - region
    run_base = region_start[None, :] + jnp.cumsum(padded, axis=0) - padded

    big = padded // COPY_ROWS[0]
    big_first = jnp.cumsum(big, axis=1) - big
    k_big = jnp.arange(COPY_CAP[0], dtype=I32)
    owned = k_big[None, :, None] >= big_first[:, None, :]
    big_list = lambda row0: COPY_ROWS[0] * k_big[None, :] + jnp.sum(
        jnp.where(owned, _steps(row0 - COPY_ROWS[0] * big_first)[:, None, :], 0), axis=2)
    lists_src, lists_dst, counts = [big_list(run_off)], [big_list(run_base)], [jnp.sum(big, axis=1)]
    done = big * COPY_ROWS[0]
    for rows, cap in zip(COPY_ROWS[1:], COPY_CAP[1:]):
        has = (padded // rows) % 2
        pos = jnp.cumsum(has, axis=1) - has
        hit = (has[:, None, :] != 0) & (pos[:, None, :] == jnp.arange(cap, dtype=I32)[None, :, None])
        lists_src.append(jnp.sum(jnp.where(hit, (run_off + done)[:, None, :], 0), axis=2))
        lists_dst.append(jnp.sum(jnp.where(hit, (run_base + done)[:, None, :], 0), axis=2))
        counts.append(jnp.sum(has, axis=1))
        done = done + has * rows
    copy_src = jnp.concatenate(lists_src, axis=1)
    copy_dst = jnp.concatenate(lists_dst, axis=1)
    copy_n = jnp.stack(counts, axis=1)

    n_used = region_end[-1] // ROW_BLOCK
    blocks = jnp.arange(n_blocks, dtype=I32)
    block_expert = jnp.minimum(jnp.sum(blocks[:, None] >= (region_end // ROW_BLOCK)[None, :], axis=1), N_EXPERTS - 1)
    experts = jnp.arange(N_EXPERTS, dtype=I32)
    present = (region > 0).astype(I32)
    ordinal = jnp.cumsum(present) - present
    seq_expert = jnp.sum(jnp.where((present[None, :] != 0) & (ordinal[None, :] == experts[:, None]),
                                   experts[None, :], 0), axis=1)
    block_ord = jnp.sum(jnp.where(block_expert[:, None] == experts[None, :], ordinal[None, :], 0), axis=1)
    tail = (region - rows_e) // GRANULE
    tail_end = jnp.cumsum(tail)
    tail_first = tail_end - tail
    slot = jnp.arange(N_EXPERTS * (ROW_BLOCK // GRANULE - 1), dtype=I32)
    t_step = _steps((region_start + rows_e) // GRANULE - tail_first)
    tail_dst = slot + jnp.sum(jnp.where(slot[:, None] >= tail_first[None, :], t_step[None, :], 0), axis=1)
    as_i32 = lambda a: a.astype(I32).reshape(-1)
    long_run = jnp.any(padded > RUN_CHUNK, axis=1)
    return dict(run_off=as_i32(run_off), run_len=as_i32(padded), long_run=as_i32(long_run),
                n_gran=as_i32(n_gran), copy_src=as_i32(copy_src), copy_dst=as_i32(copy_dst), copy_n=as_i32(copy_n),
                n_used=as_i32(n_used), block_expert=as_i32(block_expert), block_ord=as_i32(block_ord),
                seq_expert=as_i32(seq_expert), n_seq=as_i32(jnp.sum(present)), n_tail=as_i32(tail_end[-1]),
                tail_dst=as_i32(tail_dst * GRANULE))


def _row_copy(src, src_row, dst, dst_row, sem, rows):
    s = pl.multiple_of(src_row, GRANULE)
    d = pl.multiple_of(dst_row, GRANULE)
    return pltpu.make_async_copy(src.at[pl.ds(s, rows), :], dst.at[pl.ds(d, rows), :], sem)


def _grouped_loop(n, body):
    n_groups = n // COPY_GROUP

    def group(j, carry):
        for i in range(COPY_GROUP):
            body(j * COPY_GROUP + i)
        return carry

    def single(i, carry):
        body(i)
        return carry

    lax.fori_loop(0, n_groups, group, 0)
    lax.fori_loop(n_groups * COPY_GROUP, n, single, 0)


def _start_copies(n, make_copy):
    _grouped_loop(n, lambda i: make_copy(i).start())


def _wait_copies(n, src, dst, sem, rows):
    n_groups = n // COPY_GROUP

    def group(j, carry):
        _row_copy(src, 0, dst, 0, sem, COPY_GROUP * rows).wait()
        return carry

    def single(i, carry):
        _row_copy(src, 0, dst, 0, sem, rows).wait()
        return carry

    lax.fori_loop(0, n_groups, group, 0)
    lax.fori_loop(n_groups * COPY_GROUP, n, single, 0)


def _start_tile_copies(tile, copy_n, make_copy):
    for c, rows in enumerate(COPY_ROWS):
        base = tile * sum(COPY_CAP) + sum(COPY_CAP[:c])
        _start_copies(copy_n[tile * len(COPY_ROWS) + c], lambda i, base=base, rows=rows: make_copy(base + i, rows))


def _wait_tile_copies(tile, copy_n, src, dst, sem):
    for c, rows in enumerate(COPY_ROWS):
        _wait_copies(copy_n[tile * len(COPY_ROWS) + c], src, dst, sem, rows)


def _build_selection(sel_scr, rank, values, run_off_ref, run_len_ref, long_run_ref, tile_idx):
    slot = lax.broadcasted_iota(I32, (RUN_CHUNK, TM), 0)

    def write_runs(all_chunks):
        for e in range(N_EXPERTS):
            off = run_off_ref[tile_idx * N_EXPERTS + e]
            rank_e = rank[e:e + 1, :]
            val_e = 1.0 if values is None else values[e:e + 1, :]

            def chunk(c, carry, off=off, rank_e=rank_e, val_e=val_e):
                hit = rank_e == slot + c * RUN_CHUNK
                rows = pl.ds(pl.multiple_of(off + c * RUN_CHUNK, GRANULE), RUN_CHUNK)
                sel_scr[rows, :] = jnp.where(hit, val_e, 0.0)
                return carry

            chunk(0, 0)
            if all_chunks:
                n_chunks = (run_len_ref[tile_idx * N_EXPERTS + e] + (RUN_CHUNK - 1)) // RUN_CHUNK
                lax.fori_loop(1, n_chunks, chunk, 0)

    write_runs(False)

    @pl.when(long_run_ref[tile_idx] != 0)
    def _():
        write_runs(True)


def _dispatch_kernel(run_off, run_len, long_run, n_gran, copy_src, copy_dst, copy_n, n_tail, tail_dst, n_used,
                     x1_ref, rank_ref, xs_hbm, sel_scr, buf_scr, zero_scr, sem):
    t = pl.program_id(0)

    @pl.when(t == 0)
    def _():
        sel_scr[...] = jnp.zeros_like(sel_scr)
        buf_scr[...] = jnp.zeros_like(buf_scr)
        zero_scr[...] = jnp.zeros_like(zero_scr)

    _build_selection(sel_scr, rank_ref[...], None, run_off, run_len, long_run, t)
    xb = x1_ref[...].astype(BF16)
    n_g = n_gran[t]
    buf = buf_scr.at[t % 2]
    for c in range(L_MAX // SEL_CHUNK):
        @pl.when(c * SEL_CHUNK < n_g * GRANULE)
        def _(c=c):
            rows = slice(c * SEL_CHUNK, (c + 1) * SEL_CHUNK)
            buf[rows, :] = _pack_halves(_dot(sel_scr[rows, :].astype(BF16), xb))

    @pl.when(t > 0)
    def _():
        _wait_tile_copies(jnp.maximum(t - 1, 0), copy_n, buf, xs_hbm, sem)

    _start_tile_copies(
        t, copy_n, lambda i, rows: _row_copy(buf, copy_src[i], xs_hbm, copy_dst[i], sem, rows))

    @pl.when(t == pl.num_programs(0) - 1)
    def _():
        _wait_tile_copies(t, copy_n, buf, xs_hbm, sem)
        _start_copies(n_tail[0], lambda i: _row_copy(zero_scr, 0, xs_hbm, tail_dst[i], sem, GRANULE))
        _wait_copies(n_tail[0], zero_scr, xs_hbm, sem, GRANULE)

        def spare_copy(b):
            rows = pl.ds(pl.multiple_of(b * ROW_BLOCK, ROW_BLOCK), ROW_BLOCK)
            return pltpu.make_async_copy(zero_scr, xs_hbm.at[rows, :], sem)

        def start_spare(b, carry):
            spare_copy(b).start()
            return carry

        def wait_spare(b, carry):
            spare_copy(b).wait()
            return carry

        n_blocks = xs_hbm.shape[0] // ROW_BLOCK
        lax.fori_loop(n_used[0], n_blocks, start_spare, 0)
        lax.fori_loop(n_used[0], n_blocks, wait_spare, 0)


def _dispatch(x1, rank, tables, n_blocks):
    n = x1.shape[0] // TM
    return pl.pallas_call(
        _dispatch_kernel,
        grid_spec=pltpu.PrefetchScalarGridSpec(
            num_scalar_prefetch=10, grid=(n,),
            in_specs=[pl.BlockSpec((TM, D_MODEL), lambda i, *_: (i, 0)),
                      pl.BlockSpec((N_EXPERTS, TM), lambda i, *_: (0, i))],
            out_specs=pl.BlockSpec(memory_space=pl.ANY),
            scratch_shapes=[pltpu.VMEM((L_MAX, TM), F32), pltpu.VMEM((2, L_MAX, HALF), U32),
                            pltpu.VMEM((ROW_BLOCK, HALF), U32), pltpu.SemaphoreType.DMA(())]),
        out_shape=jax.ShapeDtypeStruct((n_blocks * ROW_BLOCK, HALF), U32),
        compiler_params=_params("arbitrary"), name="dispatch",
    )(tables["run_off"], tables["run_len"], tables["long_run"], tables["n_gran"], tables["copy_src"],
      tables["copy_dst"], tables["copy_n"], tables["n_tail"],
      tables["tail_dst"], tables["n_used"], x1, rank)


def _experts_kernel(block_expert, block_ord, seq_expert, n_seq, n_used,
                    xs_hbm, wg_hbm, wu_hbm, wd_hbm, y_ref,
                    x_ring, wg_stage, wu_stage, wd_stage, wgu_scr, wd_scr, x_sems, w_sems, *, layer):
    b = pl.program_id(0)
    n_u = n_used[0]

    def fetch(block):
        slot = block % RING
        rows = pl.ds(pl.multiple_of(block * ROW_BLOCK, ROW_BLOCK), ROW_BLOCK)
        return pltpu.make_async_copy(xs_hbm.at[rows, :], x_ring.at[slot], x_sems.at[slot])

    def weight_copies(ordinal):
        e = seq_expert[ordinal]
        slot = ordinal % 2
        return [pltpu.make_async_copy(src.at[layer, e], stage.at[slot], w_sems.at[slot])
                for src, stage in ((wg_hbm, wg_stage), (wu_hbm, wu_stage), (wd_hbm, wd_stage))]

    @pl.when(b == 0)
    def _():
        for copy in weight_copies(0):
            copy.start()
        for first in range(RING - 1):
            @pl.when(first < n_u)
            def _(first=first):
                fetch(first).start()

    @pl.when(b + (RING - 1) < n_u)
    def _():
        fetch(b + (RING - 1)).start()

    @pl.when(b >= n_u)
    def _():
        y_ref[...] = jnp.zeros_like(y_ref)

    @pl.when(b < n_u)
    def _():
        fresh = (b == 0) | (block_expert[b] != block_expert[jnp.maximum(b - 1, 0)])

        @pl.when(fresh)
        def _():
            ordinal = block_ord[b]
            slot = ordinal % 2
            for copy in weight_copies(ordinal):
                copy.wait()
            wgu_scr[:, :D_EXPERT] = wg_stage[slot].astype(BF16)
            wgu_scr[:, D_EXPERT:] = wu_stage[slot].astype(BF16)
            wd_scr[...] = wd_stage[slot].astype(BF16)

            @pl.when(ordinal + 1 < n_seq[0])
            def _():
                for copy in weight_copies(ordinal + 1):
                    copy.start()

        fetch(b).wait()
        lo, hi = _unpack_halves(x_ring[b % RING])
        gu = _dot(lo, wgu_scr[:HALF, :]) + _dot(hi, wgu_scr[HALF:, :])
        hid = (_silu(gu[:, :D_EXPERT]) * gu[:, D_EXPERT:]).astype(BF16)
        y = _dot(hid, wd_scr[...])
        y_ref[...] = _pack_halves(y.astype(BF16).astype(F32))


def _experts(xs, tables, w_e_gate, w_e_up, w_e_down, layer, n_blocks):
    any_spec = pl.BlockSpec(memory_space=pl.ANY)
    return pl.pallas_call(
        functools.partial(_experts_kernel, layer=layer),
        grid_spec=pltpu.PrefetchScalarGridSpec(
            num_scalar_prefetch=5, grid=(n_blocks,),
            in_specs=[any_spec, any_spec, any_spec, any_spec],
            out_specs=pl.BlockSpec((ROW_BLOCK, HALF), lambda b, *_: (b, 0)),
            scratch_shapes=[pltpu.VMEM((RING, ROW_BLOCK, HALF), U32),
                            pltpu.VMEM((2, D_MODEL, D_EXPERT), F32), pltpu.VMEM((2, D_MODEL, D_EXPERT), F32),
                            pltpu.VMEM((2, D_EXPERT, D_MODEL), F32),
                            pltpu.VMEM((D_MODEL, 2 * D_EXPERT), BF16), pltpu.VMEM((D_EXPERT, D_MODEL), BF16),
                            pltpu.SemaphoreType.DMA((RING,)), pltpu.SemaphoreType.DMA((2,))]),
        out_shape=jax.ShapeDtypeStruct(xs.shape, U32),
        compiler_params=_params("arbitrary"), name="experts",
    )(tables["block_expert"], tables["block_ord"], tables["seq_expert"], tables["n_seq"], tables["n_used"],
      xs, w_e_gate, w_e_up, w_e_down)


def _combine_kernel(run_off, run_len, long_run, n_gran, copy_src, copy_dst, copy_n,
                    x1_ref, rank_ref, wt_ref, y_hbm, wsgu, wsd, g2, b2, out_m_ref, out_t_ref,
                    sel_scr, buf_scr, acc_scr, sem):
    t = pl.program_id(0)
    n_tiles = pl.num_programs(0)
    n_g = n_gran[t]

    def fetch(tile):
        dst = buf_scr.at[tile % 2]
        _start_tile_copies(
            tile, copy_n, lambda i, rows: _row_copy(y_hbm, copy_dst[i], dst, copy_src[i], sem, rows))

    @pl.when(t == 0)
    def _():
        buf_scr[...] = jnp.zeros_like(buf_scr)
        fetch(t)

    stale = pl.multiple_of(jnp.minimum(n_g * GRANULE, L_MAX - SEL_CHUNK), GRANULE)
    sel_scr[pl.ds(stale, SEL_CHUNK), :] = jnp.zeros((SEL_CHUNK, TM), F32)
    _build_selection(sel_scr, rank_ref[...], wt_ref[...], run_off, run_len, long_run, t)
    x1 = x1_ref[...]
    xb = x1.astype(BF16)
    gu = _dot(xb, wsgu[...])
    hs = (_silu(gu[:, :D_EXPERT]) * gu[:, D_EXPERT:]).astype(BF16)
    acc_scr[...] = _dot(hs, wsd[...])
    buf = buf_scr.at[t % 2]
    _wait_tile_copies(t, copy_n, y_hbm, buf, sem)

    @pl.when(t + 1 < n_tiles)
    def _():
        fetch(jnp.minimum(t + 1, n_tiles - 1))

    for c in range(L_MAX // SEL_CHUNK):
        @pl.when(c * SEL_CHUNK < n_g * GRANULE)
        def _(c=c):
            rows = slice(c * SEL_CHUNK, (c + 1) * SEL_CHUNK)
            lo, hi = _unpack_halves(buf[rows, :])
            sel = sel_scr[rows, :].astype(BF16)
            acc_scr[:, :HALF] += _dot_tn(sel, lo)
            acc_scr[:, HALF:] += _dot_tn(sel, hi)

    out = _layer_norm(ALPHA * x1 + acc_scr[...], g2[...], b2[...])

    @pl.when(t + 1 < n_tiles)
    def _():
        out_m_ref[...] = out

    @pl.when(t + 1 == n_tiles)
    def _():
        out_t_ref[...] = out


def _combine(x1, rank, wt, y, tables, w):
    n = x1.shape[0] // TM
    small = (w["wsgu"], w["wsd"], w["g2"], w["b2"])
    tok = pl.BlockSpec((N_EXPERTS, TM), lambda i, *_: (0, i))
    return pl.pallas_call(
        _combine_kernel,
        grid_spec=pltpu.PrefetchScalarGridSpec(
            num_scalar_prefetch=7, grid=(n,),
            in_specs=[pl.BlockSpec((TM, D_MODEL), lambda i, *_: (i, 0)), tok, tok,
                      pl.BlockSpec(memory_space=pl.ANY)]
                     + [pl.BlockSpec(a.shape, lambda i, *_, nd=a.ndim: (0,) * nd) for a in small],
            out_specs=_main_tail_specs(n, D_MODEL),
            scratch_shapes=[pltpu.VMEM((L_MAX, TM), F32), pltpu.VMEM((2, L_MAX, HALF), U32),
                            pltpu.VMEM((TM, D_MODEL), F32), pltpu.SemaphoreType.DMA(())]),
        out_shape=(jax.ShapeDtypeStruct((x1.shape[0] - TM, D_MODEL), F32),
                   jax.ShapeDtypeStruct((TM, D_MODEL), F32)),
        compiler_params=_params("arbitrary"), name="combine",
    )(tables["run_off"], tables["run_len"], tables["long_run"], tables["n_gran"], tables["copy_src"],
      tables["copy_dst"], tables["copy_n"], x1, rank, wt, y, *small)


def _moe(x1, w, w_e_gate, w_e_up, w_e_down, layer):
    n_blocks = _n_row_blocks(x1.shape[0])
    wt, rank, cnt = _router(x1, w)
    tables = _dispatch_tables(cnt[:, :N_EXPERTS, 0], n_blocks)
    xs = _dispatch(x1, rank, tables, n_blocks)
    y = _experts(xs, tables, w_e_gate, w_e_up, w_e_down, layer, n_blocks)
    return _combine(x1, rank, wt, y, tables, w)


def _layer_weights(l, w_in, w_gk2, b_gk, gla_norm_g, w_pool, pool_scale, w_br_a, w_br_b, w_out, ln1_g, ln1_b,
                   w_router, b_router, w_sh_gate, w_sh_up, w_sh_down, ln2_g, ln2_b):
    pieces = {}
    pad_rank = LANES - GATE_RANK
    gate_end = IN_COLS["gk"].start + GATE_RANK
    pieces["win"] = jnp.concatenate(
        [w_in[l][:, :gate_end], jnp.zeros((D_MODEL, pad_rank), F32), w_in[l][:, gate_end:]], axis=1).astype(BF16)
    pieces["wgk2"] = jnp.pad(w_gk2[l].astype(BF16), ((0, pad_rank), (0, 0)))
    pieces["bgk"] = b_gk[l].reshape(1, KEY)
    pieces["gn"] = gla_norm_g[l].reshape(1, DV)
    pieces["wbra"] = w_br_a[l].astype(BF16)
    pieces["wpool"] = w_pool[l].astype(BF16)
    pieces["pscale"] = pool_scale[l].reshape(1, POOL_WIDTH)
    pieces["wbrb"] = w_br_b[l].astype(BF16)
    pieces["wout"] = w_out[l].astype(BF16)
    pieces["g1"] = ln1_g[l].reshape(1, D_MODEL)
    pieces["b1"] = ln1_b[l].reshape(1, D_MODEL)
    wrt = w_router[l].T
    wrh = wrt.astype(BF16)
    pieces["wrh"] = wrh
    pieces["wrl"] = (wrt - wrh.astype(F32)).astype(BF16)
    pieces["br"] = b_router[l].reshape(N_EXPERTS, 1)
    pieces["wsgu"] = jnp.concatenate([w_sh_gate[l], w_sh_up[l]], axis=1).astype(BF16)
    pieces["wsd"] = w_sh_down[l].astype(BF16)
    pieces["g2"] = ln2_g[l].reshape(1, D_MODEL)
    pieces["b2"] = ln2_b[l].reshape(1, D_MODEL)
    return pieces


def kernel(x_prompt, x_sample, state_gla, cache_pool, w_in, w_gk2, b_gk, gla_norm_g, w_pool, pool_scale, w_br_a, w_br_b, w_out, ln1_g, ln1_b, w_router, b_router, w_e_gate, w_e_up, w_e_down, w_sh_gate, w_sh_up, w_sh_down, ln2_g, ln2_b):
    assert T_SAMPLE == TM, "the decode tokens must fill exactly one token tile"
    xm = x_prompt.reshape(T_PROMPT, D_MODEL)
    xt = x_sample.reshape(T_SAMPLE, D_MODEL)
    n_prompt_tiles = T_PROMPT // TM
    sample_block0 = T_PROMPT // DEC_SEQ
    sp, hp, ss, hs = [], [], [], []
    for l in range(DEPTH):
        w = _layer_weights(l, w_in, w_gk2, b_gk, gla_norm_g, w_pool, pool_scale, w_br_a, w_br_b, w_out, ln1_g,
                           ln1_b, w_router, b_router, w_sh_gate, w_sh_up, w_sh_down, ln2_g, ln2_b)
        q, k, v, r, lf, u, ga, gb = _inproj(xm, xt, w)

        og_p, st_p = _gla(q, k, v, lf, r, w["gn"], None, tile=TM, chunk=CHUNK, n_tiles=n_prompt_tiles, block0=0)
        s0t = jnp.swapaxes(state_gla[l], -1, -2)
        og_s, st_s = _gla(q, k, v, lf, r, w["gn"], s0t, tile=DEC_SEQ, chunk=DEC_SEQ, n_tiles=DEC_BATCH,
                          block0=sample_block0)

        u_p = u[:T_PROMPT].reshape(n_prompt_tiles, TM, POOL_WIDTH)
        halo_p = jnp.concatenate([jnp.zeros((1, HALO, POOL_WIDTH), F32), u_p[:-1, TM - HALO:, :]], axis=0)
        halo_s = jnp.concatenate([jnp.zeros((DEC_BATCH, HALO - POOL_HIST, POOL_WIDTH), F32), cache_pool[l]], axis=1)
        x1 = _mix(xm, xt, og_p, og_s, u, halo_p, halo_s, ga, gb, w)

        xm, xt = _moe(x1, w, w_e_gate, w_e_up, w_e_down, l)

        sp.append(jnp.swapaxes(st_p, -1, -2))
        ss.append(jnp.swapaxes(st_s, -1, -2))
        hp.append(u[T_PROMPT - POOL_HIST:T_PROMPT].reshape(1, POOL_HIST, POOL_WIDTH))
        hs.append(u[T_PROMPT:].reshape(DEC_BATCH, DEC_SEQ, POOL_WIDTH)[:, DEC_SEQ - POOL_HIST:, :])
    y_prompt = xm.reshape(1, SEQ, D_MODEL)
    y_sample = xt.reshape(DEC_BATCH, DEC_SEQ, D_MODEL)
    return (y_prompt, y_sample, jnp.stack(sp), jnp.stack(hp), jnp.stack(ss), jnp.stack(hs))
```

```python
import functools

import jax
import jax.numpy as jnp
from jax import lax
from jax.experimental import pallas as pl
from jax.experimental.pallas import tpu as pltpu

F32 = jnp.float32
BF16 = jnp.bfloat16
U32 = jnp.uint32
I32 = jnp.int32

D_MODEL = 1024
DEPTH = 2
SEQ = 16384
DEC_BATCH = 8
DEC_SEQ = 32
PAST_LEN = 4096
CHUNK = 64
HEADS = 4
DK = 128
DV = 256
KEY = HEADS * DK
VAL = HEADS * DV
GATE_RANK = 16
GATE_NORMALIZER = 16.0
POOL_WIDTH = 512
POOL_WINDOWS = (2, 4, 8, 16)
POOL_GROUP_DIM = 128
POOL_HIST = 15
N_EXPERTS = 64
GROUP_SIZE = 8
TOPK_GROUPS = 4
TOP_K = 8
D_EXPERT = 256
ROUTED_SCALE = 2.5
ALPHA = (2 * DEPTH) ** 0.25
LN_EPS = 1e-5
RMS_EPS = 1e-6
LOG2_E = 1.4426950408889634

LANES = 128
SUBLANES = 8
VMEM_LIMIT_BYTES = 56 * 1024 * 1024

T_PROMPT = SEQ
T_SAMPLE = DEC_BATCH * DEC_SEQ
T_ALL = T_PROMPT + T_SAMPLE
TM = 256
HALO = 16
NEG_INF = float("-inf")

HALF = D_MODEL // 2
GRANULE = SUBLANES
ROW_BLOCK = 512
COPY_GROUP = 8
COPY_ROWS = (32, 16, 8)
RING = 3
RUN_CHUNK = 64
L_MAX = 2560
SEL_CHUNK = 512
COPY_CAP = (L_MAX // COPY_ROWS[0], N_EXPERTS, N_EXPERTS)
HI_MASK = 0xFFFF0000


def _dot(a, b):
    return jnp.dot(a, b, preferred_element_type=F32)


def _dot_nt(a, b):
    return lax.dot_general(a, b, (((1,), (1,)), ((), ())), preferred_element_type=F32)


def _dot_tn(a, b):
    return lax.dot_general(a, b, (((0,), (0,)), ((), ())), preferred_element_type=F32)


def _sigmoid(x):
    return 0.5 * jnp.tanh(0.5 * x) + 0.5


def _silu(x):
    return x * _sigmoid(x)


def _layer_norm(x, g, b):
    mu = jnp.mean(x, axis=-1, keepdims=True)
    xc = x - mu
    var = jnp.mean(xc * xc, axis=-1, keepdims=True)
    return xc * lax.rsqrt(var + LN_EPS) * g + b


def _pack_halves(v):
    bits = lax.bitcast_convert_type(v, U32)
    return (bits[:, HALF:] & jnp.uint32(HI_MASK)) | (bits[:, :HALF] >> 16)


def _unpack_halves(w):
    lo = lax.bitcast_convert_type(w << 16, F32).astype(BF16)
    hi = lax.bitcast_convert_type(w & jnp.uint32(HI_MASK), F32).astype(BF16)
    return lo, hi


def _params(*sem):
    return pltpu.CompilerParams(dimension_semantics=sem, vmem_limit_bytes=VMEM_LIMIT_BYTES)


def _const_spec(shape):
    nd = len(shape)
    return pl.BlockSpec(shape, lambda *_: (0,) * nd)


def _is_tail_tile():
    return pl.program_id(0) == pl.num_programs(0) - 1


def _main_tail_specs(n_tiles, width):
    return [pl.BlockSpec((TM, width), lambda i, *_: (jnp.minimum(i, n_tiles - 2), 0)),
            pl.BlockSpec((TM, width), lambda i, *_: (0, 0))]


IN_PIECES = ((("q", KEY), ("k", KEY), ("v", VAL), ("r", VAL)),
             (("gk", LANES), ("u", POOL_WIDTH), ("ga", D_MODEL), ("gb", D_MODEL)))
IN_COLS = {}
for _part, _pieces in enumerate(IN_PIECES):
    _col = 0
    for _name, _width in _pieces:
        IN_COLS[_name] = (_part, slice(_col, _col + _width))
        _col += _width
IN_SPLIT = sum(width for _, width in IN_PIECES[0])


def _inproj_kernel(xm_ref, xt_ref, wa_ref, wb_ref, wgk2, bgk, q_o, k_o, v_o, r_o, lf_o, u_o, ga_o, gb_o):
    xb = jnp.where(_is_tail_tile(), xt_ref[...], xm_ref[...]).astype(BF16)
    proj = lambda name: _dot(xb, (wa_ref, wb_ref)[IN_COLS[name][0]][:, IN_COLS[name][1]])
    q_o[...] = proj("q") * (DK ** -0.5)
    k_o[...] = proj("k")
    v_o[...] = proj("v").astype(BF16)
    r_o[...] = proj("r")
    gk = proj("gk").astype(BF16)
    z = _dot(gk, wgk2[...]) + bgk[...]
    log_sig = jnp.minimum(z, 0.0) - jnp.log1p(jnp.exp(-jnp.abs(z)))
    lf_o[...] = log_sig * (1.0 / GATE_NORMALIZER)
    u_o[...] = proj("u")
    ga_o[...] = proj("ga")
    gb_o[...] = proj("gb")


def _inproj(xm, xt, w):
    n = xm.shape[0] // TM + 1
    rows = n * TM
    row = lambda width: pl.BlockSpec((TM, width), lambda i: (i, 0))
    out_shapes = (
        jax.ShapeDtypeStruct((rows, KEY), F32), jax.ShapeDtypeStruct((rows, KEY), F32),
        jax.ShapeDtypeStruct((rows, VAL), BF16), jax.ShapeDtypeStruct((rows, VAL), F32),
        jax.ShapeDtypeStruct((rows, KEY), F32), jax.ShapeDtypeStruct((rows, POOL_WIDTH), F32),
        jax.ShapeDtypeStruct((rows, D_MODEL), F32), jax.ShapeDtypeStruct((rows, D_MODEL), F32))
    weights = (w["win_a"], w["win_b"], w["wgk2"], w["bgk"])
    return pl.pallas_call(
        _inproj_kernel, grid=(n,),
        in_specs=_main_tail_specs(n, D_MODEL) + [_const_spec(a.shape) for a in weights],
        out_specs=[row(KEY), row(KEY), row(VAL), row(VAL), row(KEY), row(POOL_WIDTH), row(D_MODEL), row(D_MODEL)],
        out_shape=out_shapes, compiler_params=_params("arbitrary"), name="inproj",
    )(xm, xt, *weights)


def _gla_kernel(*refs, chunk, n_chunks, has_init):
    if has_init:
        q_ref, k_ref, v_ref, lf_ref, r_ref, gn_ref, s0_ref, o_ref, st_ref = refs
        st_ref[...] = s0_ref[...]
    else:
        q_ref, k_ref, v_ref, lf_ref, r_ref, gn_ref, o_ref, st_ref = refs

        @pl.when(pl.program_id(0) == 0)
        def _():
            st_ref[...] = jnp.zeros_like(st_ref)

    row = lax.broadcasted_iota(I32, (chunk, KEY), 0)
    ta = lax.broadcasted_iota(I32, (chunk, chunk), 0)
    sa = lax.broadcasted_iota(I32, (chunk, chunk), 1)
    gn = gn_ref[...]
    halves = [1 << i for i in range(chunk.bit_length() - 1)]
    diag = ta == sa
    pairs = [((ta ^ sa) < 2 * half) & ((ta & half) != 0) & ((sa & half) == 0) for half in halves]

    def chunk_body(c, carry):
        rows = pl.ds(pl.multiple_of(c * chunk, chunk), chunk)
        lf = lf_ref[rows, :] * LOG2_E
        q = q_ref[rows, :]
        k = k_ref[rows, :]
        qb = q.astype(BF16)
        kb = k.astype(BF16)
        att = []
        for h in range(HEADS):
            hs = slice(h * DK, (h + 1) * DK)
            att.append(jnp.where(diag, _dot_nt(qb[:, hs], kb[:, hs]), 0.0))
        seg, tot = lf, lf
        for half, pair in zip(halves, pairs):
            qs = (q * jnp.exp2(seg)).astype(BF16)
            ks = (k * jnp.exp2(tot - seg)).astype(BF16)
            for h in range(HEADS):
                hs = slice(h * DK, (h + 1) * DK)
                att[h] = jnp.where(pair, _dot_nt(qs[:, hs], ks[:, hs]), att[h])
            if half < SUBLANES:
                upper = (row & half) != 0
                below = pltpu.roll(tot, half, 0)
                above = pltpu.roll(tot, chunk - half, 0)
                seg = seg + jnp.where(upper, below, 0.0)
                tot = tot + jnp.where(upper, below, above)
            else:
                step, n_tiles = half // SUBLANES, chunk // SUBLANES
                seg_t = [seg[i * SUBLANES:(i + 1) * SUBLANES, :] for i in range(n_tiles)]
                tot_t = [tot[i * SUBLANES:(i + 1) * SUBLANES, :] for i in range(n_tiles)]
                seg = jnp.concatenate([seg_t[i] + tot_t[i - step] if i & step else seg_t[i] for i in range(n_tiles)], axis=0)
                tot = jnp.concatenate([tot_t[i] + tot_t[i ^ step] for i in range(n_tiles)], axis=0)
        q_in = (q * jnp.exp2(seg)).astype(BF16)
        k_out = (k * jnp.exp2(tot - seg)).astype(BF16)
        decay = jnp.exp2(tot[0:1, :])
        for h in range(HEADS):
            hs = slice(h * DK, (h + 1) * DK)
            vs = slice(h * DV, (h + 1) * DV)
            state = st_ref[0, h]
            vh = v_ref[rows, vs]
            o = _dot_nt(q_in[:, hs], state.astype(BF16)) + _dot(att[h].astype(BF16), vh)
            st_ref[0, h] = state * decay[:, hs] + _dot_tn(vh, k_out[:, hs])
            ms = jnp.mean(o * o, axis=-1, keepdims=True)
            o = o * lax.rsqrt(ms + RMS_EPS) * gn
            o_ref[rows, vs] = (o * _silu(r_ref[rows, vs])).astype(BF16)
        return carry

    lax.fori_loop(0, n_chunks, chunk_body, 0)


def _gla(q, k, v, lf, r, gn, s0t, *, tile, chunk, n_tiles, block0):
    has_init = s0t is not None
    row = lambda width: pl.BlockSpec((tile, width), lambda i: (block0 + i, 0))
    st_spec = pl.BlockSpec((1, HEADS, DV, DK), (lambda i: (i, 0, 0, 0)) if has_init else (lambda i: (0, 0, 0, 0)))
    n_states = n_tiles if has_init else 1
    in_specs = [row(KEY), row(KEY), row(VAL), row(KEY), row(VAL), _const_spec(gn.shape)]
    args = [q, k, v, lf, r, gn]
    if has_init:
        in_specs.append(st_spec)
        args.append(s0t)
    return pl.pallas_call(
        functools.partial(_gla_kernel, chunk=chunk, n_chunks=tile // chunk, has_init=has_init),
        grid=(n_tiles,), in_specs=in_specs,
        out_specs=[pl.BlockSpec((tile, VAL), lambda i: (i, 0)), st_spec],
        out_shape=(jax.ShapeDtypeStruct((n_tiles * tile, VAL), BF16),
                   jax.ShapeDtypeStruct((n_states, HEADS, DV, DK), F32)),
        compiler_params=_params("arbitrary"), name="gla_init" if has_init else "gla",
    )(*args)


def _window_sums(ext, window):
    shift = 1
    while shift < window:
        ext = ext + pltpu.roll(ext, shift, 0)
        shift *= 2
    return ext


def _mix_kernel(xm_ref, xt_ref, ogm_ref, ogt_ref, u_ref, halo_m_ref, halo_t_ref, ga_ref, gb_ref,
                wbra, wpool, pscale, wbrb, wout, g1, b1, x1_ref):
    tail = _is_tail_tile()
    ya = _dot(jnp.where(tail, ogt_ref[...], ogm_ref[...]), wbra[...])
    u = u_ref[...]
    ext_m = jnp.concatenate([halo_m_ref[0], u], axis=0)
    seg = HALO + DEC_SEQ
    pieces = []
    for b in range(DEC_BATCH):
        pieces += [halo_t_ref[b], u[b * DEC_SEQ:(b + 1) * DEC_SEQ, :]]
    ext_t = jnp.concatenate(pieces, axis=0)
    rowi = lax.broadcasted_iota(I32, (TM, 1), 0)
    pos = jnp.where(tail, PAST_LEN + (rowi & (DEC_SEQ - 1)), pl.program_id(0) * TM + rowi)
    parts = []
    for g, window in enumerate(POOL_WINDOWS):
        cols = slice(g * POOL_GROUP_DIM, (g + 1) * POOL_GROUP_DIM)
        win_m = _window_sums(ext_m[:, cols], window)[HALO:, :]
        win_all = _window_sums(ext_t[:, cols], window)
        win_t = jnp.concatenate([win_all[b * seg + HALO:(b + 1) * seg, :] for b in range(DEC_BATCH)], axis=0)
        cnt = jnp.minimum(window, pos + 1).astype(F32)
        d = jnp.where(tail, win_t, win_m) / cnt - u[:, cols]
        parts.append(_dot(d.astype(BF16), wpool[g]))
    yb_in = jnp.concatenate(parts, axis=1) * pscale[...]
    yb = _dot(yb_in.astype(BF16), wbrb[...])
    mixed = _sigmoid(ga_ref[...]) * ya + _sigmoid(gb_ref[...]) * yb
    mix = _dot(mixed.astype(BF16), wout[...])
    x = jnp.where(tail, xt_ref[...], xm_ref[...])
    x1_ref[...] = _layer_norm(ALPHA * x + mix, g1[...], b1[...])


def _mix(xm, xt, og_m, og_t, u, halo_m, halo_t, ga, gb, w):
    n = xm.shape[0] // TM + 1
    row = lambda width: pl.BlockSpec((TM, width), lambda i: (i, 0))
    weights = (w["wbra"], w["wpool"], w["pscale"], w["wbrb"], w["wout"], w["g1"], w["b1"])
    return pl.pallas_call(
        _mix_kernel, grid=(n,),
        in_specs=_main_tail_specs(n, D_MODEL) + _main_tail_specs(n, VAL) + [
            row(POOL_WIDTH), pl.BlockSpec((1, HALO, POOL_WIDTH), lambda i: (jnp.minimum(i, n - 2), 0, 0)),
            _const_spec(halo_t.shape), row(D_MODEL), row(D_MODEL)] + [_const_spec(a.shape) for a in weights],
        out_specs=row(D_MODEL),
        out_shape=jax.ShapeDtypeStruct((n * TM, D_MODEL), F32),
        compiler_params=_params("arbitrary"), name="mix",
    )(xm, xt, og_m, og_t, u, halo_m, halo_t, ga, gb, *weights)


def _router_kernel(x1_ref, wrh_ref, wrl_ref, br_ref, wt_ref, rank_ref, cnt_ref, *, tile):
    x = x1_ref[...]
    xh = x.astype(BF16)
    xl = (x - xh.astype(F32)).astype(BF16)
    logits = _dot_nt(wrh_ref[...], xh) + _dot_nt(wrl_ref[...], xh) + _dot_nt(wrh_ref[...], xl)
    scores = _sigmoid(logits)
    rows_pad = N_EXPERTS
    n_grp = rows_pad // GROUP_SIZE
    biased = scores + br_ref[...]
    shape3 = (n_grp, GROUP_SIZE, tile)
    b3 = biased.reshape(shape3)
    s3 = scores.reshape(shape3)
    sub = lax.broadcasted_iota(I32, shape3, 1)
    gid = lax.broadcasted_iota(I32, shape3, 0)
    eid = gid * GROUP_SIZE + sub
    m1 = jnp.max(b3, axis=1, keepdims=True)
    i1 = jnp.min(jnp.where(b3 == m1, sub, GROUP_SIZE), axis=1, keepdims=True)
    m2 = jnp.max(jnp.where(sub == i1, NEG_INF, b3), axis=1, keepdims=True)
    gscore = m1 + m2
    gid1 = lax.broadcasted_iota(I32, (n_grp, 1, tile), 0)
    gsel = jnp.zeros((n_grp, 1, tile), jnp.bool_)
    for _ in range(TOPK_GROUPS):
        gm = jnp.max(gscore, axis=0, keepdims=True)
        gi = jnp.min(jnp.where(gscore == gm, gid1, n_grp), axis=0, keepdims=True)
        pick = gid1 == gi
        gsel = gsel | pick
        gscore = jnp.where(pick, NEG_INF, gscore)
    masked = jnp.where(gsel, b3, NEG_INF)
    wsel = jnp.zeros(shape3, F32)
    chosen = jnp.zeros(shape3, jnp.bool_)
    for _ in range(TOP_K):
        m = jnp.max(jnp.max(masked, axis=1, keepdims=True), axis=0, keepdims=True)
        idx = jnp.min(jnp.min(jnp.where(masked == m, eid, rows_pad), axis=1, keepdims=True), axis=0, keepdims=True)
        pick = eid == idx
        wsel = jnp.where(pick, s3, wsel)
        chosen = chosen | pick
        masked = jnp.where(pick, NEG_INF, masked)
    wsum = jnp.sum(jnp.sum(wsel, axis=1, keepdims=True), axis=0, keepdims=True)
    wt_ref[...] = (wsel / wsum * ROUTED_SCALE).reshape(rows_pad, tile)
    sel = jnp.where(chosen, 1.0, 0.0).reshape(rows_pad, tile)
    before = lax.broadcasted_iota(I32, (tile, tile), 0) < lax.broadcasted_iota(I32, (tile, tile), 1)
    rank = _dot(sel.astype(BF16), jnp.where(before, 1.0, 0.0).astype(BF16))
    rank_ref[...] = jnp.where(sel > 0.0, rank, -1.0).astype(I32)
    cnt = jnp.sum(sel, axis=1, keepdims=True).astype(I32)
    cnt_ref[0] = jnp.broadcast_to(cnt, (rows_pad, LANES))


def _router(x1, w):
    n = x1.shape[0] // TM
    rows_pad = N_EXPERTS
    tok = pl.BlockSpec((rows_pad, TM), lambda i: (0, i))
    return pl.pallas_call(
        functools.partial(_router_kernel, tile=TM), grid=(n,),
        in_specs=[pl.BlockSpec((TM, D_MODEL), lambda i: (i, 0)), _const_spec(w["wrh"].shape),
                  _const_spec(w["wrl"].shape), _const_spec(w["br"].shape)],
        out_specs=[tok, tok, pl.BlockSpec((1, rows_pad, LANES), lambda i: (i, 0, 0))],
        out_shape=(jax.ShapeDtypeStruct((rows_pad, x1.shape[0]), F32),
                   jax.ShapeDtypeStruct((rows_pad, x1.shape[0]), I32),
                   jax.ShapeDtypeStruct((n, rows_pad, LANES), I32)),
        compiler_params=_params("arbitrary"), name="router",
    )(x1, w["wrh"], w["wrl"], w["br"])


def _n_row_blocks(n_tokens):
    n_tiles = n_tokens // TM
    worst_rows = n_tokens * TOP_K + n_tiles * N_EXPERTS * (GRANULE - 1) + N_EXPERTS * (ROW_BLOCK - 1)
    return -(-worst_rows // ROW_BLOCK)


def _steps(shift):
    return shift - jnp.concatenate([jnp.zeros_like(shift[..., :1]), shift[..., :-1]], axis=-1)


def _dispatch_tables(cnt, n_blocks):
    padded = (cnt + (GRANULE - 1)) // GRANULE * GRANULE
    run_end = jnp.cumsum(padded, axis=1)
    run_off = run_end - padded
    n_gran = run_end[:, -1] // GRANULE
    rows_e = jnp.sum(padded, axis=0)
    region = (rows_e + (ROW_BLOCK - 1)) // ROW_BLOCK * ROW_BLOCK
    region_end = jnp.cumsum(region)
    region_start = region_end - region
    run_base = region_start[None, :] + jnp.cumsum(padded, axis=0) - padded

    big = padded // COPY_ROWS[0]
    big_first = jnp.cumsum(big, axis=1) - big
    k_big = jnp.arange(COPY_CAP[0], dtype=I32)
    owned = k_big[None, :, None] >= big_first[:, None, :]
    big_list = lambda row0: COPY_ROWS[0] * k_big[None, :] + jnp.sum(
        jnp.where(owned, _steps(row0 - COPY_ROWS[0] * big_first)[:, None, :], 0), axis=2)
    lists_src, lists_dst, counts = [big_list(run_off)], [big_list(run_base)], [jnp.sum(big, axis=1)]
    done = big * COPY_ROWS[0]
    for rows, cap in zip(COPY_ROWS[1:], COPY_CAP[1:]):
        has = (padded // rows) % 2
        pos = jnp.cumsum(has, axis=1) - has
        hit = (has[:, None, :] != 0) & (pos[:, None, :] == jnp.arange(cap, dtype=I32)[None, :, None])
        lists_src.append(jnp.sum(jnp.where(hit, (run_off + done)[:, None, :], 0), axis=2))
        lists_dst.append(jnp.sum(jnp.where(hit, (run_base + done)[:, None, :], 0), axis=2))
        counts.append(jnp.sum(has, axis=1))
        done = done + has * rows
    copy_src = jnp.concatenate(lists_src, axis=1)
    copy_dst = jnp.concatenate(lists_dst, axis=1)
    copy_n = jnp.stack(counts, axis=1)

    n_used = region_end[-1] // ROW_BLOCK
    blocks = jnp.arange(n_blocks, dtype=I32)
    block_expert = jnp.minimum(jnp.sum(blocks[:, None] >= (region_end // ROW_BLOCK)[None, :], axis=1), N_EXPERTS - 1)
    experts = jnp.arange(N_EXPERTS, dtype=I32)
    present = (region > 0).astype(I32)
    ordinal = jnp.cumsum(present) - present
    seq_expert = jnp.sum(jnp.where((present[None, :] != 0) & (ordinal[None, :] == experts[:, None]),
                                   experts[None, :], 0), axis=1)
    block_ord = jnp.sum(jnp.where(block_expert[:, None] == experts[None, :], ordinal[None, :], 0), axis=1)
    tail = (region - rows_e) // GRANULE
    tail_end = jnp.cumsum(tail)
    tail_first = tail_end - tail
    slot = jnp.arange(N_EXPERTS * (ROW_BLOCK // GRANULE - 1), dtype=I32)
    t_step = _steps((region_start + rows_e) // GRANULE - tail_first)
    tail_dst = slot + jnp.sum(jnp.where(slot[:, None] >= tail_first[None, :], t_step[None, :], 0), axis=1)
    as_i32 = lambda a: a.astype(I32).reshape(-1)
    long_run = jnp.any(padded > RUN_CHUNK, axis=1)
    return dict(run_off=as_i32(run_off), run_len=as_i32(padded), long_run=as_i32(long_run),
                n_gran=as_i32(n_gran), copy_src=as_i32(copy_src), copy_dst=as_i32(copy_dst), copy_n=as_i32(copy_n),
                n_used=as_i32(n_used), block_expert=as_i32(block_expert), block_ord=as_i32(block_ord),
                seq_expert=as_i32(seq_expert), n_seq=as_i32(jnp.sum(present)), n_tail=as_i32(tail_end[-1]),
                tail_dst=as_i32(tail_dst * GRANULE))


def _row_copy(src, src_row, dst, dst_row, sem, rows):
    s = pl.multiple_of(src_row, GRANULE)
    d = pl.multiple_of(dst_row, GRANULE)
    return pltpu.make_async_copy(src.at[pl.ds(s, rows), :], dst.at[pl.ds(d, rows), :], sem)


def _grouped_loop(n, body):
    n_groups = n // COPY_GROUP

    def group(j, carry):
        for i in range(COPY_GROUP):
            body(j * COPY_GROUP + i)
        return carry

    def single(i, carry):
        body(i)
        return carry

    lax.fori_loop(0, n_groups, group, 0)
    lax.fori_loop(n_groups * COPY_GROUP, n, single, 0)


def _start_copies(n, make_copy):
    _grouped_loop(n, lambda i: make_copy(i).start())


def _wait_copies(n, src, dst, sem, rows):
    n_groups = n // COPY_GROUP

    def group(j, carry):
        _row_copy(src, 0, dst, 0, sem, COPY_GROUP * rows).wait()
        return carry

    def single(i, carry):
        _row_copy(src, 0, dst, 0, sem, rows).wait()
        return carry

    lax.fori_loop(0, n_groups, group, 0)
    lax.fori_loop(n_groups * COPY_GROUP, n, single, 0)


def _start_tile_copies(tile, copy_n, make_copy):
    for c, rows in enumerate(COPY_ROWS):
        base = tile * sum(COPY_CAP) + sum(COPY_CAP[:c])
        _start_copies(copy_n[tile * len(COPY_ROWS) + c], lambda i, base=base, rows=rows: make_copy(base + i, rows))


def _wait_tile_copies(tile, copy_n, src, dst, sem):
    for c, rows in enumerate(COPY_ROWS):
        _wait_copies(copy_n[tile * len(COPY_ROWS) + c], src, dst, sem, rows)


def _build_selection(sel_scr, rank, values, run_off_ref, run_len_ref, long_run_ref, tile_idx):
    slot = lax.broadcasted_iota(I32, (RUN_CHUNK, TM), 0)

    def write_runs(all_chunks):
        for e in range(N_EXPERTS):
            off = run_off_ref[tile_idx * N_EXPERTS + e]
            rank_e = rank[e:e + 1, :]
            val_e = 1.0 if values is None else values[e:e + 1, :]

            def chunk(c, carry, off=off, rank_e=rank_e, val_e=val_e):
                hit = rank_e == slot + c * RUN_CHUNK
                rows = pl.ds(pl.multiple_of(off + c * RUN_CHUNK, GRANULE), RUN_CHUNK)
                sel_scr[rows, :] = jnp.where(hit, val_e, 0.0)
                return carry

            chunk(0, 0)
            if all_chunks:
                n_chunks = (run_len_ref[tile_idx * N_EXPERTS + e] + (RUN_CHUNK - 1)) // RUN_CHUNK
                lax.fori_loop(1, n_chunks, chunk, 0)

    write_runs(False)

    @pl.when(long_run_ref[tile_idx] != 0)
    def _():
        write_runs(True)


def _dispatch_kernel(run_off, run_len, long_run, n_gran, copy_src, copy_dst, copy_n, n_tail, tail_dst, n_used,
                     x1_ref, rank_ref, xs_hbm, sel_scr, buf_scr, zero_scr, sem):
    t = pl.program_id(0)

    @pl.when(t == 0)
    def _():
        sel_scr[...] = jnp.zeros_like(sel_scr)
        buf_scr[...] = jnp.zeros_like(buf_scr)
        zero_scr[...] = jnp.zeros_like(zero_scr)

    _build_selection(sel_scr, rank_ref[...], None, run_off, run_len, long_run, t)
    xb = x1_ref[...].astype(BF16)
    n_g = n_gran[t]
    buf = buf_scr.at[t % 2]
    for c in range(L_MAX // SEL_CHUNK):
        @pl.when(c * SEL_CHUNK < n_g * GRANULE)
        def _(c=c):
            rows = slice(c * SEL_CHUNK, (c + 1) * SEL_CHUNK)
            buf[rows, :] = _pack_halves(_dot(sel_scr[rows, :].astype(BF16), xb))

    @pl.when(t > 0)
    def _():
        _wait_tile_copies(jnp.maximum(t - 1, 0), copy_n, buf, xs_hbm, sem)

    _start_tile_copies(
        t, copy_n, lambda i, rows: _row_copy(buf, copy_src[i], xs_hbm, copy_dst[i], sem, rows))

    @pl.when(t == pl.num_programs(0) - 1)
    def _():
        _wait_tile_copies(t, copy_n, buf, xs_hbm, sem)
        _start_copies(n_tail[0], lambda i: _row_copy(zero_scr, 0, xs_hbm, tail_dst[i], sem, GRANULE))
        _wait_copies(n_tail[0], zero_scr, xs_hbm, sem, GRANULE)

        def spare_copy(b):
            rows = pl.ds(pl.multiple_of(b * ROW_BLOCK, ROW_BLOCK), ROW_BLOCK)
            return pltpu.make_async_copy(zero_scr, xs_hbm.at[rows, :], sem)

        def start_spare(b, carry):
            spare_copy(b).start()
            return carry

        def wait_spare(b, carry):
            spare_copy(b).wait()
            return carry

        n_blocks = xs_hbm.shape[0] // ROW_BLOCK
        lax.fori_loop(n_used[0], n_blocks, start_spare, 0)
        lax.fori_loop(n_used[0], n_blocks, wait_spare, 0)


def _dispatch(x1, rank, tables, n_blocks):
    n = x1.shape[0] // TM
    return pl.pallas_call(
        _dispatch_kernel,
        grid_spec=pltpu.PrefetchScalarGridSpec(
            num_scalar_prefetch=10, grid=(n,),
            in_specs=[pl.BlockSpec((TM, D_MODEL), lambda i, *_: (i, 0)),
                      pl.BlockSpec((N_EXPERTS, TM), lambda i, *_: (0, i))],
            out_specs=pl.BlockSpec(memory_space=pl.ANY),
            scratch_shapes=[pltpu.VMEM((L_MAX, TM), F32), pltpu.VMEM((2, L_MAX, HALF), U32),
                            pltpu.VMEM((ROW_BLOCK, HALF), U32), pltpu.SemaphoreType.DMA(())]),
        out_shape=jax.ShapeDtypeStruct((n_blocks * ROW_BLOCK, HALF), U32),
        compiler_params=_params("arbitrary"), name="dispatch",
    )(tables["run_off"], tables["run_len"], tables["long_run"], tables["n_gran"], tables["copy_src"],
      tables["copy_dst"], tables["copy_n"], tables["n_tail"],
      tables["tail_dst"], tables["n_used"], x1, rank)


def _experts_kernel(block_expert, block_ord, seq_expert, n_seq, n_used,
                    xs_hbm, wg_hbm, wu_hbm, wd_hbm, y_ref,
                    x_ring, wg_stage, wu_stage, wd_stage, wgu_scr, wd_scr, x_sems, w_sems, *, layer):
    b = pl.program_id(0)
    n_u = n_used[0]

    def fetch(block):
        slot = block % RING
        rows = pl.ds(pl.multiple_of(block * ROW_BLOCK, ROW_BLOCK), ROW_BLOCK)
        return pltpu.make_async_copy(xs_hbm.at[rows, :], x_ring.at[slot], x_sems.at[slot])

    def weight_copies(ordinal):
        e = seq_expert[ordinal]
        slot = ordinal % 2
        return [pltpu.make_async_copy(src.at[layer, e], stage.at[slot], w_sems.at[slot])
                for src, stage in ((wg_hbm, wg_stage), (wu_hbm, wu_stage), (wd_hbm, wd_stage))]

    @pl.when(b == 0)
    def _():
        for copy in weight_copies(0):
            copy.start()
        for first in range(RING - 1):
            @pl.when(first < n_u)
            def _(first=first):
                fetch(first).start()

    @pl.when(b + (RING - 1) < n_u)
    def _():
        fetch(b + (RING - 1)).start()

    @pl.when(b >= n_u)
    def _():
        y_ref[...] = jnp.zeros_like(y_ref)

    @pl.when(b < n_u)
    def _():
        fresh = (b == 0) | (block_expert[b] != block_expert[jnp.maximum(b - 1, 0)])

        @pl.when(fresh)
        def _():
            ordinal = block_ord[b]
            slot = ordinal % 2
            for copy in weight_copies(ordinal):
                copy.wait()
            wgu_scr[:, :D_EXPERT] = wg_stage[slot].astype(BF16)
            wgu_scr[:, D_EXPERT:] = wu_stage[slot].astype(BF16)
            wd_scr[...] = wd_stage[slot].astype(BF16)

            @pl.when(ordinal + 1 < n_seq[0])
            def _():
                for copy in weight_copies(ordinal + 1):
                    copy.start()

        fetch(b).wait()
        lo, hi = _unpack_halves(x_ring[b % RING])
        gu = _dot(lo, wgu_scr[:HALF, :]) + _dot(hi, wgu_scr[HALF:, :])
        hid = (_silu(gu[:, :D_EXPERT]) * gu[:, D_EXPERT:]).astype(BF16)
        y = _dot(hid, wd_scr[...])
        y_ref[...] = _pack_halves(y.astype(BF16).astype(F32))


def _experts(xs, tables, w_e_gate, w_e_up, w_e_down, layer, n_blocks):
    any_spec = pl.BlockSpec(memory_space=pl.ANY)
    return pl.pallas_call(
        functools.partial(_experts_kernel, layer=layer),
        grid_spec=pltpu.PrefetchScalarGridSpec(
            num_scalar_prefetch=5, grid=(n_blocks,),
            in_specs=[any_spec, any_spec, any_spec, any_spec],
            out_specs=pl.BlockSpec((ROW_BLOCK, HALF), lambda b, *_: (b, 0)),
            scratch_shapes=[pltpu.VMEM((RING, ROW_BLOCK, HALF), U32),
                            pltpu.VMEM((2, D_MODEL, D_EXPERT), F32), pltpu.VMEM((2, D_MODEL, D_EXPERT), F32),
                            pltpu.VMEM((2, D_EXPERT, D_MODEL), F32),
                            pltpu.VMEM((D_MODEL, 2 * D_EXPERT), BF16), pltpu.VMEM((D_EXPERT, D_MODEL), BF16),
                            pltpu.SemaphoreType.DMA((RING,)), pltpu.SemaphoreType.DMA((2,))]),
        out_shape=jax.ShapeDtypeStruct(xs.shape, U32),
        compiler_params=_params("arbitrary"), name="experts",
    )(tables["block_expert"], tables["block_ord"], tables["seq_expert"], tables["n_seq"], tables["n_used"],
      xs, w_e_gate, w_e_up, w_e_down)


def _combine_kernel(run_off, run_len, long_run, n_gran, copy_src, copy_dst, copy_n,
                    x1_ref, rank_ref, wt_ref, y_hbm, wsgu, wsd, g2, b2, out_m_ref, out_t_ref,
                    sel_scr, buf_scr, acc_scr, sem):
    t = pl.program_id(0)
    n_tiles = pl.num_programs(0)
    n_g = n_gran[t]

    def fetch(tile):
        dst = buf_scr.at[tile % 2]
        _start_tile_copies(
            tile, copy_n, lambda i, rows: _row_copy(y_hbm, copy_dst[i], dst, copy_src[i], sem, rows))

    @pl.when(t == 0)
    def _():
        buf_scr[...] = jnp.zeros_like(buf_scr)
        fetch(t)

    stale = pl.multiple_of(jnp.minimum(n_g * GRANULE, L_MAX - SEL_CHUNK), GRANULE)
    sel_scr[pl.ds(stale, SEL_CHUNK), :] = jnp.zeros((SEL_CHUNK, TM), F32)
    _build_selection(sel_scr, rank_ref[...], wt_ref[...], run_off, run_len, long_run, t)
    x1 = x1_ref[...]
    xb = x1.astype(BF16)
    gu = _dot(xb, wsgu[...])
    hs = (_silu(gu[:, :D_EXPERT]) * gu[:, D_EXPERT:]).astype(BF16)
    acc_scr[...] = _dot(hs, wsd[...])
    buf = buf_scr.at[t % 2]
    _wait_tile_copies(t, copy_n, y_hbm, buf, sem)

    @pl.when(t + 1 < n_tiles)
    def _():
        fetch(jnp.minimum(t + 1, n_tiles - 1))

    for c in range(L_MAX // SEL_CHUNK):
        @pl.when(c * SEL_CHUNK < n_g * GRANULE)
        def _(c=c):
            rows = slice(c * SEL_CHUNK, (c + 1) * SEL_CHUNK)
            lo, hi = _unpack_halves(buf[rows, :])
            sel = sel_scr[rows, :].astype(BF16)
            acc_scr[:, :HALF] += _dot_tn(sel, lo)
            acc_scr[:, HALF:] += _dot_tn(sel, hi)

    out = _layer_norm(ALPHA * x1 + acc_scr[...], g2[...], b2[...])

    @pl.when(t + 1 < n_tiles)
    def _():
        out_m_ref[...] = out

    @pl.when(t + 1 == n_tiles)
    def _():
        out_t_ref[...] = out


def _combine(x1, rank, wt, y, tables, w):
    n = x1.shape[0] // TM
    small = (w["wsgu"], w["wsd"], w["g2"], w["b2"])
    tok = pl.BlockSpec((N_EXPERTS, TM), lambda i, *_: (0, i))
    return pl.pallas_call(
        _combine_kernel,
        grid_spec=pltpu.PrefetchScalarGridSpec(
            num_scalar_prefetch=7, grid=(n,),
            in_specs=[pl.BlockSpec((TM, D_MODEL), lambda i, *_: (i, 0)), tok, tok,
                      pl.BlockSpec(memory_space=pl.ANY)]
                     + [pl.BlockSpec(a.shape, lambda i, *_, nd=a.ndim: (0,) * nd) for a in small],
            out_specs=_main_tail_specs(n, D_MODEL),
            scratch_shapes=[pltpu.VMEM((L_MAX, TM), F32), pltpu.VMEM((2, L_MAX, HALF), U32),
                            pltpu.VMEM((TM, D_MODEL), F32), pltpu.SemaphoreType.DMA(())]),
        out_shape=(jax.ShapeDtypeStruct((x1.shape[0] - TM, D_MODEL), F32),
                   jax.ShapeDtypeStruct((TM, D_MODEL), F32)),
        compiler_params=_params("arbitrary"), name="combine",
    )(tables["run_off"], tables["run_len"], tables["long_run"], tables["n_gran"], tables["copy_src"],
      tables["copy_dst"], tables["copy_n"], x1, rank, wt, y, *small)


def _moe(x1, w, w_e_gate, w_e_up, w_e_down, layer):
    n_blocks = _n_row_blocks(x1.shape[0])
    wt, rank, cnt = _router(x1, w)
    tables = _dispatch_tables(cnt[:, :N_EXPERTS, 0], n_blocks)
    xs = _dispatch(x1, rank, tables, n_blocks)
    y = _experts(xs, tables, w_e_gate, w_e_up, w_e_down, layer, n_blocks)
    return _combine(x1, rank, wt, y, tables, w)


def _layer_weights(l, w_in, w_gk2, b_gk, gla_norm_g, w_pool, pool_scale, w_br_a, w_br_b, w_out, ln1_g, ln1_b,
                   w_router, b_router, w_sh_gate, w_sh_up, w_sh_down, ln2_g, ln2_b):
    pieces = {}
    pad_rank = LANES - GATE_RANK
    pieces["win_a"] = w_in[l][:, :IN_SPLIT].astype(BF16)
    pieces["win_b"] = jnp.pad(w_in[l][:, IN_SPLIT:], ((0, 0), (pad_rank, 0))).astype(BF16)
    pieces["wgk2"] = jnp.pad(w_gk2[l], ((pad_rank, 0), (0, 0))).astype(BF16)
    pieces["bgk"] = b_gk[l].reshape(1, KEY)
    pieces["gn"] = gla_norm_g[l].reshape(1, DV)
    pieces["wbra"] = w_br_a[l].astype(BF16)
    pieces["wpool"] = w_pool[l].astype(BF16)
    pieces["pscale"] = pool_scale[l].reshape(1, POOL_WIDTH)
    pieces["wbrb"] = w_br_b[l].astype(BF16)
    pieces["wout"] = w_out[l].astype(BF16)
    pieces["g1"] = ln1_g[l].reshape(1, D_MODEL)
    pieces["b1"] = ln1_b[l].reshape(1, D_MODEL)
    wrt = w_router[l].T
    wrh = wrt.astype(BF16)
    pieces["wrh"] = wrh
    pieces["wrl"] = (wrt - wrh.astype(F32)).astype(BF16)
    pieces["br"] = b_router[l].reshape(N_EXPERTS, 1)
    pieces["wsgu"] = jnp.concatenate([w_sh_gate[l], w_sh_up[l]], axis=1).astype(BF16)
    pieces["wsd"] = w_sh_down[l].astype(BF16)
    pieces["g2"] = ln2_g[l].reshape(1, D_MODEL)
    pieces["b2"] = ln2_b[l].reshape(1, D_MODEL)
    return pieces


def kernel(x_prompt, x_sample, state_gla, cache_pool, w_in, w_gk2, b_gk, gla_norm_g, w_pool, pool_scale, w_br_a, w_br_b, w_out, ln1_g, ln1_b, w_router, b_router, w_e_gate, w_e_up, w_e_down, w_sh_gate, w_sh_up, w_sh_down, ln2_g, ln2_b):
    assert T_SAMPLE == TM, "the decode tokens must fill exactly one token tile"
    xm = x_prompt.reshape(T_PROMPT, D_MODEL)
    xt = x_sample.reshape(T_SAMPLE, D_MODEL)
    n_prompt_tiles = T_PROMPT // TM
    sample_block0 = T_PROMPT // DEC_SEQ
    sp, hp, ss, hs = [], [], [], []
    for l in range(DEPTH):
        w = _layer_weights(l, w_in, w_gk2, b_gk, gla_norm_g, w_pool, pool_scale, w_br_a, w_br_b, w_out, ln1_g,
                           ln1_b, w_router, b_router, w_sh_gate, w_sh_up, w_sh_down, ln2_g, ln2_b)
        q, k, v, r, lf, u, ga, gb = _inproj(xm, xt, w)

        og_p, st_p = _gla(q, k, v, lf, r, w["gn"], None, tile=TM, chunk=CHUNK, n_tiles=n_prompt_tiles, block0=0)
        s0t = jnp.swapaxes(state_gla[l], -1, -2)
        og_s, st_s = _gla(q, k, v, lf, r, w["gn"], s0t, tile=DEC_SEQ, chunk=DEC_SEQ, n_tiles=DEC_BATCH,
                          block0=sample_block0)

        u_p = u[:T_PROMPT].reshape(n_prompt_tiles, TM, POOL_WIDTH)
        halo_p = jnp.concatenate([jnp.zeros((1, HALO, POOL_WIDTH), F32), u_p[:-1, TM - HALO:, :]], axis=0)
        halo_s = jnp.concatenate([jnp.zeros((DEC_BATCH, HALO - POOL_HIST, POOL_WIDTH), F32), cache_pool[l]], axis=1)
        x1 = _mix(xm, xt, og_p, og_s, u, halo_p, halo_s, ga, gb, w)

        xm, xt = _moe(x1, w, w_e_gate, w_e_up, w_e_down, l)

        sp.append(jnp.swapaxes(st_p, -1, -2))
        ss.append(jnp.swapaxes(st_s, -1, -2))
        hp.append(u[T_PROMPT - POOL_HIST:T_PROMPT].reshape(1, POOL_HIST, POOL_WIDTH))
        hs.append(u[T_PROMPT:].reshape(DEC_BATCH, DEC_SEQ, POOL_WIDTH)[:, DEC_SEQ - POOL_HIST:, :])
    y_prompt = xm.reshape(1, SEQ, D_MODEL)
    y_sample = xt.reshape(DEC_BATCH, DEC_SEQ, D_MODEL)
    return (y_prompt, y_sample, jnp.stack(sp), jnp.stack(hp), jnp.stack(ss), jnp.stack(hs))
```

```python
import functools

import jax
import jax.numpy as jnp
from jax import lax
from jax.experimental import pallas as pl
from jax.experimental.pallas import tpu as pltpu

F32 = jnp.float32
BF16 = jnp.bfloat16
U32 = jnp.uint32
I32 = jnp.int32

D_MODEL = 1024
DEPTH = 2
SEQ = 16384
DEC_BATCH = 8
DEC_SEQ = 32
PAST_LEN = 4096
CHUNK = 64
HEADS = 4
DK = 128
DV = 256
KEY = HEADS * DK
VAL = HEADS * DV
GATE_RANK = 16
GATE_NORMALIZER = 16.0
POOL_WIDTH = 512
POOL_WINDOWS = (2, 4, 8, 16)
POOL_GROUP_DIM = 128
POOL_HIST = 15
N_EXPERTS = 64
GROUP_SIZE = 8
TOPK_GROUPS = 4
TOP_K = 8
D_EXPERT = 256
ROUTED_SCALE = 2.5
ALPHA = (2 * DEPTH) ** 0.25
LN_EPS = 1e-5
RMS_EPS = 1e-6
LOG2_E = 1.4426950408889634

LANES = 128
SUBLANES = 8
VMEM_LIMIT_BYTES = 56 * 1024 * 1024

T_PROMPT = SEQ
T_SAMPLE = DEC_BATCH * DEC_SEQ
T_ALL = T_PROMPT + T_SAMPLE
TM = 256
HALO = 16
NEG_INF = float("-inf")

HALF = D_MODEL // 2
GRANULE = SUBLANES
ROW_BLOCK = 512
COPY_GROUP = 8
COPY_ROWS = (32, 16, 8)
RING = 3
RUN_CHUNK = 64
L_MAX = 2560
SEL_CHUNK = 512
COPY_CAP = (L_MAX // COPY_ROWS[0], N_EXPERTS, N_EXPERTS)
HI_MASK = 0xFFFF0000


def _dot(a, b):
    return jnp.dot(a, b, preferred_element_type=F32)


def _dot_nt(a, b):
    return lax.dot_general(a, b, (((1,), (1,)), ((), ())), preferred_element_type=F32)


def _dot_tn(a, b):
    return lax.dot_general(a, b, (((0,), (0,)), ((), ())), preferred_element_type=F32)


def _sigmoid(x):
    return 0.5 * jnp.tanh(0.5 * x) + 0.5


def _silu(x):
    return x * _sigmoid(x)


def _layer_norm(x, g, b):
    mu = jnp.mean(x, axis=-1, keepdims=True)
    xc = x - mu
    var = jnp.mean(xc * xc, axis=-1, keepdims=True)
    return xc * lax.rsqrt(var + LN_EPS) * g + b


def _pack_halves(v):
    bits = lax.bitcast_convert_type(v, U32)
    return (bits[:, HALF:] & jnp.uint32(HI_MASK)) | (bits[:, :HALF] >> 16)


def _unpack_halves(w):
    lo = lax.bitcast_convert_type(w << 16, F32).astype(BF16)
    hi = lax.bitcast_convert_type(w & jnp.uint32(HI_MASK), F32).astype(BF16)
    return lo, hi


def _params(*sem):
    return pltpu.CompilerParams(dimension_semantics=sem, vmem_limit_bytes=VMEM_LIMIT_BYTES)


def _const_spec(shape):
    nd = len(shape)
    return pl.BlockSpec(shape, lambda *_: (0,) * nd)


def _is_tail_tile():
    return pl.program_id(0) == pl.num_programs(0) - 1


def _main_tail_specs(n_tiles, width):
    return [pl.BlockSpec((TM, width), lambda i, *_: (jnp.minimum(i, n_tiles - 2), 0)),
            pl.BlockSpec((TM, width), lambda i, *_: (0, 0))]


IN_PIECES = ((("q", KEY), ("k", KEY), ("v", VAL), ("r", VAL)),
             (("gk", LANES), ("u", POOL_WIDTH), ("ga", D_MODEL), ("gb", D_MODEL)))
IN_COLS = {}
for _part, _pieces in enumerate(IN_PIECES):
    _col = 0
    for _name, _width in _pieces:
        IN_COLS[_name] = (_part, slice(_col, _col + _width))
        _col += _width
IN_SPLIT = sum(width for _, width in IN_PIECES[0])


def _inproj_kernel(xm_ref, xt_ref, wa_ref, wb_ref, wgk2, bgk, q_o, k_o, v_o, r_o, lf_o, u_o, ga_o, gb_o):
    xb = jnp.where(_is_tail_tile(), xt_ref[...], xm_ref[...]).astype(BF16)
    proj = lambda name: _dot(xb, (wa_ref, wb_ref)[IN_COLS[name][0]][:, IN_COLS[name][1]])
    q_o[...] = proj("q") * (DK ** -0.5)
    k_o[...] = proj("k")
    v_o[...] = proj("v").astype(BF16)
    r_o[...] = proj("r")
    gk = proj("gk").astype(BF16)
    z = _dot(gk, wgk2[...]) + bgk[...]
    log_sig = jnp.minimum(z, 0.0) - jnp.log1p(jnp.exp(-jnp.abs(z)))
    lf_o[...] = log_sig * (1.0 / GATE_NORMALIZER)
    u_o[...] = proj("u")
    ga_o[...] = proj("ga")
    gb_o[...] = proj("gb")


def _inproj(xm, xt, w):
    n = xm.shape[0] // TM + 1
    rows = n * TM
    row = lambda width: pl.BlockSpec((TM, width), lambda i: (i, 0))
    out_shapes = (
        jax.ShapeDtypeStruct((rows, KEY), F32), jax.ShapeDtypeStruct((rows, KEY), F32),
        jax.ShapeDtypeStruct((rows, VAL), BF16), jax.ShapeDtypeStruct((rows, VAL), F32),
        jax.ShapeDtypeStruct((rows, KEY), F32), jax.ShapeDtypeStruct((rows, POOL_WIDTH), F32),
        jax.ShapeDtypeStruct((rows, D_MODEL), F32), jax.ShapeDtypeStruct((rows, D_MODEL), F32))
    weights = (w["win_a"], w["win_b"], w["wgk2"], w["bgk"])
    return pl.pallas_call(
        _inproj_kernel, grid=(n,),
        in_specs=_main_tail_specs(n, D_MODEL) + [_const_spec(a.shape) for a in weights],
        out_specs=[row(KEY), row(KEY), row(VAL), row(VAL), row(KEY), row(POOL_WIDTH), row(D_MODEL), row(D_MODEL)],
        out_shape=out_shapes, compiler_params=_params("arbitrary"), name="inproj",
    )(xm, xt, *weights)


def _gla_kernel(*refs, chunk, n_chunks, has_init):
    if has_init:
        q_ref, k_ref, v_ref, lf_ref, r_ref, gn_ref, s0_ref, o_ref, s_out_ref, st_ref = refs
        for h in range(HEADS):
            st_ref[h] = s0_ref[0, h].T
    else:
        q_ref, k_ref, v_ref, lf_ref, r_ref, gn_ref, o_ref, s_out_ref, st_ref = refs

        @pl.when(pl.program_id(0) == 0)
        def _():
            st_ref[...] = jnp.zeros_like(st_ref)

    row = lax.broadcasted_iota(I32, (chunk, KEY), 0)
    ta = lax.broadcasted_iota(I32, (chunk, chunk), 0)
    sa = lax.broadcasted_iota(I32, (chunk, chunk), 1)
    gn = gn_ref[...]
    halves = [1 << i for i in range(chunk.bit_length() - 1)]
    diag = ta == sa
    pairs = [((ta ^ sa) < 2 * half) & ((ta & half) != 0) & ((sa & half) == 0) for half in halves]

    def chunk_body(c, carry):
        rows = pl.ds(pl.multiple_of(c * chunk, chunk), chunk)
        lf = lf_ref[rows, :] * LOG2_E
        q = q_ref[rows, :]
        k = k_ref[rows, :]
        qb = q.astype(BF16)
        kb = k.astype(BF16)
        att = []
        for h in range(HEADS):
            hs = slice(h * DK, (h + 1) * DK)
            att.append(jnp.where(diag, _dot_nt(qb[:, hs], kb[:, hs]), 0.0))
        seg, tot = lf, lf
        for half, pair in zip(halves, pairs):
            qs = (q * jnp.exp2(seg)).astype(BF16)
            ks = (k * jnp.exp2(tot - seg)).astype(BF16)
            for h in range(HEADS):
                hs = slice(h * DK, (h + 1) * DK)
                att[h] = jnp.where(pair, _dot_nt(qs[:, hs], ks[:, hs]), att[h])
            if half < SUBLANES:
                upper = (row & half) != 0
                below = pltpu.roll(tot, half, 0)
                above = pltpu.roll(tot, chunk - half, 0)
                seg = seg + jnp.where(upper, below, 0.0)
                tot = tot + jnp.where(upper, below, above)
            else:
                step, n_tiles = half // SUBLANES, chunk // SUBLANES
                seg_t = [seg[i * SUBLANES:(i + 1) * SUBLANES, :] for i in range(n_tiles)]
                tot_t = [tot[i * SUBLANES:(i + 1) * SUBLANES, :] for i in range(n_tiles)]
                seg = jnp.concatenate([seg_t[i] + tot_t[i - step] if i & step else seg_t[i] for i in range(n_tiles)], axis=0)
                tot = jnp.concatenate([tot_t[i] + tot_t[i ^ step] for i in range(n_tiles)], axis=0)
        q_in = (q * jnp.exp2(seg)).astype(BF16)
        k_out = (k * jnp.exp2(tot - seg)).astype(BF16)
        decay = jnp.exp2(tot[0:1, :])
        for h in range(HEADS):
            hs = slice(h * DK, (h + 1) * DK)
            vs = slice(h * DV, (h + 1) * DV)
            state = st_ref[h]
            vh = v_ref[rows, vs]
            o = _dot_nt(q_in[:, hs], state.astype(BF16)) + _dot(att[h].astype(BF16), vh)
            st_ref[h] = state * decay[:, hs] + _dot_tn(vh, k_out[:, hs])
            ms = jnp.mean(o * o, axis=-1, keepdims=True)
            o = o * lax.rsqrt(ms + RMS_EPS) * gn
            o_ref[rows, vs] = (o * _silu(r_ref[rows, vs])).astype(BF16)
        return carry

    lax.fori_loop(0, n_chunks, chunk_body, 0)

    def write_state():
        for h in range(HEADS):
            s_out_ref[0, h] = st_ref[h].T

    if has_init:
        write_state()
    else:
        pl.when(pl.program_id(0) == pl.num_programs(0) - 1)(write_state)


def _gla(q, k, v, lf, r, gn, s0, *, tile, chunk, n_tiles, block0):
    has_init = s0 is not None
    row = lambda width: pl.BlockSpec((tile, width), lambda i: (block0 + i, 0))
    st_spec = pl.BlockSpec((1, HEADS, DK, DV), (lambda i: (i, 0, 0, 0)) if has_init else (lambda i: (0, 0, 0, 0)))
    n_states = n_tiles if has_init else 1
    in_specs = [row(KEY), row(KEY), row(VAL), row(KEY), row(VAL), _const_spec(gn.shape)]
    args = [q, k, v, lf, r, gn]
    if has_init:
        in_specs.append(st_spec)
        args.append(s0)
    return pl.pallas_call(
        functools.partial(_gla_kernel, chunk=chunk, n_chunks=tile // chunk, has_init=has_init),
        grid=(n_tiles,), in_specs=in_specs,
        out_specs=[pl.BlockSpec((tile, VAL), lambda i: (i, 0)), st_spec],
        out_shape=(jax.ShapeDtypeStruct((n_tiles * tile, VAL), BF16),
                   jax.ShapeDtypeStruct((n_states, HEADS, DK, DV), F32)),
        scratch_shapes=[pltpu.VMEM((HEADS, DV, DK), F32)],
        compiler_params=_params("arbitrary"), name="gla_init" if has_init else "gla",
    )(*args)


def _window_sums(ext, window):
    shift = 1
    while shift < window:
        ext = ext + pltpu.roll(ext, shift, 0)
        shift *= 2
    return ext


def _mix_kernel(xm_ref, xt_ref, ogm_ref, ogt_ref, u_ref, halo_m_ref, halo_t_ref, ga_ref, gb_ref,
                wbra, wpool, pscale, wbrb, wout, g1, b1, x1_ref):
    tail = _is_tail_tile()
    ya = _dot(jnp.where(tail, ogt_ref[...], ogm_ref[...]), wbra[...])
    u = u_ref[...]
    ext_m = jnp.concatenate([halo_m_ref[0], u], axis=0)
    seg = HALO + DEC_SEQ
    pieces = []
    for b in range(DEC_BATCH):
        pieces += [halo_t_ref[b], u[b * DEC_SEQ:(b + 1) * DEC_SEQ, :]]
    ext_t = jnp.concatenate(pieces, axis=0)
    rowi = lax.broadcasted_iota(I32, (TM, 1), 0)
    pos = jnp.where(tail, PAST_LEN + (rowi & (DEC_SEQ - 1)), pl.program_id(0) * TM + rowi)
    parts = []
    for g, window in enumerate(POOL_WINDOWS):
        cols = slice(g * POOL_GROUP_DIM, (g + 1) * POOL_GROUP_DIM)
        win_m = _window_sums(ext_m[:, cols], window)[HALO:, :]
        win_all = _window_sums(ext_t[:, cols], window)
        win_t = jnp.concatenate([win_all[b * seg + HALO:(b + 1) * seg, :] for b in range(DEC_BATCH)], axis=0)
        cnt = jnp.minimum(window, pos + 1).astype(F32)
        d = jnp.where(tail, win_t, win_m) / cnt - u[:, cols]
        parts.append(_dot(d.astype(BF16), wpool[g]))
    yb_in = jnp.concatenate(parts, axis=1) * pscale[...]
    yb = _dot(yb_in.astype(BF16), wbrb[...])
    mixed = _sigmoid(ga_ref[...]) * ya + _sigmoid(gb_ref[...]) * yb
    mix = _dot(mixed.astype(BF16), wout[...])
    x = jnp.where(tail, xt_ref[...], xm_ref[...])
    x1_ref[...] = _layer_norm(ALPHA * x + mix, g1[...], b1[...])


def _mix(xm, xt, og_m, og_t, u, halo_m, halo_t, ga, gb, w):
    n = xm.shape[0] // TM + 1
    row = lambda width: pl.BlockSpec((TM, width), lambda i: (i, 0))
    weights = (w["wbra"], w["wpool"], w["pscale"], w["wbrb"], w["wout"], w["g1"], w["b1"])
    return pl.pallas_call(
        _mix_kernel, grid=(n,),
        in_specs=_main_tail_specs(n, D_MODEL) + _main_tail_specs(n, VAL) + [
            row(POOL_WIDTH), pl.BlockSpec((1, HALO, POOL_WIDTH), lambda i: (jnp.minimum(i, n - 2), 0, 0)),
            _const_spec(halo_t.shape), row(D_MODEL), row(D_MODEL)] + [_const_spec(a.shape) for a in weights],
        out_specs=row(D_MODEL),
        out_shape=jax.ShapeDtypeStruct((n * TM, D_MODEL), F32),
        compiler_params=_params("arbitrary"), name="mix",
    )(xm, xt, og_m, og_t, u, halo_m, halo_t, ga, gb, *weights)


def _router_kernel(x1_ref, wrh_ref, wrl_ref, br_ref, wt_ref, rank_ref, cnt_ref, *, tile):
    x = x1_ref[...]
    xh = x.astype(BF16)
    xl = (x - xh.astype(F32)).astype(BF16)
    logits = _dot_nt(wrh_ref[...], xh) + _dot_nt(wrl_ref[...], xh) + _dot_nt(wrh_ref[...], xl)
    scores = _sigmoid(logits)
    rows_pad = N_EXPERTS
    n_grp = rows_pad // GROUP_SIZE
    biased = scores + br_ref[...]
    shape3 = (n_grp, GROUP_SIZE, tile)
    b3 = biased.reshape(shape3)
    s3 = scores.reshape(shape3)
    sub = lax.broadcasted_iota(I32, shape3, 1)
    gid = lax.broadcasted_iota(I32, shape3, 0)
    eid = gid * GROUP_SIZE + sub
    m1 = jnp.max(b3, axis=1, keepdims=True)
    i1 = jnp.min(jnp.where(b3 == m1, sub, GROUP_SIZE), axis=1, keepdims=True)
    m2 = jnp.max(jnp.where(sub == i1, NEG_INF, b3), axis=1, keepdims=True)
    gscore = m1 + m2
    gid1 = lax.broadcasted_iota(I32, (n_grp, 1, tile), 0)
    gsel = jnp.zeros((n_grp, 1, tile), jnp.bool_)
    for _ in range(TOPK_GROUPS):
        gm = jnp.max(gscore, axis=0, keepdims=True)
        gi = jnp.min(jnp.where(gscore == gm, gid1, n_grp), axis=0, keepdims=True)
        pick = gid1 == gi
        gsel = gsel | pick
        gscore = jnp.where(pick, NEG_INF, gscore)
    masked = jnp.where(gsel, b3, NEG_INF)
    wsel = jnp.zeros(shape3, F32)
    chosen = jnp.zeros(shape3, jnp.bool_)
    for _ in range(TOP_K):
        m = jnp.max(jnp.max(masked, axis=1, keepdims=True), axis=0, keepdims=True)
        idx = jnp.min(jnp.min(jnp.where(masked == m, eid, rows_pad), axis=1, keepdims=True), axis=0, keepdims=True)
        pick = eid == idx
        wsel = jnp.where(pick, s3, wsel)
        chosen = chosen | pick
        masked = jnp.where(pick, NEG_INF, masked)
    wsum = jnp.sum(jnp.sum(wsel, axis=1, keepdims=True), axis=0, keepdims=True)
    wt_ref[...] = (wsel / wsum * ROUTED_SCALE).reshape(rows_pad, tile)
    sel = jnp.where(chosen, 1.0, 0.0).reshape(rows_pad, tile)
    before = lax.broadcasted_iota(I32, (tile, tile), 0) < lax.broadcasted_iota(I32, (tile, tile), 1)
    rank = _dot(sel.astype(BF16), jnp.where(before, 1.0, 0.0).astype(BF16))
    rank_ref[...] = jnp.where(sel > 0.0, rank, -1.0).astype(I32)
    cnt = jnp.sum(sel, axis=1, keepdims=True).astype(I32)
    cnt_ref[0] = jnp.broadcast_to(cnt, (rows_pad, LANES))


def _router(x1, w):
    n = x1.shape[0] // TM
    rows_pad = N_EXPERTS
    tok = pl.BlockSpec((rows_pad, TM), lambda i: (0, i))
    return pl.pallas_call(
        functools.partial(_router_kernel, tile=TM), grid=(n,),
        in_specs=[pl.BlockSpec((TM, D_MODEL), lambda i: (i, 0)), _const_spec(w["wrh"].shape),
                  _const_spec(w["wrl"].shape), _const_spec(w["br"].shape)],
        out_specs=[tok, tok, pl.BlockSpec((1, rows_pad, LANES), lambda i: (i, 0, 0))],
        out_shape=(jax.ShapeDtypeStruct((rows_pad, x1.shape[0]), F32),
                   jax.ShapeDtypeStruct((rows_pad, x1.shape[0]), I32),
                   jax.ShapeDtypeStruct((n, rows_pad, LANES), I32)),
        compiler_params=_params("arbitrary"), name="router",
    )(x1, w["wrh"], w["wrl"], w["br"])


def _n_row_blocks(n_tokens):
    n_tiles = n_tokens // TM
    worst_rows = n_tokens * TOP_K + n_tiles * N_EXPERTS * (GRANULE - 1) + N_EXPERTS * (ROW_BLOCK - 1)
    return -(-worst_rows // ROW_BLOCK)


def _steps(shift):
    return shift - jnp.concatenate([jnp.zeros_like(shift[..., :1]), shift[..., :-1]], axis=-1)


def _dispatch_tables(cnt, n_blocks):
    padded = (cnt + (GRANULE - 1)) // GRANULE * GRANULE
    run_end = jnp.cumsum(padded, axis=1)
    run_off = run_end - padded
    n_gran = run_end[:, -1] // GRANULE
    rows_e = jnp.sum(padded, axis=0)
    region = (rows_e + (ROW_BLOCK - 1)) // ROW_BLOCK * ROW_BLOCK
    region_end = jnp.cumsum(region)
    region_start = region_end - region
    run_base = region_start[None, :] + jnp.cumsum(padded, axis=0) - padded

    big = padded // COPY_ROWS[0]
    big_first = jnp.cumsum(big, axis=1) - big
    k_big = jnp.arange(COPY_CAP[0], dtype=I32)
    owned = k_big[None, :, None] >= big_first[:, None, :]
    big_list = lambda row0: COPY_ROWS[0] * k_big[None, :] + jnp.sum(
        jnp.where(owned, _steps(row0 - COPY_ROWS[0] * big_first)[:, None, :], 0), axis=2)
    lists_src, lists_dst, counts = [big_list(run_off)], [big_list(run_base)], [jnp.sum(big, axis=1)]
    done = big * COPY_ROWS[0]
    for rows, cap in zip(COPY_ROWS[1:], COPY_CAP[1:]):
        has = (padded // rows) % 2
        pos = jnp.cumsum(has, axis=1) - has
        hit = (has[:, None, :] != 0) & (pos[:, None, :] == jnp.arange(cap, dtype=I32)[None, :, None])
        lists_src.append(jnp.sum(jnp.where(hit, (run_off + done)[:, None, :], 0), axis=2))
        lists_dst.append(jnp.sum(jnp.where(hit, (run_base + done)[:, None, :], 0), axis=2))
        counts.append(jnp.sum(has, axis=1))
        done = done + has * rows
    copy_src = jnp.concatenate(lists_src, axis=1)
    copy_dst = jnp.concatenate(lists_dst, axis=1)
    copy_n = jnp.stack(counts, axis=1)

    n_used = region_end[-1] // ROW_BLOCK
    blocks = jnp.arange(n_blocks, dtype=I32)
    block_expert = jnp.minimum(jnp.sum(blocks[:, None] >= (region_end // ROW_BLOCK)[None, :], axis=1), N_EXPERTS - 1)
    experts = jnp.arange(N_EXPERTS, dtype=I32)
    present = (region > 0).astype(I32)
    ordinal = jnp.cumsum(present) - present
    seq_expert = jnp.sum(jnp.where((present[None, :] != 0) & (ordinal[None, :] == experts[:, None]),
                                   experts[None, :], 0), axis=1)
    block_ord = jnp.sum(jnp.where(block_expert[:, None] == experts[None, :], ordinal[None, :], 0), axis=1)
    tail = (region - rows_e) // GRANULE
    tail_end = jnp.cumsum(tail)
    tail_first = tail_end - tail
    slot = jnp.arange(N_EXPERTS * (ROW_BLOCK // GRANULE - 1), dtype=I32)
    t_step = _steps((region_start + rows_e) // GRANULE - tail_first)
    tail_dst = slot + jnp.sum(jnp.where(slot[:, None] >= tail_first[None, :], t_step[None, :], 0), axis=1)
    as_i32 = lambda a: a.astype(I32).reshape(-1)
    long_run = jnp.any(padded > RUN_CHUNK, axis=1)
    return dict(run_off=as_i32(run_off), run_len=as_i32(padded), long_run=as_i32(long_run),
                n_gran=as_i32(n_gran), copy_src=as_i32(copy_src), copy_dst=as_i32(copy_dst), copy_n=as_i32(copy_n),
                n_used=as_i32(n_used), block_expert=as_i32(block_expert), block_ord=as_i32(block_ord),
                seq_expert=as_i32(seq_expert), n_seq=as_i32(jnp.sum(present)), n_tail=as_i32(tail_end[-1]),
                tail_dst=as_i32(tail_dst * GRANULE))


def _row_copy(src, src_row, dst, dst_row, sem, rows):
    s = pl.multiple_of(src_row, GRANULE)
    d = pl.multiple_of(dst_row, GRANULE)
    return pltpu.make_async_copy(src.at[pl.ds(s, rows), :], dst.at[pl.ds(d, rows), :], sem)


def _grouped_loop(n, body):
    n_groups = n // COPY_GROUP

    def group(j, carry):
        for i in range(COPY_GROUP):
            body(j * COPY_GROUP + i)
        return carry

    def single(i, carry):
        body(i)
        return carry

    lax.fori_loop(0, n_groups, group, 0)
    lax.fori_loop(n_groups * COPY_GROUP, n, single, 0)


def _start_copies(n, make_copy):
    _grouped_loop(n, lambda i: make_copy(i).start())


def _wait_copies(n, src, dst, sem, rows):
    n_groups = n // COPY_GROUP

    def group(j, carry):
        _row_copy(src, 0, dst, 0, sem, COPY_GROUP * rows).wait()
        return carry

    def single(i, carry):
        _row_copy(src, 0, dst, 0, sem, rows).wait()
        return carry

    lax.fori_loop(0, n_groups, group, 0)
    lax.fori_loop(n_groups * COPY_GROUP, n, single, 0)


def _start_tile_copies(tile, copy_n, make_copy):
    for c, rows in enumerate(COPY_ROWS):
        base = tile * sum(COPY_CAP) + sum(COPY_CAP[:c])
        _start_copies(copy_n[tile * len(COPY_ROWS) + c], lambda i, base=base, rows=rows: make_copy(base + i, rows))


def _wait_tile_copies(tile, copy_n, src, dst, sem):
    for c, rows in enumerate(COPY_ROWS):
        _wait_copies(copy_n[tile * len(COPY_ROWS) + c], src, dst, sem, rows)


def _build_selection(sel_scr, rank, values, run_off_ref, run_len_ref, long_run_ref, tile_idx):
    slot = lax.broadcasted_iota(I32, (RUN_CHUNK, TM), 0)

    def write_runs(all_chunks):
        for e in range(N_EXPERTS):
            off = run_off_ref[tile_idx * N_EXPERTS + e]
            rank_e = rank[e:e + 1, :]
            val_e = 1.0 if values is None else values[e:e + 1, :]

            def chunk(c, carry, off=off, rank_e=rank_e, val_e=val_e):
                hit = rank_e == slot + c * RUN_CHUNK
                rows = pl.ds(pl.multiple_of(off + c * RUN_CHUNK, GRANULE), RUN_CHUNK)
                sel_scr[rows, :] = jnp.where(hit, val_e, 0.0)
                return carry

            chunk(0, 0)
            if all_chunks:
                n_chunks = (run_len_ref[tile_idx * N_EXPERTS + e] + (RUN_CHUNK - 1)) // RUN_CHUNK
                lax.fori_loop(1, n_chunks, chunk, 0)

    write_runs(False)

    @pl.when(long_run_ref[tile_idx] != 0)
    def _():
        write_runs(True)


def _dispatch_kernel(run_off, run_len, long_run, n_gran, copy_src, copy_dst, copy_n, n_tail, tail_dst, n_used,
                     x1_ref, rank_ref, xs_hbm, sel_scr, buf_scr, zero_scr, sem):
    t = pl.program_id(0)

    @pl.when(t == 0)
    def _():
        sel_scr[...] = jnp.zeros_like(sel_scr)
        buf_scr[...] = jnp.zeros_like(buf_scr)
        zero_scr[...] = jnp.zeros_like(zero_scr)

    _build_selection(sel_scr, rank_ref[...], None, run_off, run_len, long_run, t)
    xb = x1_ref[...].astype(BF16)
    n_g = n_gran[t]
    buf = buf_scr.at[t % 2]
    for c in range(L_MAX // SEL_CHUNK):
        @pl.when(c * SEL_CHUNK < n_g * GRANULE)
        def _(c=c):
            rows = slice(c * SEL_CHUNK, (c + 1) * SEL_CHUNK)
            buf[rows, :] = _pack_halves(_dot(sel_scr[rows, :].astype(BF16), xb))

    @pl.when(t > 0)
    def _():
        _wait_tile_copies(jnp.maximum(t - 1, 0), copy_n, buf, xs_hbm, sem)

    _start_tile_copies(
        t, copy_n, lambda i, rows: _row_copy(buf, copy_src[i], xs_hbm, copy_dst[i], sem, rows))

    @pl.when(t == pl.num_programs(0) - 1)
    def _():
        _wait_tile_copies(t, copy_n, buf, xs_hbm, sem)
        _start_copies(n_tail[0], lambda i: _row_copy(zero_scr, 0, xs_hbm, tail_dst[i], sem, GRANULE))
        _wait_copies(n_tail[0], zero_scr, xs_hbm, sem, GRANULE)

        def spare_copy(b):
            rows = pl.ds(pl.multiple_of(b * ROW_BLOCK, ROW_BLOCK), ROW_BLOCK)
            return pltpu.make_async_copy(zero_scr, xs_hbm.at[rows, :], sem)

        def start_spare(b, carry):
            spare_copy(b).start()
            return carry

        def wait_spare(b, carry):
            spare_copy(b).wait()
            return carry

        n_blocks = xs_hbm.shape[0] // ROW_BLOCK
        lax.fori_loop(n_used[0], n_blocks, start_spare, 0)
        lax.fori_loop(n_used[0], n_blocks, wait_spare, 0)


def _dispatch(x1, rank, tables, n_blocks):
    n = x1.shape[0] // TM
    return pl.pallas_call(
        _dispatch_kernel,
        grid_spec=pltpu.PrefetchScalarGridSpec(
            num_scalar_prefetch=10, grid=(n,),
            in_specs=[pl.BlockSpec((TM, D_MODEL), lambda i, *_: (i, 0)),
                      pl.BlockSpec((N_EXPERTS, TM), lambda i, *_: (0, i))],
            out_specs=pl.BlockSpec(memory_space=pl.ANY),
            scratch_shapes=[pltpu.VMEM((L_MAX, TM), F32), pltpu.VMEM((2, L_MAX, HALF), U32),
                            pltpu.VMEM((ROW_BLOCK, HALF), U32), pltpu.SemaphoreType.DMA(())]),
        out_shape=jax.ShapeDtypeStruct((n_blocks * ROW_BLOCK, HALF), U32),
        compiler_params=_params("arbitrary"), name="dispatch",
    )(tables["run_off"], tables["run_len"], tables["long_run"], tables["n_gran"], tables["copy_src"],
      tables["copy_dst"], tables["copy_n"], tables["n_tail"],
      tables["tail_dst"], tables["n_used"], x1, rank)


def _experts_kernel(block_expert, block_ord, seq_expert, n_seq, n_used,
                    xs_hbm, wg_hbm, wu_hbm, wd_hbm, y_ref,
                    x_ring, wg_stage, wu_stage, wd_stage, wgu_scr, wd_scr, x_sems, w_sems, *, layer):
    b = pl.program_id(0)
    n_u = n_used[0]

    def fetch(block):
        slot = block % RING
        rows = pl.ds(pl.multiple_of(block * ROW_BLOCK, ROW_BLOCK), ROW_BLOCK)
        return pltpu.make_async_copy(xs_hbm.at[rows, :], x_ring.at[slot], x_sems.at[slot])

    def weight_copies(ordinal):
        e = seq_expert[ordinal]
        slot = ordinal % 2
        return [pltpu.make_async_copy(src.at[layer, e], stage.at[slot], w_sems.at[slot])
                for src, stage in ((wg_hbm, wg_stage), (wu_hbm, wu_stage), (wd_hbm, wd_stage))]

    @pl.when(b == 0)
    def _():
        for copy in weight_copies(0):
            copy.start()
        for first in range(RING - 1):
            @pl.when(first < n_u)
            def _(first=first):
                fetch(first).start()

    @pl.when(b + (RING - 1) < n_u)
    def _():
        fetch(b + (RING - 1)).start()

    @pl.when(b >= n_u)
    def _():
        y_ref[...] = jnp.zeros_like(y_ref)

    @pl.when(b < n_u)
    def _():
        fresh = (b == 0) | (block_expert[b] != block_expert[jnp.maximum(b - 1, 0)])

        @pl.when(fresh)
        def _():
            ordinal = block_ord[b]
            slot = ordinal % 2
            for copy in weight_copies(ordinal):
                copy.wait()
            wgu_scr[:, :D_EXPERT] = wg_stage[slot].astype(BF16)
            wgu_scr[:, D_EXPERT:] = wu_stage[slot].astype(BF16)
            wd_scr[...] = wd_stage[slot].astype(BF16)

            @pl.when(ordinal + 1 < n_seq[0])
            def _():
                for copy in weight_copies(ordinal + 1):
                    copy.start()

        fetch(b).wait()
        lo, hi = _unpack_halves(x_ring[b % RING])
        gu = _dot(lo, wgu_scr[:HALF, :]) + _dot(hi, wgu_scr[HALF:, :])
        hid = (_silu(gu[:, :D_EXPERT]) * gu[:, D_EXPERT:]).astype(BF16)
        y = _dot(hid, wd_scr[...])
        y_ref[...] = _pack_halves(y.astype(BF16).astype(F32))


def _experts(xs, tables, w_e_gate, w_e_up, w_e_down, layer, n_blocks):
    any_spec = pl.BlockSpec(memory_space=pl.ANY)
    return pl.pallas_call(
        functools.partial(_experts_kernel, layer=layer),
        grid_spec=pltpu.PrefetchScalarGridSpec(
            num_scalar_prefetch=5, grid=(n_blocks,),
            in_specs=[any_spec, any_spec, any_spec, any_spec],
            out_specs=pl.BlockSpec((ROW_BLOCK, HALF), lambda b, *_: (b, 0)),
            scratch_shapes=[pltpu.VMEM((RING, ROW_BLOCK, HALF), U32),
                            pltpu.VMEM((2, D_MODEL, D_EXPERT), F32), pltpu.VMEM((2, D_MODEL, D_EXPERT), F32),
                            pltpu.VMEM((2, D_EXPERT, D_MODEL), F32),
                            pltpu.VMEM((D_MODEL, 2 * D_EXPERT), BF16), pltpu.VMEM((D_EXPERT, D_MODEL), BF16),
                            pltpu.SemaphoreType.DMA((RING,)), pltpu.SemaphoreType.DMA((2,))]),
        out_shape=jax.ShapeDtypeStruct(xs.shape, U32),
        compiler_params=_params("arbitrary"), name="experts",
    )(tables["block_expert"], tables["block_ord"], tables["seq_expert"], tables["n_seq"], tables["n_used"],
      xs, w_e_gate, w_e_up, w_e_down)


def _combine_kernel(run_off, run_len, long_run, n_gran, copy_src, copy_dst, copy_n,
                    x1_ref, rank_ref, wt_ref, y_hbm, wsgu, wsd, g2, b2, out_m_ref, out_t_ref,
                    sel_scr, buf_scr, acc_scr, sem):
    t = pl.program_id(0)
    n_tiles = pl.num_programs(0)
    n_g = n_gran[t]

    def fetch(tile):
        dst = buf_scr.at[tile % 2]
        _start_tile_copies(
            tile, copy_n, lambda i, rows: _row_copy(y_hbm, copy_dst[i], dst, copy_src[i], sem, rows))

    @pl.when(t == 0)
    def _():
        buf_scr[...] = jnp.zeros_like(buf_scr)
        fetch(t)

    stale = pl.multiple_of(jnp.minimum(n_g * GRANULE, L_MAX - SEL_CHUNK), GRANULE)
    sel_scr[pl.ds(stale, SEL_CHUNK), :] = jnp.zeros((SEL_CHUNK, TM), F32)
    _build_selection(sel_scr, rank_ref[...], wt_ref[...], run_off, run_len, long_run, t)
    x1 = x1_ref[...]
    xb = x1.astype(BF16)
    gu = _dot(xb, wsgu[...])
    hs = (_silu(gu[:, :D_EXPERT]) * gu[:, D_EXPERT:]).astype(BF16)
    acc_scr[...] = _dot(hs, wsd[...])
    buf = buf_scr.at[t % 2]
    _wait_tile_copies(t, copy_n, y_hbm, buf, sem)

    @pl.when(t + 1 < n_tiles)
    def _():
        fetch(jnp.minimum(t + 1, n_tiles - 1))

    for c in range(L_MAX // SEL_CHUNK):
        @pl.when(c * SEL_CHUNK < n_g * GRANULE)
        def _(c=c):
            rows = slice(c * SEL_CHUNK, (c + 1) * SEL_CHUNK)
            lo, hi = _unpack_halves(buf[rows, :])
            sel = sel_scr[rows, :].astype(BF16)
            acc_scr[:, :HALF] += _dot_tn(sel, lo)
            acc_scr[:, HALF:] += _dot_tn(sel, hi)

    out = _layer_norm(ALPHA * x1 + acc_scr[...], g2[...], b2[...])

    @pl.when(t + 1 < n_tiles)
    def _():
        out_m_ref[...] = out

    @pl.when(t + 1 == n_tiles)
    def _():
        out_t_ref[...] = out


def _combine(x1, rank, wt, y, tables, w):
    n = x1.shape[0] // TM
    small = (w["wsgu"], w["wsd"], w["g2"], w["b2"])
    tok = pl.BlockSpec((N_EXPERTS, TM), lambda i, *_: (0, i))
    return pl.pallas_call(
        _combine_kernel,
        grid_spec=pltpu.PrefetchScalarGridSpec(
            num_scalar_prefetch=7, grid=(n,),
            in_specs=[pl.BlockSpec((TM, D_MODEL), lambda i, *_: (i, 0)), tok, tok,
                      pl.BlockSpec(memory_space=pl.ANY)]
                     + [pl.BlockSpec(a.shape, lambda i, *_, nd=a.ndim: (0,) * nd) for a in small],
            out_specs=_main_tail_specs(n, D_MODEL),
            scratch_shapes=[pltpu.VMEM((L_MAX, TM), F32), pltpu.VMEM((2, L_MAX, HALF), U32),
                            pltpu.VMEM((TM, D_MODEL), F32), pltpu.SemaphoreType.DMA(())]),
        out_shape=(jax.ShapeDtypeStruct((x1.shape[0] - TM, D_MODEL), F32),
                   jax.ShapeDtypeStruct((TM, D_MODEL), F32)),
        compiler_params=_params("arbitrary"), name="combine",
    )(tables["run_off"], tables["run_len"], tables["long_run"], tables["n_gran"], tables["copy_src"],
      tables["copy_dst"], tables["copy_n"], x1, rank, wt, y, *small)


def _moe(x1, w, w_e_gate, w_e_up, w_e_down, layer):
    n_blocks = _n_row_blocks(x1.shape[0])
    wt, rank, cnt = _router(x1, w)
    tables = _dispatch_tables(cnt[:, :N_EXPERTS, 0], n_blocks)
    xs = _dispatch(x1, rank, tables, n_blocks)
    y = _experts(xs, tables, w_e_gate, w_e_up, w_e_down, layer, n_blocks)
    return _combine(x1, rank, wt, y, tables, w)


def _layer_weights(l, w_in, w_gk2, b_gk, gla_norm_g, w_pool, pool_scale, w_br_a, w_br_b, w_out, ln1_g, ln1_b,
                   w_router, b_router, w_sh_gate, w_sh_up, w_sh_down, ln2_g, ln2_b):
    pieces = {}
    pad_rank = LANES - GATE_RANK
    w_in_l = w_in[l].astype(BF16)
    pieces["win_a"] = w_in_l[:, :IN_SPLIT]
    pieces["win_b"] = jnp.pad(w_in_l[:, IN_SPLIT:], ((0, 0), (pad_rank, 0)))
    pieces["wgk2"] = jnp.pad(w_gk2[l], ((pad_rank, 0), (0, 0))).astype(BF16)
    pieces["bgk"] = b_gk[l].reshape(1, KEY)
    pieces["gn"] = gla_norm_g[l].reshape(1, DV)
    pieces["wbra"] = w_br_a[l].astype(BF16)
    pieces["wpool"] = w_pool[l].astype(BF16)
    pieces["pscale"] = pool_scale[l].reshape(1, POOL_WIDTH)
    pieces["wbrb"] = w_br_b[l].astype(BF16)
    pieces["wout"] = w_out[l].astype(BF16)
    pieces["g1"] = ln1_g[l].reshape(1, D_MODEL)
    pieces["b1"] = ln1_b[l].reshape(1, D_MODEL)
    wrt = w_router[l].T
    wrh = wrt.astype(BF16)
    pieces["wrh"] = wrh
    pieces["wrl"] = (wrt - wrh.astype(F32)).astype(BF16)
    pieces["br"] = b_router[l].reshape(N_EXPERTS, 1)
    pieces["wsgu"] = jnp.concatenate([w_sh_gate[l], w_sh_up[l]], axis=1).astype(BF16)
    pieces["wsd"] = w_sh_down[l].astype(BF16)
    pieces["g2"] = ln2_g[l].reshape(1, D_MODEL)
    pieces["b2"] = ln2_b[l].reshape(1, D_MODEL)
    return pieces


def kernel(x_prompt, x_sample, state_gla, cache_pool, w_in, w_gk2, b_gk, gla_norm_g, w_pool, pool_scale, w_br_a, w_br_b, w_out, ln1_g, ln1_b, w_router, b_router, w_e_gate, w_e_up, w_e_down, w_sh_gate, w_sh_up, w_sh_down, ln2_g, ln2_b):
    assert T_SAMPLE == TM, "the decode tokens must fill exactly one token tile"
    xm = x_prompt.reshape(T_PROMPT, D_MODEL)
    xt = x_sample.reshape(T_SAMPLE, D_MODEL)
    n_prompt_tiles = T_PROMPT // TM
    sample_block0 = T_PROMPT // DEC_SEQ
    sp, hp, ss, hs = [], [], [], []
    for l in range(DEPTH):
        w = _layer_weights(l, w_in, w_gk2, b_gk, gla_norm_g, w_pool, pool_scale, w_br_a, w_br_b, w_out, ln1_g,
                           ln1_b, w_router, b_router, w_sh_gate, w_sh_up, w_sh_down, ln2_g, ln2_b)
        q, k, v, r, lf, u, ga, gb = _inproj(xm, xt, w)

        og_p, st_p = _gla(q, k, v, lf, r, w["gn"], None, tile=TM, chunk=CHUNK, n_tiles=n_prompt_tiles, block0=0)
        og_s, st_s = _gla(q, k, v, lf, r, w["gn"], state_gla[l], tile=DEC_SEQ, chunk=DEC_SEQ, n_tiles=DEC_BATCH,
                          block0=sample_block0)

        u_p = u[:T_PROMPT].reshape(n_prompt_tiles, TM, POOL_WIDTH)
        halo_p = jnp.concatenate([jnp.zeros((1, HALO, POOL_WIDTH), F32), u_p[:-1, TM - HALO:, :]], axis=0)
        halo_s = jnp.concatenate([jnp.zeros((DEC_BATCH, HALO - POOL_HIST, POOL_WIDTH), F32), cache_pool[l]], axis=1)
        x1 = _mix(xm, xt, og_p, og_s, u, halo_p, halo_s, ga, gb, w)

        xm, xt = _moe(x1, w, w_e_gate, w_e_up, w_e_down, l)

        sp.append(st_p)
        ss.append(st_s)
        hp.append(u[T_PROMPT - POOL_HIST:T_PROMPT].reshape(1, POOL_HIST, POOL_WIDTH))
        hs.append(u[T_PROMPT:].reshape(DEC_BATCH, DEC_SEQ, POOL_WIDTH)[:, DEC_SEQ - POOL_HIST:, :])
    y_prompt = xm.reshape(1, SEQ, D_MODEL)
    y_sample = xt.reshape(DEC_BATCH, DEC_SEQ, D_MODEL)
    return (y_prompt, y_sample, jnp.stack(sp), jnp.stack(hp), jnp.stack(ss), jnp.stack(hs))
```

```python
import functools

import jax
import jax.numpy as jnp
from jax import lax
from jax.experimental import pallas as pl
from jax.experimental.pallas import tpu as pltpu

F32 = jnp.float32
BF16 = jnp.bfloat16
U32 = jnp.uint32
I32 = jnp.int32

D_MODEL = 1024
DEPTH = 2
SEQ = 16384
DEC_BATCH = 8
DEC_SEQ = 32
PAST_LEN = 4096
CHUNK = 64
HEADS = 4
DK = 128
DV = 256
KEY = HEADS * DK
VAL = HEADS * DV
GATE_RANK = 16
GATE_NORMALIZER = 16.0
POOL_WIDTH = 512
POOL_WINDOWS = (2, 4, 8, 16)
POOL_GROUP_DIM = 128
POOL_HIST = 15
N_EXPERTS = 64
GROUP_SIZE = 8
TOPK_GROUPS = 4
TOP_K = 8
D_EXPERT = 256
ROUTED_SCALE = 2.5
ALPHA = (2 * DEPTH) ** 0.25
LN_EPS = 1e-5
RMS_EPS = 1e-6
LOG2_E = 1.4426950408889634

LANES = 128
SUBLANES = 8
VMEM_LIMIT_BYTES = 56 * 1024 * 1024

T_PROMPT = SEQ
T_SAMPLE = DEC_BATCH * DEC_SEQ
T_ALL = T_PROMPT + T_SAMPLE
TM = 256
HALO = 16
NEG_INF = float("-inf")

HALF = D_MODEL // 2
GRANULE = SUBLANES
ROW_BLOCK = 512
COPY_GROUP = 8
COPY_ROWS = (32, 16, 8)
RING = 3
RUN_CHUNK = 64
L_MAX = 2560
SEL_CHUNK = 512
COPY_CAP = (L_MAX // COPY_ROWS[0], N_EXPERTS, N_EXPERTS)
HI_MASK = 0xFFFF0000


def _dot(a, b):
    return jnp.dot(a, b, preferred_element_type=F32)


def _dot_nt(a, b):
    return lax.dot_general(a, b, (((1,), (1,)), ((), ())), preferred_element_type=F32)


def _dot_tn(a, b):
    return lax.dot_general(a, b, (((0,), (0,)), ((), ())), preferred_element_type=F32)


def _sigmoid(x):
    return 0.5 * jnp.tanh(0.5 * x) + 0.5


def _silu(x):
    return x * _sigmoid(x)


def _layer_norm(x, g, b):
    mu = jnp.mean(x, axis=-1, keepdims=True)
    xc = x - mu
    var = jnp.mean(xc * xc, axis=-1, keepdims=True)
    return xc * lax.rsqrt(var + LN_EPS) * g + b


def _pack_halves(v):
    bits = lax.bitcast_convert_type(v, U32)
    return (bits[:, HALF:] & jnp.uint32(HI_MASK)) | (bits[:, :HALF] >> 16)


def _unpack_halves(w):
    lo = lax.bitcast_convert_type(w << 16, F32).astype(BF16)
    hi = lax.bitcast_convert_type(w & jnp.uint32(HI_MASK), F32).astype(BF16)
    return lo, hi


def _params(*sem):
    return pltpu.CompilerParams(dimension_semantics=sem, vmem_limit_bytes=VMEM_LIMIT_BYTES)


def _const_spec(shape):
    nd = len(shape)
    return pl.BlockSpec(shape, lambda *_: (0,) * nd)


def _is_tail_tile():
    return pl.program_id(0) == pl.num_programs(0) - 1


def _main_tail_specs(n_tiles, width):
    return [pl.BlockSpec((TM, width), lambda i, *_: (jnp.minimum(i, n_tiles - 2), 0)),
            pl.BlockSpec((TM, width), lambda i, *_: (0, 0))]


IN_PIECES = ((("q", KEY), ("k", KEY), ("v", VAL), ("r", VAL)),
             (("gk", LANES), ("u", POOL_WIDTH), ("ga", D_MODEL), ("gb", D_MODEL)))
IN_COLS = {}
for _part, _pieces in enumerate(IN_PIECES):
    _col = 0
    for _name, _width in _pieces:
        IN_COLS[_name] = (_part, slice(_col, _col + _width))
        _col += _width
IN_SPLIT = sum(width for _, width in IN_PIECES[0])
IN_REST = sum(width for _, width in IN_PIECES[1][1:])
N_PROJ = IN_SPLIT + GATE_RANK + IN_REST
W_CHUNK = 256


def _inproj_kernel(xm_ref, xt_ref, w_hbm, wgk2, bgk, q_o, k_o, v_o, r_o, lf_o, u_o, ga_o, gb_o,
                   stage, wa_scr, wb_scr, sems, *, layer):
    @pl.when(pl.program_id(0) == 0)
    def _():
        n_chunks = D_MODEL // W_CHUNK

        def copy(c):
            rows = pl.ds(c * W_CHUNK, W_CHUNK)
            return pltpu.make_async_copy(w_hbm.at[layer, rows, :], stage.at[c % 2], sems.at[c % 2])

        copy(0).start()
        lane = lax.broadcasted_iota(I32, (W_CHUNK, LANES), 1)
        for c in range(n_chunks):
            if c + 1 < n_chunks:
                copy(c + 1).start()
            copy(c).wait()
            blk = stage[c % 2]
            rows = slice(c * W_CHUNK, (c + 1) * W_CHUNK)
            wa_scr[rows, :] = blk[:, :IN_SPLIT].astype(BF16)
            gate = jnp.where(lane < GATE_RANK, blk[:, IN_SPLIT:IN_SPLIT + LANES], 0.0)
            wb_scr[rows, :LANES] = gate.astype(BF16)
            wb_scr[rows, LANES:] = blk[:, IN_SPLIT + GATE_RANK:].astype(BF16)

    xb = jnp.where(_is_tail_tile(), xt_ref[...], xm_ref[...]).astype(BF16)
    proj = lambda name: _dot(xb, (wa_scr, wb_scr)[IN_COLS[name][0]][:, IN_COLS[name][1]])
    q_o[...] = proj("q") * (DK ** -0.5)
    k_o[...] = proj("k")
    v_o[...] = proj("v").astype(BF16)
    r_o[...] = proj("r")
    gk = proj("gk").astype(BF16)
    z = _dot(gk, wgk2[...]) + bgk[...]
    log_sig = jnp.minimum(z, 0.0) - jnp.log1p(jnp.exp(-jnp.abs(z)))
    lf_o[...] = log_sig * (1.0 / GATE_NORMALIZER)
    u_o[...] = proj("u")
    ga_o[...] = proj("ga")
    gb_o[...] = proj("gb")


def _inproj(xm, xt, w_in, w, layer):
    n = xm.shape[0] // TM + 1
    rows = n * TM
    row = lambda width: pl.BlockSpec((TM, width), lambda i: (i, 0))
    out_shapes = (
        jax.ShapeDtypeStruct((rows, KEY), F32), jax.ShapeDtypeStruct((rows, KEY), F32),
        jax.ShapeDtypeStruct((rows, VAL), BF16), jax.ShapeDtypeStruct((rows, VAL), F32),
        jax.ShapeDtypeStruct((rows, KEY), F32), jax.ShapeDtypeStruct((rows, POOL_WIDTH), F32),
        jax.ShapeDtypeStruct((rows, D_MODEL), F32), jax.ShapeDtypeStruct((rows, D_MODEL), F32))
    small = (w["wgk2"], w["bgk"])
    return pl.pallas_call(
        functools.partial(_inproj_kernel, layer=layer), grid=(n,),
        in_specs=_main_tail_specs(n, D_MODEL) + [pl.BlockSpec(memory_space=pl.ANY)]
                 + [_const_spec(a.shape) for a in small],
        out_specs=[row(KEY), row(KEY), row(VAL), row(VAL), row(KEY), row(POOL_WIDTH), row(D_MODEL), row(D_MODEL)],
        out_shape=out_shapes,
        scratch_shapes=[pltpu.VMEM((2, W_CHUNK, N_PROJ), F32), pltpu.VMEM((D_MODEL, IN_SPLIT), BF16),
                        pltpu.VMEM((D_MODEL, LANES + IN_REST), BF16), pltpu.SemaphoreType.DMA((2,))],
        compiler_params=_params("arbitrary"), name="inproj",
    )(xm, xt, w_in, *small)


def _gla_kernel(*refs, chunk, n_chunks, has_init):
    if has_init:
        q_ref, k_ref, v_ref, lf_ref, r_ref, gn_ref, s0_ref, o_ref, s_out_ref, st_ref = refs
        for h in range(HEADS):
            st_ref[h] = s0_ref[0, h].T
    else:
        q_ref, k_ref, v_ref, lf_ref, r_ref, gn_ref, o_ref, s_out_ref, st_ref = refs

        @pl.when(pl.program_id(0) == 0)
        def _():
            st_ref[...] = jnp.zeros_like(st_ref)

    row = lax.broadcasted_iota(I32, (chunk, KEY), 0)
    ta = lax.broadcasted_iota(I32, (chunk, chunk), 0)
    sa = lax.broadcasted_iota(I32, (chunk, chunk), 1)
    gn = gn_ref[...]
    halves = [1 << i for i in range(chunk.bit_length() - 1)]
    diag = ta == sa
    pairs = [((ta ^ sa) < 2 * half) & ((ta & half) != 0) & ((sa & half) == 0) for half in halves]

    def chunk_body(c, carry):
        rows = pl.ds(pl.multiple_of(c * chunk, chunk), chunk)
        lf = lf_ref[rows, :] * LOG2_E
        q = q_ref[rows, :]
        k = k_ref[rows, :]
        qb = q.astype(BF16)
        kb = k.astype(BF16)
        att = []
        for h in range(HEADS):
            hs = slice(h * DK, (h + 1) * DK)
            att.append(jnp.where(diag, _dot_nt(qb[:, hs], kb[:, hs]), 0.0))
        seg, tot = lf, lf
        for half, pair in zip(halves, pairs):
            qs = (q * jnp.exp2(seg)).astype(BF16)
            ks = (k * jnp.exp2(tot - seg)).astype(BF16)
            for h in range(HEADS):
                hs = slice(h * DK, (h + 1) * DK)
                att[h] = jnp.where(pair, _dot_nt(qs[:, hs], ks[:, hs]), att[h])
            if half < SUBLANES:
                upper = (row & half) != 0
                below = pltpu.roll(tot, half, 0)
                above = pltpu.roll(tot, chunk - half, 0)
                seg = seg + jnp.where(upper, below, 0.0)
                tot = tot + jnp.where(upper, below, above)
            else:
                step, n_tiles = half // SUBLANES, chunk // SUBLANES
                seg_t = [seg[i * SUBLANES:(i + 1) * SUBLANES, :] for i in range(n_tiles)]
                tot_t = [tot[i * SUBLANES:(i + 1) * SUBLANES, :] for i in range(n_tiles)]
                seg = jnp.concatenate([seg_t[i] + tot_t[i - step] if i & step else seg_t[i] for i in range(n_tiles)], axis=0)
                tot = jnp.concatenate([tot_t[i] + tot_t[i ^ step] for i in range(n_tiles)], axis=0)
        q_in = (q * jnp.exp2(seg)).astype(BF16)
        k_out = (k * jnp.exp2(tot - seg)).astype(BF16)
        decay = jnp.exp2(tot[0:1, :])
        for h in range(HEADS):
            hs = slice(h * DK, (h + 1) * DK)
            vs = slice(h * DV, (h + 1) * DV)
            state = st_ref[h]
            vh = v_ref[rows, vs]
            o = _dot_nt(q_in[:, hs], state.astype(BF16)) + _dot(att[h].astype(BF16), vh)
            st_ref[h] = state * decay[:, hs] + _dot_tn(vh, k_out[:, hs])
            ms = jnp.mean(o * o, axis=-1, keepdims=True)
            o = o * lax.rsqrt(ms + RMS_EPS) * gn
            o_ref[rows, vs] = (o * _silu(r_ref[rows, vs])).astype(BF16)
        return carry

    lax.fori_loop(0, n_chunks, chunk_body, 0)

    def write_state():
        for h in range(HEADS):
            s_out_ref[0, h] = st_ref[h].T

    if has_init:
        write_state()
    else:
        pl.when(pl.program_id(0) == pl.num_programs(0) - 1)(write_state)


def _gla(q, k, v, lf, r, gn, s0, *, tile, chunk, n_tiles, block0):
    has_init = s0 is not None
    row = lambda width: pl.BlockSpec((tile, width), lambda i: (block0 + i, 0))
    st_spec = pl.BlockSpec((1, HEADS, DK, DV), (lambda i: (i, 0, 0, 0)) if has_init else (lambda i: (0, 0, 0, 0)))
    n_states = n_tiles if has_init else 1
    in_specs = [row(KEY), row(KEY), row(VAL), row(KEY), row(VAL), _const_spec(gn.shape)]
    args = [q, k, v, lf, r, gn]
    if has_init:
        in_specs.append(st_spec)
        args.append(s0)
    return pl.pallas_call(
        functools.partial(_gla_kernel, chunk=chunk, n_chunks=tile // chunk, has_init=has_init),
        grid=(n_tiles,), in_specs=in_specs,
        out_specs=[pl.BlockSpec((tile, VAL), lambda i: (i, 0)), st_spec],
        out_shape=(jax.ShapeDtypeStruct((n_tiles * tile, VAL), BF16),
                   jax.ShapeDtypeStruct((n_states, HEADS, DK, DV), F32)),
        scratch_shapes=[pltpu.VMEM((HEADS, DV, DK), F32)],
        compiler_params=_params("arbitrary"), name="gla_init" if has_init else "gla",
    )(*args)


def _window_sums(ext, window):
    shift = 1
    while shift < window:
        ext = ext + pltpu.roll(ext, shift, 0)
        shift *= 2
    return ext


def _mix_kernel(xm_ref, xt_ref, ogm_ref, ogt_ref, u_ref, halo_m_ref, halo_t_ref, ga_ref, gb_ref,
                wbra, wpool, pscale, wbrb, wout, g1, b1, x1_ref):
    tail = _is_tail_tile()
    ya = _dot(jnp.where(tail, ogt_ref[...], ogm_ref[...]), wbra[...])
    u = u_ref[...]
    ext_m = jnp.concatenate([halo_m_ref[0], u], axis=0)
    seg = HALO + DEC_SEQ
    pieces = []
    for b in range(DEC_BATCH):
        pieces += [halo_t_ref[b], u[b * DEC_SEQ:(b + 1) * DEC_SEQ, :]]
    ext_t = jnp.concatenate(pieces, axis=0)
    rowi = lax.broadcasted_iota(I32, (TM, 1), 0)
    pos = jnp.where(tail, PAST_LEN + (rowi & (DEC_SEQ - 1)), pl.program_id(0) * TM + rowi)
    parts = []
    for g, window in enumerate(POOL_WINDOWS):
        cols = slice(g * POOL_GROUP_DIM, (g + 1) * POOL_GROUP_DIM)
        win_m = _window_sums(ext_m[:, cols], window)[HALO:, :]
        win_all = _window_sums(ext_t[:, cols], window)
        win_t = jnp.concatenate([win_all[b * seg + HALO:(b + 1) * seg, :] for b in range(DEC_BATCH)], axis=0)
        cnt = jnp.minimum(window, pos + 1).astype(F32)
        d = jnp.where(tail, win_t, win_m) / cnt - u[:, cols]
        parts.append(_dot(d.astype(BF16), wpool[g]))
    yb_in = jnp.concatenate(parts, axis=1) * pscale[...]
    yb = _dot(yb_in.astype(BF16), wbrb[...])
    mixed = _sigmoid(ga_ref[...]) * ya + _sigmoid(gb_ref[...]) * yb
    mix = _dot(mixed.astype(BF16), wout[...])
    x = jnp.where(tail, xt_ref[...], xm_ref[...])
    x1_ref[...] = _layer_norm(ALPHA * x + mix, g1[...], b1[...])


def _mix(xm, xt, og_m, og_t, u, halo_m, halo_t, ga, gb, w):
    n = xm.shape[0] // TM + 1
    row = lambda width: pl.BlockSpec((TM, width), lambda i: (i, 0))
    weights = (w["wbra"], w["wpool"], w["pscale"], w["wbrb"], w["wout"], w["g1"], w["b1"])
    return pl.pallas_call(
        _mix_kernel, grid=(n,),
        in_specs=_main_tail_specs(n, D_MODEL) + _main_tail_specs(n, VAL) + [
            row(POOL_WIDTH), pl.BlockSpec((1, HALO, POOL_WIDTH), lambda i: (jnp.minimum(i, n - 2), 0, 0)),
            _const_spec(halo_t.shape), row(D_MODEL), row(D_MODEL)] + [_const_spec(a.shape) for a in weights],
        out_specs=row(D_MODEL),
        out_shape=jax.ShapeDtypeStruct((n * TM, D_MODEL), F32),
        compiler_params=_params("arbitrary"), name="mix",
    )(xm, xt, og_m, og_t, u, halo_m, halo_t, ga, gb, *weights)


def _router_kernel(x1_ref, wrh_ref, wrl_ref, br_ref, wt_ref, rank_ref, cnt_ref, *, tile):
    x = x1_ref[...]
    xh = x.astype(BF16)
    xl = (x - xh.astype(F32)).astype(BF16)
    logits = _dot_nt(wrh_ref[...], xh) + _dot_nt(wrl_ref[...], xh) + _dot_nt(wrh_ref[...], xl)
    scores = _sigmoid(logits)
    rows_pad = N_EXPERTS
    n_grp = rows_pad // GROUP_SIZE
    biased = scores + br_ref[...]
    shape3 = (n_grp, GROUP_SIZE, tile)
    b3 = biased.reshape(shape3)
    s3 = scores.reshape(shape3)
    sub = lax.broadcasted_iota(I32, shape3, 1)
    gid = lax.broadcasted_iota(I32, shape3, 0)
    eid = gid * GROUP_SIZE + sub
    m1 = jnp.max(b3, axis=1, keepdims=True)
    i1 = jnp.min(jnp.where(b3 == m1, sub, GROUP_SIZE), axis=1, keepdims=True)
    m2 = jnp.max(jnp.where(sub == i1, NEG_INF, b3), axis=1, keepdims=True)
    gscore = m1 + m2
    gid1 = lax.broadcasted_iota(I32, (n_grp, 1, tile), 0)
    gsel = jnp.zeros((n_grp, 1, tile), jnp.bool_)
    for _ in range(TOPK_GROUPS):
        gm = jnp.max(gscore, axis=0, keepdims=True)
        gi = jnp.min(jnp.where(gscore == gm, gid1, n_grp), axis=0, keepdims=True)
        pick = gid1 == gi
        gsel = gsel | pick
        gscore = jnp.where(pick, NEG_INF, gscore)
    masked = jnp.where(gsel, b3, NEG_INF)
    wsel = jnp.zeros(shape3, F32)
    chosen = jnp.zeros(shape3, jnp.bool_)
    for _ in range(TOP_K):
        m = jnp.max(jnp.max(masked, axis=1, keepdims=True), axis=0, keepdims=True)
        idx = jnp.min(jnp.min(jnp.where(masked == m, eid, rows_pad), axis=1, keepdims=True), axis=0, keepdims=True)
        pick = eid == idx
        wsel = jnp.where(pick, s3, wsel)
        chosen = chosen | pick
        masked = jnp.where(pick, NEG_INF, masked)
    wsum = jnp.sum(jnp.sum(wsel, axis=1, keepdims=True), axis=0, keepdims=True)
    wt_ref[...] = (wsel / wsum * ROUTED_SCALE).reshape(rows_pad, tile)
    sel = jnp.where(chosen, 1.0, 0.0).reshape(rows_pad, tile)
    before = lax.broadcasted_iota(I32, (tile, tile), 0) < lax.broadcasted_iota(I32, (tile, tile), 1)
    rank = _dot(sel.astype(BF16), jnp.where(before, 1.0, 0.0).astype(BF16))
    rank_ref[...] = jnp.where(sel > 0.0, rank, -1.0).astype(I32)
    cnt = jnp.sum(sel, axis=1, keepdims=True).astype(I32)
    cnt_ref[0] = jnp.broadcast_to(cnt, (rows_pad, LANES))


def _router(x1, w):
    n = x1.shape[0] // TM
    rows_pad = N_EXPERTS
    tok = pl.BlockSpec((rows_pad, TM), lambda i: (0, i))
    return pl.pallas_call(
        functools.partial(_router_kernel, tile=TM), grid=(n,),
        in_specs=[pl.BlockSpec((TM, D_MODEL), lambda i: (i, 0)), _const_spec(w["wrh"].shape),
                  _const_spec(w["wrl"].shape), _const_spec(w["br"].shape)],
        out_specs=[tok, tok, pl.BlockSpec((1, rows_pad, LANES), lambda i: (i, 0, 0))],
        out_shape=(jax.ShapeDtypeStruct((rows_pad, x1.shape[0]), F32),
                   jax.ShapeDtypeStruct((rows_pad, x1.shape[0]), I32),
                   jax.ShapeDtypeStruct((n, rows_pad, LANES), I32)),
        compiler_params=_params("arbitrary"), name="router",
    )(x1, w["wrh"], w["wrl"], w["br"])


def _n_row_blocks(n_tokens):
    n_tiles = n_tokens // TM
    worst_rows = n_tokens * TOP_K + n_tiles * N_EXPERTS * (GRANULE - 1) + N_EXPERTS * (ROW_BLOCK - 1)
    return -(-worst_rows // ROW_BLOCK)


def _steps(shift):
    return shift - jnp.concatenate([jnp.zeros_like(shift[..., :1]), shift[..., :-1]], axis=-1)


def _dispatch_tables(cnt, n_blocks):
    padded = (cnt + (GRANULE - 1)) // GRANULE * GRANULE
    run_end = jnp.cumsum(padded, axis=1)
    run_off = run_end - padded
    n_gran = run_end[:, -1] // GRANULE
    rows_e = jnp.sum(padded, axis=0)
    region = (rows_e + (ROW_BLOCK - 1)) // ROW_BLOCK * ROW_BLOCK
    region_end = jnp.cumsum(region)
    region_start = region_end - region
    run_base = region_start[None, :] + jnp.cumsum(padded, axis=0) - padded

    big = padded // COPY_ROWS[0]
    big_first = jnp.cumsum(big, axis=1) - big
    k_big = jnp.arange(COPY_CAP[0], dtype=I32)
    owned = k_big[None, :, None] >= big_first[:, None, :]
    big_list = lambda row0: COPY_ROWS[0] * k_big[None, :] + jnp.sum(
        jnp.where(owned, _steps(row0 - COPY_ROWS[0] * big_first)[:, None, :], 0), axis=2)
    lists_src, lists_dst, counts = [big_list(run_off)], [big_list(run_base)], [jnp.sum(big, axis=1)]
    done = big * COPY_ROWS[0]
    for rows, cap in zip(COPY_ROWS[1:], COPY_CAP[1:]):
        has = (padded // rows) % 2
        pos = jnp.cumsum(has, axis=1) - has
        hit = (has[:, None, :] != 0) & (pos[:, None, :] == jnp.arange(cap, dtype=I32)[None, :, None])
        lists_src.append(jnp.sum(jnp.where(hit, (run_off + done)[:, None, :], 0), axis=2))
        lists_dst.append(jnp.sum(jnp.where(hit, (run_base + done)[:, None, :], 0), axis=2))
        counts.append(jnp.sum(has, axis=1))
        done = done + has * rows
    copy_src = jnp.concatenate(lists_src, axis=1)
    copy_dst = jnp.concatenate(lists_dst, axis=1)
    copy_n = jnp.stack(counts, axis=1)

    n_used = region_end[-1] // ROW_BLOCK
    blocks = jnp.arange(n_blocks, dtype=I32)
    block_expert = jnp.minimum(jnp.sum(blocks[:, None] >= (region_end // ROW_BLOCK)[None, :], axis=1), N_EXPERTS - 1)
    experts = jnp.arange(N_EXPERTS, dtype=I32)
    present = (region > 0).astype(I32)
    ordinal = jnp.cumsum(present) - present
    seq_expert = jnp.sum(jnp.where((present[None, :] != 0) & (ordinal[None, :] == experts[:, None]),
                                   experts[None, :], 0), axis=1)
    block_ord = jnp.sum(jnp.where(block_expert[:, None] == experts[None, :], ordinal[None, :], 0), axis=1)
    tail = (region - rows_e) // GRANULE
    tail_end = jnp.cumsum(tail)
    tail_first = tail_end - tail
    slot = jnp.arange(N_EXPERTS * (ROW_BLOCK // GRANULE - 1), dtype=I32)
    t_step = _steps((region_start + rows_e) // GRANULE - tail_first)
    tail_dst = slot + jnp.sum(jnp.where(slot[:, None] >= tail_first[None, :], t_step[None, :], 0), axis=1)
    as_i32 = lambda a: a.astype(I32).reshape(-1)
    long_run = jnp.any(padded > RUN_CHUNK, axis=1)
    return dict(run_off=as_i32(run_off), run_len=as_i32(padded), long_run=as_i32(long_run),
                n_gran=as_i32(n_gran), copy_src=as_i32(copy_src), copy_dst=as_i32(copy_dst), copy_n=as_i32(copy_n),
                n_used=as_i32(n_used), block_expert=as_i32(block_expert), block_ord=as_i32(block_ord),
                seq_expert=as_i32(seq_expert), n_seq=as_i32(jnp.sum(present)), n_tail=as_i32(tail_end[-1]),
                tail_dst=as_i32(tail_dst * GRANULE))


def _row_copy(src, src_row, dst, dst_row, sem, rows):
    s = pl.multiple_of(src_row, GRANULE)
    d = pl.multiple_of(dst_row, GRANULE)
    return pltpu.make_async_copy(src.at[pl.ds(s, rows), :], dst.at[pl.ds(d, rows), :], sem)


def _grouped_loop(n, body):
    n_groups = n // COPY_GROUP

    def group(j, carry):
        for i in range(COPY_GROUP):
            body(j * COPY_GROUP + i)
        return carry

    def single(i, carry):
        body(i)
        return carry

    lax.fori_loop(0, n_groups, group, 0)
    lax.fori_loop(n_groups * COPY_GROUP, n, single, 0)


def _start_copies(n, make_copy):
    _grouped_loop(n, lambda i: make_copy(i).start())


def _wait_copies(n, src, dst, sem, rows):
    n_groups = n // COPY_GROUP

    def group(j, carry):
        _row_copy(src, 0, dst, 0, sem, COPY_GROUP * rows).wait()
        return carry

    def single(i, carry):
        _row_copy(src, 0, dst, 0, sem, rows).wait()
        return carry

    lax.fori_loop(0, n_groups, group, 0)
    lax.fori_loop(n_groups * COPY_GROUP, n, single, 0)


def _start_tile_copies(tile, copy_n, make_copy):
    for c, rows in enumerate(COPY_ROWS):
        base = tile * sum(COPY_CAP) + sum(COPY_CAP[:c])
        _start_copies(copy_n[tile * len(COPY_ROWS) + c], lambda i, base=base, rows=rows: make_copy(base + i, rows))


def _wait_tile_copies(tile, copy_n, src, dst, sem):
    for c, rows in enumerate(COPY_ROWS):
        _wait_copies(copy_n[tile * len(COPY_ROWS) + c], src, dst, sem, rows)


def _build_selection(sel_scr, rank, values, run_off_ref, run_len_ref, long_run_ref, tile_idx):
    slot = lax.broadcasted_iota(I32, (RUN_CHUNK, TM), 0)

    def write_runs(all_chunks):
        for e in range(N_EXPERTS):
            off = run_off_ref[tile_idx * N_EXPERTS + e]
            rank_e = rank[e:e + 1, :]
            val_e = 1.0 if values is None else values[e:e + 1, :]

            def chunk(c, carry, off=off, rank_e=rank_e, val_e=val_e):
                hit = rank_e == slot + c * RUN_CHUNK
                rows = pl.ds(pl.multiple_of(off + c * RUN_CHUNK, GRANULE), RUN_CHUNK)
                sel_scr[rows, :] = jnp.where(hit, val_e, 0.0)
                return carry

            chunk(0, 0)
            if all_chunks:
                n_chunks = (run_len_ref[tile_idx * N_EXPERTS + e] + (RUN_CHUNK - 1)) // RUN_CHUNK
                lax.fori_loop(1, n_chunks, chunk, 0)

    write_runs(False)

    @pl.when(long_run_ref[tile_idx] != 0)
    def _():
        write_runs(True)


def _dispatch_kernel(run_off, run_len, long_run, n_gran, copy_src, copy_dst, copy_n, n_tail, tail_dst, n_used,
                     x1_ref, rank_ref, xs_hbm, sel_scr, buf_scr, zero_scr, sem):
    t = pl.program_id(0)

    @pl.when(t == 0)
    def _():
        sel_scr[...] = jnp.zeros_like(sel_scr)
        buf_scr[...] = jnp.zeros_like(buf_scr)
        zero_scr[...] = jnp.zeros_like(zero_scr)

    _build_selection(sel_scr, rank_ref[...], None, run_off, run_len, long_run, t)
    xb = x1_ref[...].astype(BF16)
    n_g = n_gran[t]
    buf = buf_scr.at[t % 2]
    for c in range(L_MAX // SEL_CHUNK):
        @pl.when(c * SEL_CHUNK < n_g * GRANULE)
        def _(c=c):
            rows = slice(c * SEL_CHUNK, (c + 1) * SEL_CHUNK)
            buf[rows, :] = _pack_halves(_dot(sel_scr[rows, :].astype(BF16), xb))

    @pl.when(t > 0)
    def _():
        _wait_tile_copies(jnp.maximum(t - 1, 0), copy_n, buf, xs_hbm, sem)

    _start_tile_copies(
        t, copy_n, lambda i, rows: _row_copy(buf, copy_src[i], xs_hbm, copy_dst[i], sem, rows))

    @pl.when(t == pl.num_programs(0) - 1)
    def _():
        _wait_tile_copies(t, copy_n, buf, xs_hbm, sem)
        _start_copies(n_tail[0], lambda i: _row_copy(zero_scr, 0, xs_hbm, tail_dst[i], sem, GRANULE))
        _wait_copies(n_tail[0], zero_scr, xs_hbm, sem, GRANULE)

        def spare_copy(b):
            rows = pl.ds(pl.multiple_of(b * ROW_BLOCK, ROW_BLOCK), ROW_BLOCK)
            return pltpu.make_async_copy(zero_scr, xs_hbm.at[rows, :], sem)

        def start_spare(b, carry):
            spare_copy(b).start()
            return carry

        def wait_spare(b, carry):
            spare_copy(b).wait()
            return carry

        n_blocks = xs_hbm.shape[0] // ROW_BLOCK
        lax.fori_loop(n_used[0], n_blocks, start_spare, 0)
        lax.fori_loop(n_used[0], n_blocks, wait_spare, 0)


def _dispatch(x1, rank, tables, n_blocks):
    n = x1.shape[0] // TM
    return pl.pallas_call(
        _dispatch_kernel,
        grid_spec=pltpu.PrefetchScalarGridSpec(
            num_scalar_prefetch=10, grid=(n,),
            in_specs=[pl.BlockSpec((TM, D_MODEL), lambda i, *_: (i, 0)),
                      pl.BlockSpec((N_EXPERTS, TM), lambda i, *_: (0, i))],
            out_specs=pl.BlockSpec(memory_space=pl.ANY),
            scratch_shapes=[pltpu.VMEM((L_MAX, TM), F32), pltpu.VMEM((2, L_MAX, HALF), U32),
                            pltpu.VMEM((ROW_BLOCK, HALF), U32), pltpu.SemaphoreType.DMA(())]),
        out_shape=jax.ShapeDtypeStruct((n_blocks * ROW_BLOCK, HALF), U32),
        compiler_params=_params("arbitrary"), name="dispatch",
    )(tables["run_off"], tables["run_len"], tables["long_run"], tables["n_gran"], tables["copy_src"],
      tables["copy_dst"], tables["copy_n"], tables["n_tail"],
      tables["tail_dst"], tables["n_used"], x1, rank)


def _experts_kernel(block_expert, block_ord, seq_expert, n_seq, n_used,
                    xs_hbm, wg_hbm, wu_hbm, wd_hbm, y_ref,
                    x_ring, wg_stage, wu_stage, wd_stage, wgu_scr, wd_scr, x_sems, w_sems, *, layer):
    b = pl.program_id(0)
    n_u = n_used[0]

    def fetch(block):
        slot = block % RING
        rows = pl.ds(pl.multiple_of(block * ROW_BLOCK, ROW_BLOCK), ROW_BLOCK)
        return pltpu.make_async_copy(xs_hbm.at[rows, :], x_ring.at[slot], x_sems.at[slot])

    def weight_copies(ordinal):
        e = seq_expert[ordinal]
        slot = ordinal % 2
        return [pltpu.make_async_copy(src.at[layer, e], stage.at[slot], w_sems.at[slot])
                for src, stage in ((wg_hbm, wg_stage), (wu_hbm, wu_stage), (wd_hbm, wd_stage))]

    @pl.when(b == 0)
    def _():
        for copy in weight_copies(0):
            copy.start()
        for first in range(RING - 1):
            @pl.when(first < n_u)
            def _(first=first):
                fetch(first).start()

    @pl.when(b + (RING - 1) < n_u)
    def _():
        fetch(b + (RING - 1)).start()

    @pl.when(b >= n_u)
    def _():
        y_ref[...] = jnp.zeros_like(y_ref)

    @pl.when(b < n_u)
    def _():
        fresh = (b == 0) | (block_expert[b] != block_expert[jnp.maximum(b - 1, 0)])

        @pl.when(fresh)
        def _():
            ordinal = block_ord[b]
            slot = ordinal % 2
            for copy in weight_copies(ordinal):
                copy.wait()
            wgu_scr[:, :D_EXPERT] = wg_stage[slot].astype(BF16)
            wgu_scr[:, D_EXPERT:] = wu_stage[slot].astype(BF16)
            wd_scr[...] = wd_stage[slot].astype(BF16)

            @pl.when(ordinal + 1 < n_seq[0])
            def _():
                for copy in weight_copies(ordinal + 1):
                    copy.start()

        fetch(b).wait()
        lo, hi = _unpack_halves(x_ring[b % RING])
        gu = _dot(lo, wgu_scr[:HALF, :]) + _dot(hi, wgu_scr[HALF:, :])
        hid = (_silu(gu[:, :D_EXPERT]) * gu[:, D_EXPERT:]).astype(BF16)
        y = _dot(hid, wd_scr[...])
        y_ref[...] = _pack_halves(y.astype(BF16).astype(F32))


def _experts(xs, tables, w_e_gate, w_e_up, w_e_down, layer, n_blocks):
    any_spec = pl.BlockSpec(memory_space=pl.ANY)
    return pl.pallas_call(
        functools.partial(_experts_kernel, layer=layer),
        grid_spec=pltpu.PrefetchScalarGridSpec(
            num_scalar_prefetch=5, grid=(n_blocks,),
            in_specs=[any_spec, any_spec, any_spec, any_spec],
            out_specs=pl.BlockSpec((ROW_BLOCK, HALF), lambda b, *_: (b, 0)),
            scratch_shapes=[pltpu.VMEM((RING, ROW_BLOCK, HALF), U32),
                            pltpu.VMEM((2, D_MODEL, D_EXPERT), F32), pltpu.VMEM((2, D_MODEL, D_EXPERT), F32),
                            pltpu.VMEM((2, D_EXPERT, D_MODEL), F32),
                            pltpu.VMEM((D_MODEL, 2 * D_EXPERT), BF16), pltpu.VMEM((D_EXPERT, D_MODEL), BF16),
                            pltpu.SemaphoreType.DMA((RING,)), pltpu.SemaphoreType.DMA((2,))]),
        out_shape=jax.ShapeDtypeStruct(xs.shape, U32),
        compiler_params=_params("arbitrary"), name="experts",
    )(tables["block_expert"], tables["block_ord"], tables["seq_expert"], tables["n_seq"], tables["n_used"],
      xs, w_e_gate, w_e_up, w_e_down)


def _combine_kernel(run_off, run_len, long_run, n_gran, copy_src, copy_dst, copy_n,
                    x1_ref, rank_ref, wt_ref, y_hbm, wsgu, wsd, g2, b2, out_m_ref, out_t_ref,
                    sel_scr, buf_scr, acc_scr, sem):
    t = pl.program_id(0)
    n_tiles = pl.num_programs(0)
    n_g = n_gran[t]

    def fetch(tile):
        dst = buf_scr.at[tile % 2]
        _start_tile_copies(
            tile, copy_n, lambda i, rows: _row_copy(y_hbm, copy_dst[i], dst, copy_src[i], sem, rows))

    @pl.when(t == 0)
    def _():
        buf_scr[...] = jnp.zeros_like(buf_scr)
        fetch(t)

    stale = pl.multiple_of(jnp.minimum(n_g * GRANULE, L_MAX - SEL_CHUNK), GRANULE)
    sel_scr[pl.ds(stale, SEL_CHUNK), :] = jnp.zeros((SEL_CHUNK, TM), F32)
    _build_selection(sel_scr, rank_ref[...], wt_ref[...], run_off, run_len, long_run, t)
    x1 = x1_ref[...]
    xb = x1.astype(BF16)
    gu = _dot(xb, wsgu[...])
    hs = (_silu(gu[:, :D_EXPERT]) * gu[:, D_EXPERT:]).astype(BF16)
    acc_scr[...] = _dot(hs, wsd[...])
    buf = buf_scr.at[t % 2]
    _wait_tile_copies(t, copy_n, y_hbm, buf, sem)

    @pl.when(t + 1 < n_tiles)
    def _():
        fetch(jnp.minimum(t + 1, n_tiles - 1))

    for c in range(L_MAX // SEL_CHUNK):
        @pl.when(c * SEL_CHUNK < n_g * GRANULE)
        def _(c=c):
            rows = slice(c * SEL_CHUNK, (c + 1) * SEL_CHUNK)
            lo, hi = _unpack_halves(buf[rows, :])
            sel = sel_scr[rows, :].astype(BF16)
            acc_scr[:, :HALF] += _dot_tn(sel, lo)
            acc_scr[:, HALF:] += _dot_tn(sel, hi)

    out = _layer_norm(ALPHA * x1 + acc_scr[...], g2[...], b2[...])

    @pl.when(t + 1 < n_tiles)
    def _():
        out_m_ref[...] = out

    @pl.when(t + 1 == n_tiles)
    def _():
        out_t_ref[...] = out


def _combine(x1, rank, wt, y, tables, w):
    n = x1.shape[0] // TM
    small = (w["wsgu"], w["wsd"], w["g2"], w["b2"])
    tok = pl.BlockSpec((N_EXPERTS, TM), lambda i, *_: (0, i))
    return pl.pallas_call(
        _combine_kernel,
        grid_spec=pltpu.PrefetchScalarGridSpec(
            num_scalar_prefetch=7, grid=(n,),
            in_specs=[pl.BlockSpec((TM, D_MODEL), lambda i, *_: (i, 0)), tok, tok,
                      pl.BlockSpec(memory_space=pl.ANY)]
                     + [pl.BlockSpec(a.shape, lambda i, *_, nd=a.ndim: (0,) * nd) for a in small],
            out_specs=_main_tail_specs(n, D_MODEL),
            scratch_shapes=[pltpu.VMEM((L_MAX, TM), F32), pltpu.VMEM((2, L_MAX, HALF), U32),
                            pltpu.VMEM((TM, D_MODEL), F32), pltpu.SemaphoreType.DMA(())]),
        out_shape=(jax.ShapeDtypeStruct((x1.shape[0] - TM, D_MODEL), F32),
                   jax.ShapeDtypeStruct((TM, D_MODEL), F32)),
        compiler_params=_params("arbitrary"), name="combine",
    )(tables["run_off"], tables["run_len"], tables["long_run"], tables["n_gran"], tables["copy_src"],
      tables["copy_dst"], tables["copy_n"], x1, rank, wt, y, *small)


def _moe(x1, w, w_e_gate, w_e_up, w_e_down, layer):
    n_blocks = _n_row_blocks(x1.shape[0])
    wt, rank, cnt = _router(x1, w)
    tables = _dispatch_tables(cnt[:, :N_EXPERTS, 0], n_blocks)
    xs = _dispatch(x1, rank, tables, n_blocks)
    y = _experts(xs, tables, w_e_gate, w_e_up, w_e_down, layer, n_blocks)
    return _combine(x1, rank, wt, y, tables, w)


def _layer_weights(l, w_in, w_gk2, b_gk, gla_norm_g, w_pool, pool_scale, w_br_a, w_br_b, w_out, ln1_g, ln1_b,
                   w_router, b_router, w_sh_gate, w_sh_up, w_sh_down, ln2_g, ln2_b):
    pieces = {}
    pad_rank = LANES - GATE_RANK
    pieces["wgk2"] = jnp.pad(w_gk2[l], ((0, pad_rank), (0, 0))).astype(BF16)
    pieces["bgk"] = b_gk[l].reshape(1, KEY)
    pieces["gn"] = gla_norm_g[l].reshape(1, DV)
    pieces["wbra"] = w_br_a[l].astype(BF16)
    pieces["wpool"] = w_pool[l].astype(BF16)
    pieces["pscale"] = pool_scale[l].reshape(1, POOL_WIDTH)
    pieces["wbrb"] = w_br_b[l].astype(BF16)
    pieces["wout"] = w_out[l].astype(BF16)
    pieces["g1"] = ln1_g[l].reshape(1, D_MODEL)
    pieces["b1"] = ln1_b[l].reshape(1, D_MODEL)
    wrt = w_router[l].T
    wrh = wrt.astype(BF16)
    pieces["wrh"] = wrh
    pieces["wrl"] = (wrt - wrh.astype(F32)).astype(BF16)
    pieces["br"] = b_router[l].reshape(N_EXPERTS, 1)
    pieces["wsgu"] = jnp.concatenate([w_sh_gate[l], w_sh_up[l]], axis=1).astype(BF16)
    pieces["wsd"] = w_sh_down[l].astype(BF16)
    pieces["g2"] = ln2_g[l].reshape(1, D_MODEL)
    pieces["b2"] = ln2_b[l].reshape(1, D_MODEL)
    return pieces


def kernel(x_prompt, x_sample, state_gla, cache_pool, w_in, w_gk2, b_gk, gla_norm_g, w_pool, pool_scale, w_br_a, w_br_b, w_out, ln1_g, ln1_b, w_router, b_router, w_e_gate, w_e_up, w_e_down, w_sh_gate, w_sh_up, w_sh_down, ln2_g, ln2_b):
    assert T_SAMPLE == TM, "the decode tokens must fill exactly one token tile"
    xm = x_prompt.reshape(T_PROMPT, D_MODEL)
    xt = x_sample.reshape(T_SAMPLE, D_MODEL)
    n_prompt_tiles = T_PROMPT // TM
    sample_block0 = T_PROMPT // DEC_SEQ
    sp, hp, ss, hs = [], [], [], []
    for l in range(DEPTH):
        w = _layer_weights(l, w_in, w_gk2, b_gk, gla_norm_g, w_pool, pool_scale, w_br_a, w_br_b, w_out, ln1_g,
                           ln1_b, w_router, b_router, w_sh_gate, w_sh_up, w_sh_down, ln2_g, ln2_b)
        q, k, v, r, lf, u, ga, gb = _inproj(xm, xt, w_in, w, l)

        og_p, st_p = _gla(q, k, v, lf, r, w["gn"], None, tile=TM, chunk=CHUNK, n_tiles=n_prompt_tiles, block0=0)
        og_s, st_s = _gla(q, k, v, lf, r, w["gn"], state_gla[l], tile=DEC_SEQ, chunk=DEC_SEQ, n_tiles=DEC_BATCH,
                          block0=sample_block0)

        u_p = u[:T_PROMPT].reshape(n_prompt_tiles, TM, POOL_WIDTH)
        halo_p = jnp.concatenate([jnp.zeros((1, HALO, POOL_WIDTH), F32), u_p[:-1, TM - HALO:, :]], axis=0)
        halo_s = jnp.concatenate([jnp.zeros((DEC_BATCH, HALO - POOL_HIST, POOL_WIDTH), F32), cache_pool[l]], axis=1)
        x1 = _mix(xm, xt, og_p, og_s, u, halo_p, halo_s, ga, gb, w)

        xm, xt = _moe(x1, w, w_e_gate, w_e_up, w_e_down, l)

        sp.append(st_p)
        ss.append(st_s)
        hp.append(u[T_PROMPT - POOL_HIST:T_PROMPT].reshape(1, POOL_HIST, POOL_WIDTH))
        hs.append(u[T_PROMPT:].reshape(DEC_BATCH, DEC_SEQ, POOL_WIDTH)[:, DEC_SEQ - POOL_HIST:, :])
    y_prompt = xm.reshape(1, SEQ, D_MODEL)
    y_sample = xt.reshape(DEC_BATCH, DEC_SEQ, D_MODEL)
    return (y_prompt, y_sample, jnp.stack(sp), jnp.stack(hp), jnp.stack(ss), jnp.stack(hs))
```

```python
import functools

import jax
import jax.numpy as jnp
from jax import lax
from jax.experimental import pallas as pl
from jax.experimental.pallas import tpu as pltpu

F32 = jnp.float32
BF16 = jnp.bfloat16
U32 = jnp.uint32
I32 = jnp.int32

D_MODEL = 1024
DEPTH = 2
SEQ = 16384
DEC_BATCH = 8
DEC_SEQ = 32
PAST_LEN = 4096
CHUNK = 64
HEADS = 4
DK = 128
DV = 256
KEY = HEADS * DK
VAL = HEADS * DV
GATE_RANK = 16
GATE_NORMALIZER = 16.0
POOL_WIDTH = 512
POOL_WINDOWS = (2, 4, 8, 16)
POOL_GROUP_DIM = 128
POOL_HIST = 15
N_EXPERTS = 64
GROUP_SIZE = 8
TOPK_GROUPS = 4
TOP_K = 8
D_EXPERT = 256
ROUTED_SCALE = 2.5
ALPHA = (2 * DEPTH) ** 0.25
LN_EPS = 1e-5
RMS_EPS = 1e-6
LOG2_E = 1.4426950408889634

LANES = 128
SUBLANES = 8
VMEM_LIMIT_BYTES = 56 * 1024 * 1024

T_PROMPT = SEQ
T_SAMPLE = DEC_BATCH * DEC_SEQ
T_ALL = T_PROMPT + T_SAMPLE
TM = 256
HALO = 16
NEG_INF = float("-inf")

HALF = D_MODEL // 2
GRANULE = SUBLANES
ROW_BLOCK = 512
COPY_GROUP = 8
COPY_ROWS = (32, 16, 8)
RING = 3
RUN_CHUNK = 64
L_MAX = 2560
SEL_CHUNK = 512
COPY_CAP = (L_MAX // COPY_ROWS[0], N_EXPERTS, N_EXPERTS)
HI_MASK = 0xFFFF0000


def _dot(a, b):
    return jnp.dot(a, b, preferred_element_type=F32)


def _dot_nt(a, b):
    return lax.dot_general(a, b, (((1,), (1,)), ((), ())), preferred_element_type=F32)


def _dot_tn(a, b):
    return lax.dot_general(a, b, (((0,), (0,)), ((), ())), preferred_element_type=F32)


def _sigmoid(x):
    return 0.5 * jnp.tanh(0.5 * x) + 0.5


def _silu(x):
    return x * _sigmoid(x)


def _layer_norm(x, g, b):
    mu = jnp.mean(x, axis=-1, keepdims=True)
    xc = x - mu
    var = jnp.mean(xc * xc, axis=-1, keepdims=True)
    return xc * lax.rsqrt(var + LN_EPS) * g + b


def _pack_halves(v):
    bits = lax.bitcast_convert_type(v, U32)
    return (bits[:, HALF:] & jnp.uint32(HI_MASK)) | (bits[:, :HALF] >> 16)


def _unpack_halves(w):
    lo = lax.bitcast_convert_type(w << 16, F32).astype(BF16)
    hi = lax.bitcast_convert_type(w & jnp.uint32(HI_MASK), F32).astype(BF16)
    return lo, hi


def _params(*sem):
    return pltpu.CompilerParams(dimension_semantics=sem, vmem_limit_bytes=VMEM_LIMIT_BYTES)


def _const_spec(shape):
    nd = len(shape)
    return pl.BlockSpec(shape, lambda *_: (0,) * nd)


def _is_tail_tile():
    return pl.program_id(0) == pl.num_programs(0) - 1


def _main_tail_specs(n_tiles, width):
    return [pl.BlockSpec((TM, width), lambda i, *_: (jnp.minimum(i, n_tiles - 2), 0)),
            pl.BlockSpec((TM, width), lambda i, *_: (0, 0))]


IN_PIECES = ((("q", KEY), ("k", KEY), ("v", VAL), ("r", VAL)),
             (("gk", LANES), ("u", POOL_WIDTH), ("ga", D_MODEL), ("gb", D_MODEL)))
IN_COLS = {}
for _part, _pieces in enumerate(IN_PIECES):
    _col = 0
    for _name, _width in _pieces:
        IN_COLS[_name] = (_part, slice(_col, _col + _width))
        _col += _width
IN_SPLIT = sum(width for _, width in IN_PIECES[0])
IN_REST = sum(width for _, width in IN_PIECES[1][1:])
N_PROJ = IN_SPLIT + GATE_RANK + IN_REST
W_CHUNK = 256


def _inproj_kernel(xm_ref, xt_ref, w_hbm, wgk2, bgk, q_o, k_o, v_o, r_o, lf_o, u_o, ga_o, gb_o,
                   stage, wa_scr, wb_scr, sems, *, layer):
    @pl.when(pl.program_id(0) == 0)
    def _():
        n_chunks = D_MODEL // W_CHUNK

        def copy(c):
            rows = pl.ds(c * W_CHUNK, W_CHUNK)
            return pltpu.make_async_copy(w_hbm.at[layer, rows, :], stage.at[c % 2], sems.at[c % 2])

        copy(0).start()
        lane = lax.broadcasted_iota(I32, (W_CHUNK, LANES), 1)
        for c in range(n_chunks):
            if c + 1 < n_chunks:
                copy(c + 1).start()
            copy(c).wait()
            blk = stage[c % 2]
            rows = slice(c * W_CHUNK, (c + 1) * W_CHUNK)
            wa_scr[rows, :] = blk[:, :IN_SPLIT].astype(BF16)
            gate = jnp.where(lane < GATE_RANK, blk[:, IN_SPLIT:IN_SPLIT + LANES], 0.0)
            wb_scr[rows, :LANES] = gate.astype(BF16)
            wb_scr[rows, LANES:] = blk[:, IN_SPLIT + GATE_RANK:].astype(BF16)

    xb = jnp.where(_is_tail_tile(), xt_ref[...], xm_ref[...]).astype(BF16)
    proj = lambda name: _dot(xb, (wa_scr, wb_scr)[IN_COLS[name][0]][:, IN_COLS[name][1]])
    q_o[...] = proj("q") * (DK ** -0.5)
    k_o[...] = proj("k")
    v_o[...] = proj("v").astype(BF16)
    r_o[...] = proj("r")
    gk = proj("gk").astype(BF16)
    z = _dot(gk, wgk2[...]) + bgk[...]
    log_sig = jnp.minimum(z, 0.0) - jnp.log1p(jnp.exp(-jnp.abs(z)))
    lf_o[...] = log_sig * (1.0 / GATE_NORMALIZER)
    u_o[...] = proj("u")
    ga_o[...] = proj("ga")
    gb_o[...] = proj("gb")


def _inproj(xm, xt, w_in, w, layer):
    n = xm.shape[0] // TM + 1
    rows = n * TM
    row = lambda width: pl.BlockSpec((TM, width), lambda i: (i, 0))
    out_shapes = (
        jax.ShapeDtypeStruct((rows, KEY), F32), jax.ShapeDtypeStruct((rows, KEY), F32),
        jax.ShapeDtypeStruct((rows, VAL), BF16), jax.ShapeDtypeStruct((rows, VAL), F32),
        jax.ShapeDtypeStruct((rows, KEY), F32), jax.ShapeDtypeStruct((rows, POOL_WIDTH), F32),
        jax.ShapeDtypeStruct((rows, D_MODEL), F32), jax.ShapeDtypeStruct((rows, D_MODEL), F32))
    small = (w["wgk2"], w["bgk"])
    return pl.pallas_call(
        functools.partial(_inproj_kernel, layer=layer), grid=(n,),
        in_specs=_main_tail_specs(n, D_MODEL) + [pl.BlockSpec(memory_space=pltpu.HBM)]
                 + [_const_spec(a.shape) for a in small],
        out_specs=[row(KEY), row(KEY), row(VAL), row(VAL), row(KEY), row(POOL_WIDTH), row(D_MODEL), row(D_MODEL)],
        out_shape=out_shapes,
        scratch_shapes=[pltpu.VMEM((2, W_CHUNK, N_PROJ), F32), pltpu.VMEM((D_MODEL, IN_SPLIT), BF16),
                        pltpu.VMEM((D_MODEL, LANES + IN_REST), BF16), pltpu.SemaphoreType.DMA((2,))],
        compiler_params=_params("arbitrary"), name="inproj",
    )(xm, xt, w_in, *small)


def _gla_kernel(*refs, chunk, n_chunks, has_init):
    if has_init:
        q_ref, k_ref, v_ref, lf_ref, r_ref, gn_ref, s0_ref, o_ref, s_out_ref, st_ref = refs
        for h in range(HEADS):
            st_ref[h] = s0_ref[0, h].T
    else:
        q_ref, k_ref, v_ref, lf_ref, r_ref, gn_ref, o_ref, s_out_ref, st_ref = refs

        @pl.when(pl.program_id(0) == 0)
        def _():
            st_ref[...] = jnp.zeros_like(st_ref)

    row = lax.broadcasted_iota(I32, (chunk, KEY), 0)
    ta = lax.broadcasted_iota(I32, (chunk, chunk), 0)
    sa = lax.broadcasted_iota(I32, (chunk, chunk), 1)
    gn = gn_ref[...]
    halves = [1 << i for i in range(chunk.bit_length() - 1)]
    diag = ta == sa
    pairs = [((ta ^ sa) < 2 * half) & ((ta & half) != 0) & ((sa & half) == 0) for half in halves]

    def chunk_body(c, carry):
        rows = pl.ds(pl.multiple_of(c * chunk, chunk), chunk)
        lf = lf_ref[rows, :] * LOG2_E
        q = q_ref[rows, :]
        k = k_ref[rows, :]
        qb = q.astype(BF16)
        kb = k.astype(BF16)
        att = []
        for h in range(HEADS):
            hs = slice(h * DK, (h + 1) * DK)
            att.append(jnp.where(diag, _dot_nt(qb[:, hs], kb[:, hs]), 0.0))
        seg, tot = lf, lf
        for half, pair in zip(halves, pairs):
            qs = (q * jnp.exp2(seg)).astype(BF16)
            ks = (k * jnp.exp2(tot - seg)).astype(BF16)
            for h in range(HEADS):
                hs = slice(h * DK, (h + 1) * DK)
                att[h] = jnp.where(pair, _dot_nt(qs[:, hs], ks[:, hs]), att[h])
            if half < SUBLANES:
                upper = (row & half) != 0
                below = pltpu.roll(tot, half, 0)
                above = pltpu.roll(tot, chunk - half, 0)
                seg = seg + jnp.where(upper, below, 0.0)
                tot = tot + jnp.where(upper, below, above)
            else:
                step, n_tiles = half // SUBLANES, chunk // SUBLANES
                seg_t = [seg[i * SUBLANES:(i + 1) * SUBLANES, :] for i in range(n_tiles)]
                tot_t = [tot[i * SUBLANES:(i + 1) * SUBLANES, :] for i in range(n_tiles)]
                seg = jnp.concatenate([seg_t[i] + tot_t[i - step] if i & step else seg_t[i] for i in range(n_tiles)], axis=0)
                tot = jnp.concatenate([tot_t[i] + tot_t[i ^ step] for i in range(n_tiles)], axis=0)
        q_in = (q * jnp.exp2(seg)).astype(BF16)
        k_out = (k * jnp.exp2(tot - seg)).astype(BF16)
        decay = jnp.exp2(tot[0:1, :])
        for h in range(HEADS):
            hs = slice(h * DK, (h + 1) * DK)
            vs = slice(h * DV, (h + 1) * DV)
            state = st_ref[h]
            vh = v_ref[rows, vs]
            o = _dot_nt(q_in[:, hs], state.astype(BF16)) + _dot(att[h].astype(BF16), vh)
            st_ref[h] = state * decay[:, hs] + _dot_tn(vh, k_out[:, hs])
            ms = jnp.mean(o * o, axis=-1, keepdims=True)
            o = o * lax.rsqrt(ms + RMS_EPS) * gn
            o_ref[rows, vs] = (o * _silu(r_ref[rows, vs])).astype(BF16)
        return carry

    lax.fori_loop(0, n_chunks, chunk_body, 0)

    def write_state():
        for h in range(HEADS):
            s_out_ref[0, h] = st_ref[h].T

    if has_init:
        write_state()
    else:
        pl.when(pl.program_id(0) == pl.num_programs(0) - 1)(write_state)


def _gla(q, k, v, lf, r, gn, s0, *, tile, chunk, n_tiles, block0):
    has_init = s0 is not None
    row = lambda width: pl.BlockSpec((tile, width), lambda i: (block0 + i, 0))
    st_spec = pl.BlockSpec((1, HEADS, DK, DV), (lambda i: (i, 0, 0, 0)) if has_init else (lambda i: (0, 0, 0, 0)))
    n_states = n_tiles if has_init else 1
    in_specs = [row(KEY), row(KEY), row(VAL), row(KEY), row(VAL), _const_spec(gn.shape)]
    args = [q, k, v, lf, r, gn]
    if has_init:
        in_specs.append(st_spec)
        args.append(s0)
    return pl.pallas_call(
        functools.partial(_gla_kernel, chunk=chunk, n_chunks=tile // chunk, has_init=has_init),
        grid=(n_tiles,), in_specs=in_specs,
        out_specs=[pl.BlockSpec((tile, VAL), lambda i: (i, 0)), st_spec],
        out_shape=(jax.ShapeDtypeStruct((n_tiles * tile, VAL), BF16),
                   jax.ShapeDtypeStruct((n_states, HEADS, DK, DV), F32)),
        scratch_shapes=[pltpu.VMEM((HEADS, DV, DK), F32)],
        compiler_params=_params("arbitrary"), name="gla_init" if has_init else "gla",
    )(*args)


def _window_sums(ext, window):
    shift = 1
    while shift < window:
        ext = ext + pltpu.roll(ext, shift, 0)
        shift *= 2
    return ext


def _mix_kernel(xm_ref, xt_ref, ogm_ref, ogt_ref, u_ref, halo_m_ref, halo_t_ref, ga_ref, gb_ref,
                wbra, wpool, pscale, wbrb, wout, g1, b1, x1_ref):
    tail = _is_tail_tile()
    ya = _dot(jnp.where(tail, ogt_ref[...], ogm_ref[...]), wbra[...])
    u = u_ref[...]
    ext_m = jnp.concatenate([halo_m_ref[0], u], axis=0)
    seg = HALO + DEC_SEQ
    pieces = []
    for b in range(DEC_BATCH):
        pieces += [halo_t_ref[b], u[b * DEC_SEQ:(b + 1) * DEC_SEQ, :]]
    ext_t = jnp.concatenate(pieces, axis=0)
    rowi = lax.broadcasted_iota(I32, (TM, 1), 0)
    pos = jnp.where(tail, PAST_LEN + (rowi & (DEC_SEQ - 1)), pl.program_id(0) * TM + rowi)
    parts = []
    for g, window in enumerate(POOL_WINDOWS):
        cols = slice(g * POOL_GROUP_DIM, (g + 1) * POOL_GROUP_DIM)
        win_m = _window_sums(ext_m[:, cols], window)[HALO:, :]
        win_all = _window_sums(ext_t[:, cols], window)
        win_t = jnp.concatenate([win_all[b * seg + HALO:(b + 1) * seg, :] for b in range(DEC_BATCH)], axis=0)
        cnt = jnp.minimum(window, pos + 1).astype(F32)
        d = jnp.where(tail, win_t, win_m) / cnt - u[:, cols]
        parts.append(_dot(d.astype(BF16), wpool[g]))
    yb_in = jnp.concatenate(parts, axis=1) * pscale[...]
    yb = _dot(yb_in.astype(BF16), wbrb[...])
    mixed = _sigmoid(ga_ref[...]) * ya + _sigmoid(gb_ref[...]) * yb
    mix = _dot(mixed.astype(BF16), wout[...])
    x = jnp.where(tail, xt_ref[...], xm_ref[...])
    x1_ref[...] = _layer_norm(ALPHA * x + mix, g1[...], b1[...])


def _mix(xm, xt, og_m, og_t, u, halo_m, halo_t, ga, gb, w):
    n = xm.shape[0] // TM + 1
    row = lambda width: pl.BlockSpec((TM, width), lambda i: (i, 0))
    weights = (w["wbra"], w["wpool"], w["pscale"], w["wbrb"], w["wout"], w["g1"], w["b1"])
    return pl.pallas_call(
        _mix_kernel, grid=(n,),
        in_specs=_main_tail_specs(n, D_MODEL) + _main_tail_specs(n, VAL) + [
            row(POOL_WIDTH), pl.BlockSpec((1, HALO, POOL_WIDTH), lambda i: (jnp.minimum(i, n - 2), 0, 0)),
            _const_spec(halo_t.shape), row(D_MODEL), row(D_MODEL)] + [_const_spec(a.shape) for a in weights],
        out_specs=row(D_MODEL),
        out_shape=jax.ShapeDtypeStruct((n * TM, D_MODEL), F32),
        compiler_params=_params("arbitrary"), name="mix",
    )(xm, xt, og_m, og_t, u, halo_m, halo_t, ga, gb, *weights)


def _router_kernel(x1_ref, wrh_ref, wrl_ref, br_ref, wt_ref, rank_ref, cnt_ref, *, tile):
    x = x1_ref[...]
    xh = x.astype(BF16)
    xl = (x - xh.astype(F32)).astype(BF16)
    logits = _dot_nt(wrh_ref[...], xh) + _dot_nt(wrl_ref[...], xh) + _dot_nt(wrh_ref[...], xl)
    scores = _sigmoid(logits)
    rows_pad = N_EXPERTS
    n_grp = rows_pad // GROUP_SIZE
    biased = scores + br_ref[...]
    shape3 = (n_grp, GROUP_SIZE, tile)
    b3 = biased.reshape(shape3)
    s3 = scores.reshape(shape3)
    sub = lax.broadcasted_iota(I32, shape3, 1)
    gid = lax.broadcasted_iota(I32, shape3, 0)
    eid = gid * GROUP_SIZE + sub
    m1 = jnp.max(b3, axis=1, keepdims=True)
    i1 = jnp.min(jnp.where(b3 == m1, sub, GROUP_SIZE), axis=1, keepdims=True)
    m2 = jnp.max(jnp.where(sub == i1, NEG_INF, b3), axis=1, keepdims=True)
    gscore = m1 + m2
    gid1 = lax.broadcasted_iota(I32, (n_grp, 1, tile), 0)
    gsel = jnp.zeros((n_grp, 1, tile), jnp.bool_)
    for _ in range(TOPK_GROUPS):
        gm = jnp.max(gscore, axis=0, keepdims=True)
        gi = jnp.min(jnp.where(gscore == gm, gid1, n_grp), axis=0, keepdims=True)
        pick = gid1 == gi
        gsel = gsel | pick
        gscore = jnp.where(pick, NEG_INF, gscore)
    masked = jnp.where(gsel, b3, NEG_INF)
    wsel = jnp.zeros(shape3, F32)
    chosen = jnp.zeros(shape3, jnp.bool_)
    for _ in range(TOP_K):
        m = jnp.max(jnp.max(masked, axis=1, keepdims=True), axis=0, keepdims=True)
        idx = jnp.min(jnp.min(jnp.where(masked == m, eid, rows_pad), axis=1, keepdims=True), axis=0, keepdims=True)
        pick = eid == idx
        wsel = jnp.where(pick, s3, wsel)
        chosen = chosen | pick
        masked = jnp.where(pick, NEG_INF, masked)
    wsum = jnp.sum(jnp.sum(wsel, axis=1, keepdims=True), axis=0, keepdims=True)
    wt_ref[...] = (wsel / wsum * ROUTED_SCALE).reshape(rows_pad, tile)
    sel = jnp.where(chosen, 1.0, 0.0).reshape(rows_pad, tile)
    before = lax.broadcasted_iota(I32, (tile, tile), 0) < lax.broadcasted_iota(I32, (tile, tile), 1)
    rank = _dot(sel.astype(BF16), jnp.where(before, 1.0, 0.0).astype(BF16))
    rank_ref[...] = jnp.where(sel > 0.0, rank, -1.0).astype(I32)
    cnt = jnp.sum(sel, axis=1, keepdims=True).astype(I32)
    cnt_ref[0] = jnp.broadcast_to(cnt, (rows_pad, LANES))


def _router(x1, w):
    n = x1.shape[0] // TM
    rows_pad = N_EXPERTS
    tok = pl.BlockSpec((rows_pad, TM), lambda i: (0, i))
    return pl.pallas_call(
        functools.partial(_router_kernel, tile=TM), grid=(n,),
        in_specs=[pl.BlockSpec((TM, D_MODEL), lambda i: (i, 0)), _const_spec(w["wrh"].shape),
                  _const_spec(w["wrl"].shape), _const_spec(w["br"].shape)],
        out_specs=[tok, tok, pl.BlockSpec((1, rows_pad, LANES), lambda i: (i, 0, 0))],
        out_shape=(jax.ShapeDtypeStruct((rows_pad, x1.shape[0]), F32),
                   jax.ShapeDtypeStruct((rows_pad, x1.shape[0]), I32),
                   jax.ShapeDtypeStruct((n, rows_pad, LANES), I32)),
        compiler_params=_params("arbitrary"), name="router",
    )(x1, w["wrh"], w["wrl"], w["br"])


def _n_row_blocks(n_tokens):
    n_tiles = n_tokens // TM
    worst_rows = n_tokens * TOP_K + n_tiles * N_EXPERTS * (GRANULE - 1) + N_EXPERTS * (ROW_BLOCK - 1)
    return -(-worst_rows // ROW_BLOCK)


def _steps(shift):
    return shift - jnp.concatenate([jnp.zeros_like(shift[..., :1]), shift[..., :-1]], axis=-1)


def _dispatch_tables(cnt, n_blocks):
    padded = (cnt + (GRANULE - 1)) // GRANULE * GRANULE
    run_end = jnp.cumsum(padded, axis=1)
    run_off = run_end - padded
    n_gran = run_end[:, -1] // GRANULE
    rows_e = jnp.sum(padded, axis=0)
    region = (rows_e + (ROW_BLOCK - 1)) // ROW_BLOCK * ROW_BLOCK
    region_end = jnp.cumsum(region)
    region_start = region_end - region
    run_base = region_start[None, :] + jnp.cumsum(padded, axis=0) - padded

    big = padded // COPY_ROWS[0]
    big_first = jnp.cumsum(big, axis=1) - big
    k_big = jnp.arange(COPY_CAP[0], dtype=I32)
    owned = k_big[None, :, None] >= big_first[:, None, :]
    big_list = lambda row0: COPY_ROWS[0] * k_big[None, :] + jnp.sum(
        jnp.where(owned, _steps(row0 - COPY_ROWS[0] * big_first)[:, None, :], 0), axis=2)
    lists_src, lists_dst, counts = [big_list(run_off)], [big_list(run_base)], [jnp.sum(big, axis=1)]
    done = big * COPY_ROWS[0]
    for rows, cap in zip(COPY_ROWS[1:], COPY_CAP[1:]):
        has = (padded // rows) % 2
        pos = jnp.cumsum(has, axis=1) - has
        hit = (has[:, None, :] != 0) & (pos[:, None, :] == jnp.arange(cap, dtype=I32)[None, :, None])
        lists_src.append(jnp.sum(jnp.where(hit, (run_off + done)[:, None, :], 0), axis=2))
        lists_dst.append(jnp.sum(jnp.where(hit, (run_base + done)[:, None, :], 0), axis=2))
        counts.append(jnp.sum(has, axis=1))
        done = done + has * rows
    copy_src = jnp.concatenate(lists_src, axis=1)
    copy_dst = jnp.concatenate(lists_dst, axis=1)
    copy_n = jnp.stack(counts, axis=1)

    n_used = region_end[-1] // ROW_BLOCK
    blocks = jnp.arange(n_blocks, dtype=I32)
    block_expert = jnp.minimum(jnp.sum(blocks[:, None] >= (region_end // ROW_BLOCK)[None, :], axis=1), N_EXPERTS - 1)
    experts = jnp.arange(N_EXPERTS, dtype=I32)
    present = (region > 0).astype(I32)
    ordinal = jnp.cumsum(present) - present
    seq_expert = jnp.sum(jnp.where((present[None, :] != 0) & (ordinal[None, :] == experts[:, None]),
                                   experts[None, :], 0), axis=1)
    block_ord = jnp.sum(jnp.where(block_expert[:, None] == experts[None, :], ordinal[None, :], 0), axis=1)
    tail = (region - rows_e) // GRANULE
    tail_end = jnp.cumsum(tail)
    tail_first = tail_end - tail
    slot = jnp.arange(N_EXPERTS * (ROW_BLOCK // GRANULE - 1), dtype=I32)
    t_step = _steps((region_start + rows_e) // GRANULE - tail_first)
    tail_dst = slot + jnp.sum(jnp.where(slot[:, None] >= tail_first[None, :], t_step[None, :], 0), axis=1)
    as_i32 = lambda a: a.astype(I32).reshape(-1)
    long_run = jnp.any(padded > RUN_CHUNK, axis=1)
    return dict(run_off=as_i32(run_off), run_len=as_i32(padded), long_run=as_i32(long_run),
                n_gran=as_i32(n_gran), copy_src=as_i32(copy_src), copy_dst=as_i32(copy_dst), copy_n=as_i32(copy_n),
                n_used=as_i32(n_used), block_expert=as_i32(block_expert), block_ord=as_i32(block_ord),
                seq_expert=as_i32(seq_expert), n_seq=as_i32(jnp.sum(present)), n_tail=as_i32(tail_end[-1]),
                tail_dst=as_i32(tail_dst * GRANULE))


def _row_copy(src, src_row, dst, dst_row, sem, rows):
    s = pl.multiple_of(src_row, GRANULE)
    d = pl.multiple_of(dst_row, GRANULE)
    return pltpu.make_async_copy(src.at[pl.ds(s, rows), :], dst.at[pl.ds(d, rows), :], sem)


def _grouped_loop(n, body):
    n_groups = n // COPY_GROUP

    def group(j, carry):
        for i in range(COPY_GROUP):
            body(j * COPY_GROUP + i)
        return carry

    def single(i, carry):
        body(i)
        return carry

    lax.fori_loop(0, n_groups, group, 0)
    lax.fori_loop(n_groups * COPY_GROUP, n, single, 0)


def _start_copies(n, make_copy):
    _grouped_loop(n, lambda i: make_copy(i).start())


def _wait_copies(n, src, dst, sem, rows):
    n_groups = n // COPY_GROUP

    def group(j, carry):
        _row_copy(src, 0, dst, 0, sem, COPY_GROUP * rows).wait()
        return carry

    def single(i, carry):
        _row_copy(src, 0, dst, 0, sem, rows).wait()
        return carry

    lax.fori_loop(0, n_groups, group, 0)
    lax.fori_loop(n_groups * COPY_GROUP, n, single, 0)


def _start_tile_copies(tile, copy_n, make_copy):
    for c, rows in enumerate(COPY_ROWS):
        base = tile * sum(COPY_CAP) + sum(COPY_CAP[:c])
        _start_copies(copy_n[tile * len(COPY_ROWS) + c], lambda i, base=base, rows=rows: make_copy(base + i, rows))


def _wait_tile_copies(tile, copy_n, src, dst, sem):
    for c, rows in enumerate(COPY_ROWS):
        _wait_copies(copy_n[tile * len(COPY_ROWS) + c], src, dst, sem, rows)


def _build_selection(sel_scr, rank, values, run_off_ref, run_len_ref, long_run_ref, tile_idx):
    slot = lax.broadcasted_iota(I32, (RUN_CHUNK, TM), 0)

    def write_runs(all_chunks):
        for e in range(N_EXPERTS):
            off = run_off_ref[tile_idx * N_EXPERTS + e]
            rank_e = rank[e:e + 1, :]
            val_e = 1.0 if values is None else values[e:e + 1, :]

            def chunk(c, carry, off=off, rank_e=rank_e, val_e=val_e):
                hit = rank_e == slot + c * RUN_CHUNK
                rows = pl.ds(pl.multiple_of(off + c * RUN_CHUNK, GRANULE), RUN_CHUNK)
                sel_scr[rows, :] = jnp.where(hit, val_e, 0.0)
                return carry

            chunk(0, 0)
            if all_chunks:
                n_chunks = (run_len_ref[tile_idx * N_EXPERTS + e] + (RUN_CHUNK - 1)) // RUN_CHUNK
                lax.fori_loop(1, n_chunks, chunk, 0)

    write_runs(False)

    @pl.when(long_run_ref[tile_idx] != 0)
    def _():
        write_runs(True)


def _dispatch_kernel(run_off, run_len, long_run, n_gran, copy_src, copy_dst, copy_n, n_tail, tail_dst, n_used,
                     x1_ref, rank_ref, xs_hbm, sel_scr, buf_scr, zero_scr, sem):
    t = pl.program_id(0)

    @pl.when(t == 0)
    def _():
        sel_scr[...] = jnp.zeros_like(sel_scr)
        buf_scr[...] = jnp.zeros_like(buf_scr)
        zero_scr[...] = jnp.zeros_like(zero_scr)

    _build_selection(sel_scr, rank_ref[...], None, run_off, run_len, long_run, t)
    xb = x1_ref[...].astype(BF16)
    n_g = n_gran[t]
    buf = buf_scr.at[t % 2]
    for c in range(L_MAX // SEL_CHUNK):
        @pl.when(c * SEL_CHUNK < n_g * GRANULE)
        def _(c=c):
            rows = slice(c * SEL_CHUNK, (c + 1) * SEL_CHUNK)
            buf[rows, :] = _pack_halves(_dot(sel_scr[rows, :].astype(BF16), xb))

    @pl.when(t > 0)
    def _():
        _wait_tile_copies(jnp.maximum(t - 1, 0), copy_n, buf, xs_hbm, sem)

    _start_tile_copies(
        t, copy_n, lambda i, rows: _row_copy(buf, copy_src[i], xs_hbm, copy_dst[i], sem, rows))

    @pl.when(t == pl.num_programs(0) - 1)
    def _():
        _wait_tile_copies(t, copy_n, buf, xs_hbm, sem)
        _start_copies(n_tail[0], lambda i: _row_copy(zero_scr, 0, xs_hbm, tail_dst[i], sem, GRANULE))
        _wait_copies(n_tail[0], zero_scr, xs_hbm, sem, GRANULE)

        def spare_copy(b):
            rows = pl.ds(pl.multiple_of(b * ROW_BLOCK, ROW_BLOCK), ROW_BLOCK)
            return pltpu.make_async_copy(zero_scr, xs_hbm.at[rows, :], sem)

        def start_spare(b, carry):
            spare_copy(b).start()
            return carry

        def wait_spare(b, carry):
            spare_copy(b).wait()
            return carry

        n_blocks = xs_hbm.shape[0] // ROW_BLOCK
        lax.fori_loop(n_used[0], n_blocks, start_spare, 0)
        lax.fori_loop(n_used[0], n_blocks, wait_spare, 0)


def _dispatch(x1, rank, tables, n_blocks):
    n = x1.shape[0] // TM
    return pl.pallas_call(
        _dispatch_kernel,
        grid_spec=pltpu.PrefetchScalarGridSpec(
            num_scalar_prefetch=10, grid=(n,),
            in_specs=[pl.BlockSpec((TM, D_MODEL), lambda i, *_: (i, 0)),
                      pl.BlockSpec((N_EXPERTS, TM), lambda i, *_: (0, i))],
            out_specs=pl.BlockSpec(memory_space=pl.ANY),
            scratch_shapes=[pltpu.VMEM((L_MAX, TM), F32), pltpu.VMEM((2, L_MAX, HALF), U32),
                            pltpu.VMEM((ROW_BLOCK, HALF), U32), pltpu.SemaphoreType.DMA(())]),
        out_shape=jax.ShapeDtypeStruct((n_blocks * ROW_BLOCK, HALF), U32),
        compiler_params=_params("arbitrary"), name="dispatch",
    )(tables["run_off"], tables["run_len"], tables["long_run"], tables["n_gran"], tables["copy_src"],
      tables["copy_dst"], tables["copy_n"], tables["n_tail"],
      tables["tail_dst"], tables["n_used"], x1, rank)


def _experts_kernel(block_expert, block_ord, seq_expert, n_seq, n_used,
                    xs_hbm, wg_hbm, wu_hbm, wd_hbm, y_ref,
                    x_ring, wg_stage, wu_stage, wd_stage, wgu_scr, wd_scr, x_sems, w_sems, *, layer):
    b = pl.program_id(0)
    n_u = n_used[0]

    def fetch(block):
        slot = block % RING
        rows = pl.ds(pl.multiple_of(block * ROW_BLOCK, ROW_BLOCK), ROW_BLOCK)
        return pltpu.make_async_copy(xs_hbm.at[rows, :], x_ring.at[slot], x_sems.at[slot])

    def weight_copies(ordinal):
        e = seq_expert[ordinal]
        slot = ordinal % 2
        return [pltpu.make_async_copy(src.at[layer, e], stage.at[slot], w_sems.at[slot])
                for src, stage in ((wg_hbm, wg_stage), (wu_hbm, wu_stage), (wd_hbm, wd_stage))]

    @pl.when(b == 0)
    def _():
        for copy in weight_copies(0):
            copy.start()
        for first in range(RING - 1):
            @pl.when(first < n_u)
            def _(first=first):
                fetch(first).start()

    @pl.when(b + (RING - 1) < n_u)
    def _():
        fetch(b + (RING - 1)).start()

    @pl.when(b >= n_u)
    def _():
        y_ref[...] = jnp.zeros_like(y_ref)

    @pl.when(b < n_u)
    def _():
        fresh = (b == 0) | (block_expert[b] != block_expert[jnp.maximum(b - 1, 0)])

        @pl.when(fresh)
        def _():
            ordinal = block_ord[b]
            slot = ordinal % 2
            for copy in weight_copies(ordinal):
                copy.wait()
            wgu_scr[:, :D_EXPERT] = wg_stage[slot].astype(BF16)
            wgu_scr[:, D_EXPERT:] = wu_stage[slot].astype(BF16)
            wd_scr[...] = wd_stage[slot].astype(BF16)

            @pl.when(ordinal + 1 < n_seq[0])
            def _():
                for copy in weight_copies(ordinal + 1):
                    copy.start()

        fetch(b).wait()
        lo, hi = _unpack_halves(x_ring[b % RING])
        gu = _dot(lo, wgu_scr[:HALF, :]) + _dot(hi, wgu_scr[HALF:, :])
        hid = (_silu(gu[:, :D_EXPERT]) * gu[:, D_EXPERT:]).astype(BF16)
        y = _dot(hid, wd_scr[...])
        y_ref[...] = _pack_halves(y.astype(BF16).astype(F32))


def _experts(xs, tables, w_e_gate, w_e_up, w_e_down, layer, n_blocks):
    any_spec = pl.BlockSpec(memory_space=pl.ANY)
    return pl.pallas_call(
        functools.partial(_experts_kernel, layer=layer),
        grid_spec=pltpu.PrefetchScalarGridSpec(
            num_scalar_prefetch=5, grid=(n_blocks,),
            in_specs=[any_spec, any_spec, any_spec, any_spec],
            out_specs=pl.BlockSpec((ROW_BLOCK, HALF), lambda b, *_: (b, 0)),
            scratch_shapes=[pltpu.VMEM((RING, ROW_BLOCK, HALF), U32),
                            pltpu.VMEM((2, D_MODEL, D_EXPERT), F32), pltpu.VMEM((2, D_MODEL, D_EXPERT), F32),
                            pltpu.VMEM((2, D_EXPERT, D_MODEL), F32),
                            pltpu.VMEM((D_MODEL, 2 * D_EXPERT), BF16), pltpu.VMEM((D_EXPERT, D_MODEL), BF16),
                            pltpu.SemaphoreType.DMA((RING,)), pltpu.SemaphoreType.DMA((2,))]),
        out_shape=jax.ShapeDtypeStruct(xs.shape, U32),
        compiler_params=_params("arbitrary"), name="experts",
    )(tables["block_expert"], tables["block_ord"], tables["seq_expert"], tables["n_seq"], tables["n_used"],
      xs, w_e_gate, w_e_up, w_e_down)


def _combine_kernel(run_off, run_len, long_run, n_gran, copy_src, copy_dst, copy_n,
                    x1_ref, rank_ref, wt_ref, y_hbm, wsgu, wsd, g2, b2, out_m_ref, out_t_ref,
                    sel_scr, buf_scr, acc_scr, sem):
    t = pl.program_id(0)
    n_tiles = pl.num_programs(0)
    n_g = n_gran[t]

    def fetch(tile):
        dst = buf_scr.at[tile % 2]
        _start_tile_copies(
            tile, copy_n, lambda i, rows: _row_copy(y_hbm, copy_dst[i], dst, copy_src[i], sem, rows))

    @pl.when(t == 0)
    def _():
        buf_scr[...] = jnp.zeros_like(buf_scr)
        fetch(t)

    stale = pl.multiple_of(jnp.minimum(n_g * GRANULE, L_MAX - SEL_CHUNK), GRANULE)
    sel_scr[pl.ds(stale, SEL_CHUNK), :] = jnp.zeros((SEL_CHUNK, TM), F32)
    _build_selection(sel_scr, rank_ref[...], wt_ref[...], run_off, run_len, long_run, t)
    x1 = x1_ref[...]
    xb = x1.astype(BF16)
    gu = _dot(xb, wsgu[...])
    hs = (_silu(gu[:, :D_EXPERT]) * gu[:, D_EXPERT:]).astype(BF16)
    acc_scr[...] = _dot(hs, wsd[...])
    buf = buf_scr.at[t % 2]
    _wait_tile_copies(t, copy_n, y_hbm, buf, sem)

    @pl.when(t + 1 < n_tiles)
    def _():
        fetch(jnp.minimum(t + 1, n_tiles - 1))

    for c in range(L_MAX // SEL_CHUNK):
        @pl.when(c * SEL_CHUNK < n_g * GRANULE)
        def _(c=c):
            rows = slice(c * SEL_CHUNK, (c + 1) * SEL_CHUNK)
            lo, hi = _unpack_halves(buf[rows, :])
            sel = sel_scr[rows, :].astype(BF16)
            acc_scr[:, :HALF] += _dot_tn(sel, lo)
            acc_scr[:, HALF:] += _dot_tn(sel, hi)

    out = _layer_norm(ALPHA * x1 + acc_scr[...], g2[...], b2[...])

    @pl.when(t + 1 < n_tiles)
    def _():
        out_m_ref[...] = out

    @pl.when(t + 1 == n_tiles)
    def _():
        out_t_ref[...] = out


def _combine(x1, rank, wt, y, tables, w):
    n = x1.shape[0] // TM
    small = (w["wsgu"], w["wsd"], w["g2"], w["b2"])
    tok = pl.BlockSpec((N_EXPERTS, TM), lambda i, *_: (0, i))
    return pl.pallas_call(
        _combine_kernel,
        grid_spec=pltpu.PrefetchScalarGridSpec(
            num_scalar_prefetch=7, grid=(n,),
            in_specs=[pl.BlockSpec((TM, D_MODEL), lambda i, *_: (i, 0)), tok, tok,
                      pl.BlockSpec(memory_space=pl.ANY)]
                     + [pl.BlockSpec(a.shape, lambda i, *_, nd=a.ndim: (0,) * nd) for a in small],
            out_specs=_main_tail_specs(n, D_MODEL),
            scratch_shapes=[pltpu.VMEM((L_MAX, TM), F32), pltpu.VMEM((2, L_MAX, HALF), U32),
                            pltpu.VMEM((TM, D_MODEL), F32), pltpu.SemaphoreType.DMA(())]),
        out_shape=(jax.ShapeDtypeStruct((x1.shape[0] - TM, D_MODEL), F32),
                   jax.ShapeDtypeStruct((TM, D_MODEL), F32)),
        compiler_params=_params("arbitrary"), name="combine",
    )(tables["run_off"], tables["run_len"], tables["long_run"], tables["n_gran"], tables["copy_src"],
      tables["copy_dst"], tables["copy_n"], x1, rank, wt, y, *small)


def _moe(x1, w, w_e_gate, w_e_up, w_e_down, layer):
    n_blocks = _n_row_blocks(x1.shape[0])
    wt, rank, cnt = _router(x1, w)
    tables = _dispatch_tables(cnt[:, :N_EXPERTS, 0], n_blocks)
    xs = _dispatch(x1, rank, tables, n_blocks)
    y = _experts(xs, tables, w_e_gate, w_e_up, w_e_down, layer, n_blocks)
    return _combine(x1, rank, wt, y, tables, w)


def _layer_weights(l, w_in, w_gk2, b_gk, gla_norm_g, w_pool, pool_scale, w_br_a, w_br_b, w_out, ln1_g, ln1_b,
                   w_router, b_router, w_sh_gate, w_sh_up, w_sh_down, ln2_g, ln2_b):
    pieces = {}
    pad_rank = LANES - GATE_RANK
    pieces["wgk2"] = jnp.pad(w_gk2[l], ((0, pad_rank), (0, 0))).astype(BF16)
    pieces["bgk"] = b_gk[l].reshape(1, KEY)
    pieces["gn"] = gla_norm_g[l].reshape(1, DV)
    pieces["wbra"] = w_br_a[l].astype(BF16)
    pieces["wpool"] = w_pool[l].astype(BF16)
    pieces["pscale"] = pool_scale[l].reshape(1, POOL_WIDTH)
    pieces["wbrb"] = w_br_b[l].astype(BF16)
    pieces["wout"] = w_out[l].astype(BF16)
    pieces["g1"] = ln1_g[l].reshape(1, D_MODEL)
    pieces["b1"] = ln1_b[l].reshape(1, D_MODEL)
    wrt = w_router[l].T
    wrh = wrt.astype(BF16)
    pieces["wrh"] = wrh
    pieces["wrl"] = (wrt - wrh.astype(F32)).astype(BF16)
    pieces["br"] = b_router[l].reshape(N_EXPERTS, 1)
    pieces["wsgu"] = jnp.concatenate([w_sh_gate[l], w_sh_up[l]], axis=1).astype(BF16)
    pieces["wsd"] = w_sh_down[l].astype(BF16)
    pieces["g2"] = ln2_g[l].reshape(1, D_MODEL)
    pieces["b2"] = ln2_b[l].reshape(1, D_MODEL)
    return pieces


def kernel(x_prompt, x_sample, state_gla, cache_pool, w_in, w_gk2, b_gk, gla_norm_g, w_pool, pool_scale, w_br_a, w_br_b, w_out, ln1_g, ln1_b, w_router, b_router, w_e_gate, w_e_up, w_e_down, w_sh_gate, w_sh_up, w_sh_down, ln2_g, ln2_b):
    assert T_SAMPLE == TM, "the decode tokens must fill exactly one token tile"
    xm = x_prompt.reshape(T_PROMPT, D_MODEL)
    xt = x_sample.reshape(T_SAMPLE, D_MODEL)
    n_prompt_tiles = T_PROMPT // TM
    sample_block0 = T_PROMPT // DEC_SEQ
    sp, hp, ss, hs = [], [], [], []
    for l in range(DEPTH):
        w = _layer_weights(l, w_in, w_gk2, b_gk, gla_norm_g, w_pool, pool_scale, w_br_a, w_br_b, w_out, ln1_g,
                           ln1_b, w_router, b_router, w_sh_gate, w_sh_up, w_sh_down, ln2_g, ln2_b)
        q, k, v, r, lf, u, ga, gb = _inproj(xm, xt, w_in, w, l)

        og_p, st_p = _gla(q, k, v, lf, r, w["gn"], None, tile=TM, chunk=CHUNK, n_tiles=n_prompt_tiles, block0=0)
        og_s, st_s = _gla(q, k, v, lf, r, w["gn"], state_gla[l], tile=DEC_SEQ, chunk=DEC_SEQ, n_tiles=DEC_BATCH,
                          block0=sample_block0)

        u_tiles = u.reshape(n_prompt_tiles + 1, TM, POOL_WIDTH)
        halo_p = jnp.concatenate([jnp.zeros((1, HALO, POOL_WIDTH), F32),
                                  u_tiles[:n_prompt_tiles - 1, TM - HALO:, :]], axis=0)
        halo_s = jnp.concatenate([jnp.zeros((DEC_BATCH, HALO - POOL_HIST, POOL_WIDTH), F32), cache_pool[l]], axis=1)
        x1 = _mix(xm, xt, og_p, og_s, u, halo_p, halo_s, ga, gb, w)

        xm, xt = _moe(x1, w, w_e_gate, w_e_up, w_e_down, l)

        sp.append(st_p)
        ss.append(st_s)
        hp.append(u[T_PROMPT - POOL_HIST:T_PROMPT].reshape(1, POOL_HIST, POOL_WIDTH))
        hs.append(u[T_PROMPT:].reshape(DEC_BATCH, DEC_SEQ, POOL_WIDTH)[:, DEC_SEQ - POOL_HIST:, :])
    y_prompt = xm.reshape(1, SEQ, D_MODEL)
    y_sample = xt.reshape(DEC_BATCH, DEC_SEQ, D_MODEL)
    return (y_prompt, y_sample, jnp.stack(sp), jnp.stack(hp), jnp.stack(ss), jnp.stack(hs))
```

```python
import functools

import jax
import jax.numpy as jnp
from jax import lax
from jax.experimental import pallas as pl
from jax.experimental.pallas import tpu as pltpu

F32 = jnp.float32
BF16 = jnp.bfloat16
U32 = jnp.uint32
I32 = jnp.int32

D_MODEL = 1024
DEPTH = 2
SEQ = 16384
DEC_BATCH = 8
DEC_SEQ = 32
PAST_LEN = 4096
CHUNK = 64
HEADS = 4
DK = 128
DV = 256
KEY = HEADS * DK
VAL = HEADS * DV
GATE_RANK = 16
GATE_NORMALIZER = 16.0
POOL_WIDTH = 512
POOL_WINDOWS = (2, 4, 8, 16)
POOL_GROUP_DIM = 128
POOL_HIST = 15
N_EXPERTS = 64
GROUP_SIZE = 8
TOPK_GROUPS = 4
TOP_K = 8
D_EXPERT = 256
ROUTED_SCALE = 2.5
ALPHA = (2 * DEPTH) ** 0.25
LN_EPS = 1e-5
RMS_EPS = 1e-6
LOG2_E = 1.4426950408889634

LANES = 128
SUBLANES = 8
VMEM_LIMIT_BYTES = 56 * 1024 * 1024

T_PROMPT = SEQ
T_SAMPLE = DEC_BATCH * DEC_SEQ
T_ALL = T_PROMPT + T_SAMPLE
TM = 256
HALO = 16
CHUNK_UNROLL = 4
NEG_INF = float("-inf")

HALF = D_MODEL // 2
GRANULE = SUBLANES
ROW_BLOCK = 512
COPY_GROUP = 8
COPY_ROWS = (32, 16, 8)
RING = 3
RUN_CHUNK = 64
L_MAX = 2560
SEL_CHUNK = 512
COPY_CAP = (L_MAX // COPY_ROWS[0], N_EXPERTS, N_EXPERTS)
HI_MASK = 0xFFFF0000


def _dot(a, b):
    return jnp.dot(a, b, preferred_element_type=F32)


def _dot_nt(a, b):
    return lax.dot_general(a, b, (((1,), (1,)), ((), ())), preferred_element_type=F32)


def _dot_tn(a, b):
    return lax.dot_general(a, b, (((0,), (0,)), ((), ())), preferred_element_type=F32)


def _sigmoid(x):
    return 0.5 * jnp.tanh(0.5 * x) + 0.5


def _silu(x):
    return x * _sigmoid(x)


def _layer_norm(x, g, b):
    mu = jnp.mean(x, axis=-1, keepdims=True)
    xc = x - mu
    var = jnp.mean(xc * xc, axis=-1, keepdims=True)
    return xc * lax.rsqrt(var + LN_EPS) * g + b


def _pack_halves(v):
    bits = lax.bitcast_convert_type(v, U32)
    return (bits[:, HALF:] & jnp.uint32(HI_MASK)) | (bits[:, :HALF] >> 16)


def _unpack_halves(w):
    lo = lax.bitcast_convert_type(w << 16, F32).astype(BF16)
    hi = lax.bitcast_convert_type(w & jnp.uint32(HI_MASK), F32).astype(BF16)
    return lo, hi


def _params(*sem):
    return pltpu.CompilerParams(dimension_semantics=sem, vmem_limit_bytes=VMEM_LIMIT_BYTES)


def _const_spec(shape):
    nd = len(shape)
    return pl.BlockSpec(shape, lambda *_: (0,) * nd)


def _is_tail_tile():
    return pl.program_id(0) == pl.num_programs(0) - 1


def _main_tail_specs(n_tiles, width):
    return [pl.BlockSpec((TM, width), lambda i, *_: (jnp.minimum(i, n_tiles - 2), 0)),
            pl.BlockSpec((TM, width), lambda i, *_: (0, 0))]


IN_PIECES = ((("q", KEY), ("k", KEY), ("v", VAL), ("r", VAL)),
             (("gk", LANES), ("u", POOL_WIDTH), ("ga", D_MODEL), ("gb", D_MODEL)))
IN_COLS = {}
for _part, _pieces in enumerate(IN_PIECES):
    _col = 0
    for _name, _width in _pieces:
        IN_COLS[_name] = (_part, slice(_col, _col + _width))
        _col += _width
IN_SPLIT = sum(width for _, width in IN_PIECES[0])
IN_REST = sum(width for _, width in IN_PIECES[1][1:])
N_PROJ = IN_SPLIT + GATE_RANK + IN_REST
W_CHUNK = 256


def _inproj_kernel(xm_ref, xt_ref, w_hbm, wgk2, bgk, q_o, k_o, v_o, r_o, lf_o, u_o, ga_o, gb_o,
                   stage, wa_scr, wb_scr, sems, *, layer):
    @pl.when(pl.program_id(0) == 0)
    def _():
        n_chunks = D_MODEL // W_CHUNK

        def copy(c):
            rows = pl.ds(c * W_CHUNK, W_CHUNK)
            return pltpu.make_async_copy(w_hbm.at[layer, rows, :], stage.at[c % 2], sems.at[c % 2])

        copy(0).start()
        lane = lax.broadcasted_iota(I32, (W_CHUNK, LANES), 1)
        for c in range(n_chunks):
            if c + 1 < n_chunks:
                copy(c + 1).start()
            copy(c).wait()
            blk = stage[c % 2]
            rows = slice(c * W_CHUNK, (c + 1) * W_CHUNK)
            wa_scr[rows, :] = blk[:, :IN_SPLIT].astype(BF16)
            gate = jnp.where(lane < GATE_RANK, blk[:, IN_SPLIT:IN_SPLIT + LANES], 0.0)
            wb_scr[rows, :LANES] = gate.astype(BF16)
            wb_scr[rows, LANES:] = blk[:, IN_SPLIT + GATE_RANK:].astype(BF16)

    xb = jnp.where(_is_tail_tile(), xt_ref[...], xm_ref[...]).astype(BF16)
    proj = lambda name: _dot(xb, (wa_scr, wb_scr)[IN_COLS[name][0]][:, IN_COLS[name][1]])
    q_o[...] = proj("q") * (DK ** -0.5)
    k_o[...] = proj("k")
    v_o[...] = proj("v").astype(BF16)
    r_o[...] = proj("r")
    gk = proj("gk").astype(BF16)
    z = _dot(gk, wgk2[...]) + bgk[...]
    log_sig = jnp.minimum(z, 0.0) - jnp.log1p(jnp.exp(-jnp.abs(z)))
    lf_o[...] = log_sig * (1.0 / GATE_NORMALIZER)
    u_o[...] = proj("u")
    ga_o[...] = proj("ga")
    gb_o[...] = proj("gb")


def _inproj(xm, xt, w_in, w, layer):
    n = xm.shape[0] // TM + 1
    rows = n * TM
    row = lambda width: pl.BlockSpec((TM, width), lambda i: (i, 0))
    out_shapes = (
        jax.ShapeDtypeStruct((rows, KEY), F32), jax.ShapeDtypeStruct((rows, KEY), F32),
        jax.ShapeDtypeStruct((rows, VAL), BF16), jax.ShapeDtypeStruct((rows, VAL), F32),
        jax.ShapeDtypeStruct((rows, KEY), F32), jax.ShapeDtypeStruct((rows, POOL_WIDTH), F32),
        jax.ShapeDtypeStruct((rows, D_MODEL), F32), jax.ShapeDtypeStruct((rows, D_MODEL), F32))
    small = (w["wgk2"], w["bgk"])
    return pl.pallas_call(
        functools.partial(_inproj_kernel, layer=layer), grid=(n,),
        in_specs=_main_tail_specs(n, D_MODEL) + [pl.BlockSpec(memory_space=pltpu.HBM)]
                 + [_const_spec(a.shape) for a in small],
        out_specs=[row(KEY), row(KEY), row(VAL), row(VAL), row(KEY), row(POOL_WIDTH), row(D_MODEL), row(D_MODEL)],
        out_shape=out_shapes,
        scratch_shapes=[pltpu.VMEM((2, W_CHUNK, N_PROJ), F32), pltpu.VMEM((D_MODEL, IN_SPLIT), BF16),
                        pltpu.VMEM((D_MODEL, LANES + IN_REST), BF16), pltpu.SemaphoreType.DMA((2,))],
        compiler_params=_params("arbitrary"), name="inproj",
    )(xm, xt, w_in, *small)


def _gla_kernel(*refs, chunk, n_chunks, has_init):
    if has_init:
        q_ref, k_ref, v_ref, lf_ref, r_ref, gn_ref, s0_ref, o_ref, s_out_ref, st_ref = refs
        for h in range(HEADS):
            st_ref[h] = s0_ref[0, h].T
    else:
        q_ref, k_ref, v_ref, lf_ref, r_ref, gn_ref, o_ref, s_out_ref, st_ref = refs

        @pl.when(pl.program_id(0) == 0)
        def _():
            st_ref[...] = jnp.zeros_like(st_ref)

    row = lax.broadcasted_iota(I32, (chunk, KEY), 0)
    ta = lax.broadcasted_iota(I32, (chunk, chunk), 0)
    sa = lax.broadcasted_iota(I32, (chunk, chunk), 1)
    gn = gn_ref[...]
    halves = [1 << i for i in range(chunk.bit_length() - 1)]
    diag = ta == sa
    pairs = [((ta ^ sa) < 2 * half) & ((ta & half) != 0) & ((sa & half) == 0) for half in halves]

    def chunk_body(c, carry):
        rows = pl.ds(pl.multiple_of(c * chunk, chunk), chunk)
        lf = lf_ref[rows, :] * LOG2_E
        q = q_ref[rows, :]
        k = k_ref[rows, :]
        qb = q.astype(BF16)
        kb = k.astype(BF16)
        att = []
        for h in range(HEADS):
            hs = slice(h * DK, (h + 1) * DK)
            att.append(jnp.where(diag, _dot_nt(qb[:, hs], kb[:, hs]), 0.0))
        seg, tot = lf, lf
        for half, pair in zip(halves, pairs):
            qs = (q * jnp.exp2(seg)).astype(BF16)
            ks = (k * jnp.exp2(tot - seg)).astype(BF16)
            for h in range(HEADS):
                hs = slice(h * DK, (h + 1) * DK)
                att[h] = jnp.where(pair, _dot_nt(qs[:, hs], ks[:, hs]), att[h])
            if half < SUBLANES:
                upper = (row & half) != 0
                below = pltpu.roll(tot, half, 0)
                above = pltpu.roll(tot, chunk - half, 0)
                seg = seg + jnp.where(upper, below, 0.0)
                tot = tot + jnp.where(upper, below, above)
            else:
                step, n_tiles = half // SUBLANES, chunk // SUBLANES
                seg_t = [seg[i * SUBLANES:(i + 1) * SUBLANES, :] for i in range(n_tiles)]
                tot_t = [tot[i * SUBLANES:(i + 1) * SUBLANES, :] for i in range(n_tiles)]
                seg = jnp.concatenate([seg_t[i] + tot_t[i - step] if i & step else seg_t[i] for i in range(n_tiles)], axis=0)
                tot = jnp.concatenate([tot_t[i] + tot_t[i ^ step] for i in range(n_tiles)], axis=0)
        q_in = (q * jnp.exp2(seg)).astype(BF16)
        k_out = (k * jnp.exp2(tot - seg)).astype(BF16)
        decay = jnp.exp2(tot[0:1, :])
        for h in range(HEADS):
            hs = slice(h * DK, (h + 1) * DK)
            vs = slice(h * DV, (h + 1) * DV)
            state = st_ref[h]
            vh = v_ref[rows, vs]
            o = _dot_nt(q_in[:, hs], state.astype(BF16)) + _dot(att[h].astype(BF16), vh)
            st_ref[h] = state * decay[:, hs] + _dot_tn(vh, k_out[:, hs])
            ms = jnp.mean(o * o, axis=-1, keepdims=True)
            o = o * lax.rsqrt(ms + RMS_EPS) * gn
            o_ref[rows, vs] = (o * _silu(r_ref[rows, vs])).astype(BF16)
        return carry

    lax.fori_loop(0, n_chunks, chunk_body, 0, unroll=min(n_chunks, CHUNK_UNROLL))

    def write_state():
        for h in range(HEADS):
            s_out_ref[0, h] = st_ref[h].T

    if has_init:
        write_state()
    else:
        pl.when(pl.program_id(0) == pl.num_programs(0) - 1)(write_state)


def _gla(q, k, v, lf, r, gn, s0, *, tile, chunk, n_tiles, block0):
    has_init = s0 is not None
    row = lambda width: pl.BlockSpec((tile, width), lambda i: (block0 + i, 0))
    st_spec = pl.BlockSpec((1, HEADS, DK, DV), (lambda i: (i, 0, 0, 0)) if has_init else (lambda i: (0, 0, 0, 0)))
    n_states = n_tiles if has_init else 1
    in_specs = [row(KEY), row(KEY), row(VAL), row(KEY), row(VAL), _const_spec(gn.shape)]
    args = [q, k, v, lf, r, gn]
    if has_init:
        in_specs.append(st_spec)
        args.append(s0)
    return pl.pallas_call(
        functools.partial(_gla_kernel, chunk=chunk, n_chunks=tile // chunk, has_init=has_init),
        grid=(n_tiles,), in_specs=in_specs,
        out_specs=[pl.BlockSpec((tile, VAL), lambda i: (i, 0)), st_spec],
        out_shape=(jax.ShapeDtypeStruct((n_tiles * tile, VAL), BF16),
                   jax.ShapeDtypeStruct((n_states, HEADS, DK, DV), F32)),
        scratch_shapes=[pltpu.VMEM((HEADS, DV, DK), F32)],
        compiler_params=_params("arbitrary"), name="gla_init" if has_init else "gla",
    )(*args)


def _window_sums(ext, window):
    shift = 1
    while shift < window:
        ext = ext + pltpu.roll(ext, shift, 0)
        shift *= 2
    return ext


def _mix_kernel(xm_ref, xt_ref, ogm_ref, ogt_ref, u_ref, halo_m_ref, halo_t_ref, ga_ref, gb_ref,
                wbra, wpool, pscale, wbrb, wout, g1, b1, x1_ref):
    tail = _is_tail_tile()
    ya = _dot(jnp.where(tail, ogt_ref[...], ogm_ref[...]), wbra[...])
    u = u_ref[...]
    ext_m = jnp.concatenate([halo_m_ref[0], u], axis=0)
    seg = HALO + DEC_SEQ
    pieces = []
    for b in range(DEC_BATCH):
        pieces += [halo_t_ref[b], u[b * DEC_SEQ:(b + 1) * DEC_SEQ, :]]
    ext_t = jnp.concatenate(pieces, axis=0)
    rowi = lax.broadcasted_iota(I32, (TM, 1), 0)
    pos = jnp.where(tail, PAST_LEN + (rowi & (DEC_SEQ - 1)), pl.program_id(0) * TM + rowi)
    parts = []
    for g, window in enumerate(POOL_WINDOWS):
        cols = slice(g * POOL_GROUP_DIM, (g + 1) * POOL_GROUP_DIM)
        win_m = _window_sums(ext_m[:, cols], window)[HALO:, :]
        win_all = _window_sums(ext_t[:, cols], window)
        win_t = jnp.concatenate([win_all[b * seg + HALO:(b + 1) * seg, :] for b in range(DEC_BATCH)], axis=0)
        cnt = jnp.minimum(window, pos + 1).astype(F32)
        d = jnp.where(tail, win_t, win_m) / cnt - u[:, cols]
        parts.append(_dot(d.astype(BF16), wpool[g]))
    yb_in = jnp.concatenate(parts, axis=1) * pscale[...]
    yb = _dot(yb_in.astype(BF16), wbrb[...])
    mixed = _sigmoid(ga_ref[...]) * ya + _sigmoid(gb_ref[...]) * yb
    mix = _dot(mixed.astype(BF16), wout[...])
    x = jnp.where(tail, xt_ref[...], xm_ref[...])
    x1_ref[...] = _layer_norm(ALPHA * x + mix, g1[...], b1[...])


def _mix(xm, xt, og_m, og_t, u, halo_m, halo_t, ga, gb, w):
    n = xm.shape[0] // TM + 1
    row = lambda width: pl.BlockSpec((TM, width), lambda i: (i, 0))
    weights = (w["wbra"], w["wpool"], w["pscale"], w["wbrb"], w["wout"], w["g1"], w["b1"])
    return pl.pallas_call(
        _mix_kernel, grid=(n,),
        in_specs=_main_tail_specs(n, D_MODEL) + _main_tail_specs(n, VAL) + [
            row(POOL_WIDTH), pl.BlockSpec((1, HALO, POOL_WIDTH), lambda i: (jnp.minimum(i, n - 2), 0, 0)),
            _const_spec(halo_t.shape), row(D_MODEL), row(D_MODEL)] + [_const_spec(a.shape) for a in weights],
        out_specs=row(D_MODEL),
        out_shape=jax.ShapeDtypeStruct((n * TM, D_MODEL), F32),
        compiler_params=_params("arbitrary"), name="mix",
    )(xm, xt, og_m, og_t, u, halo_m, halo_t, ga, gb, *weights)


def _router_kernel(x1_ref, wrh_ref, wrl_ref, br_ref, wt_ref, rank_ref, cnt_ref, *, tile):
    x = x1_ref[...]
    xh = x.astype(BF16)
    xl = (x - xh.astype(F32)).astype(BF16)
    logits = _dot_nt(wrh_ref[...], xh) + _dot_nt(wrl_ref[...], xh) + _dot_nt(wrh_ref[...], xl)
    scores = _sigmoid(logits)
    rows_pad = N_EXPERTS
    n_grp = rows_pad // GROUP_SIZE
    biased = scores + br_ref[...]
    shape3 = (n_grp, GROUP_SIZE, tile)
    b3 = biased.reshape(shape3)
    s3 = scores.reshape(shape3)
    sub = lax.broadcasted_iota(I32, shape3, 1)
    gid = lax.broadcasted_iota(I32, shape3, 0)
    eid = gid * GROUP_SIZE + sub
    m1 = jnp.max(b3, axis=1, keepdims=True)
    i1 = jnp.min(jnp.where(b3 == m1, sub, GROUP_SIZE), axis=1, keepdims=True)
    m2 = jnp.max(jnp.where(sub == i1, NEG_INF, b3), axis=1, keepdims=True)
    gscore = m1 + m2
    gid1 = lax.broadcasted_iota(I32, (n_grp, 1, tile), 0)
    gsel = jnp.zeros((n_grp, 1, tile), jnp.bool_)
    for _ in range(TOPK_GROUPS):
        gm = jnp.max(gscore, axis=0, keepdims=True)
        gi = jnp.min(jnp.where(gscore == gm, gid1, n_grp), axis=0, keepdims=True)
        pick = gid1 == gi
        gsel = gsel | pick
        gscore = jnp.where(pick, NEG_INF, gscore)
    masked = jnp.where(gsel, b3, NEG_INF)
    wsel = jnp.zeros(shape3, F32)
    chosen = jnp.zeros(shape3, jnp.bool_)
    for _ in range(TOP_K):
        m = jnp.max(jnp.max(masked, axis=1, keepdims=True), axis=0, keepdims=True)
        idx = jnp.min(jnp.min(jnp.where(masked == m, eid, rows_pad), axis=1, keepdims=True), axis=0, keepdims=True)
        pick = eid == idx
        wsel = jnp.where(pick, s3, wsel)
        chosen = chosen | pick
        masked = jnp.where(pick, NEG_INF, masked)
    wsum = jnp.sum(jnp.sum(wsel, axis=1, keepdims=True), axis=0, keepdims=True)
    wt_ref[...] = (wsel / wsum * ROUTED_SCALE).reshape(rows_pad, tile)
    sel = jnp.where(chosen, 1.0, 0.0).reshape(rows_pad, tile)
    before = lax.broadcasted_iota(I32, (tile, tile), 0) < lax.broadcasted_iota(I32, (tile, tile), 1)
    rank = _dot(sel.astype(BF16), jnp.where(before, 1.0, 0.0).astype(BF16))
    rank_ref[...] = jnp.where(sel > 0.0, rank, -1.0).astype(I32)
    cnt = jnp.sum(sel, axis=1, keepdims=True).astype(I32)
    cnt_ref[0] = jnp.broadcast_to(cnt, (rows_pad, LANES))


def _router(x1, w):
    n = x1.shape[0] // TM
    rows_pad = N_EXPERTS
    tok = pl.BlockSpec((rows_pad, TM), lambda i: (0, i))
    return pl.pallas_call(
        functools.partial(_router_kernel, tile=TM), grid=(n,),
        in_specs=[pl.BlockSpec((TM, D_MODEL), lambda i: (i, 0)), _const_spec(w["wrh"].shape),
                  _const_spec(w["wrl"].shape), _const_spec(w["br"].shape)],
        out_specs=[tok, tok, pl.BlockSpec((1, rows_pad, LANES), lambda i: (i, 0, 0))],
        out_shape=(jax.ShapeDtypeStruct((rows_pad, x1.shape[0]), F32),
                   jax.ShapeDtypeStruct((rows_pad, x1.shape[0]), I32),
                   jax.ShapeDtypeStruct((n, rows_pad, LANES), I32)),
        compiler_params=_params("arbitrary"), name="router",
    )(x1, w["wrh"], w["wrl"], w["br"])


def _n_row_blocks(n_tokens):
    n_tiles = n_tokens // TM
    worst_rows = n_tokens * TOP_K + n_tiles * N_EXPERTS * (GRANULE - 1) + N_EXPERTS * (ROW_BLOCK - 1)
    return -(-worst_rows // ROW_BLOCK)


def _steps(shift):
    return shift - jnp.concatenate([jnp.zeros_like(shift[..., :1]), shift[..., :-1]], axis=-1)


def _dispatch_tables(cnt, n_blocks):
    padded = (cnt + (GRANULE - 1)) // GRANULE * GRANULE
    run_end = jnp.cumsum(padded, axis=1)
    run_off = run_end - padded
    n_gran = run_end[:, -1] // GRANULE
    rows_e = jnp.sum(padded, axis=0)
    region = (rows_e + (ROW_BLOCK - 1)) // ROW_BLOCK * ROW_BLOCK
    region_end = jnp.cumsum(region)
    region_start = region_end - region
    run_base = region_start[None, :] + jnp.cumsum(padded, axis=0) - padded

    big = padded // COPY_ROWS[0]
    big_first = jnp.cumsum(big, axis=1) - big
    k_big = jnp.arange(COPY_CAP[0], dtype=I32)
    owned = k_big[None, :, None] >= big_first[:, None, :]
    big_list = lambda row0: COPY_ROWS[0] * k_big[None, :] + jnp.sum(
        jnp.where(owned, _steps(row0 - COPY_ROWS[0] * big_first)[:, None, :], 0), axis=2)
    lists_src, lists_dst, counts = [big_list(run_off)], [big_list(run_base)], [jnp.sum(big, axis=1)]
    done = big * COPY_ROWS[0]
    for rows, cap in zip(COPY_ROWS[1:], COPY_CAP[1:]):
        has = (padded // rows) % 2
        pos = jnp.cumsum(has, axis=1) - has
        hit = (has[:, None, :] != 0) & (pos[:, None, :] == jnp.arange(cap, dtype=I32)[None, :, None])
        lists_src.append(jnp.sum(jnp.where(hit, (run_off + done)[:, None, :], 0), axis=2))
        lists_dst.append(jnp.sum(jnp.where(hit, (run_base + done)[:, None, :], 0), axis=2))
        counts.append(jnp.sum(has, axis=1))
        done = done + has * rows
    copy_src = jnp.concatenate(lists_src, axis=1)
    copy_dst = jnp.concatenate(lists_dst, axis=1)
    copy_n = jnp.stack(counts, axis=1)

    n_used = region_end[-1] // ROW_BLOCK
    blocks = jnp.arange(n_blocks, dtype=I32)
    block_expert = jnp.minimum(jnp.sum(blocks[:, None] >= (region_end // ROW_BLOCK)[None, :], axis=1), N_EXPERTS - 1)
    experts = jnp.arange(N_EXPERTS, dtype=I32)
    present = (region > 0).astype(I32)
    ordinal = jnp.cumsum(present) - present
    seq_expert = jnp.sum(jnp.where((present[None, :] != 0) & (ordinal[None, :] == experts[:, None]),
                                   experts[None, :], 0), axis=1)
    block_ord = jnp.sum(jnp.where(block_expert[:, None] == experts[None, :], ordinal[None, :], 0), axis=1)
    tail = (region - rows_e) // GRANULE
    tail_end = jnp.cumsum(tail)
    tail_first = tail_end - tail
    slot = jnp.arange(N_EXPERTS * (ROW_BLOCK // GRANULE - 1), dtype=I32)
    t_step = _steps((region_start + rows_e) // GRANULE - tail_first)
    tail_dst = slot + jnp.sum(jnp.where(slot[:, None] >= tail_first[None, :], t_step[None, :], 0), axis=1)
    as_i32 = lambda a: a.astype(I32).reshape(-1)
    long_run = jnp.any(padded > RUN_CHUNK, axis=1)
    return dict(run_off=as_i32(run_off), run_len=as_i32(padded), long_run=as_i32(long_run),
                n_gran=as_i32(n_gran), copy_src=as_i32(copy_src), copy_dst=as_i32(copy_dst), copy_n=as_i32(copy_n),
                n_used=as_i32(n_used), block_expert=as_i32(block_expert), block_ord=as_i32(block_ord),
                seq_expert=as_i32(seq_expert), n_seq=as_i32(jnp.sum(present)), n_tail=as_i32(tail_end[-1]),
                tail_dst=as_i32(tail_dst * GRANULE))


def _row_copy(src, src_row, dst, dst_row, sem, rows):
    s = pl.multiple_of(src_row, GRANULE)
    d = pl.multiple_of(dst_row, GRANULE)
    return pltpu.make_async_copy(src.at[pl.ds(s, rows), :], dst.at[pl.ds(d, rows), :], sem)


def _grouped_loop(n, body):
    n_groups = n // COPY_GROUP

    def group(j, carry):
        for i in range(COPY_GROUP):
            body(j * COPY_GROUP + i)
        return carry

    def single(i, carry):
        body(i)
        return carry

    lax.fori_loop(0, n_groups, group, 0)
    lax.fori_loop(n_groups * COPY_GROUP, n, single, 0)


def _start_copies(n, make_copy):
    _grouped_loop(n, lambda i: make_copy(i).start())


def _wait_copies(n, src, dst, sem, rows):
    n_groups = n // COPY_GROUP

    def group(j, carry):
        _row_copy(src, 0, dst, 0, sem, COPY_GROUP * rows).wait()
        return carry

    def single(i, carry):
        _row_copy(src, 0, dst, 0, sem, rows).wait()
        return carry

    lax.fori_loop(0, n_groups, group, 0)
    lax.fori_loop(n_groups * COPY_GROUP, n, single, 0)


def _start_tile_copies(tile, copy_n, make_copy):
    for c, rows in enumerate(COPY_ROWS):
        base = tile * sum(COPY_CAP) + sum(COPY_CAP[:c])
        _start_copies(copy_n[tile * len(COPY_ROWS) + c], lambda i, base=base, rows=rows: make_copy(base + i, rows))


def _wait_tile_copies(tile, copy_n, src, dst, sem):
    for c, rows in enumerate(COPY_ROWS):
        _wait_copies(copy_n[tile * len(COPY_ROWS) + c], src, dst, sem, rows)


def _build_selection(sel_scr, rank, values, run_off_ref, run_len_ref, long_run_ref, tile_idx):
    slot = lax.broadcasted_iota(I32, (RUN_CHUNK, TM), 0)

    def write_runs(all_chunks):
        for e in range(N_EXPERTS):
            off = run_off_ref[tile_idx * N_EXPERTS + e]
            rank_e = rank[e:e + 1, :]
            val_e = 1.0 if values is None else values[e:e + 1, :]

            def chunk(c, carry, off=off, rank_e=rank_e, val_e=val_e):
                hit = rank_e == slot + c * RUN_CHUNK
                rows = pl.ds(pl.multiple_of(off + c * RUN_CHUNK, GRANULE), RUN_CHUNK)
                sel_scr[rows, :] = jnp.where(hit, val_e, 0.0)
                return carry

            chunk(0, 0)
            if all_chunks:
                n_chunks = (run_len_ref[tile_idx * N_EXPERTS + e] + (RUN_CHUNK - 1)) // RUN_CHUNK
                lax.fori_loop(1, n_chunks, chunk, 0)

    write_runs(False)

    @pl.when(long_run_ref[tile_idx] != 0)
    def _():
        write_runs(True)


def _dispatch_kernel(run_off, run_len, long_run, n_gran, copy_src, copy_dst, copy_n, n_tail, tail_dst, n_used,
                     x1_ref, rank_ref, xs_hbm, sel_scr, buf_scr, zero_scr, sem):
    t = pl.program_id(0)

    @pl.when(t == 0)
    def _():
        sel_scr[...] = jnp.zeros_like(sel_scr)
        buf_scr[...] = jnp.zeros_like(buf_scr)
        zero_scr[...] = jnp.zeros_like(zero_scr)

    _build_selection(sel_scr, rank_ref[...], None, run_off, run_len, long_run, t)
    xb = x1_ref[...].astype(BF16)
    n_g = n_gran[t]
    buf = buf_scr.at[t % 2]
    for c in range(L_MAX // SEL_CHUNK):
        @pl.when(c * SEL_CHUNK < n_g * GRANULE)
        def _(c=c):
            rows = slice(c * SEL_CHUNK, (c + 1) * SEL_CHUNK)
            buf[rows, :] = _pack_halves(_dot(sel_scr[rows, :].astype(BF16), xb))

    @pl.when(t > 0)
    def _():
        _wait_tile_copies(jnp.maximum(t - 1, 0), copy_n, buf, xs_hbm, sem)

    _start_tile_copies(
        t, copy_n, lambda i, rows: _row_copy(buf, copy_src[i], xs_hbm, copy_dst[i], sem, rows))

    @pl.when(t == pl.num_programs(0) - 1)
    def _():
        _wait_tile_copies(t, copy_n, buf, xs_hbm, sem)
        _start_copies(n_tail[0], lambda i: _row_copy(zero_scr, 0, xs_hbm, tail_dst[i], sem, GRANULE))
        _wait_copies(n_tail[0], zero_scr, xs_hbm, sem, GRANULE)

        def spare_copy(b):
            rows = pl.ds(pl.multiple_of(b * ROW_BLOCK, ROW_BLOCK), ROW_BLOCK)
            return pltpu.make_async_copy(zero_scr, xs_hbm.at[rows, :], sem)

        def start_spare(b, carry):
            spare_copy(b).start()
            return carry

        def wait_spare(b, carry):
            spare_copy(b).wait()
            return carry

        n_blocks = xs_hbm.shape[0] // ROW_BLOCK
        lax.fori_loop(n_used[0], n_blocks, start_spare, 0)
        lax.fori_loop(n_used[0], n_blocks, wait_spare, 0)


def _dispatch(x1, rank, tables, n_blocks):
    n = x1.shape[0] // TM
    return pl.pallas_call(
        _dispatch_kernel,
        grid_spec=pltpu.PrefetchScalarGridSpec(
            num_scalar_prefetch=10, grid=(n,),
            in_specs=[pl.BlockSpec((TM, D_MODEL), lambda i, *_: (i, 0)),
                      pl.BlockSpec((N_EXPERTS, TM), lambda i, *_: (0, i))],
            out_specs=pl.BlockSpec(memory_space=pl.ANY),
            scratch_shapes=[pltpu.VMEM((L_MAX, TM), F32), pltpu.VMEM((2, L_MAX, HALF), U32),
                            pltpu.VMEM((ROW_BLOCK, HALF), U32), pltpu.SemaphoreType.DMA(())]),
        out_shape=jax.ShapeDtypeStruct((n_blocks * ROW_BLOCK, HALF), U32),
        compiler_params=_params("arbitrary"), name="dispatch",
    )(tables["run_off"], tables["run_len"], tables["long_run"], tables["n_gran"], tables["copy_src"],
      tables["copy_dst"], tables["copy_n"], tables["n_tail"],
      tables["tail_dst"], tables["n_used"], x1, rank)


def _experts_kernel(block_expert, block_ord, seq_expert, n_seq, n_used,
                    xs_hbm, wg_hbm, wu_hbm, wd_hbm, y_ref,
                    x_ring, wg_stage, wu_stage, wd_stage, wgu_scr, wd_scr, x_sems, w_sems, *, layer):
    b = pl.program_id(0)
    n_u = n_used[0]

    def fetch(block):
        slot = block % RING
        rows = pl.ds(pl.multiple_of(block * ROW_BLOCK, ROW_BLOCK), ROW_BLOCK)
        return pltpu.make_async_copy(xs_hbm.at[rows, :], x_ring.at[slot], x_sems.at[slot])

    def weight_copies(ordinal):
        e = seq_expert[ordinal]
        slot = ordinal % 2
        return [pltpu.make_async_copy(src.at[layer, e], stage.at[slot], w_sems.at[slot])
                for src, stage in ((wg_hbm, wg_stage), (wu_hbm, wu_stage), (wd_hbm, wd_stage))]

    @pl.when(b == 0)
    def _():
        for copy in weight_copies(0):
            copy.start()
        for first in range(RING - 1):
            @pl.when(first < n_u)
            def _(first=first):
                fetch(first).start()

    @pl.when(b + (RING - 1) < n_u)
    def _():
        fetch(b + (RING - 1)).start()

    @pl.when(b >= n_u)
    def _():
        y_ref[...] = jnp.zeros_like(y_ref)

    @pl.when(b < n_u)
    def _():
        fresh = (b == 0) | (block_expert[b] != block_expert[jnp.maximum(b - 1, 0)])

        @pl.when(fresh)
        def _():
            ordinal = block_ord[b]
            slot = ordinal % 2
            for copy in weight_copies(ordinal):
                copy.wait()
            wgu_scr[:, :D_EXPERT] = wg_stage[slot].astype(BF16)
            wgu_scr[:, D_EXPERT:] = wu_stage[slot].astype(BF16)
            wd_scr[...] = wd_stage[slot].astype(BF16)

            @pl.when(ordinal + 1 < n_seq[0])
            def _():
                for copy in weight_copies(ordinal + 1):
                    copy.start()

        fetch(b).wait()
        lo, hi = _unpack_halves(x_ring[b % RING])
        gu = _dot(lo, wgu_scr[:HALF, :]) + _dot(hi, wgu_scr[HALF:, :])
        hid = (_silu(gu[:, :D_EXPERT]) * gu[:, D_EXPERT:]).astype(BF16)
        y = _dot(hid, wd_scr[...])
        y_ref[...] = _pack_halves(y.astype(BF16).astype(F32))


def _experts(xs, tables, w_e_gate, w_e_up, w_e_down, layer, n_blocks):
    any_spec = pl.BlockSpec(memory_space=pl.ANY)
    return pl.pallas_call(
        functools.partial(_experts_kernel, layer=layer),
        grid_spec=pltpu.PrefetchScalarGridSpec(
            num_scalar_prefetch=5, grid=(n_blocks,),
            in_specs=[any_spec, any_spec, any_spec, any_spec],
            out_specs=pl.BlockSpec((ROW_BLOCK, HALF), lambda b, *_: (b, 0)),
            scratch_shapes=[pltpu.VMEM((RING, ROW_BLOCK, HALF), U32),
                            pltpu.VMEM((2, D_MODEL, D_EXPERT), F32), pltpu.VMEM((2, D_MODEL, D_EXPERT), F32),
                            pltpu.VMEM((2, D_EXPERT, D_MODEL), F32),
                            pltpu.VMEM((D_MODEL, 2 * D_EXPERT), BF16), pltpu.VMEM((D_EXPERT, D_MODEL), BF16),
                            pltpu.SemaphoreType.DMA((RING,)), pltpu.SemaphoreType.DMA((2,))]),
        out_shape=jax.ShapeDtypeStruct(xs.shape, U32),
        compiler_params=_params("arbitrary"), name="experts",
    )(tables["block_expert"], tables["block_ord"], tables["seq_expert"], tables["n_seq"], tables["n_used"],
      xs, w_e_gate, w_e_up, w_e_down)


def _combine_kernel(run_off, run_len, long_run, n_gran, copy_src, copy_dst, copy_n,
                    x1_ref, rank_ref, wt_ref, y_hbm, wsgu, wsd, g2, b2, out_m_ref, out_t_ref,
                    sel_scr, buf_scr, acc_scr, sem):
    t = pl.program_id(0)
    n_tiles = pl.num_programs(0)
    n_g = n_gran[t]

    def fetch(tile):
        dst = buf_scr.at[tile % 2]
        _start_tile_copies(
            tile, copy_n, lambda i, rows: _row_copy(y_hbm, copy_dst[i], dst, copy_src[i], sem, rows))

    @pl.when(t == 0)
    def _():
        buf_scr[...] = jnp.zeros_like(buf_scr)
        fetch(t)

    stale = pl.multiple_of(jnp.minimum(n_g * GRANULE, L_MAX - SEL_CHUNK), GRANULE)
    sel_scr[pl.ds(stale, SEL_CHUNK), :] = jnp.zeros((SEL_CHUNK, TM), F32)
    _build_selection(sel_scr, rank_ref[...], wt_ref[...], run_off, run_len, long_run, t)
    x1 = x1_ref[...]
    xb = x1.astype(BF16)
    gu = _dot(xb, wsgu[...])
    hs = (_silu(gu[:, :D_EXPERT]) * gu[:, D_EXPERT:]).astype(BF16)
    acc_scr[...] = _dot(hs, wsd[...])
    buf = buf_scr.at[t % 2]
    _wait_tile_copies(t, copy_n, y_hbm, buf, sem)

    @pl.when(t + 1 < n_tiles)
    def _():
        fetch(jnp.minimum(t + 1, n_tiles - 1))

    for c in range(L_MAX // SEL_CHUNK):
        @pl.when(c * SEL_CHUNK < n_g * GRANULE)
        def _(c=c):
            rows = slice(c * SEL_CHUNK, (c + 1) * SEL_CHUNK)
            lo, hi = _unpack_halves(buf[rows, :])
            sel = sel_scr[rows, :].astype(BF16)
            acc_scr[:, :HALF] += _dot_tn(sel, lo)
            acc_scr[:, HALF:] += _dot_tn(sel, hi)

    out = _layer_norm(ALPHA * x1 + acc_scr[...], g2[...], b2[...])

    @pl.when(t + 1 < n_tiles)
    def _():
        out_m_ref[...] = out

    @pl.when(t + 1 == n_tiles)
    def _():
        out_t_ref[...] = out


def _combine(x1, rank, wt, y, tables, w):
    n = x1.shape[0] // TM
    small = (w["wsgu"], w["wsd"], w["g2"], w["b2"])
    tok = pl.BlockSpec((N_EXPERTS, TM), lambda i, *_: (0, i))
    return pl.pallas_call(
        _combine_kernel,
        grid_spec=pltpu.PrefetchScalarGridSpec(
            num_scalar_prefetch=7, grid=(n,),
            in_specs=[pl.BlockSpec((TM, D_MODEL), lambda i, *_: (i, 0)), tok, tok,
                      pl.BlockSpec(memory_space=pl.ANY)]
                     + [pl.BlockSpec(a.shape, lambda i, *_, nd=a.ndim: (0,) * nd) for a in small],
            out_specs=_main_tail_specs(n, D_MODEL),
            scratch_shapes=[pltpu.VMEM((L_MAX, TM), F32), pltpu.VMEM((2, L_MAX, HALF), U32),
                            pltpu.VMEM((TM, D_MODEL), F32), pltpu.SemaphoreType.DMA(())]),
        out_shape=(jax.ShapeDtypeStruct((x1.shape[0] - TM, D_MODEL), F32),
                   jax.ShapeDtypeStruct((TM, D_MODEL), F32)),
        compiler_params=_params("arbitrary"), name="combine",
    )(tables["run_off"], tables["run_len"], tables["long_run"], tables["n_gran"], tables["copy_src"],
      tables["copy_dst"], tables["copy_n"], x1, rank, wt, y, *small)


def _moe(x1, w, w_e_gate, w_e_up, w_e_down, layer):
    n_blocks = _n_row_blocks(x1.shape[0])
    wt, rank, cnt = _router(x1, w)
    tables = _dispatch_tables(cnt[:, :N_EXPERTS, 0], n_blocks)
    xs = _dispatch(x1, rank, tables, n_blocks)
    y = _experts(xs, tables, w_e_gate, w_e_up, w_e_down, layer, n_blocks)
    return _combine(x1, rank, wt, y, tables, w)


def _layer_weights(l, w_in, w_gk2, b_gk, gla_norm_g, w_pool, pool_scale, w_br_a, w_br_b, w_out, ln1_g, ln1_b,
                   w_router, b_router, w_sh_gate, w_sh_up, w_sh_down, ln2_g, ln2_b):
    pieces = {}
    pad_rank = LANES - GATE_RANK
    pieces["wgk2"] = jnp.pad(w_gk2[l], ((0, pad_rank), (0, 0))).astype(BF16)
    pieces["bgk"] = b_gk[l].reshape(1, KEY)
    pieces["gn"] = gla_norm_g[l].reshape(1, DV)
    pieces["wbra"] = w_br_a[l].astype(BF16)
    pieces["wpool"] = w_pool[l].astype(BF16)
    pieces["pscale"] = pool_scale[l].reshape(1, POOL_WIDTH)
    pieces["wbrb"] = w_br_b[l].astype(BF16)
    pieces["wout"] = w_out[l].astype(BF16)
    pieces["g1"] = ln1_g[l].reshape(1, D_MODEL)
    pieces["b1"] = ln1_b[l].reshape(1, D_MODEL)
    wrt = w_router[l].T
    wrh = wrt.astype(BF16)
    pieces["wrh"] = wrh
    pieces["wrl"] = (wrt - wrh.astype(F32)).astype(BF16)
    pieces["br"] = b_router[l].reshape(N_EXPERTS, 1)
    pieces["wsgu"] = jnp.concatenate([w_sh_gate[l], w_sh_up[l]], axis=1).astype(BF16)
    pieces["wsd"] = w_sh_down[l].astype(BF16)
    pieces["g2"] = ln2_g[l].reshape(1, D_MODEL)
    pieces["b2"] = ln2_b[l].reshape(1, D_MODEL)
    return pieces


def kernel(x_prompt, x_sample, state_gla, cache_pool, w_in, w_gk2, b_gk, gla_norm_g, w_pool, pool_scale, w_br_a, w_br_b, w_out, ln1_g, ln1_b, w_router, b_router, w_e_gate, w_e_up, w_e_down, w_sh_gate, w_sh_up, w_sh_down, ln2_g, ln2_b):
    assert T_SAMPLE == TM, "the decode tokens must fill exactly one token tile"
    xm = x_prompt.reshape(T_PROMPT, D_MODEL)
    xt = x_sample.reshape(T_SAMPLE, D_MODEL)
    n_prompt_tiles = T_PROMPT // TM
    sample_block0 = T_PROMPT // DEC_SEQ
    sp, hp, ss, hs = [], [], [], []
    for l in range(DEPTH):
        w = _layer_weights(l, w_in, w_gk2, b_gk, gla_norm_g, w_pool, pool_scale, w_br_a, w_br_b, w_out, ln1_g,
                           ln1_b, w_router, b_router, w_sh_gate, w_sh_up, w_sh_down, ln2_g, ln2_b)
        q, k, v, r, lf, u, ga, gb = _inproj(xm, xt, w_in, w, l)

        og_p, st_p = _gla(q, k, v, lf, r, w["gn"], None, tile=TM, chunk=CHUNK, n_tiles=n_prompt_tiles, block0=0)
        og_s, st_s = _gla(q, k, v, lf, r, w["gn"], state_gla[l], tile=DEC_SEQ, chunk=DEC_SEQ, n_tiles=DEC_BATCH,
                          block0=sample_block0)

        u_tiles = u.reshape(n_prompt_tiles + 1, TM, POOL_WIDTH)
        halo_p = jnp.concatenate([jnp.zeros((1, HALO, POOL_WIDTH), F32),
                                  u_tiles[:n_prompt_tiles - 1, TM - HALO:, :]], axis=0)
        halo_s = jnp.concatenate([jnp.zeros((DEC_BATCH, HALO - POOL_HIST, POOL_WIDTH), F32), cache_pool[l]], axis=1)
        x1 = _mix(xm, xt, og_p, og_s, u, halo_p, halo_s, ga, gb, w)

        xm, xt = _moe(x1, w, w_e_gate, w_e_up, w_e_down, l)

        sp.append(st_p)
        ss.append(st_s)
        hp.append(u[T_PROMPT - POOL_HIST:T_PROMPT].reshape(1, POOL_HIST, POOL_WIDTH))
        hs.append(u[T_PROMPT:].reshape(DEC_BATCH, DEC_SEQ, POOL_WIDTH)[:, DEC_SEQ - POOL_HIST:, :])
    y_prompt = xm.reshape(1, SEQ, D_MODEL)
    y_sample = xt.reshape(DEC_BATCH, DEC_SEQ, D_MODEL)
    return (y_prompt, y_sample, jnp.stack(sp), jnp.stack(hp), jnp.stack(ss), jnp.stack(hs))
```

```python
import functools

import jax
import jax.numpy as jnp
from jax import lax
from jax.experimental import pallas as pl
from jax.experimental.pallas import tpu as pltpu

F32 = jnp.float32
BF16 = jnp.bfloat16
U32 = jnp.uint32
I32 = jnp.int32

D_MODEL = 1024
DEPTH = 2
SEQ = 16384
DEC_BATCH = 8
DEC_SEQ = 32
PAST_LEN = 4096
CHUNK = 64
HEADS = 4
DK = 128
DV = 256
KEY = HEADS * DK
VAL = HEADS * DV
GATE_RANK = 16
GATE_NORMALIZER = 16.0
POOL_WIDTH = 512
POOL_WINDOWS = (2, 4, 8, 16)
POOL_GROUP_DIM = 128
POOL_HIST = 15
N_EXPERTS = 64
GROUP_SIZE = 8
TOPK_GROUPS = 4
TOP_K = 8
D_EXPERT = 256
ROUTED_SCALE = 2.5
ALPHA = (2 * DEPTH) ** 0.25
LN_EPS = 1e-5
RMS_EPS = 1e-6
LOG2_E = 1.4426950408889634

LANES = 128
SUBLANES = 8
VMEM_LIMIT_BYTES = 56 * 1024 * 1024

T_PROMPT = SEQ
T_SAMPLE = DEC_BATCH * DEC_SEQ
T_ALL = T_PROMPT + T_SAMPLE
TM = 256
HALO = 16
CHUNK_UNROLL = 4
NEG_INF = float("-inf")

HALF = D_MODEL // 2
GRANULE = SUBLANES
ROW_BLOCK = 512
COPY_GROUP = 8
COPY_ROWS = (32, 16, 8)
RING = 3
RUN_CHUNK = 64
L_MAX = 2560
SEL_CHUNK = 512
MIN_TILE_ROWS = TM * TOP_K
COPY_CAP = (L_MAX // COPY_ROWS[0], N_EXPERTS, N_EXPERTS)
HI_MASK = 0xFFFF0000


def _dot(a, b):
    return jnp.dot(a, b, preferred_element_type=F32)


def _dot_nt(a, b):
    return lax.dot_general(a, b, (((1,), (1,)), ((), ())), preferred_element_type=F32)


def _dot_tn(a, b):
    return lax.dot_general(a, b, (((0,), (0,)), ((), ())), preferred_element_type=F32)


def _sigmoid(x):
    return 0.5 * jnp.tanh(0.5 * x) + 0.5


def _silu(x):
    return x * _sigmoid(x)


def _layer_norm(x, g, b):
    mu = jnp.mean(x, axis=-1, keepdims=True)
    xc = x - mu
    var = jnp.mean(xc * xc, axis=-1, keepdims=True)
    return xc * lax.rsqrt(var + LN_EPS) * g + b


def _pack_halves(v):
    bits = lax.bitcast_convert_type(v, U32)
    return (bits[:, HALF:] & jnp.uint32(HI_MASK)) | (bits[:, :HALF] >> 16)


def _unpack_halves(w):
    lo = lax.bitcast_convert_type(w << 16, F32).astype(BF16)
    hi = lax.bitcast_convert_type(w & jnp.uint32(HI_MASK), F32).astype(BF16)
    return lo, hi


def _params(*sem):
    return pltpu.CompilerParams(dimension_semantics=sem, vmem_limit_bytes=VMEM_LIMIT_BYTES)


def _const_spec(shape):
    nd = len(shape)
    return pl.BlockSpec(shape, lambda *_: (0,) * nd)


def _is_tail_tile():
    return pl.program_id(0) == pl.num_programs(0) - 1


def _main_tail_specs(n_tiles, width):
    return [pl.BlockSpec((TM, width), lambda i, *_: (jnp.minimum(i, n_tiles - 2), 0)),
            pl.BlockSpec((TM, width), lambda i, *_: (0, 0))]


IN_PIECES = ((("q", KEY), ("k", KEY), ("v", VAL), ("r", VAL)),
             (("gk", LANES), ("u", POOL_WIDTH), ("ga", D_MODEL), ("gb", D_MODEL)))
IN_COLS = {}
for _part, _pieces in enumerate(IN_PIECES):
    _col = 0
    for _name, _width in _pieces:
        IN_COLS[_name] = (_part, slice(_col, _col + _width))
        _col += _width
IN_SPLIT = sum(width for _, width in IN_PIECES[0])
IN_REST = sum(width for _, width in IN_PIECES[1][1:])
N_PROJ = IN_SPLIT + GATE_RANK + IN_REST
W_CHUNK = 256


def _inproj_kernel(xm_ref, xt_ref, w_hbm, wgk2, bgk, q_o, k_o, v_o, r_o, lf_o, u_o, ga_o, gb_o,
                   stage, wa_scr, wb_scr, sems, *, layer):
    @pl.when(pl.program_id(0) == 0)
    def _():
        n_chunks = D_MODEL // W_CHUNK

        def copy(c):
            rows = pl.ds(c * W_CHUNK, W_CHUNK)
            return pltpu.make_async_copy(w_hbm.at[layer, rows, :], stage.at[c % 2], sems.at[c % 2])

        copy(0).start()
        lane = lax.broadcasted_iota(I32, (W_CHUNK, LANES), 1)
        for c in range(n_chunks):
            if c + 1 < n_chunks:
                copy(c + 1).start()
            copy(c).wait()
            blk = stage[c % 2]
            rows = slice(c * W_CHUNK, (c + 1) * W_CHUNK)
            wa_scr[rows, :] = blk[:, :IN_SPLIT].astype(BF16)
            gate = jnp.where(lane < GATE_RANK, blk[:, IN_SPLIT:IN_SPLIT + LANES], 0.0)
            wb_scr[rows, :LANES] = gate.astype(BF16)
            wb_scr[rows, LANES:] = blk[:, IN_SPLIT + GATE_RANK:].astype(BF16)

    xb = jnp.where(_is_tail_tile(), xt_ref[...], xm_ref[...]).astype(BF16)
    proj = lambda name: _dot(xb, (wa_scr, wb_scr)[IN_COLS[name][0]][:, IN_COLS[name][1]])
    q_o[...] = proj("q") * (DK ** -0.5)
    k_o[...] = proj("k")
    v_o[...] = proj("v").astype(BF16)
    r_o[...] = proj("r")
    gk = proj("gk").astype(BF16)
    z = _dot(gk, wgk2[...]) + bgk[...]
    log_sig = jnp.minimum(z, 0.0) - jnp.log1p(jnp.exp(-jnp.abs(z)))
    lf_o[...] = log_sig * (1.0 / GATE_NORMALIZER)
    u_o[...] = proj("u")
    ga_o[...] = proj("ga")
    gb_o[...] = proj("gb")


def _inproj(xm, xt, w_in, w, layer):
    n = xm.shape[0] // TM + 1
    rows = n * TM
    row = lambda width: pl.BlockSpec((TM, width), lambda i: (i, 0))
    out_shapes = (
        jax.ShapeDtypeStruct((rows, KEY), F32), jax.ShapeDtypeStruct((rows, KEY), F32),
        jax.ShapeDtypeStruct((rows, VAL), BF16), jax.ShapeDtypeStruct((rows, VAL), F32),
        jax.ShapeDtypeStruct((rows, KEY), F32), jax.ShapeDtypeStruct((rows, POOL_WIDTH), F32),
        jax.ShapeDtypeStruct((rows, D_MODEL), F32), jax.ShapeDtypeStruct((rows, D_MODEL), F32))
    small = (w["wgk2"], w["bgk"])
    return pl.pallas_call(
        functools.partial(_inproj_kernel, layer=layer), grid=(n,),
        in_specs=_main_tail_specs(n, D_MODEL) + [pl.BlockSpec(memory_space=pltpu.HBM)]
                 + [_const_spec(a.shape) for a in small],
        out_specs=[row(KEY), row(KEY), row(VAL), row(VAL), row(KEY), row(POOL_WIDTH), row(D_MODEL), row(D_MODEL)],
        out_shape=out_shapes,
        scratch_shapes=[pltpu.VMEM((2, W_CHUNK, N_PROJ), F32), pltpu.VMEM((D_MODEL, IN_SPLIT), BF16),
                        pltpu.VMEM((D_MODEL, LANES + IN_REST), BF16), pltpu.SemaphoreType.DMA((2,))],
        compiler_params=_params("arbitrary"), name="inproj",
    )(xm, xt, w_in, *small)


def _gla_kernel(*refs, chunk, n_chunks, has_init):
    if has_init:
        q_ref, k_ref, v_ref, lf_ref, r_ref, gn_ref, s0_ref, o_ref, s_out_ref, st_ref = refs
        for h in range(HEADS):
            st_ref[h] = s0_ref[0, h].T
    else:
        q_ref, k_ref, v_ref, lf_ref, r_ref, gn_ref, o_ref, s_out_ref, st_ref = refs

        @pl.when(pl.program_id(0) == 0)
        def _():
            st_ref[...] = jnp.zeros_like(st_ref)

    row = lax.broadcasted_iota(I32, (chunk, KEY), 0)
    ta = lax.broadcasted_iota(I32, (chunk, chunk), 0)
    sa = lax.broadcasted_iota(I32, (chunk, chunk), 1)
    gn = gn_ref[...]
    halves = [1 << i for i in range(chunk.bit_length() - 1)]
    diag = ta == sa
    pairs = [((ta ^ sa) < 2 * half) & ((ta & half) != 0) & ((sa & half) == 0) for half in halves]

    def chunk_body(c, carry):
        rows = pl.ds(pl.multiple_of(c * chunk, chunk), chunk)
        lf = lf_ref[rows, :] * LOG2_E
        q = q_ref[rows, :]
        k = k_ref[rows, :]
        qb = q.astype(BF16)
        kb = k.astype(BF16)
        att = []
        for h in range(HEADS):
            hs = slice(h * DK, (h + 1) * DK)
            att.append(jnp.where(diag, _dot_nt(qb[:, hs], kb[:, hs]), 0.0))
        seg, tot = lf, lf
        for half, pair in zip(halves, pairs):
            qs = (q * jnp.exp2(seg)).astype(BF16)
            ks = (k * jnp.exp2(tot - seg)).astype(BF16)
            for h in range(HEADS):
                hs = slice(h * DK, (h + 1) * DK)
                att[h] = jnp.where(pair, _dot_nt(qs[:, hs], ks[:, hs]), att[h])
            if half < SUBLANES:
                upper = (row & half) != 0
                below = pltpu.roll(tot, half, 0)
                above = pltpu.roll(tot, chunk - half, 0)
                seg = seg + jnp.where(upper, below, 0.0)
                tot = tot + jnp.where(upper, below, above)
            else:
                step, n_tiles = half // SUBLANES, chunk // SUBLANES
                seg_t = [seg[i * SUBLANES:(i + 1) * SUBLANES, :] for i in range(n_tiles)]
                tot_t = [tot[i * SUBLANES:(i + 1) * SUBLANES, :] for i in range(n_tiles)]
                seg = jnp.concatenate([seg_t[i] + tot_t[i - step] if i & step else seg_t[i] for i in range(n_tiles)], axis=0)
                tot = jnp.concatenate([tot_t[i] + tot_t[i ^ step] for i in range(n_tiles)], axis=0)
        q_in = (q * jnp.exp2(seg)).astype(BF16)
        k_out = (k * jnp.exp2(tot - seg)).astype(BF16)
        decay = jnp.exp2(tot[0:1, :])
        for h in range(HEADS):
            hs = slice(h * DK, (h + 1) * DK)
            vs = slice(h * DV, (h + 1) * DV)
            state = st_ref[h]
            vh = v_ref[rows, vs]
            o = _dot_nt(q_in[:, hs], state.astype(BF16)) + _dot(att[h].astype(BF16), vh)
            st_ref[h] = state * decay[:, hs] + _dot_tn(vh, k_out[:, hs])
            ms = jnp.mean(o * o, axis=-1, keepdims=True)
            o = o * lax.rsqrt(ms + RMS_EPS) * gn
            o_ref[rows, vs] = (o * _silu(r_ref[rows, vs])).astype(BF16)
        return carry

    lax.fori_loop(0, n_chunks, chunk_body, 0, unroll=min(n_chunks, CHUNK_UNROLL))

    def write_state():
        for h in range(HEADS):
            s_out_ref[0, h] = st_ref[h].T

    if has_init:
        write_state()
    else:
        pl.when(pl.program_id(0) == pl.num_programs(0) - 1)(write_state)


def _gla(q, k, v, lf, r, gn, s0, *, tile, chunk, n_tiles, block0):
    has_init = s0 is not None
    row = lambda width: pl.BlockSpec((tile, width), lambda i: (block0 + i, 0))
    st_spec = pl.BlockSpec((1, HEADS, DK, DV), (lambda i: (i, 0, 0, 0)) if has_init else (lambda i: (0, 0, 0, 0)))
    n_states = n_tiles if has_init else 1
    in_specs = [row(KEY), row(KEY), row(VAL), row(KEY), row(VAL), _const_spec(gn.shape)]
    args = [q, k, v, lf, r, gn]
    if has_init:
        in_specs.append(st_spec)
        args.append(s0)
    return pl.pallas_call(
        functools.partial(_gla_kernel, chunk=chunk, n_chunks=tile // chunk, has_init=has_init),
        grid=(n_tiles,), in_specs=in_specs,
        out_specs=[pl.BlockSpec((tile, VAL), lambda i: (i, 0)), st_spec],
        out_shape=(jax.ShapeDtypeStruct((n_tiles * tile, VAL), BF16),
                   jax.ShapeDtypeStruct((n_states, HEADS, DK, DV), F32)),
        scratch_shapes=[pltpu.VMEM((HEADS, DV, DK), F32)],
        compiler_params=_params("arbitrary"), name="gla_init" if has_init else "gla",
    )(*args)


def _window_sums(ext, window):
    shift = 1
    while shift < window:
        ext = ext + pltpu.roll(ext, shift, 0)
        shift *= 2
    return ext


def _mix_kernel(xm_ref, xt_ref, ogm_ref, ogt_ref, u_ref, halo_m_ref, halo_t_ref, ga_ref, gb_ref,
                wbra, wpool, pscale, wbrb, wout, g1, b1, x1_ref):
    tail = _is_tail_tile()
    ya = _dot(jnp.where(tail, ogt_ref[...], ogm_ref[...]), wbra[...])
    u = u_ref[...]
    ext_m = jnp.concatenate([halo_m_ref[0], u], axis=0)
    seg = HALO + DEC_SEQ
    pieces = []
    for b in range(DEC_BATCH):
        pieces += [halo_t_ref[b], u[b * DEC_SEQ:(b + 1) * DEC_SEQ, :]]
    ext_t = jnp.concatenate(pieces, axis=0)
    rowi = lax.broadcasted_iota(I32, (TM, 1), 0)
    pos = jnp.where(tail, PAST_LEN + (rowi & (DEC_SEQ - 1)), pl.program_id(0) * TM + rowi)
    parts = []
    for g, window in enumerate(POOL_WINDOWS):
        cols = slice(g * POOL_GROUP_DIM, (g + 1) * POOL_GROUP_DIM)
        win_m = _window_sums(ext_m[:, cols], window)[HALO:, :]
        win_all = _window_sums(ext_t[:, cols], window)
        win_t = jnp.concatenate([win_all[b * seg + HALO:(b + 1) * seg, :] for b in range(DEC_BATCH)], axis=0)
        cnt = jnp.minimum(window, pos + 1).astype(F32)
        d = jnp.where(tail, win_t, win_m) / cnt - u[:, cols]
        parts.append(_dot(d.astype(BF16), wpool[g]))
    yb_in = jnp.concatenate(parts, axis=1) * pscale[...]
    yb = _dot(yb_in.astype(BF16), wbrb[...])
    mixed = _sigmoid(ga_ref[...]) * ya + _sigmoid(gb_ref[...]) * yb
    mix = _dot(mixed.astype(BF16), wout[...])
    x = jnp.where(tail, xt_ref[...], xm_ref[...])
    x1_ref[...] = _layer_norm(ALPHA * x + mix, g1[...], b1[...])


def _mix(xm, xt, og_m, og_t, u, halo_m, halo_t, ga, gb, w):
    n = xm.shape[0] // TM + 1
    row = lambda width: pl.BlockSpec((TM, width), lambda i: (i, 0))
    weights = (w["wbra"], w["wpool"], w["pscale"], w["wbrb"], w["wout"], w["g1"], w["b1"])
    return pl.pallas_call(
        _mix_kernel, grid=(n,),
        in_specs=_main_tail_specs(n, D_MODEL) + _main_tail_specs(n, VAL) + [
            row(POOL_WIDTH), pl.BlockSpec((1, HALO, POOL_WIDTH), lambda i: (jnp.minimum(i, n - 2), 0, 0)),
            _const_spec(halo_t.shape), row(D_MODEL), row(D_MODEL)] + [_const_spec(a.shape) for a in weights],
        out_specs=row(D_MODEL),
        out_shape=jax.ShapeDtypeStruct((n * TM, D_MODEL), F32),
        compiler_params=_params("arbitrary"), name="mix",
    )(xm, xt, og_m, og_t, u, halo_m, halo_t, ga, gb, *weights)


def _router_kernel(x1_ref, wrh_ref, wrl_ref, br_ref, wt_ref, rank_ref, cnt_ref, *, tile):
    x = x1_ref[...]
    xh = x.astype(BF16)
    xl = (x - xh.astype(F32)).astype(BF16)
    logits = _dot_nt(wrh_ref[...], xh) + _dot_nt(wrl_ref[...], xh) + _dot_nt(wrh_ref[...], xl)
    scores = _sigmoid(logits)
    rows_pad = N_EXPERTS
    n_grp = rows_pad // GROUP_SIZE
    biased = scores + br_ref[...]
    shape3 = (n_grp, GROUP_SIZE, tile)
    b3 = biased.reshape(shape3)
    s3 = scores.reshape(shape3)
    sub = lax.broadcasted_iota(I32, shape3, 1)
    gid = lax.broadcasted_iota(I32, shape3, 0)
    eid = gid * GROUP_SIZE + sub
    m1 = jnp.max(b3, axis=1, keepdims=True)
    i1 = jnp.min(jnp.where(b3 == m1, sub, GROUP_SIZE), axis=1, keepdims=True)
    m2 = jnp.max(jnp.where(sub == i1, NEG_INF, b3), axis=1, keepdims=True)
    gscore = m1 + m2
    gid1 = lax.broadcasted_iota(I32, (n_grp, 1, tile), 0)
    gsel = jnp.zeros((n_grp, 1, tile), jnp.bool_)
    for _ in range(TOPK_GROUPS):
        gm = jnp.max(gscore, axis=0, keepdims=True)
        gi = jnp.min(jnp.where(gscore == gm, gid1, n_grp), axis=0, keepdims=True)
        pick = gid1 == gi
        gsel = gsel | pick
        gscore = jnp.where(pick, NEG_INF, gscore)
    masked = jnp.where(gsel, b3, NEG_INF)
    wsel = jnp.zeros(shape3, F32)
    chosen = jnp.zeros(shape3, jnp.bool_)
    for _ in range(TOP_K):
        m = jnp.max(jnp.max(masked, axis=1, keepdims=True), axis=0, keepdims=True)
        idx = jnp.min(jnp.min(jnp.where(masked == m, eid, rows_pad), axis=1, keepdims=True), axis=0, keepdims=True)
        pick = eid == idx
        wsel = jnp.where(pick, s3, wsel)
        chosen = chosen | pick
        masked = jnp.where(pick, NEG_INF, masked)
    wsum = jnp.sum(jnp.sum(wsel, axis=1, keepdims=True), axis=0, keepdims=True)
    wt_ref[...] = (wsel / wsum * ROUTED_SCALE).reshape(rows_pad, tile)
    sel = jnp.where(chosen, 1.0, 0.0).reshape(rows_pad, tile)
    before = lax.broadcasted_iota(I32, (tile, tile), 0) < lax.broadcasted_iota(I32, (tile, tile), 1)
    rank = _dot(sel.astype(BF16), jnp.where(before, 1.0, 0.0).astype(BF16))
    rank_ref[...] = jnp.where(sel > 0.0, rank, -1.0).astype(I32)
    cnt = jnp.sum(sel, axis=1, keepdims=True).astype(I32)
    cnt_ref[0] = jnp.broadcast_to(cnt, (rows_pad, LANES))


def _router(x1, w):
    n = x1.shape[0] // TM
    rows_pad = N_EXPERTS
    tok = pl.BlockSpec((rows_pad, TM), lambda i: (0, i))
    return pl.pallas_call(
        functools.partial(_router_kernel, tile=TM), grid=(n,),
        in_specs=[pl.BlockSpec((TM, D_MODEL), lambda i: (i, 0)), _const_spec(w["wrh"].shape),
                  _const_spec(w["wrl"].shape), _const_spec(w["br"].shape)],
        out_specs=[tok, tok, pl.BlockSpec((1, rows_pad, LANES), lambda i: (i, 0, 0))],
        out_shape=(jax.ShapeDtypeStruct((rows_pad, x1.shape[0]), F32),
                   jax.ShapeDtypeStruct((rows_pad, x1.shape[0]), I32),
                   jax.ShapeDtypeStruct((n, rows_pad, LANES), I32)),
        compiler_params=_params("arbitrary"), name="router",
    )(x1, w["wrh"], w["wrl"], w["br"])


def _n_row_blocks(n_tokens):
    n_tiles = n_tokens // TM
    worst_rows = n_tokens * TOP_K + n_tiles * N_EXPERTS * (GRANULE - 1) + N_EXPERTS * (ROW_BLOCK - 1)
    return -(-worst_rows // ROW_BLOCK)


def _steps(shift):
    return shift - jnp.concatenate([jnp.zeros_like(shift[..., :1]), shift[..., :-1]], axis=-1)


def _dispatch_tables(cnt, n_blocks):
    padded = (cnt + (GRANULE - 1)) // GRANULE * GRANULE
    run_end = jnp.cumsum(padded, axis=1)
    run_off = run_end - padded
    n_gran = run_end[:, -1] // GRANULE
    rows_e = jnp.sum(padded, axis=0)
    region = (rows_e + (ROW_BLOCK - 1)) // ROW_BLOCK * ROW_BLOCK
    region_end = jnp.cumsum(region)
    region_start = region_end - region
    run_base = region_start[None, :] + jnp.cumsum(padded, axis=0) - padded

    big = padded // COPY_ROWS[0]
    big_first = jnp.cumsum(big, axis=1) - big
    k_big = jnp.arange(COPY_CAP[0], dtype=I32)
    owned = k_big[None, :, None] >= big_first[:, None, :]
    big_list = lambda row0: COPY_ROWS[0] * k_big[None, :] + jnp.sum(
        jnp.where(owned, _steps(row0 - COPY_ROWS[0] * big_first)[:, None, :], 0), axis=2)
    lists_src, lists_dst, counts = [big_list(run_off)], [big_list(run_base)], [jnp.sum(big, axis=1)]
    done = big * COPY_ROWS[0]
    for rows, cap in zip(COPY_ROWS[1:], COPY_CAP[1:]):
        has = (padded // rows) % 2
        pos = jnp.cumsum(has, axis=1) - has
        hit = (has[:, None, :] != 0) & (pos[:, None, :] == jnp.arange(cap, dtype=I32)[None, :, None])
        lists_src.append(jnp.sum(jnp.where(hit, (run_off + done)[:, None, :], 0), axis=2))
        lists_dst.append(jnp.sum(jnp.where(hit, (run_base + done)[:, None, :], 0), axis=2))
        counts.append(jnp.sum(has, axis=1))
        done = done + has * rows
    copy_src = jnp.concatenate(lists_src, axis=1)
    copy_dst = jnp.concatenate(lists_dst, axis=1)
    copy_n = jnp.stack(counts, axis=1)

    n_used = region_end[-1] // ROW_BLOCK
    blocks = jnp.arange(n_blocks, dtype=I32)
    block_expert = jnp.minimum(jnp.sum(blocks[:, None] >= (region_end // ROW_BLOCK)[None, :], axis=1), N_EXPERTS - 1)
    experts = jnp.arange(N_EXPERTS, dtype=I32)
    present = (region > 0).astype(I32)
    ordinal = jnp.cumsum(present) - present
    seq_expert = jnp.sum(jnp.where((present[None, :] != 0) & (ordinal[None, :] == experts[:, None]),
                                   experts[None, :], 0), axis=1)
    block_ord = jnp.sum(jnp.where(block_expert[:, None] == experts[None, :], ordinal[None, :], 0), axis=1)
    tail = (region - rows_e) // GRANULE
    tail_end = jnp.cumsum(tail)
    tail_first = tail_end - tail
    slot = jnp.arange(N_EXPERTS * (ROW_BLOCK // GRANULE - 1), dtype=I32)
    t_step = _steps((region_start + rows_e) // GRANULE - tail_first)
    tail_dst = slot + jnp.sum(jnp.where(slot[:, None] >= tail_first[None, :], t_step[None, :], 0), axis=1)
    as_i32 = lambda a: a.astype(I32).reshape(-1)
    long_run = jnp.any(padded > RUN_CHUNK, axis=1)
    return dict(run_off=as_i32(run_off), run_len=as_i32(padded), long_run=as_i32(long_run),
                n_gran=as_i32(n_gran), copy_src=as_i32(copy_src), copy_dst=as_i32(copy_dst), copy_n=as_i32(copy_n),
                n_used=as_i32(n_used), block_expert=as_i32(block_expert), block_ord=as_i32(block_ord),
                seq_expert=as_i32(seq_expert), n_seq=as_i32(jnp.sum(present)), n_tail=as_i32(tail_end[-1]),
                tail_dst=as_i32(tail_dst * GRANULE))


def _row_copy(src, src_row, dst, dst_row, sem, rows):
    s = pl.multiple_of(src_row, GRANULE)
    d = pl.multiple_of(dst_row, GRANULE)
    return pltpu.make_async_copy(src.at[pl.ds(s, rows), :], dst.at[pl.ds(d, rows), :], sem)


def _grouped_loop(n, body):
    n_groups = n // COPY_GROUP

    def group(j, carry):
        for i in range(COPY_GROUP):
            body(j * COPY_GROUP + i)
        return carry

    def single(i, carry):
        body(i)
        return carry

    lax.fori_loop(0, n_groups, group, 0)
    lax.fori_loop(n_groups * COPY_GROUP, n, single, 0)


def _start_copies(n, make_copy):
    _grouped_loop(n, lambda i: make_copy(i).start())


def _wait_copies(n, src, dst, sem, rows):
    n_groups = n // COPY_GROUP

    def group(j, carry):
        _row_copy(src, 0, dst, 0, sem, COPY_GROUP * rows).wait()
        return carry

    def single(i, carry):
        _row_copy(src, 0, dst, 0, sem, rows).wait()
        return carry

    lax.fori_loop(0, n_groups, group, 0)
    lax.fori_loop(n_groups * COPY_GROUP, n, single, 0)


def _start_tile_copies(tile, copy_n, make_copy):
    for c, rows in enumerate(COPY_ROWS):
        base = tile * sum(COPY_CAP) + sum(COPY_CAP[:c])
        _start_copies(copy_n[tile * len(COPY_ROWS) + c], lambda i, base=base, rows=rows: make_copy(base + i, rows))


def _wait_tile_copies(tile, copy_n, src, dst, sem):
    for c, rows in enumerate(COPY_ROWS):
        _wait_copies(copy_n[tile * len(COPY_ROWS) + c], src, dst, sem, rows)


def _build_selection(sel_scr, rank, values, run_off_ref, run_len_ref, long_run_ref, tile_idx):
    slot = lax.broadcasted_iota(I32, (RUN_CHUNK, TM), 0)

    def write_runs(all_chunks):
        for e in range(N_EXPERTS):
            off = run_off_ref[tile_idx * N_EXPERTS + e]
            rank_e = rank[e:e + 1, :]
            val_e = 1.0 if values is None else values[e:e + 1, :]

            def chunk(c, carry, off=off, rank_e=rank_e, val_e=val_e):
                hit = rank_e == slot + c * RUN_CHUNK
                rows = pl.ds(pl.multiple_of(off + c * RUN_CHUNK, GRANULE), RUN_CHUNK)
                sel_scr[rows, :] = jnp.where(hit, val_e, 0.0)
                return carry

            chunk(0, 0)
            if all_chunks:
                n_chunks = (run_len_ref[tile_idx * N_EXPERTS + e] + (RUN_CHUNK - 1)) // RUN_CHUNK
                lax.fori_loop(1, n_chunks, chunk, 0)

    write_runs(False)

    @pl.when(long_run_ref[tile_idx] != 0)
    def _():
        write_runs(True)


def _dispatch_kernel(run_off, run_len, long_run, n_gran, copy_src, copy_dst, copy_n, n_tail, tail_dst, n_used,
                     x1_ref, rank_ref, xs_hbm, sel_scr, buf_scr, zero_scr, sem):
    t = pl.program_id(0)

    @pl.when(t == 0)
    def _():
        sel_scr[...] = jnp.zeros_like(sel_scr)
        buf_scr[...] = jnp.zeros_like(buf_scr)
        zero_scr[...] = jnp.zeros_like(zero_scr)

    _build_selection(sel_scr, rank_ref[...], None, run_off, run_len, long_run, t)
    xb = x1_ref[...].astype(BF16)
    n_g = n_gran[t]
    buf = buf_scr.at[t % 2]
    def order_rows(c):
        rows = slice(c * SEL_CHUNK, (c + 1) * SEL_CHUNK)
        buf[rows, :] = _pack_halves(_dot(sel_scr[rows, :].astype(BF16), xb))

    for c in range(L_MAX // SEL_CHUNK):
        if (c + 1) * SEL_CHUNK <= MIN_TILE_ROWS:
            order_rows(c)
        else:
            pl.when(c * SEL_CHUNK < n_g * GRANULE)(functools.partial(order_rows, c))

    @pl.when(t > 0)
    def _():
        _wait_tile_copies(jnp.maximum(t - 1, 0), copy_n, buf, xs_hbm, sem)

    _start_tile_copies(
        t, copy_n, lambda i, rows: _row_copy(buf, copy_src[i], xs_hbm, copy_dst[i], sem, rows))

    @pl.when(t == pl.num_programs(0) - 1)
    def _():
        _wait_tile_copies(t, copy_n, buf, xs_hbm, sem)
        _start_copies(n_tail[0], lambda i: _row_copy(zero_scr, 0, xs_hbm, tail_dst[i], sem, GRANULE))
        _wait_copies(n_tail[0], zero_scr, xs_hbm, sem, GRANULE)

        def spare_copy(b):
            rows = pl.ds(pl.multiple_of(b * ROW_BLOCK, ROW_BLOCK), ROW_BLOCK)
            return pltpu.make_async_copy(zero_scr, xs_hbm.at[rows, :], sem)

        def start_spare(b, carry):
            spare_copy(b).start()
            return carry

        def wait_spare(b, carry):
            spare_copy(b).wait()
            return carry

        n_blocks = xs_hbm.shape[0] // ROW_BLOCK
        lax.fori_loop(n_used[0], n_blocks, start_spare, 0)
        lax.fori_loop(n_used[0], n_blocks, wait_spare, 0)


def _dispatch(x1, rank, tables, n_blocks):
    n = x1.shape[0] // TM
    return pl.pallas_call(
        _dispatch_kernel,
        grid_spec=pltpu.PrefetchScalarGridSpec(
            num_scalar_prefetch=10, grid=(n,),
            in_specs=[pl.BlockSpec((TM, D_MODEL), lambda i, *_: (i, 0)),
                      pl.BlockSpec((N_EXPERTS, TM), lambda i, *_: (0, i))],
            out_specs=pl.BlockSpec(memory_space=pl.ANY),
            scratch_shapes=[pltpu.VMEM((L_MAX, TM), F32), pltpu.VMEM((2, L_MAX, HALF), U32),
                            pltpu.VMEM((ROW_BLOCK, HALF), U32), pltpu.SemaphoreType.DMA(())]),
        out_shape=jax.ShapeDtypeStruct((n_blocks * ROW_BLOCK, HALF), U32),
        compiler_params=_params("arbitrary"), name="dispatch",
    )(tables["run_off"], tables["run_len"], tables["long_run"], tables["n_gran"], tables["copy_src"],
      tables["copy_dst"], tables["copy_n"], tables["n_tail"],
      tables["tail_dst"], tables["n_used"], x1, rank)


def _experts_kernel(block_expert, block_ord, seq_expert, n_seq, n_used,
                    xs_hbm, wg_hbm, wu_hbm, wd_hbm, y_ref,
                    x_ring, wg_stage, wu_stage, wd_stage, wgu_scr, wd_scr, x_sems, w_sems, *, layer):
    b = pl.program_id(0)
    n_u = n_used[0]

    def fetch(block):
        slot = block % RING
        rows = pl.ds(pl.multiple_of(block * ROW_BLOCK, ROW_BLOCK), ROW_BLOCK)
        return pltpu.make_async_copy(xs_hbm.at[rows, :], x_ring.at[slot], x_sems.at[slot])

    def weight_copies(ordinal):
        e = seq_expert[ordinal]
        slot = ordinal % 2
        return [pltpu.make_async_copy(src.at[layer, e], stage.at[slot], w_sems.at[slot])
                for src, stage in ((wg_hbm, wg_stage), (wu_hbm, wu_stage), (wd_hbm, wd_stage))]

    @pl.when(b == 0)
    def _():
        for copy in weight_copies(0):
            copy.start()
        for first in range(RING - 1):
            @pl.when(first < n_u)
            def _(first=first):
                fetch(first).start()

    @pl.when(b + (RING - 1) < n_u)
    def _():
        fetch(b + (RING - 1)).start()

    @pl.when(b >= n_u)
    def _():
        y_ref[...] = jnp.zeros_like(y_ref)

    @pl.when(b < n_u)
    def _():
        fresh = (b == 0) | (block_expert[b] != block_expert[jnp.maximum(b - 1, 0)])

        @pl.when(fresh)
        def _():
            ordinal = block_ord[b]
            slot = ordinal % 2
            for copy in weight_copies(ordinal):
                copy.wait()
            wgu_scr[:, :D_EXPERT] = wg_stage[slot].astype(BF16)
            wgu_scr[:, D_EXPERT:] = wu_stage[slot].astype(BF16)
            wd_scr[...] = wd_stage[slot].astype(BF16)

            @pl.when(ordinal + 1 < n_seq[0])
            def _():
                for copy in weight_copies(ordinal + 1):
                    copy.start()

        fetch(b).wait()
        lo, hi = _unpack_halves(x_ring[b % RING])
        gu = _dot(lo, wgu_scr[:HALF, :]) + _dot(hi, wgu_scr[HALF:, :])
        hid = (_silu(gu[:, :D_EXPERT]) * gu[:, D_EXPERT:]).astype(BF16)
        y = _dot(hid, wd_scr[...])
        y_ref[...] = _pack_halves(y.astype(BF16).astype(F32))


def _experts(xs, tables, w_e_gate, w_e_up, w_e_down, layer, n_blocks):
    any_spec = pl.BlockSpec(memory_space=pl.ANY)
    return pl.pallas_call(
        functools.partial(_experts_kernel, layer=layer),
        grid_spec=pltpu.PrefetchScalarGridSpec(
            num_scalar_prefetch=5, grid=(n_blocks,),
            in_specs=[any_spec, any_spec, any_spec, any_spec],
            out_specs=pl.BlockSpec((ROW_BLOCK, HALF), lambda b, *_: (b, 0)),
            scratch_shapes=[pltpu.VMEM((RING, ROW_BLOCK, HALF), U32),
                            pltpu.VMEM((2, D_MODEL, D_EXPERT), F32), pltpu.VMEM((2, D_MODEL, D_EXPERT), F32),
                            pltpu.VMEM((2, D_EXPERT, D_MODEL), F32),
                            pltpu.VMEM((D_MODEL, 2 * D_EXPERT), BF16), pltpu.VMEM((D_EXPERT, D_MODEL), BF16),
                            pltpu.SemaphoreType.DMA((RING,)), pltpu.SemaphoreType.DMA((2,))]),
        out_shape=jax.ShapeDtypeStruct(xs.shape, U32),
        compiler_params=_params("arbitrary"), name="experts",
    )(tables["block_expert"], tables["block_ord"], tables["seq_expert"], tables["n_seq"], tables["n_used"],
      xs, w_e_gate, w_e_up, w_e_down)


def _combine_kernel(run_off, run_len, long_run, n_gran, copy_src, copy_dst, copy_n,
                    x1_ref, rank_ref, wt_ref, y_hbm, wsgu, wsd, g2, b2, out_m_ref, out_t_ref,
                    sel_scr, buf_scr, acc_scr, sem):
    t = pl.program_id(0)
    n_tiles = pl.num_programs(0)
    n_g = n_gran[t]

    def fetch(tile):
        dst = buf_scr.at[tile % 2]
        _start_tile_copies(
            tile, copy_n, lambda i, rows: _row_copy(y_hbm, copy_dst[i], dst, copy_src[i], sem, rows))

    @pl.when(t == 0)
    def _():
        buf_scr[...] = jnp.zeros_like(buf_scr)
        fetch(t)

    stale = pl.multiple_of(jnp.minimum(n_g * GRANULE, L_MAX - SEL_CHUNK), GRANULE)
    sel_scr[pl.ds(stale, SEL_CHUNK), :] = jnp.zeros((SEL_CHUNK, TM), F32)
    _build_selection(sel_scr, rank_ref[...], wt_ref[...], run_off, run_len, long_run, t)
    x1 = x1_ref[...]
    xb = x1.astype(BF16)
    gu = _dot(xb, wsgu[...])
    hs = (_silu(gu[:, :D_EXPERT]) * gu[:, D_EXPERT:]).astype(BF16)
    acc_scr[...] = _dot(hs, wsd[...])
    buf = buf_scr.at[t % 2]
    _wait_tile_copies(t, copy_n, y_hbm, buf, sem)

    @pl.when(t + 1 < n_tiles)
    def _():
        fetch(jnp.minimum(t + 1, n_tiles - 1))

    def sum_rows(c):
        rows = slice(c * SEL_CHUNK, (c + 1) * SEL_CHUNK)
        lo, hi = _unpack_halves(buf[rows, :])
        sel = sel_scr[rows, :].astype(BF16)
        acc_scr[:, :HALF] += _dot_tn(sel, lo)
        acc_scr[:, HALF:] += _dot_tn(sel, hi)

    for c in range(L_MAX // SEL_CHUNK):
        if (c + 1) * SEL_CHUNK <= MIN_TILE_ROWS:
            sum_rows(c)
        else:
            pl.when(c * SEL_CHUNK < n_g * GRANULE)(functools.partial(sum_rows, c))

    out = _layer_norm(ALPHA * x1 + acc_scr[...], g2[...], b2[...])

    @pl.when(t + 1 < n_tiles)
    def _():
        out_m_ref[...] = out

    @pl.when(t + 1 == n_tiles)
    def _():
        out_t_ref[...] = out


def _combine(x1, rank, wt, y, tables, w):
    n = x1.shape[0] // TM
    small = (w["wsgu"], w["wsd"], w["g2"], w["b2"])
    tok = pl.BlockSpec((N_EXPERTS, TM), lambda i, *_: (0, i))
    return pl.pallas_call(
        _combine_kernel,
        grid_spec=pltpu.PrefetchScalarGridSpec(
            num_scalar_prefetch=7, grid=(n,),
            in_specs=[pl.BlockSpec((TM, D_MODEL), lambda i, *_: (i, 0)), tok, tok,
                      pl.BlockSpec(memory_space=pl.ANY)]
                     + [pl.BlockSpec(a.shape, lambda i, *_, nd=a.ndim: (0,) * nd) for a in small],
            out_specs=_main_tail_specs(n, D_MODEL),
            scratch_shapes=[pltpu.VMEM((L_MAX, TM), F32), pltpu.VMEM((2, L_MAX, HALF), U32),
                            pltpu.VMEM((TM, D_MODEL), F32), pltpu.SemaphoreType.DMA(())]),
        out_shape=(jax.ShapeDtypeStruct((x1.shape[0] - TM, D_MODEL), F32),
                   jax.ShapeDtypeStruct((TM, D_MODEL), F32)),
        compiler_params=_params("arbitrary"), name="combine",
    )(tables["run_off"], tables["run_len"], tables["long_run"], tables["n_gran"], tables["copy_src"],
      tables["copy_dst"], tables["copy_n"], x1, rank, wt, y, *small)


def _moe(x1, w, w_e_gate, w_e_up, w_e_down, layer):
    n_blocks = _n_row_blocks(x1.shape[0])
    wt, rank, cnt = _router(x1, w)
    tables = _dispatch_tables(cnt[:, :N_EXPERTS, 0], n_blocks)
    xs = _dispatch(x1, rank, tables, n_blocks)
    y = _experts(xs, tables, w_e_gate, w_e_up, w_e_down, layer, n_blocks)
    return _combine(x1, rank, wt, y, tables, w)


def _layer_weights(l, w_in, w_gk2, b_gk, gla_norm_g, w_pool, pool_scale, w_br_a, w_br_b, w_out, ln1_g, ln1_b,
                   w_router, b_router, w_sh_gate, w_sh_up, w_sh_down, ln2_g, ln2_b):
    pieces = {}
    pad_rank = LANES - GATE_RANK
    pieces["wgk2"] = jnp.pad(w_gk2[l], ((0, pad_rank), (0, 0))).astype(BF16)
    pieces["bgk"] = b_gk[l].reshape(1, KEY)
    pieces["gn"] = gla_norm_g[l].reshape(1, DV)
    pieces["wbra"] = w_br_a[l].astype(BF16)
    pieces["wpool"] = w_pool[l].astype(BF16)
    pieces["pscale"] = pool_scale[l].reshape(1, POOL_WIDTH)
    pieces["wbrb"] = w_br_b[l].astype(BF16)
    pieces["wout"] = w_out[l].astype(BF16)
    pieces["g1"] = ln1_g[l].reshape(1, D_MODEL)
    pieces["b1"] = ln1_b[l].reshape(1, D_MODEL)
    wrt = w_router[l].T
    wrh = wrt.astype(BF16)
    pieces["wrh"] = wrh
    pieces["wrl"] = (wrt - wrh.astype(F32)).astype(BF16)
    pieces["br"] = b_router[l].reshape(N_EXPERTS, 1)
    pieces["wsgu"] = jnp.concatenate([w_sh_gate[l], w_sh_up[l]], axis=1).astype(BF16)
    pieces["wsd"] = w_sh_down[l].astype(BF16)
    pieces["g2"] = ln2_g[l].reshape(1, D_MODEL)
    pieces["b2"] = ln2_b[l].reshape(1, D_MODEL)
    return pieces


def kernel(x_prompt, x_sample, state_gla, cache_pool, w_in, w_gk2, b_gk, gla_norm_g, w_pool, pool_scale, w_br_a, w_br_b, w_out, ln1_g, ln1_b, w_router, b_router, w_e_gate, w_e_up, w_e_down, w_sh_gate, w_sh_up, w_sh_down, ln2_g, ln2_b):
    assert T_SAMPLE == TM, "the decode tokens must fill exactly one token tile"
    xm = x_prompt.reshape(T_PROMPT, D_MODEL)
    xt = x_sample.reshape(T_SAMPLE, D_MODEL)
    n_prompt_tiles = T_PROMPT // TM
    sample_block0 = T_PROMPT // DEC_SEQ
    sp, hp, ss, hs = [], [], [], []
    for l in range(DEPTH):
        w = _layer_weights(l, w_in, w_gk2, b_gk, gla_norm_g, w_pool, pool_scale, w_br_a, w_br_b, w_out, ln1_g,
                           ln1_b, w_router, b_router, w_sh_gate, w_sh_up, w_sh_down, ln2_g, ln2_b)
        q, k, v, r, lf, u, ga, gb = _inproj(xm, xt, w_in, w, l)

        og_p, st_p = _gla(q, k, v, lf, r, w["gn"], None, tile=TM, chunk=CHUNK, n_tiles=n_prompt_tiles, block0=0)
        og_s, st_s = _gla(q, k, v, lf, r, w["gn"], state_gla[l], tile=DEC_SEQ, chunk=DEC_SEQ, n_tiles=DEC_BATCH,
                          block0=sample_block0)

        u_tiles = u.reshape(n_prompt_tiles + 1, TM, POOL_WIDTH)
        halo_p = jnp.concatenate([jnp.zeros((1, HALO, POOL_WIDTH), F32),
                                  u_tiles[:n_prompt_tiles - 1, TM - HALO:, :]], axis=0)
        halo_s = jnp.concatenate([jnp.zeros((DEC_BATCH, HALO - POOL_HIST, POOL_WIDTH), F32), cache_pool[l]], axis=1)
        x1 = _mix(xm, xt, og_p, og_s, u, halo_p, halo_s, ga, gb, w)

        xm, xt = _moe(x1, w, w_e_gate, w_e_up, w_e_down, l)

        sp.append(st_p)
        ss.append(st_s)
        hp.append(u[T_PROMPT - POOL_HIST:T_PROMPT].reshape(1, POOL_HIST, POOL_WIDTH))
        hs.append(u[T_PROMPT:].reshape(DEC_BATCH, DEC_SEQ, POOL_WIDTH)[:, DEC_SEQ - POOL_HIST:, :])
    y_prompt = xm.reshape(1, SEQ, D_MODEL)
    y_sample = xt.reshape(DEC_BATCH, DEC_SEQ, D_MODEL)
    return (y_prompt, y_sample, jnp.stack(sp), jnp.stack(hp), jnp.stack(ss), jnp.stack(hs))
```

```python
import functools

import jax
import jax.numpy as jnp
from jax import lax
from jax.experimental import pallas as pl
from jax.experimental.pallas import tpu as pltpu

F32 = jnp.float32
BF16 = jnp.bfloat16
U32 = jnp.uint32
I32 = jnp.int32

D_MODEL = 1024
DEPTH = 2
SEQ = 16384
DEC_BATCH = 8
DEC_SEQ = 32
PAST_LEN = 4096
CHUNK = 64
HEADS = 4
DK = 128
DV = 256
KEY = HEADS * DK
VAL = HEADS * DV
GATE_RANK = 16
GATE_NORMALIZER = 16.0
POOL_WIDTH = 512
POOL_WINDOWS = (2, 4, 8, 16)
POOL_GROUP_DIM = 128
POOL_HIST = 15
N_EXPERTS = 64
GROUP_SIZE = 8
TOPK_GROUPS = 4
TOP_K = 8
D_EXPERT = 256
ROUTED_SCALE = 2.5
ALPHA = (2 * DEPTH) ** 0.25
LN_EPS = 1e-5
RMS_EPS = 1e-6
LOG2_E = 1.4426950408889634

LANES = 128
SUBLANES = 8
VMEM_LIMIT_BYTES = 56 * 1024 * 1024

T_PROMPT = SEQ
T_SAMPLE = DEC_BATCH * DEC_SEQ
T_ALL = T_PROMPT + T_SAMPLE
TM = 256
HALO = 16
CHUNK_UNROLL = 4
NEG_INF = float("-inf")

HALF = D_MODEL // 2
GRANULE = SUBLANES
ROW_BLOCK = 512
COPY_GROUP = 8
COPY_ROWS = (32, 16, 8)
RING = 3
RUN_CHUNK = 64
L_MAX = 2560
SEL_CHUNK = 512
MIN_TILE_ROWS = TM * TOP_K
COPY_CAP = (L_MAX // COPY_ROWS[0], N_EXPERTS, N_EXPERTS)
HI_MASK = 0xFFFF0000


def _dot(a, b):
    return jnp.dot(a, b, preferred_element_type=F32)


def _dot_nt(a, b):
    return lax.dot_general(a, b, (((1,), (1,)), ((), ())), preferred_element_type=F32)


def _dot_tn(a, b):
    return lax.dot_general(a, b, (((0,), (0,)), ((), ())), preferred_element_type=F32)


def _sigmoid(x):
    return 0.5 * jnp.tanh(0.5 * x) + 0.5


def _silu(x):
    return x * _sigmoid(x)


def _layer_norm(x, g, b):
    mu = jnp.mean(x, axis=-1, keepdims=True)
    xc = x - mu
    var = jnp.mean(xc * xc, axis=-1, keepdims=True)
    return xc * lax.rsqrt(var + LN_EPS) * g + b


def _pack_halves(v):
    bits = lax.bitcast_convert_type(v, U32)
    return (bits[:, HALF:] & jnp.uint32(HI_MASK)) | (bits[:, :HALF] >> 16)


def _unpack_halves(w):
    lo = lax.bitcast_convert_type(w << 16, F32).astype(BF16)
    hi = lax.bitcast_convert_type(w & jnp.uint32(HI_MASK), F32).astype(BF16)
    return lo, hi


def _params(*sem):
    return pltpu.CompilerParams(dimension_semantics=sem, vmem_limit_bytes=VMEM_LIMIT_BYTES)


def _const_spec(shape):
    nd = len(shape)
    return pl.BlockSpec(shape, lambda *_: (0,) * nd)


def _is_tail_tile():
    return pl.program_id(0) == pl.num_programs(0) - 1


def _main_tail_specs(n_tiles, width):
    return [pl.BlockSpec((TM, width), lambda i, *_: (jnp.minimum(i, n_tiles - 2), 0)),
            pl.BlockSpec((TM, width), lambda i, *_: (0, 0))]


IN_PIECES = ((("q", KEY), ("k", KEY), ("v", VAL), ("r", VAL)),
             (("gk", LANES), ("u", POOL_WIDTH), ("ga", D_MODEL), ("gb", D_MODEL)))
IN_COLS = {}
for _part, _pieces in enumerate(IN_PIECES):
    _col = 0
    for _name, _width in _pieces:
        IN_COLS[_name] = (_part, slice(_col, _col + _width))
        _col += _width
IN_SPLIT = sum(width for _, width in IN_PIECES[0])
IN_REST = sum(width for _, width in IN_PIECES[1][1:])
N_PROJ = IN_SPLIT + GATE_RANK + IN_REST
W_CHUNK = 256


def _inproj_kernel(xm_ref, xt_ref, w_hbm, wgk2, bgk, q_o, k_o, v_o, r_o, lf_o, u_o, ga_o, gb_o,
                   stage, wa_scr, wb_scr, sems, *, layer):
    @pl.when(pl.program_id(0) == 0)
    def _():
        n_chunks = D_MODEL // W_CHUNK

        def copy(c):
            rows = pl.ds(c * W_CHUNK, W_CHUNK)
            return pltpu.make_async_copy(w_hbm.at[layer, rows, :], stage.at[c % 2], sems.at[c % 2])

        copy(0).start()
        lane = lax.broadcasted_iota(I32, (W_CHUNK, LANES), 1)
        for c in range(n_chunks):
            if c + 1 < n_chunks:
                copy(c + 1).start()
            copy(c).wait()
            blk = stage[c % 2]
            rows = slice(c * W_CHUNK, (c + 1) * W_CHUNK)
            wa_scr[rows, :] = blk[:, :IN_SPLIT].astype(BF16)
            gate = jnp.where(lane < GATE_RANK, blk[:, IN_SPLIT:IN_SPLIT + LANES], 0.0)
            wb_scr[rows, :LANES] = gate.astype(BF16)
            wb_scr[rows, LANES:] = blk[:, IN_SPLIT + GATE_RANK:].astype(BF16)

    xb = jnp.where(_is_tail_tile(), xt_ref[...], xm_ref[...]).astype(BF16)
    proj = lambda name: _dot(xb, (wa_scr, wb_scr)[IN_COLS[name][0]][:, IN_COLS[name][1]])
    q_o[...] = proj("q") * (DK ** -0.5)
    k_o[...] = proj("k")
    v_o[...] = proj("v").astype(BF16)
    r_o[...] = proj("r")
    gk = proj("gk").astype(BF16)
    z = _dot(gk, wgk2[...]) + bgk[...]
    log_sig = jnp.minimum(z, 0.0) - jnp.log1p(jnp.exp(-jnp.abs(z)))
    lf_o[...] = log_sig * (1.0 / GATE_NORMALIZER)
    u_o[...] = proj("u")
    ga_o[...] = proj("ga")
    gb_o[...] = proj("gb")


def _inproj(xm, xt, w_in, w, layer):
    n = xm.shape[0] // TM + 1
    rows = n * TM
    row = lambda width: pl.BlockSpec((TM, width), lambda i: (i, 0))
    out_shapes = (
        jax.ShapeDtypeStruct((rows, KEY), F32), jax.ShapeDtypeStruct((rows, KEY), F32),
        jax.ShapeDtypeStruct((rows, VAL), BF16), jax.ShapeDtypeStruct((rows, VAL), F32),
        jax.ShapeDtypeStruct((rows, KEY), F32), jax.ShapeDtypeStruct((rows, POOL_WIDTH), F32),
        jax.ShapeDtypeStruct((rows, D_MODEL), F32), jax.ShapeDtypeStruct((rows, D_MODEL), F32))
    small = (w["wgk2"], w["bgk"])
    return pl.pallas_call(
        functools.partial(_inproj_kernel, layer=layer), grid=(n,),
        in_specs=_main_tail_specs(n, D_MODEL) + [pl.BlockSpec(memory_space=pltpu.HBM)]
                 + [_const_spec(a.shape) for a in small],
        out_specs=[row(KEY), row(KEY), row(VAL), row(VAL), row(KEY), row(POOL_WIDTH), row(D_MODEL), row(D_MODEL)],
        out_shape=out_shapes,
        scratch_shapes=[pltpu.VMEM((2, W_CHUNK, N_PROJ), F32), pltpu.VMEM((D_MODEL, IN_SPLIT), BF16),
                        pltpu.VMEM((D_MODEL, LANES + IN_REST), BF16), pltpu.SemaphoreType.DMA((2,))],
        compiler_params=_params("arbitrary"), name="inproj",
    )(xm, xt, w_in, *small)


def _gla_kernel(*refs, chunk, n_chunks, has_init):
    if has_init:
        q_ref, k_ref, v_ref, lf_ref, r_ref, gn_ref, s0_ref, o_ref, s_out_ref, st_ref = refs
        for h in range(HEADS):
            st_ref[h] = s0_ref[0, h].T
    else:
        q_ref, k_ref, v_ref, lf_ref, r_ref, gn_ref, o_ref, s_out_ref, st_ref = refs

        @pl.when(pl.program_id(0) == 0)
        def _():
            st_ref[...] = jnp.zeros_like(st_ref)

    row = lax.broadcasted_iota(I32, (chunk, KEY), 0)
    ta = lax.broadcasted_iota(I32, (chunk, chunk), 0)
    sa = lax.broadcasted_iota(I32, (chunk, chunk), 1)
    gn = gn_ref[...]
    halves = [1 << i for i in range(chunk.bit_length() - 1)]
    diag = ta == sa
    pairs = [((ta ^ sa) < 2 * half) & ((ta & half) != 0) & ((sa & half) == 0) for half in halves]

    def chunk_body(c, carry):
        rows = pl.ds(pl.multiple_of(c * chunk, chunk), chunk)
        lf = lf_ref[rows, :] * LOG2_E
        q = q_ref[rows, :]
        k = k_ref[rows, :]
        qb = q.astype(BF16)
        kb = k.astype(BF16)
        att = []
        for h in range(HEADS):
            hs = slice(h * DK, (h + 1) * DK)
            att.append(jnp.where(diag, _dot_nt(qb[:, hs], kb[:, hs]), 0.0))
        seg, tot = lf, lf
        for half, pair in zip(halves, pairs):
            qs = (q * jnp.exp2(seg)).astype(BF16)
            ks = (k * jnp.exp2(tot - seg)).astype(BF16)
            for h in range(HEADS):
                hs = slice(h * DK, (h + 1) * DK)
                att[h] = jnp.where(pair, _dot_nt(qs[:, hs], ks[:, hs]), att[h])
            if half < SUBLANES:
                upper = (row & half) != 0
                below = pltpu.roll(tot, half, 0)
                above = pltpu.roll(tot, chunk - half, 0)
                seg = seg + jnp.where(upper, below, 0.0)
                tot = tot + jnp.where(upper, below, above)
            else:
                step, n_tiles = half // SUBLANES, chunk // SUBLANES
                seg_t = [seg[i * SUBLANES:(i + 1) * SUBLANES, :] for i in range(n_tiles)]
                tot_t = [tot[i * SUBLANES:(i + 1) * SUBLANES, :] for i in range(n_tiles)]
                seg = jnp.concatenate([seg_t[i] + tot_t[i - step] if i & step else seg_t[i] for i in range(n_tiles)], axis=0)
                tot = jnp.concatenate([tot_t[i] + tot_t[i ^ step] for i in range(n_tiles)], axis=0)
        q_in = (q * jnp.exp2(seg)).astype(BF16)
        k_out = (k * jnp.exp2(tot - seg)).astype(BF16)
        decay = jnp.exp2(tot[0:1, :])
        for h in range(HEADS):
            hs = slice(h * DK, (h + 1) * DK)
            vs = slice(h * DV, (h + 1) * DV)
            state = st_ref[h]
            vh = v_ref[rows, vs]
            o = _dot_nt(q_in[:, hs], state.astype(BF16)) + _dot(att[h].astype(BF16), vh)
            st_ref[h] = state * decay[:, hs] + _dot_tn(vh, k_out[:, hs])
            ms = jnp.mean(o * o, axis=-1, keepdims=True)
            o = o * lax.rsqrt(ms + RMS_EPS) * gn
            o_ref[rows, vs] = (o * _silu(r_ref[rows, vs])).astype(BF16)
        return carry

    lax.fori_loop(0, n_chunks, chunk_body, 0, unroll=min(n_chunks, CHUNK_UNROLL))

    def write_state():
        for h in range(HEADS):
            s_out_ref[0, h] = st_ref[h].T

    if has_init:
        write_state()
    else:
        pl.when(pl.program_id(0) == pl.num_programs(0) - 1)(write_state)


def _gla(q, k, v, lf, r, gn, s0, *, tile, chunk, n_tiles, block0):
    has_init = s0 is not None
    row = lambda width: pl.BlockSpec((tile, width), lambda i: (block0 + i, 0))
    st_spec = pl.BlockSpec((1, HEADS, DK, DV), (lambda i: (i, 0, 0, 0)) if has_init else (lambda i: (0, 0, 0, 0)))
    n_states = n_tiles if has_init else 1
    in_specs = [row(KEY), row(KEY), row(VAL), row(KEY), row(VAL), _const_spec(gn.shape)]
    args = [q, k, v, lf, r, gn]
    if has_init:
        in_specs.append(st_spec)
        args.append(s0)
    return pl.pallas_call(
        functools.partial(_gla_kernel, chunk=chunk, n_chunks=tile // chunk, has_init=has_init),
        grid=(n_tiles,), in_specs=in_specs,
        out_specs=[pl.BlockSpec((tile, VAL), lambda i: (i, 0)), st_spec],
        out_shape=(jax.ShapeDtypeStruct((n_tiles * tile, VAL), BF16),
                   jax.ShapeDtypeStruct((n_states, HEADS, DK, DV), F32)),
        scratch_shapes=[pltpu.VMEM((HEADS, DV, DK), F32)],
        compiler_params=_params("arbitrary"), name="gla_init" if has_init else "gla",
    )(*args)


def _window_sums(ext, window):
    shift = 1
    while shift < window:
        ext = ext + pltpu.roll(ext, shift, 0)
        shift *= 2
    return ext


def _mix_kernel(xm_ref, xt_ref, ogm_ref, ogt_ref, u_ref, halo_m_ref, halo_t_ref, ga_ref, gb_ref,
                wbra, wpool, pscale, wbrb, wout, g1, b1, x1_ref):
    tail = _is_tail_tile()
    ya = _dot(jnp.where(tail, ogt_ref[...], ogm_ref[...]), wbra[...])
    u = u_ref[...]
    ext_m = jnp.concatenate([halo_m_ref[0], u], axis=0)
    seg = HALO + DEC_SEQ
    pieces = []
    for b in range(DEC_BATCH):
        pieces += [halo_t_ref[b], u[b * DEC_SEQ:(b + 1) * DEC_SEQ, :]]
    ext_t = jnp.concatenate(pieces, axis=0)
    rowi = lax.broadcasted_iota(I32, (TM, 1), 0)
    pos = jnp.where(tail, PAST_LEN + (rowi & (DEC_SEQ - 1)), pl.program_id(0) * TM + rowi)
    parts = []
    for g, window in enumerate(POOL_WINDOWS):
        cols = slice(g * POOL_GROUP_DIM, (g + 1) * POOL_GROUP_DIM)
        win_m = _window_sums(ext_m[:, cols], window)[HALO:, :]
        win_all = _window_sums(ext_t[:, cols], window)
        win_t = jnp.concatenate([win_all[b * seg + HALO:(b + 1) * seg, :] for b in range(DEC_BATCH)], axis=0)
        cnt = jnp.minimum(window, pos + 1).astype(F32)
        d = jnp.where(tail, win_t, win_m) / cnt - u[:, cols]
        parts.append(_dot(d.astype(BF16), wpool[g]))
    yb_in = jnp.concatenate(parts, axis=1) * pscale[...]
    yb = _dot(yb_in.astype(BF16), wbrb[...])
    mixed = _sigmoid(ga_ref[...]) * ya + _sigmoid(gb_ref[...]) * yb
    mix = _dot(mixed.astype(BF16), wout[...])
    x = jnp.where(tail, xt_ref[...], xm_ref[...])
    x1_ref[...] = _layer_norm(ALPHA * x + mix, g1[...], b1[...])


def _mix(xm, xt, og_m, og_t, u, halo_m, halo_t, ga, gb, w):
    n = xm.shape[0] // TM + 1
    row = lambda width: pl.BlockSpec((TM, width), lambda i: (i, 0))
    weights = (w["wbra"], w["wpool"], w["pscale"], w["wbrb"], w["wout"], w["g1"], w["b1"])
    return pl.pallas_call(
        _mix_kernel, grid=(n,),
        in_specs=_main_tail_specs(n, D_MODEL) + _main_tail_specs(n, VAL) + [
            row(POOL_WIDTH), pl.BlockSpec((1, HALO, POOL_WIDTH), lambda i: (jnp.minimum(i, n - 2), 0, 0)),
            _const_spec(halo_t.shape), row(D_MODEL), row(D_MODEL)] + [_const_spec(a.shape) for a in weights],
        out_specs=row(D_MODEL),
        out_shape=jax.ShapeDtypeStruct((n * TM, D_MODEL), F32),
        compiler_params=_params("arbitrary"), name="mix",
    )(xm, xt, og_m, og_t, u, halo_m, halo_t, ga, gb, *weights)


def _router_kernel(x1_ref, wrh_ref, wrl_ref, br_ref, wt_ref, rank_ref, cnt_ref, *, tile):
    x = x1_ref[...]
    xh = x.astype(BF16)
    xl = (x - xh.astype(F32)).astype(BF16)
    logits = _dot_nt(wrh_ref[...], xh) + _dot_nt(wrl_ref[...], xh) + _dot_nt(wrh_ref[...], xl)
    scores = _sigmoid(logits)
    rows_pad = N_EXPERTS
    n_grp = rows_pad // GROUP_SIZE
    biased = scores + br_ref[...]
    shape3 = (n_grp, GROUP_SIZE, tile)
    b3 = biased.reshape(shape3)
    s3 = scores.reshape(shape3)
    sub = lax.broadcasted_iota(I32, shape3, 1)
    gid = lax.broadcasted_iota(I32, shape3, 0)
    eid = gid * GROUP_SIZE + sub
    m1 = jnp.max(b3, axis=1, keepdims=True)
    i1 = jnp.min(jnp.where(b3 == m1, sub, GROUP_SIZE), axis=1, keepdims=True)
    m2 = jnp.max(jnp.where(sub == i1, NEG_INF, b3), axis=1, keepdims=True)
    gscore = m1 + m2
    gid1 = lax.broadcasted_iota(I32, (n_grp, 1, tile), 0)
    gsel = jnp.zeros((n_grp, 1, tile), jnp.bool_)
    for _ in range(TOPK_GROUPS):
        gm = jnp.max(gscore, axis=0, keepdims=True)
        gi = jnp.min(jnp.where(gscore == gm, gid1, n_grp), axis=0, keepdims=True)
        pick = gid1 == gi
        gsel = gsel | pick
        gscore = jnp.where(pick, NEG_INF, gscore)
    masked = jnp.where(gsel, b3, NEG_INF)
    wsel = jnp.zeros(shape3, F32)
    chosen = jnp.zeros(shape3, jnp.bool_)
    for _ in range(TOP_K):
        m = jnp.max(jnp.max(masked, axis=1, keepdims=True), axis=0, keepdims=True)
        idx = jnp.min(jnp.min(jnp.where(masked == m, eid, rows_pad), axis=1, keepdims=True), axis=0, keepdims=True)
        pick = eid == idx
        wsel = jnp.where(pick, s3, wsel)
        chosen = chosen | pick
        masked = jnp.where(pick, NEG_INF, masked)
    wsum = jnp.sum(jnp.sum(wsel, axis=1, keepdims=True), axis=0, keepdims=True)
    wt_ref[...] = (wsel / wsum * ROUTED_SCALE).reshape(rows_pad, tile)
    sel = jnp.where(chosen, 1.0, 0.0).reshape(rows_pad, tile)
    before = lax.broadcasted_iota(I32, (tile, tile), 0) < lax.broadcasted_iota(I32, (tile, tile), 1)
    rank = _dot(sel.astype(BF16), jnp.where(before, 1.0, 0.0).astype(BF16))
    rank_ref[...] = jnp.where(sel > 0.0, rank, -1.0).astype(I32)
    cnt = jnp.sum(sel, axis=1, keepdims=True).astype(I32)
    cnt_ref[0] = jnp.broadcast_to(cnt, (rows_pad, LANES))


def _router(x1, w):
    n = x1.shape[0] // TM
    rows_pad = N_EXPERTS
    tok = pl.BlockSpec((rows_pad, TM), lambda i: (0, i))
    return pl.pallas_call(
        functools.partial(_router_kernel, tile=TM), grid=(n,),
        in_specs=[pl.BlockSpec((TM, D_MODEL), lambda i: (i, 0)), _const_spec(w["wrh"].shape),
                  _const_spec(w["wrl"].shape), _const_spec(w["br"].shape)],
        out_specs=[tok, tok, pl.BlockSpec((1, rows_pad, LANES), lambda i: (i, 0, 0))],
        out_shape=(jax.ShapeDtypeStruct((rows_pad, x1.shape[0]), F32),
                   jax.ShapeDtypeStruct((rows_pad, x1.shape[0]), I32),
                   jax.ShapeDtypeStruct((n, rows_pad, LANES), I32)),
        compiler_params=_params("arbitrary"), name="router",
    )(x1, w["wrh"], w["wrl"], w["br"])


def _n_row_blocks(n_tokens):
    n_tiles = n_tokens // TM
    worst_rows = n_tokens * TOP_K + n_tiles * N_EXPERTS * (GRANULE - 1) + N_EXPERTS * (ROW_BLOCK - 1)
    return -(-worst_rows // ROW_BLOCK)


def _steps(shift):
    return shift - jnp.concatenate([jnp.zeros_like(shift[..., :1]), shift[..., :-1]], axis=-1)


def _dispatch_tables(cnt, n_blocks):
    padded = (cnt + (GRANULE - 1)) // GRANULE * GRANULE
    run_end = jnp.cumsum(padded, axis=1)
    run_off = run_end - padded
    rows_e = jnp.sum(padded, axis=0)
    region = (rows_e + (ROW_BLOCK - 1)) // ROW_BLOCK * ROW_BLOCK
    region_end = jnp.cumsum(region)
    region_start = region_end - region
    run_base = region_start[None, :] + jnp.cumsum(padded, axis=0) - padded

    big = padded // COPY_ROWS[0]
    big_first = jnp.cumsum(big, axis=1) - big
    k_big = jnp.arange(COPY_CAP[0], dtype=I32)
    owned = k_big[None, :, None] >= big_first[:, None, :]
    big_list = lambda row0: COPY_ROWS[0] * k_big[None, :] + jnp.sum(
        jnp.where(owned, _steps(row0 - COPY_ROWS[0] * big_first)[:, None, :], 0), axis=2)
    lists_src, lists_dst, counts = [big_list(run_off)], [big_list(run_base)], [jnp.sum(big, axis=1)]
    done = big * COPY_ROWS[0]
    for rows, cap in zip(COPY_ROWS[1:], COPY_CAP[1:]):
        has = (padded // rows) % 2
        pos = jnp.cumsum(has, axis=1) - has
        hit = (has[:, None, :] != 0) & (pos[:, None, :] == jnp.arange(cap, dtype=I32)[None, :, None])
        lists_src.append(jnp.sum(jnp.where(hit, (run_off + done)[:, None, :], 0), axis=2))
        lists_dst.append(jnp.sum(jnp.where(hit, (run_base + done)[:, None, :], 0), axis=2))
        counts.append(jnp.sum(has, axis=1))
        done = done + has * rows
    copy_src = jnp.concatenate(lists_src, axis=1)
    copy_dst = jnp.concatenate(lists_dst, axis=1)
    copy_n = jnp.stack(counts, axis=1)

    n_used = region_end[-1] // ROW_BLOCK
    blocks = jnp.arange(n_blocks, dtype=I32)
    block_expert = jnp.minimum(jnp.sum(blocks[:, None] >= (region_end // ROW_BLOCK)[None, :], axis=1), N_EXPERTS - 1)
    experts = jnp.arange(N_EXPERTS, dtype=I32)
    present = (region > 0).astype(I32)
    ordinal = jnp.cumsum(present) - present
    seq_expert = jnp.sum(jnp.where((present[None, :] != 0) & (ordinal[None, :] == experts[:, None]),
                                   experts[None, :], 0), axis=1)
    block_ord = jnp.sum(jnp.where(block_expert[:, None] == experts[None, :], ordinal[None, :], 0), axis=1)
    tail = (region - rows_e) // GRANULE
    tail_end = jnp.cumsum(tail)
    tail_first = tail_end - tail
    slot = jnp.arange(N_EXPERTS * (ROW_BLOCK // GRANULE - 1), dtype=I32)
    t_step = _steps((region_start + rows_e) // GRANULE - tail_first)
    tail_dst = slot + jnp.sum(jnp.where(slot[:, None] >= tail_first[None, :], t_step[None, :], 0), axis=1)
    as_i32 = lambda a: a.astype(I32).reshape(-1)
    long_run = jnp.any(padded > RUN_CHUNK, axis=1)
    return dict(run_off=as_i32(run_off), run_len=as_i32(padded), long_run=as_i32(long_run),
                copy_src=as_i32(copy_src), copy_dst=as_i32(copy_dst), copy_n=as_i32(copy_n),
                n_used=as_i32(n_used), block_expert=as_i32(block_expert), block_ord=as_i32(block_ord),
                seq_expert=as_i32(seq_expert), n_seq=as_i32(jnp.sum(present)), n_tail=as_i32(tail_end[-1]),
                tail_dst=as_i32(tail_dst * GRANULE))


def _row_copy(src, src_row, dst, dst_row, sem, rows):
    s = pl.multiple_of(src_row, GRANULE)
    d = pl.multiple_of(dst_row, GRANULE)
    return pltpu.make_async_copy(src.at[pl.ds(s, rows), :], dst.at[pl.ds(d, rows), :], sem)


def _grouped_loop(n, body):
    n_groups = n // COPY_GROUP

    def group(j, carry):
        for i in range(COPY_GROUP):
            body(j * COPY_GROUP + i)
        return carry

    def single(i, carry):
        body(i)
        return carry

    lax.fori_loop(0, n_groups, group, 0)
    lax.fori_loop(n_groups * COPY_GROUP, n, single, 0)


def _start_copies(n, make_copy):
    _grouped_loop(n, lambda i: make_copy(i).start())


def _wait_copies(n, src, dst, sem, rows):
    n_groups = n // COPY_GROUP

    def group(j, carry):
        _row_copy(src, 0, dst, 0, sem, COPY_GROUP * rows).wait()
        return carry

    def single(i, carry):
        _row_copy(src, 0, dst, 0, sem, rows).wait()
        return carry

    lax.fori_loop(0, n_groups, group, 0)
    lax.fori_loop(n_groups * COPY_GROUP, n, single, 0)


def _start_tile_copies(tile, copy_n, make_copy):
    for c, rows in enumerate(COPY_ROWS):
        base = tile * sum(COPY_CAP) + sum(COPY_CAP[:c])
        _start_copies(copy_n[tile * len(COPY_ROWS) + c], lambda i, base=base, rows=rows: make_copy(base + i, rows))


def _wait_tile_copies(tile, copy_n, src, dst, sem):
    for c, rows in enumerate(COPY_ROWS):
        _wait_copies(copy_n[tile * len(COPY_ROWS) + c], src, dst, sem, rows)


def _build_selection(sel_scr, rank, values, run_off_ref, run_len_ref, long_run_ref, tile_idx):
    slot = lax.broadcasted_iota(I32, (RUN_CHUNK, TM), 0)

    def write_runs(all_chunks):
        for e in range(N_EXPERTS):
            off = run_off_ref[tile_idx * N_EXPERTS + e]
            rank_e = rank[e:e + 1, :]
            val_e = 1.0 if values is None else values[e:e + 1, :]

            def chunk(c, carry, off=off, rank_e=rank_e, val_e=val_e):
                hit = rank_e == slot + c * RUN_CHUNK
                rows = pl.ds(pl.multiple_of(off + c * RUN_CHUNK, GRANULE), RUN_CHUNK)
                sel_scr[rows, :] = jnp.where(hit, val_e, 0.0)
                return carry

            chunk(0, 0)
            if all_chunks:
                n_chunks = (run_len_ref[tile_idx * N_EXPERTS + e] + (RUN_CHUNK - 1)) // RUN_CHUNK
                lax.fori_loop(1, n_chunks, chunk, 0)

    write_runs(False)

    @pl.when(long_run_ref[tile_idx] != 0)
    def _():
        write_runs(True)


def _dispatch_kernel(run_off, run_len, long_run, copy_src, copy_dst, copy_n, n_tail, tail_dst, n_used,
                     x1_ref, rank_ref, xs_hbm, sel_scr, buf_scr, zero_scr, sem):
    t = pl.program_id(0)

    @pl.when(t == 0)
    def _():
        sel_scr[...] = jnp.zeros_like(sel_scr)
        buf_scr[...] = jnp.zeros_like(buf_scr)
        zero_scr[...] = jnp.zeros_like(zero_scr)

    _build_selection(sel_scr, rank_ref[...], None, run_off, run_len, long_run, t)
    xb = x1_ref[...].astype(BF16)
    buf = buf_scr.at[t % 2]

    def order_rows(c):
        rows = slice(c * SEL_CHUNK, (c + 1) * SEL_CHUNK)
        buf[rows, :] = _pack_halves(_dot(sel_scr[rows, :].astype(BF16), xb))

    for c in range(L_MAX // SEL_CHUNK):
        order_rows(c)

    @pl.when(t > 0)
    def _():
        _wait_tile_copies(jnp.maximum(t - 1, 0), copy_n, buf, xs_hbm, sem)

    _start_tile_copies(
        t, copy_n, lambda i, rows: _row_copy(buf, copy_src[i], xs_hbm, copy_dst[i], sem, rows))

    @pl.when(t == pl.num_programs(0) - 1)
    def _():
        _wait_tile_copies(t, copy_n, buf, xs_hbm, sem)
        _start_copies(n_tail[0], lambda i: _row_copy(zero_scr, 0, xs_hbm, tail_dst[i], sem, GRANULE))
        _wait_copies(n_tail[0], zero_scr, xs_hbm, sem, GRANULE)

        def spare_copy(b):
            rows = pl.ds(pl.multiple_of(b * ROW_BLOCK, ROW_BLOCK), ROW_BLOCK)
            return pltpu.make_async_copy(zero_scr, xs_hbm.at[rows, :], sem)

        def start_spare(b, carry):
            spare_copy(b).start()
            return carry

        def wait_spare(b, carry):
            spare_copy(b).wait()
            return carry

        n_blocks = xs_hbm.shape[0] // ROW_BLOCK
        lax.fori_loop(n_used[0], n_blocks, start_spare, 0)
        lax.fori_loop(n_used[0], n_blocks, wait_spare, 0)


def _dispatch(x1, rank, tables, n_blocks):
    n = x1.shape[0] // TM
    return pl.pallas_call(
        _dispatch_kernel,
        grid_spec=pltpu.PrefetchScalarGridSpec(
            num_scalar_prefetch=9, grid=(n,),
            in_specs=[pl.BlockSpec((TM, D_MODEL), lambda i, *_: (i, 0)),
                      pl.BlockSpec((N_EXPERTS, TM), lambda i, *_: (0, i))],
            out_specs=pl.BlockSpec(memory_space=pl.ANY),
            scratch_shapes=[pltpu.VMEM((L_MAX, TM), F32), pltpu.VMEM((2, L_MAX, HALF), U32),
                            pltpu.VMEM((ROW_BLOCK, HALF), U32), pltpu.SemaphoreType.DMA(())]),
        out_shape=jax.ShapeDtypeStruct((n_blocks * ROW_BLOCK, HALF), U32),
        compiler_params=_params("arbitrary"), name="dispatch",
    )(tables["run_off"], tables["run_len"], tables["long_run"], tables["copy_src"],
      tables["copy_dst"], tables["copy_n"], tables["n_tail"],
      tables["tail_dst"], tables["n_used"], x1, rank)


def _experts_kernel(block_expert, block_ord, seq_expert, n_seq, n_used,
                    xs_hbm, wg_hbm, wu_hbm, wd_hbm, y_ref,
                    x_ring, wg_stage, wu_stage, wd_stage, wgu_scr, wd_scr, x_sems, w_sems, *, layer):
    b = pl.program_id(0)
    n_u = n_used[0]

    def fetch(block):
        slot = block % RING
        rows = pl.ds(pl.multiple_of(block * ROW_BLOCK, ROW_BLOCK), ROW_BLOCK)
        return pltpu.make_async_copy(xs_hbm.at[rows, :], x_ring.at[slot], x_sems.at[slot])

    def weight_copies(ordinal):
        e = seq_expert[ordinal]
        slot = ordinal % 2
        return [pltpu.make_async_copy(src.at[layer, e], stage.at[slot], w_sems.at[slot])
                for src, stage in ((wg_hbm, wg_stage), (wu_hbm, wu_stage), (wd_hbm, wd_stage))]

    @pl.when(b == 0)
    def _():
        for copy in weight_copies(0):
            copy.start()
        for first in range(RING - 1):
            @pl.when(first < n_u)
            def _(first=first):
                fetch(first).start()

    @pl.when(b + (RING - 1) < n_u)
    def _():
        fetch(b + (RING - 1)).start()

    @pl.when(b >= n_u)
    def _():
        y_ref[...] = jnp.zeros_like(y_ref)

    @pl.when(b < n_u)
    def _():
        fresh = (b == 0) | (block_expert[b] != block_expert[jnp.maximum(b - 1, 0)])

        @pl.when(fresh)
        def _():
            ordinal = block_ord[b]
            slot = ordinal % 2
            for copy in weight_copies(ordinal):
                copy.wait()
            wgu_scr[:, :D_EXPERT] = wg_stage[slot].astype(BF16)
            wgu_scr[:, D_EXPERT:] = wu_stage[slot].astype(BF16)
            wd_scr[...] = wd_stage[slot].astype(BF16)

            @pl.when(ordinal + 1 < n_seq[0])
            def _():
                for copy in weight_copies(ordinal + 1):
                    copy.start()

        fetch(b).wait()
        lo, hi = _unpack_halves(x_ring[b % RING])
        gu = _dot(lo, wgu_scr[:HALF, :]) + _dot(hi, wgu_scr[HALF:, :])
        hid = (_silu(gu[:, :D_EXPERT]) * gu[:, D_EXPERT:]).astype(BF16)
        y = _dot(hid, wd_scr[...])
        y_ref[...] = _pack_halves(y.astype(BF16).astype(F32))


def _experts(xs, tables, w_e_gate, w_e_up, w_e_down, layer, n_blocks):
    any_spec = pl.BlockSpec(memory_space=pl.ANY)
    return pl.pallas_call(
        functools.partial(_experts_kernel, layer=layer),
        grid_spec=pltpu.PrefetchScalarGridSpec(
            num_scalar_prefetch=5, grid=(n_blocks,),
            in_specs=[any_spec, any_spec, any_spec, any_spec],
            out_specs=pl.BlockSpec((ROW_BLOCK, HALF), lambda b, *_: (b, 0)),
            scratch_shapes=[pltpu.VMEM((RING, ROW_BLOCK, HALF), U32),
                            pltpu.VMEM((2, D_MODEL, D_EXPERT), F32), pltpu.VMEM((2, D_MODEL, D_EXPERT), F32),
                            pltpu.VMEM((2, D_EXPERT, D_MODEL), F32),
                            pltpu.VMEM((D_MODEL, 2 * D_EXPERT), BF16), pltpu.VMEM((D_EXPERT, D_MODEL), BF16),
                            pltpu.SemaphoreType.DMA((RING,)), pltpu.SemaphoreType.DMA((2,))]),
        out_shape=jax.ShapeDtypeStruct(xs.shape, U32),
        compiler_params=_params("arbitrary"), name="experts",
    )(tables["block_expert"], tables["block_ord"], tables["seq_expert"], tables["n_seq"], tables["n_used"],
      xs, w_e_gate, w_e_up, w_e_down)


def _combine_kernel(run_off, run_len, long_run, copy_src, copy_dst, copy_n,
                    x1_ref, rank_ref, wt_ref, y_hbm, wsgu, wsd, g2, b2, out_m_ref, out_t_ref,
                    sel_scr, buf_scr, acc_scr, sem):
    t = pl.program_id(0)
    n_tiles = pl.num_programs(0)

    def fetch(tile):
        dst = buf_scr.at[tile % 2]
        _start_tile_copies(
            tile, copy_n, lambda i, rows: _row_copy(y_hbm, copy_dst[i], dst, copy_src[i], sem, rows))

    @pl.when(t == 0)
    def _():
        buf_scr[...] = jnp.zeros_like(buf_scr)
        fetch(t)

    assert MIN_TILE_ROWS <= L_MAX and MIN_TILE_ROWS % GRANULE == 0
    sel_scr[MIN_TILE_ROWS:, :] = jnp.zeros((L_MAX - MIN_TILE_ROWS, TM), F32)
    _build_selection(sel_scr, rank_ref[...], wt_ref[...], run_off, run_len, long_run, t)
    x1 = x1_ref[...]
    xb = x1.astype(BF16)
    gu = _dot(xb, wsgu[...])
    hs = (_silu(gu[:, :D_EXPERT]) * gu[:, D_EXPERT:]).astype(BF16)
    acc_scr[...] = _dot(hs, wsd[...])
    buf = buf_scr.at[t % 2]
    _wait_tile_copies(t, copy_n, y_hbm, buf, sem)

    @pl.when(t + 1 < n_tiles)
    def _():
        fetch(jnp.minimum(t + 1, n_tiles - 1))

    def sum_rows(c):
        rows = slice(c * SEL_CHUNK, (c + 1) * SEL_CHUNK)
        lo, hi = _unpack_halves(buf[rows, :])
        sel = sel_scr[rows, :].astype(BF16)
        acc_scr[:, :HALF] += _dot_tn(sel, lo)
        acc_scr[:, HALF:] += _dot_tn(sel, hi)

    for c in range(L_MAX // SEL_CHUNK):
        sum_rows(c)

    out = _layer_norm(ALPHA * x1 + acc_scr[...], g2[...], b2[...])

    @pl.when(t + 1 < n_tiles)
    def _():
        out_m_ref[...] = out

    @pl.when(t + 1 == n_tiles)
    def _():
        out_t_ref[...] = out


def _combine(x1, rank, wt, y, tables, w):
    n = x1.shape[0] // TM
    small = (w["wsgu"], w["wsd"], w["g2"], w["b2"])
    tok = pl.BlockSpec((N_EXPERTS, TM), lambda i, *_: (0, i))
    return pl.pallas_call(
        _combine_kernel,
        grid_spec=pltpu.PrefetchScalarGridSpec(
            num_scalar_prefetch=6, grid=(n,),
            in_specs=[pl.BlockSpec((TM, D_MODEL), lambda i, *_: (i, 0)), tok, tok,
                      pl.BlockSpec(memory_space=pl.ANY)]
                     + [pl.BlockSpec(a.shape, lambda i, *_, nd=a.ndim: (0,) * nd) for a in small],
            out_specs=_main_tail_specs(n, D_MODEL),
            scratch_shapes=[pltpu.VMEM((L_MAX, TM), F32), pltpu.VMEM((2, L_MAX, HALF), U32),
                            pltpu.VMEM((TM, D_MODEL), F32), pltpu.SemaphoreType.DMA(())]),
        out_shape=(jax.ShapeDtypeStruct((x1.shape[0] - TM, D_MODEL), F32),
                   jax.ShapeDtypeStruct((TM, D_MODEL), F32)),
        compiler_params=_params("arbitrary"), name="combine",
    )(tables["run_off"], tables["run_len"], tables["long_run"], tables["copy_src"],
      tables["copy_dst"], tables["copy_n"], x1, rank, wt, y, *small)


def _moe(x1, w, w_e_gate, w_e_up, w_e_down, layer):
    n_blocks = _n_row_blocks(x1.shape[0])
    wt, rank, cnt = _router(x1, w)
    tables = _dispatch_tables(cnt[:, :N_EXPERTS, 0], n_blocks)
    xs = _dispatch(x1, rank, tables, n_blocks)
    y = _experts(xs, tables, w_e_gate, w_e_up, w_e_down, layer, n_blocks)
    return _combine(x1, rank, wt, y, tables, w)


def _layer_weights(l, w_in, w_gk2, b_gk, gla_norm_g, w_pool, pool_scale, w_br_a, w_br_b, w_out, ln1_g, ln1_b,
                   w_router, b_router, w_sh_gate, w_sh_up, w_sh_down, ln2_g, ln2_b):
    pieces = {}
    pad_rank = LANES - GATE_RANK
    pieces["wgk2"] = jnp.pad(w_gk2[l], ((0, pad_rank), (0, 0))).astype(BF16)
    pieces["bgk"] = b_gk[l].reshape(1, KEY)
    pieces["gn"] = gla_norm_g[l].reshape(1, DV)
    pieces["wbra"] = w_br_a[l].astype(BF16)
    pieces["wpool"] = w_pool[l].astype(BF16)
    pieces["pscale"] = pool_scale[l].reshape(1, POOL_WIDTH)
    pieces["wbrb"] = w_br_b[l].astype(BF16)
    pieces["wout"] = w_out[l].astype(BF16)
    pieces["g1"] = ln1_g[l].reshape(1, D_MODEL)
    pieces["b1"] = ln1_b[l].reshape(1, D_MODEL)
    wrt = w_router[l].T
    wrh = wrt.astype(BF16)
    pieces["wrh"] = wrh
    pieces["wrl"] = (wrt - wrh.astype(F32)).astype(BF16)
    pieces["br"] = b_router[l].reshape(N_EXPERTS, 1)
    pieces["wsgu"] = jnp.concatenate([w_sh_gate[l], w_sh_up[l]], axis=1).astype(BF16)
    pieces["wsd"] = w_sh_down[l].astype(BF16)
    pieces["g2"] = ln2_g[l].reshape(1, D_MODEL)
    pieces["b2"] = ln2_b[l].reshape(1, D_MODEL)
    return pieces


def kernel(x_prompt, x_sample, state_gla, cache_pool, w_in, w_gk2, b_gk, gla_norm_g, w_pool, pool_scale, w_br_a, w_br_b, w_out, ln1_g, ln1_b, w_router, b_router, w_e_gate, w_e_up, w_e_down, w_sh_gate, w_sh_up, w_sh_down, ln2_g, ln2_b):
    assert T_SAMPLE == TM, "the decode tokens must fill exactly one token tile"
    xm = x_prompt.reshape(T_PROMPT, D_MODEL)
    xt = x_sample.reshape(T_SAMPLE, D_MODEL)
    n_prompt_tiles = T_PROMPT // TM
    sample_block0 = T_PROMPT // DEC_SEQ
    sp, hp, ss, hs = [], [], [], []
    for l in range(DEPTH):
        w = _layer_weights(l, w_in, w_gk2, b_gk, gla_norm_g, w_pool, pool_scale, w_br_a, w_br_b, w_out, ln1_g,
                           ln1_b, w_router, b_router, w_sh_gate, w_sh_up, w_sh_down, ln2_g, ln2_b)
        q, k, v, r, lf, u, ga, gb = _inproj(xm, xt, w_in, w, l)

        og_p, st_p = _gla(q, k, v, lf, r, w["gn"], None, tile=TM, chunk=CHUNK, n_tiles=n_prompt_tiles, block0=0)
        og_s, st_s = _gla(q, k, v, lf, r, w["gn"], state_gla[l], tile=DEC_SEQ, chunk=DEC_SEQ, n_tiles=DEC_BATCH,
                          block0=sample_block0)

        u_tiles = u.reshape(n_prompt_tiles + 1, TM, POOL_WIDTH)
        halo_p = jnp.concatenate([jnp.zeros((1, HALO, POOL_WIDTH), F32),
                                  u_tiles[:n_prompt_tiles - 1, TM - HALO:, :]], axis=0)
        halo_s = jnp.concatenate([jnp.zeros((DEC_BATCH, HALO - POOL_HIST, POOL_WIDTH), F32), cache_pool[l]], axis=1)
        x1 = _mix(xm, xt, og_p, og_s, u, halo_p, halo_s, ga, gb, w)

        xm, xt = _moe(x1, w, w_e_gate, w_e_up, w_e_down, l)

        sp.append(st_p)
        ss.append(st_s)
        hp.append(u[T_PROMPT - POOL_HIST:T_PROMPT].reshape(1, POOL_HIST, POOL_WIDTH))
        hs.append(u[T_PROMPT:].reshape(DEC_BATCH, DEC_SEQ, POOL_WIDTH)[:, DEC_SEQ - POOL_HIST:, :])
    y_prompt = xm.reshape(1, SEQ, D_MODEL)
    y_sample = xt.reshape(DEC_BATCH, DEC_SEQ, D_MODEL)
    return (y_prompt, y_sample, jnp.stack(sp), jnp.stack(hp), jnp.stack(ss), jnp.stack(hs))
```

```python
import functools

import jax
import jax.numpy as jnp
from jax import lax
from jax.experimental import pallas as pl
from jax.experimental.pallas import tpu as pltpu

F32 = jnp.float32
BF16 = jnp.bfloat16
U32 = jnp.uint32
I32 = jnp.int32

D_MODEL = 1024
DEPTH = 2
SEQ = 16384
DEC_BATCH = 8
DEC_SEQ = 32
PAST_LEN = 4096
CHUNK = 64
HEADS = 4
DK = 128
DV = 256
KEY = HEADS * DK
VAL = HEADS * DV
GATE_RANK = 16
GATE_NORMALIZER = 16.0
POOL_WIDTH = 512
POOL_WINDOWS = (2, 4, 8, 16)
POOL_GROUP_DIM = 128
POOL_HIST = 15
N_EXPERTS = 64
GROUP_SIZE = 8
TOPK_GROUPS = 4
TOP_K = 8
D_EXPERT = 256
ROUTED_SCALE = 2.5
ALPHA = (2 * DEPTH) ** 0.25
LN_EPS = 1e-5
RMS_EPS = 1e-6
LOG2_E = 1.4426950408889634

LANES = 128
SUBLANES = 8
VMEM_LIMIT_BYTES = 56 * 1024 * 1024

T_PROMPT = SEQ
T_SAMPLE = DEC_BATCH * DEC_SEQ
T_ALL = T_PROMPT + T_SAMPLE
TM = 256
HALO = 16
CHUNK_UNROLL = 4
NEG_INF = float("-inf")

HALF = D_MODEL // 2
GRANULE = SUBLANES
ROW_BLOCK = 512
COPY_GROUP = 8
COPY_ROWS = (32, 16, 8)
RING = 3
RUN_CHUNK = 64
L_MAX = 2560
SEL_CHUNK = 512
MIN_TILE_ROWS = TM * TOP_K
COPY_CAP = (L_MAX // COPY_ROWS[0], N_EXPERTS, N_EXPERTS)
HI_MASK = 0xFFFF0000


def _dot(a, b):
    return jnp.dot(a, b, preferred_element_type=F32)


def _dot_nt(a, b):
    return lax.dot_general(a, b, (((1,), (1,)), ((), ())), preferred_element_type=F32)


def _dot_tn(a, b):
    return lax.dot_general(a, b, (((0,), (0,)), ((), ())), preferred_element_type=F32)


def _sigmoid(x):
    return 0.5 * jnp.tanh(0.5 * x) + 0.5


def _silu(x):
    return x * _sigmoid(x)


def _layer_norm(x, g, b):
    mu = jnp.mean(x, axis=-1, keepdims=True)
    xc = x - mu
    var = jnp.mean(xc * xc, axis=-1, keepdims=True)
    return xc * lax.rsqrt(var + LN_EPS) * g + b


def _pack_halves(v):
    bits = lax.bitcast_convert_type(v, U32)
    return (bits[:, HALF:] & jnp.uint32(HI_MASK)) | (bits[:, :HALF] >> 16)


def _unpack_halves(w):
    lo = lax.bitcast_convert_type(w << 16, F32).astype(BF16)
    hi = lax.bitcast_convert_type(w & jnp.uint32(HI_MASK), F32).astype(BF16)
    return lo, hi


def _params(*sem):
    return pltpu.CompilerParams(dimension_semantics=sem, vmem_limit_bytes=VMEM_LIMIT_BYTES)


def _const_spec(shape):
    nd = len(shape)
    return pl.BlockSpec(shape, lambda *_: (0,) * nd)


def _is_tail_tile():
    return pl.program_id(0) == pl.num_programs(0) - 1


def _main_tail_specs(n_tiles, width):
    return [pl.BlockSpec((TM, width), lambda i, *_: (jnp.minimum(i, n_tiles - 2), 0)),
            pl.BlockSpec((TM, width), lambda i, *_: (0, 0))]


IN_PIECES = ((("q", KEY), ("k", KEY), ("v", VAL), ("r", VAL)),
             (("gk", LANES), ("u", POOL_WIDTH), ("ga", D_MODEL), ("gb", D_MODEL)))
IN_COLS = {}
for _part, _pieces in enumerate(IN_PIECES):
    _col = 0
    for _name, _width in _pieces:
        IN_COLS[_name] = (_part, slice(_col, _col + _width))
        _col += _width
IN_SPLIT = sum(width for _, width in IN_PIECES[0])
IN_REST = sum(width for _, width in IN_PIECES[1][1:])
N_PROJ = IN_SPLIT + GATE_RANK + IN_REST
W_CHUNK = 256


def _inproj_kernel(xm_ref, xt_ref, w_hbm, wgk2, bgk, q_o, k_o, v_o, r_o, lf_o, u_o, ga_o, gb_o,
                   stage, wa_scr, wb_scr, sems, *, layer):
    @pl.when(pl.program_id(0) == 0)
    def _():
        n_chunks = D_MODEL // W_CHUNK

        def copy(c):
            rows = pl.ds(c * W_CHUNK, W_CHUNK)
            return pltpu.make_async_copy(w_hbm.at[layer, rows, :], stage.at[c % 2], sems.at[c % 2])

        copy(0).start()
        lane = lax.broadcasted_iota(I32, (W_CHUNK, LANES), 1)
        for c in range(n_chunks):
            if c + 1 < n_chunks:
                copy(c + 1).start()
            copy(c).wait()
            blk = stage[c % 2]
            rows = slice(c * W_CHUNK, (c + 1) * W_CHUNK)
            wa_scr[rows, :] = blk[:, :IN_SPLIT].astype(BF16)
            gate = jnp.where(lane < GATE_RANK, blk[:, IN_SPLIT:IN_SPLIT + LANES], 0.0)
            wb_scr[rows, :LANES] = gate.astype(BF16)
            wb_scr[rows, LANES:] = blk[:, IN_SPLIT + GATE_RANK:].astype(BF16)

    xb = jnp.where(_is_tail_tile(), xt_ref[...], xm_ref[...]).astype(BF16)
    proj = lambda name: _dot(xb, (wa_scr, wb_scr)[IN_COLS[name][0]][:, IN_COLS[name][1]])
    q_o[...] = proj("q") * (DK ** -0.5)
    k_o[...] = proj("k")
    v_o[...] = proj("v").astype(BF16)
    r_o[...] = proj("r")
    gk = proj("gk").astype(BF16)
    z = _dot(gk, wgk2[...]) + bgk[...]
    log_sig = jnp.minimum(z, 0.0) - jnp.log1p(jnp.exp(-jnp.abs(z)))
    lf_o[...] = log_sig * (1.0 / GATE_NORMALIZER)
    u_o[...] = proj("u")
    ga_o[...] = proj("ga")
    gb_o[...] = proj("gb")


def _inproj(xm, xt, w_in, w, layer):
    n = xm.shape[0] // TM + 1
    rows = n * TM
    row = lambda width: pl.BlockSpec((TM, width), lambda i: (i, 0))
    out_shapes = (
        jax.ShapeDtypeStruct((rows, KEY), F32), jax.ShapeDtypeStruct((rows, KEY), F32),
        jax.ShapeDtypeStruct((rows, VAL), BF16), jax.ShapeDtypeStruct((rows, VAL), F32),
        jax.ShapeDtypeStruct((rows, KEY), F32), jax.ShapeDtypeStruct((rows, POOL_WIDTH), F32),
        jax.ShapeDtypeStruct((rows, D_MODEL), F32), jax.ShapeDtypeStruct((rows, D_MODEL), F32))
    small = (w["wgk2"], w["bgk"])
    return pl.pallas_call(
        functools.partial(_inproj_kernel, layer=layer), grid=(n,),
        in_specs=_main_tail_specs(n, D_MODEL) + [pl.BlockSpec(memory_space=pltpu.HBM)]
                 + [_const_spec(a.shape) for a in small],
        out_specs=[row(KEY), row(KEY), row(VAL), row(VAL), row(KEY), row(POOL_WIDTH), row(D_MODEL), row(D_MODEL)],
        out_shape=out_shapes,
        scratch_shapes=[pltpu.VMEM((2, W_CHUNK, N_PROJ), F32), pltpu.VMEM((D_MODEL, IN_SPLIT), BF16),
                        pltpu.VMEM((D_MODEL, LANES + IN_REST), BF16), pltpu.SemaphoreType.DMA((2,))],
        compiler_params=_params("arbitrary"), name="inproj",
    )(xm, xt, w_in, *small)


def _gla_kernel(*refs, chunk, n_chunks, has_init):
    if has_init:
        q_ref, k_ref, v_ref, lf_ref, r_ref, gn_ref, s0_ref, o_ref, s_out_ref, st_ref = refs
        for h in range(HEADS):
            st_ref[h] = s0_ref[0, h].T
    else:
        q_ref, k_ref, v_ref, lf_ref, r_ref, gn_ref, o_ref, s_out_ref, st_ref = refs

        @pl.when(pl.program_id(0) == 0)
        def _():
            st_ref[...] = jnp.zeros_like(st_ref)

    row = lax.broadcasted_iota(I32, (chunk, KEY), 0)
    ta = lax.broadcasted_iota(I32, (chunk, chunk), 0)
    sa = lax.broadcasted_iota(I32, (chunk, chunk), 1)
    gn = gn_ref[...]
    halves = [1 << i for i in range(chunk.bit_length() - 1)]
    diag = ta == sa
    pairs = [((ta ^ sa) < 2 * half) & ((ta & half) != 0) & ((sa & half) == 0) for half in halves]

    def chunk_body(c, carry):
        rows = pl.ds(pl.multiple_of(c * chunk, chunk), chunk)
        lf = lf_ref[rows, :] * LOG2_E
        q = q_ref[rows, :]
        k = k_ref[rows, :]
        qb = q.astype(BF16)
        kb = k.astype(BF16)
        att = []
        for h in range(HEADS):
            hs = slice(h * DK, (h + 1) * DK)
            att.append(jnp.where(diag, _dot_nt(qb[:, hs], kb[:, hs]), 0.0))
        seg, tot = lf, lf
        for half, pair in zip(halves, pairs):
            qs = (q * jnp.exp2(seg)).astype(BF16)
            ks = (k * jnp.exp2(tot - seg)).astype(BF16)
            for h in range(HEADS):
                hs = slice(h * DK, (h + 1) * DK)
                att[h] = jnp.where(pair, _dot_nt(qs[:, hs], ks[:, hs]), att[h])
            if half < SUBLANES:
                upper = (row & half) != 0
                below = pltpu.roll(tot, half, 0)
                above = pltpu.roll(tot, chunk - half, 0)
                seg = seg + jnp.where(upper, below, 0.0)
                tot = tot + jnp.where(upper, below, above)
            else:
                step, n_tiles = half // SUBLANES, chunk // SUBLANES
                seg_t = [seg[i * SUBLANES:(i + 1) * SUBLANES, :] for i in range(n_tiles)]
                tot_t = [tot[i * SUBLANES:(i + 1) * SUBLANES, :] for i in range(n_tiles)]
                seg = jnp.concatenate([seg_t[i] + tot_t[i - step] if i & step else seg_t[i] for i in range(n_tiles)], axis=0)
                tot = jnp.concatenate([tot_t[i] + tot_t[i ^ step] for i in range(n_tiles)], axis=0)
        q_in = (q * jnp.exp2(seg)).astype(BF16)
        k_out = (k * jnp.exp2(tot - seg)).astype(BF16)
        decay = jnp.exp2(tot[0:1, :])
        for h in range(HEADS):
            hs = slice(h * DK, (h + 1) * DK)
            vs = slice(h * DV, (h + 1) * DV)
            state = st_ref[h]
            vh = v_ref[rows, vs]
            o = _dot_nt(q_in[:, hs], state.astype(BF16)) + _dot(att[h].astype(BF16), vh)
            st_ref[h] = state * decay[:, hs] + _dot_tn(vh, k_out[:, hs])
            ms = jnp.mean(o * o, axis=-1, keepdims=True)
            o = o * lax.rsqrt(ms + RMS_EPS) * gn
            o_ref[rows, vs] = (o * _silu(r_ref[rows, vs])).astype(BF16)
        return carry

    lax.fori_loop(0, n_chunks, chunk_body, 0, unroll=min(n_chunks, CHUNK_UNROLL))

    def write_state():
        for h in range(HEADS):
            s_out_ref[0, h] = st_ref[h].T

    if has_init:
        write_state()
    else:
        pl.when(pl.program_id(0) == pl.num_programs(0) - 1)(write_state)


def _gla(q, k, v, lf, r, gn, s0, *, tile, chunk, n_tiles, block0):
    has_init = s0 is not None
    row = lambda width: pl.BlockSpec((tile, width), lambda i: (block0 + i, 0))
    st_spec = pl.BlockSpec((1, HEADS, DK, DV), (lambda i: (i, 0, 0, 0)) if has_init else (lambda i: (0, 0, 0, 0)))
    n_states = n_tiles if has_init else 1
    in_specs = [row(KEY), row(KEY), row(VAL), row(KEY), row(VAL), _const_spec(gn.shape)]
    args = [q, k, v, lf, r, gn]
    if has_init:
        in_specs.append(st_spec)
        args.append(s0)
    return pl.pallas_call(
        functools.partial(_gla_kernel, chunk=chunk, n_chunks=tile // chunk, has_init=has_init),
        grid=(n_tiles,), in_specs=in_specs,
        out_specs=[pl.BlockSpec((tile, VAL), lambda i: (i, 0)), st_spec],
        out_shape=(jax.ShapeDtypeStruct((n_tiles * tile, VAL), BF16),
                   jax.ShapeDtypeStruct((n_states, HEADS, DK, DV), F32)),
        scratch_shapes=[pltpu.VMEM((HEADS, DV, DK), F32)],
        compiler_params=_params("arbitrary"), name="gla_init" if has_init else "gla",
    )(*args)


def _window_sums(ext, window):
    shift = 1
    while shift < window:
        ext = ext + pltpu.roll(ext, shift, 0)
        shift *= 2
    return ext


def _mix_kernel(xm_ref, xt_ref, ogm_ref, ogt_ref, u_ref, halo_m_ref, halo_t_ref, ga_ref, gb_ref,
                wbra, wpool, pscale, wbrb, wout, g1, b1, x1_ref):
    tail = _is_tail_tile()
    ya = _dot(jnp.where(tail, ogt_ref[...], ogm_ref[...]), wbra[...])
    u = u_ref[...]
    ext_m = jnp.concatenate([halo_m_ref[0], u], axis=0)
    seg = HALO + DEC_SEQ
    pieces = []
    for b in range(DEC_BATCH):
        pieces += [halo_t_ref[b], u[b * DEC_SEQ:(b + 1) * DEC_SEQ, :]]
    ext_t = jnp.concatenate(pieces, axis=0)
    rowi = lax.broadcasted_iota(I32, (TM, 1), 0)
    pos = jnp.where(tail, PAST_LEN + (rowi & (DEC_SEQ - 1)), pl.program_id(0) * TM + rowi)
    parts = []
    for g, window in enumerate(POOL_WINDOWS):
        cols = slice(g * POOL_GROUP_DIM, (g + 1) * POOL_GROUP_DIM)
        win_m = _window_sums(ext_m[:, cols], window)[HALO:, :]
        win_all = _window_sums(ext_t[:, cols], window)
        win_t = jnp.concatenate([win_all[b * seg + HALO:(b + 1) * seg, :] for b in range(DEC_BATCH)], axis=0)
        cnt = jnp.minimum(window, pos + 1).astype(F32)
        d = jnp.where(tail, win_t, win_m) / cnt - u[:, cols]
        parts.append(_dot(d.astype(BF16), wpool[g]))
    yb_in = jnp.concatenate(parts, axis=1) * pscale[...]
    yb = _dot(yb_in.astype(BF16), wbrb[...])
    mixed = _sigmoid(ga_ref[...]) * ya + _sigmoid(gb_ref[...]) * yb
    mix = _dot(mixed.astype(BF16), wout[...])
    x = jnp.where(tail, xt_ref[...], xm_ref[...])
    x1_ref[...] = _layer_norm(ALPHA * x + mix, g1[...], b1[...])


def _mix(xm, xt, og_m, og_t, u, halo_m, halo_t, ga, gb, w):
    n = xm.shape[0] // TM + 1
    row = lambda width: pl.BlockSpec((TM, width), lambda i: (i, 0))
    weights = (w["wbra"], w["wpool"], w["pscale"], w["wbrb"], w["wout"], w["g1"], w["b1"])
    return pl.pallas_call(
        _mix_kernel, grid=(n,),
        in_specs=_main_tail_specs(n, D_MODEL) + _main_tail_specs(n, VAL) + [
            row(POOL_WIDTH), pl.BlockSpec((1, HALO, POOL_WIDTH), lambda i: (jnp.minimum(i, n - 2), 0, 0)),
            _const_spec(halo_t.shape), row(D_MODEL), row(D_MODEL)] + [_const_spec(a.shape) for a in weights],
        out_specs=row(D_MODEL),
        out_shape=jax.ShapeDtypeStruct((n * TM, D_MODEL), F32),
        compiler_params=_params("arbitrary"), name="mix",
    )(xm, xt, og_m, og_t, u, halo_m, halo_t, ga, gb, *weights)


def _router_kernel(x1_ref, wrh_ref, wrl_ref, br_ref, wt_ref, rank_ref, cnt_ref, *, tile):
    x = x1_ref[...]
    xh = x.astype(BF16)
    xl = (x - xh.astype(F32)).astype(BF16)
    logits = _dot_nt(wrh_ref[...], xh) + _dot_nt(wrl_ref[...], xh) + _dot_nt(wrh_ref[...], xl)
    scores = _sigmoid(logits)
    rows_pad = N_EXPERTS
    n_grp = rows_pad // GROUP_SIZE
    biased = scores + br_ref[...]
    shape3 = (n_grp, GROUP_SIZE, tile)
    b3 = biased.reshape(shape3)
    s3 = scores.reshape(shape3)
    sub = lax.broadcasted_iota(I32, shape3, 1)
    gid = lax.broadcasted_iota(I32, shape3, 0)
    eid = gid * GROUP_SIZE + sub
    m1 = jnp.max(b3, axis=1, keepdims=True)
    i1 = jnp.min(jnp.where(b3 == m1, sub, GROUP_SIZE), axis=1, keepdims=True)
    m2 = jnp.max(jnp.where(sub == i1, NEG_INF, b3), axis=1, keepdims=True)
    gscore = m1 + m2
    gid1 = lax.broadcasted_iota(I32, (n_grp, 1, tile), 0)
    gsel = jnp.zeros((n_grp, 1, tile), jnp.bool_)
    for _ in range(TOPK_GROUPS):
        gm = jnp.max(gscore, axis=0, keepdims=True)
        gi = jnp.min(jnp.where(gscore == gm, gid1, n_grp), axis=0, keepdims=True)
        pick = gid1 == gi
        gsel = gsel | pick
        gscore = jnp.where(pick, NEG_INF, gscore)
    masked = jnp.where(gsel, b3, NEG_INF)
    wsel = jnp.zeros(shape3, F32)
    chosen = jnp.zeros(shape3, jnp.bool_)
    for _ in range(TOP_K):
        m = jnp.max(jnp.max(masked, axis=1, keepdims=True), axis=0, keepdims=True)
        idx = jnp.min(jnp.min(jnp.where(masked == m, eid, rows_pad), axis=1, keepdims=True), axis=0, keepdims=True)
        pick = eid == idx
        wsel = jnp.where(pick, s3, wsel)
        chosen = chosen | pick
        masked = jnp.where(pick, NEG_INF, masked)
    wsum = jnp.sum(jnp.sum(wsel, axis=1, keepdims=True), axis=0, keepdims=True)
    wt_ref[...] = (wsel / wsum * ROUTED_SCALE).reshape(rows_pad, tile)
    sel = jnp.where(chosen, 1.0, 0.0).reshape(rows_pad, tile)
    before = lax.broadcasted_iota(I32, (tile, tile), 0) < lax.broadcasted_iota(I32, (tile, tile), 1)
    rank = _dot(sel.astype(BF16), jnp.where(before, 1.0, 0.0).astype(BF16))
    rank_ref[...] = jnp.where(sel > 0.0, rank, -1.0).astype(I32)
    cnt = jnp.sum(sel, axis=1, keepdims=True).astype(I32)
    cnt_ref[0] = jnp.broadcast_to(cnt, (rows_pad, LANES))


def _router(x1, w):
    n = x1.shape[0] // TM
    rows_pad = N_EXPERTS
    tok = pl.BlockSpec((rows_pad, TM), lambda i: (0, i))
    return pl.pallas_call(
        functools.partial(_router_kernel, tile=TM), grid=(n,),
        in_specs=[pl.BlockSpec((TM, D_MODEL), lambda i: (i, 0)), _const_spec(w["wrh"].shape),
                  _const_spec(w["wrl"].shape), _const_spec(w["br"].shape)],
        out_specs=[tok, tok, pl.BlockSpec((1, rows_pad, LANES), lambda i: (i, 0, 0))],
        out_shape=(jax.ShapeDtypeStruct((rows_pad, x1.shape[0]), F32),
                   jax.ShapeDtypeStruct((rows_pad, x1.shape[0]), I32),
                   jax.ShapeDtypeStruct((n, rows_pad, LANES), I32)),
        compiler_params=_params("arbitrary"), name="router",
    )(x1, w["wrh"], w["wrl"], w["br"])


def _n_row_blocks(n_tokens):
    n_tiles = n_tokens // TM
    worst_rows = n_tokens * TOP_K + n_tiles * N_EXPERTS * (GRANULE - 1) + N_EXPERTS * (ROW_BLOCK - 1)
    return -(-worst_rows // ROW_BLOCK)


def _steps(shift):
    return shift - jnp.concatenate([jnp.zeros_like(shift[..., :1]), shift[..., :-1]], axis=-1)


def _dispatch_tables(cnt, n_blocks):
    padded = (cnt + (GRANULE - 1)) // GRANULE * GRANULE
    run_end = jnp.cumsum(padded, axis=1)
    run_off = run_end - padded
    rows_e = jnp.sum(padded, axis=0)
    region = (rows_e + (ROW_BLOCK - 1)) // ROW_BLOCK * ROW_BLOCK
    region_end = jnp.cumsum(region)
    region_start = region_end - region
    run_base = region_start[None, :] + jnp.cumsum(padded, axis=0) - padded

    big = padded // COPY_ROWS[0]
    big_first = jnp.cumsum(big, axis=1) - big
    k_big = jnp.arange(COPY_CAP[0], dtype=I32)
    owned = k_big[None, :, None] >= big_first[:, None, :]
    big_list = lambda row0: COPY_ROWS[0] * k_big[None, :] + jnp.sum(
        jnp.where(owned, _steps(row0 - COPY_ROWS[0] * big_first)[:, None, :], 0), axis=2)
    lists_src, lists_dst, counts = [big_list(run_off)], [big_list(run_base)], [jnp.sum(big, axis=1)]
    done = big * COPY_ROWS[0]
    for rows, cap in zip(COPY_ROWS[1:], COPY_CAP[1:]):
        has = (padded // rows) % 2
        pos = jnp.cumsum(has, axis=1) - has
        hit = (has[:, None, :] != 0) & (pos[:, None, :] == jnp.arange(cap, dtype=I32)[None, :, None])
        lists_src.append(jnp.sum(jnp.where(hit, (run_off + done)[:, None, :], 0), axis=2))
        lists_dst.append(jnp.sum(jnp.where(hit, (run_base + done)[:, None, :], 0), axis=2))
        counts.append(jnp.sum(has, axis=1))
        done = done + has * rows
    copy_src = jnp.concatenate(lists_src, axis=1)
    copy_dst = jnp.concatenate(lists_dst, axis=1)
    copy_n = jnp.stack(counts, axis=1)

    n_used = region_end[-1] // ROW_BLOCK
    blocks = jnp.arange(n_blocks, dtype=I32)
    block_expert = jnp.minimum(jnp.sum(blocks[:, None] >= (region_end // ROW_BLOCK)[None, :], axis=1), N_EXPERTS - 1)
    experts = jnp.arange(N_EXPERTS, dtype=I32)
    present = (region > 0).astype(I32)
    ordinal = jnp.cumsum(present) - present
    seq_expert = jnp.sum(jnp.where((present[None, :] != 0) & (ordinal[None, :] == experts[:, None]),
                                   experts[None, :], 0), axis=1)
    block_ord = jnp.sum(jnp.where(block_expert[:, None] == experts[None, :], ordinal[None, :], 0), axis=1)
    tail = (region - rows_e) // GRANULE
    tail_end = jnp.cumsum(tail)
    tail_first = tail_end - tail
    slot = jnp.arange(N_EXPERTS * (ROW_BLOCK // GRANULE - 1), dtype=I32)
    t_step = _steps((region_start + rows_e) // GRANULE - tail_first)
    tail_dst = slot + jnp.sum(jnp.where(slot[:, None] >= tail_first[None, :], t_step[None, :], 0), axis=1)
    as_i32 = lambda a: a.astype(I32).reshape(-1)
    long_run = jnp.any(padded > RUN_CHUNK, axis=1)
    return dict(run_off=as_i32(run_off), run_len=as_i32(padded), long_run=as_i32(long_run),
                copy_src=as_i32(copy_src), copy_dst=as_i32(copy_dst), copy_n=as_i32(copy_n),
                n_used=as_i32(n_used), block_expert=as_i32(block_expert), block_ord=as_i32(block_ord),
                seq_expert=as_i32(seq_expert), n_seq=as_i32(jnp.sum(present)), n_tail=as_i32(tail_end[-1]),
                tail_dst=as_i32(tail_dst * GRANULE))


def _row_copy(src, src_row, dst, dst_row, sem, rows):
    s = pl.multiple_of(src_row, GRANULE)
    d = pl.multiple_of(dst_row, GRANULE)
    return pltpu.make_async_copy(src.at[pl.ds(s, rows), :], dst.at[pl.ds(d, rows), :], sem)


def _grouped_loop(n, body):
    n_groups = n // COPY_GROUP

    def group(j, carry):
        for i in range(COPY_GROUP):
            body(j * COPY_GROUP + i)
        return carry

    def single(i, carry):
        body(i)
        return carry

    lax.fori_loop(0, n_groups, group, 0)
    lax.fori_loop(n_groups * COPY_GROUP, n, single, 0)


def _start_copies(n, make_copy):
    _grouped_loop(n, lambda i: make_copy(i).start())


def _wait_copies(n, src, dst, sem, rows):
    n_groups = n // COPY_GROUP

    def group(j, carry):
        _row_copy(src, 0, dst, 0, sem, COPY_GROUP * rows).wait()
        return carry

    def single(i, carry):
        _row_copy(src, 0, dst, 0, sem, rows).wait()
        return carry

    lax.fori_loop(0, n_groups, group, 0)
    lax.fori_loop(n_groups * COPY_GROUP, n, single, 0)


def _start_tile_copies(tile, copy_n, make_copy):
    for c, rows in enumerate(COPY_ROWS):
        base = tile * sum(COPY_CAP) + sum(COPY_CAP[:c])
        _start_copies(copy_n[tile * len(COPY_ROWS) + c], lambda i, base=base, rows=rows: make_copy(base + i, rows))


def _wait_tile_copies(tile, copy_n, src, dst, sem):
    for c, rows in enumerate(COPY_ROWS):
        _wait_copies(copy_n[tile * len(COPY_ROWS) + c], src, dst, sem, rows)


def _build_selection(sel_scr, rank, values, run_off_ref, run_len_ref, long_run_ref, tile_idx):
    slot = lax.broadcasted_iota(I32, (RUN_CHUNK, TM), 0)

    def write_runs(all_chunks):
        for e in range(N_EXPERTS):
            off = run_off_ref[tile_idx * N_EXPERTS + e]
            rank_e = rank[e:e + 1, :]
            val_e = 1.0 if values is None else values[e:e + 1, :]

            def chunk(c, carry, off=off, rank_e=rank_e, val_e=val_e):
                hit = rank_e == slot + c * RUN_CHUNK
                rows = pl.ds(pl.multiple_of(off + c * RUN_CHUNK, GRANULE), RUN_CHUNK)
                sel_scr[rows, :] = jnp.where(hit, val_e, 0.0)
                return carry

            chunk(0, 0)
            if all_chunks:
                n_chunks = (run_len_ref[tile_idx * N_EXPERTS + e] + (RUN_CHUNK - 1)) // RUN_CHUNK
                lax.fori_loop(1, n_chunks, chunk, 0)

    write_runs(False)

    @pl.when(long_run_ref[tile_idx] != 0)
    def _():
        write_runs(True)


def _dispatch_kernel(run_off, run_len, long_run, copy_src, copy_dst, copy_n, n_tail, tail_dst, n_used,
                     x1_ref, rank_ref, xs_hbm, sel_scr, buf_scr, zero_scr, sem):
    t = pl.program_id(0)

    @pl.when(t == 0)
    def _():
        sel_scr[...] = jnp.zeros_like(sel_scr)
        buf_scr[...] = jnp.zeros_like(buf_scr)
        zero_scr[...] = jnp.zeros_like(zero_scr)

    _build_selection(sel_scr, rank_ref[...], None, run_off, run_len, long_run, t)
    xb = x1_ref[...].astype(BF16)
    buf = buf_scr.at[t % 2]

    def order_rows(c):
        rows = slice(c * SEL_CHUNK, (c + 1) * SEL_CHUNK)
        buf[rows, :] = _pack_halves(_dot(sel_scr[rows, :].astype(BF16), xb))

    for c in range(L_MAX // SEL_CHUNK):
        order_rows(c)

    @pl.when(t > 0)
    def _():
        _wait_tile_copies(jnp.maximum(t - 1, 0), copy_n, buf, xs_hbm, sem)

    _start_tile_copies(
        t, copy_n, lambda i, rows: _row_copy(buf, copy_src[i], xs_hbm, copy_dst[i], sem, rows))

    @pl.when(t == pl.num_programs(0) - 1)
    def _():
        _wait_tile_copies(t, copy_n, buf, xs_hbm, sem)
        _start_copies(n_tail[0], lambda i: _row_copy(zero_scr, 0, xs_hbm, tail_dst[i], sem, GRANULE))
        _wait_copies(n_tail[0], zero_scr, xs_hbm, sem, GRANULE)

        def spare_copy(b):
            rows = pl.ds(pl.multiple_of(b * ROW_BLOCK, ROW_BLOCK), ROW_BLOCK)
            return pltpu.make_async_copy(zero_scr, xs_hbm.at[rows, :], sem)

        def start_spare(b, carry):
            spare_copy(b).start()
            return carry

        def wait_spare(b, carry):
            spare_copy(b).wait()
            return carry

        n_blocks = xs_hbm.shape[0] // ROW_BLOCK
        lax.fori_loop(n_used[0], n_blocks, start_spare, 0)
        lax.fori_loop(n_used[0], n_blocks, wait_spare, 0)


def _dispatch(x1, rank, tables, n_blocks):
    n = x1.shape[0] // TM
    return pl.pallas_call(
        _dispatch_kernel,
        grid_spec=pltpu.PrefetchScalarGridSpec(
            num_scalar_prefetch=9, grid=(n,),
            in_specs=[pl.BlockSpec((TM, D_MODEL), lambda i, *_: (i, 0)),
                      pl.BlockSpec((N_EXPERTS, TM), lambda i, *_: (0, i))],
            out_specs=pl.BlockSpec(memory_space=pl.ANY),
            scratch_shapes=[pltpu.VMEM((L_MAX, TM), F32), pltpu.VMEM((2, L_MAX, HALF), U32),
                            pltpu.VMEM((ROW_BLOCK, HALF), U32), pltpu.SemaphoreType.DMA(())]),
        out_shape=jax.ShapeDtypeStruct((n_blocks * ROW_BLOCK, HALF), U32),
        compiler_params=_params("arbitrary"), name="dispatch",
    )(tables["run_off"], tables["run_len"], tables["long_run"], tables["copy_src"],
      tables["copy_dst"], tables["copy_n"], tables["n_tail"],
      tables["tail_dst"], tables["n_used"], x1, rank)


def _experts_kernel(block_expert, block_ord, seq_expert, n_seq, n_used,
                    xs_hbm, wg_hbm, wu_hbm, wd_hbm, y_ref,
                    x_ring, wg_stage, wu_stage, wd_stage, wgu_scr, wd_scr, x_sems, w_sems, *, layer):
    b = pl.program_id(0)
    n_u = n_used[0]

    def fetch(block):
        slot = block % RING
        rows = pl.ds(pl.multiple_of(block * ROW_BLOCK, ROW_BLOCK), ROW_BLOCK)
        return pltpu.make_async_copy(xs_hbm.at[rows, :], x_ring.at[slot], x_sems.at[slot])

    def weight_copies(ordinal):
        e = seq_expert[ordinal]
        slot = ordinal % 2
        return [pltpu.make_async_copy(src.at[layer, e], stage.at[slot], w_sems.at[slot])
                for src, stage in ((wg_hbm, wg_stage), (wu_hbm, wu_stage), (wd_hbm, wd_stage))]

    @pl.when(b == 0)
    def _():
        for copy in weight_copies(0):
            copy.start()
        for first in range(RING - 1):
            @pl.when(first < n_u)
            def _(first=first):
                fetch(first).start()

    @pl.when(b + (RING - 1) < n_u)
    def _():
        fetch(b + (RING - 1)).start()

    @pl.when(b >= n_u)
    def _():
        y_ref[...] = jnp.zeros_like(y_ref)

    @pl.when(b < n_u)
    def _():
        fresh = (b == 0) | (block_expert[b] != block_expert[jnp.maximum(b - 1, 0)])

        @pl.when(fresh)
        def _():
            ordinal = block_ord[b]
            slot = ordinal % 2
            for copy in weight_copies(ordinal):
                copy.wait()
            wgu_scr[:, :D_EXPERT] = wg_stage[slot].astype(BF16)
            wgu_scr[:, D_EXPERT:] = wu_stage[slot].astype(BF16)
            wd_scr[...] = wd_stage[slot].astype(BF16)

            @pl.when(ordinal + 1 < n_seq[0])
            def _():
                for copy in weight_copies(ordinal + 1):
                    copy.start()

        fetch(b).wait()
        lo, hi = _unpack_halves(x_ring[b % RING])
        gu = _dot(lo, wgu_scr[:HALF, :]) + _dot(hi, wgu_scr[HALF:, :])
        hid = (_silu(gu[:, :D_EXPERT]) * gu[:, D_EXPERT:]).astype(BF16)
        y = _dot(hid, wd_scr[...])
        y_ref[...] = _pack_halves(y.astype(BF16).astype(F32))


def _experts(xs, tables, w_e_gate, w_e_up, w_e_down, layer, n_blocks):
    any_spec = pl.BlockSpec(memory_space=pl.ANY)
    return pl.pallas_call(
        functools.partial(_experts_kernel, layer=layer),
        grid_spec=pltpu.PrefetchScalarGridSpec(
            num_scalar_prefetch=5, grid=(n_blocks,),
            in_specs=[any_spec, any_spec, any_spec, any_spec],
            out_specs=pl.BlockSpec((ROW_BLOCK, HALF), lambda b, *_: (b, 0)),
            scratch_shapes=[pltpu.VMEM((RING, ROW_BLOCK, HALF), U32),
                            pltpu.VMEM((2, D_MODEL, D_EXPERT), F32), pltpu.VMEM((2, D_MODEL, D_EXPERT), F32),
                            pltpu.VMEM((2, D_EXPERT, D_MODEL), F32),
                            pltpu.VMEM((D_MODEL, 2 * D_EXPERT), BF16), pltpu.VMEM((D_EXPERT, D_MODEL), BF16),
                            pltpu.SemaphoreType.DMA((RING,)), pltpu.SemaphoreType.DMA((2,))]),
        out_shape=jax.ShapeDtypeStruct(xs.shape, U32),
        compiler_params=_params("arbitrary"), name="experts",
    )(tables["block_expert"], tables["block_ord"], tables["seq_expert"], tables["n_seq"], tables["n_used"],
      xs, w_e_gate, w_e_up, w_e_down)


def _combine_kernel(run_off, run_len, long_run, copy_src, copy_dst, copy_n,
                    x1_ref, rank_ref, wt_ref, y_hbm, wsgu, wsd, g2, b2, out_m_ref, out_t_ref,
                    sel_scr, buf_scr, acc_scr, sem):
    t = pl.program_id(0)
    n_tiles = pl.num_programs(0)

    def fetch(tile):
        dst = buf_scr.at[tile % 2]
        _start_tile_copies(
            tile, copy_n, lambda i, rows: _row_copy(y_hbm, copy_dst[i], dst, copy_src[i], sem, rows))

    @pl.when(t == 0)
    def _():
        buf_scr[...] = jnp.zeros_like(buf_scr)
        fetch(t)

    assert MIN_TILE_ROWS <= L_MAX and MIN_TILE_ROWS % GRANULE == 0
    sel_scr[MIN_TILE_ROWS:, :] = jnp.zeros((L_MAX - MIN_TILE_ROWS, TM), F32)
    x1 = x1_ref[...]
    xb = x1.astype(BF16)
    gu = _dot(xb, wsgu[...])
    hs = (_silu(gu[:, :D_EXPERT]) * gu[:, D_EXPERT:]).astype(BF16)
    acc_scr[...] = _dot(hs, wsd[...])
    _build_selection(sel_scr, rank_ref[...], wt_ref[...], run_off, run_len, long_run, t)
    buf = buf_scr.at[t % 2]
    _wait_tile_copies(t, copy_n, y_hbm, buf, sem)

    @pl.when(t + 1 < n_tiles)
    def _():
        fetch(jnp.minimum(t + 1, n_tiles - 1))

    def sum_rows(c):
        rows = slice(c * SEL_CHUNK, (c + 1) * SEL_CHUNK)
        lo, hi = _unpack_halves(buf[rows, :])
        sel = sel_scr[rows, :].astype(BF16)
        acc_scr[:, :HALF] += _dot_tn(sel, lo)
        acc_scr[:, HALF:] += _dot_tn(sel, hi)

    for c in range(L_MAX // SEL_CHUNK):
        sum_rows(c)

    out = _layer_norm(ALPHA * x1 + acc_scr[...], g2[...], b2[...])

    @pl.when(t + 1 < n_tiles)
    def _():
        out_m_ref[...] = out

    @pl.when(t + 1 == n_tiles)
    def _():
        out_t_ref[...] = out


def _combine(x1, rank, wt, y, tables, w):
    n = x1.shape[0] // TM
    small = (w["wsgu"], w["wsd"], w["g2"], w["b2"])
    tok = pl.BlockSpec((N_EXPERTS, TM), lambda i, *_: (0, i))
    return pl.pallas_call(
        _combine_kernel,
        grid_spec=pltpu.PrefetchScalarGridSpec(
            num_scalar_prefetch=6, grid=(n,),
            in_specs=[pl.BlockSpec((TM, D_MODEL), lambda i, *_: (i, 0)), tok, tok,
                      pl.BlockSpec(memory_space=pl.ANY)]
                     + [pl.BlockSpec(a.shape, lambda i, *_, nd=a.ndim: (0,) * nd) for a in small],
            out_specs=_main_tail_specs(n, D_MODEL),
            scratch_shapes=[pltpu.VMEM((L_MAX, TM), F32), pltpu.VMEM((2, L_MAX, HALF), U32),
                            pltpu.VMEM((TM, D_MODEL), F32), pltpu.SemaphoreType.DMA(())]),
        out_shape=(jax.ShapeDtypeStruct((x1.shape[0] - TM, D_MODEL), F32),
                   jax.ShapeDtypeStruct((TM, D_MODEL), F32)),
        compiler_params=_params("arbitrary"), name="combine",
    )(tables["run_off"], tables["run_len"], tables["long_run"], tables["copy_src"],
      tables["copy_dst"], tables["copy_n"], x1, rank, wt, y, *small)


def _moe(x1, w, w_e_gate, w_e_up, w_e_down, layer):
    n_blocks = _n_row_blocks(x1.shape[0])
    wt, rank, cnt = _router(x1, w)
    tables = _dispatch_tables(cnt[:, :N_EXPERTS, 0], n_blocks)
    xs = _dispatch(x1, rank, tables, n_blocks)
    y = _experts(xs, tables, w_e_gate, w_e_up, w_e_down, layer, n_blocks)
    return _combine(x1, rank, wt, y, tables, w)


def _layer_weights(l, w_in, w_gk2, b_gk, gla_norm_g, w_pool, pool_scale, w_br_a, w_br_b, w_out, ln1_g, ln1_b,
                   w_router, b_router, w_sh_gate, w_sh_up, w_sh_down, ln2_g, ln2_b):
    pieces = {}
    pad_rank = LANES - GATE_RANK
    pieces["wgk2"] = jnp.pad(w_gk2[l], ((0, pad_rank), (0, 0))).astype(BF16)
    pieces["bgk"] = b_gk[l].reshape(1, KEY)
    pieces["gn"] = gla_norm_g[l].reshape(1, DV)
    pieces["wbra"] = w_br_a[l].astype(BF16)
    pieces["wpool"] = w_pool[l].astype(BF16)
    pieces["pscale"] = pool_scale[l].reshape(1, POOL_WIDTH)
    pieces["wbrb"] = w_br_b[l].astype(BF16)
    pieces["wout"] = w_out[l].astype(BF16)
    pieces["g1"] = ln1_g[l].reshape(1, D_MODEL)
    pieces["b1"] = ln1_b[l].reshape(1, D_MODEL)
    wrt = w_router[l].T
    wrh = wrt.astype(BF16)
    pieces["wrh"] = wrh
    pieces["wrl"] = (wrt - wrh.astype(F32)).astype(BF16)
    pieces["br"] = b_router[l].reshape(N_EXPERTS, 1)
    pieces["wsgu"] = jnp.concatenate([w_sh_gate[l], w_sh_up[l]], axis=1).astype(BF16)
    pieces["wsd"] = w_sh_down[l].astype(BF16)
    pieces["g2"] = ln2_g[l].reshape(1, D_MODEL)
    pieces["b2"] = ln2_b[l].reshape(1, D_MODEL)
    return pieces


def kernel(x_prompt, x_sample, state_gla, cache_pool, w_in, w_gk2, b_gk, gla_norm_g, w_pool, pool_scale, w_br_a, w_br_b, w_out, ln1_g, ln1_b, w_router, b_router, w_e_gate, w_e_up, w_e_down, w_sh_gate, w_sh_up, w_sh_down, ln2_g, ln2_b):
    assert T_SAMPLE == TM, "the decode tokens must fill exactly one token tile"
    xm = x_prompt.reshape(T_PROMPT, D_MODEL)
    xt = x_sample.reshape(T_SAMPLE, D_MODEL)
    n_prompt_tiles = T_PROMPT // TM
    sample_block0 = T_PROMPT // DEC_SEQ
    sp, hp, ss, hs = [], [], [], []
    for l in range(DEPTH):
        w = _layer_weights(l, w_in, w_gk2, b_gk, gla_norm_g, w_pool, pool_scale, w_br_a, w_br_b, w_out, ln1_g,
                           ln1_b, w_router, b_router, w_sh_gate, w_sh_up, w_sh_down, ln2_g, ln2_b)
        q, k, v, r, lf, u, ga, gb = _inproj(xm, xt, w_in, w, l)

        og_p, st_p = _gla(q, k, v, lf, r, w["gn"], None, tile=TM, chunk=CHUNK, n_tiles=n_prompt_tiles, block0=0)
        og_s, st_s = _gla(q, k, v, lf, r, w["gn"], state_gla[l], tile=DEC_SEQ, chunk=DEC_SEQ, n_tiles=DEC_BATCH,
                          block0=sample_block0)

        u_tiles = u.reshape(n_prompt_tiles + 1, TM, POOL_WIDTH)
        halo_p = jnp.concatenate([jnp.zeros((1, HALO, POOL_WIDTH), F32),
                                  u_tiles[:n_prompt_tiles - 1, TM - HALO:, :]], axis=0)
        halo_s = jnp.concatenate([jnp.zeros((DEC_BATCH, HALO - POOL_HIST, POOL_WIDTH), F32), cache_pool[l]], axis=1)
        x1 = _mix(xm, xt, og_p, og_s, u, halo_p, halo_s, ga, gb, w)

        xm, xt = _moe(x1, w, w_e_gate, w_e_up, w_e_down, l)

        sp.append(st_p)
        ss.append(st_s)
        hp.append(u[T_PROMPT - POOL_HIST:T_PROMPT].reshape(1, POOL_HIST, POOL_WIDTH))
        hs.append(u[T_PROMPT:].reshape(DEC_BATCH, DEC_SEQ, POOL_WIDTH)[:, DEC_SEQ - POOL_HIST:, :])
    y_prompt = xm.reshape(1, SEQ, D_MODEL)
    y_sample = xt.reshape(DEC_BATCH, DEC_SEQ, D_MODEL)
    return (y_prompt, y_sample, jnp.stack(sp), jnp.stack(hp), jnp.stack(ss), jnp.stack(hs))
```
